```python
import math
import jax, jax.numpy as jnp
from jax import lax
import numpy as np

D_MODEL = 1024
BATCH = 2
SEQ = 8192
DEPTH = 2

EPS = 1e-6
MEM_LEN = 256
CONV_W = 4
N_EVEN = (DEPTH + 1) // 2
N_ODD = DEPTH // 2
RG_WIDTH = D_MODEL
RG_BLOCKS = 8
RG_BW = RG_WIDTH // RG_BLOCKS
RG_C = 8.0
NSA_HEADS = 8
NSA_KV_GROUPS = 2
NSA_REP = NSA_HEADS // NSA_KV_GROUPS
NSA_DK = 128
NSA_DV = NSA_DK
CMP_LEN = 32
CMP_STRIDE = 16
CMP_HID = 256
SLC_LEN = 64
SLC_TOPN = 16
WINDOW = 512
NSA_QBLOCK = 64
FORCE_BONUS = 100.0
EV_IN = 2 * RG_WIDTH + NSA_HEADS * NSA_DK + 6 * NSA_KV_GROUPS * NSA_DK + 3 * NSA_HEADS
EV_MIX = RG_WIDTH + NSA_HEADS * NSA_DV
D_INNER = 2 * D_MODEL
SSM_HEADDIM = 64
SSM_HEADS = D_INNER // SSM_HEADDIM
SSM_GROUPS = 4
SSM_REP = SSM_HEADS // SSM_GROUPS
SSM_STATE = 128
SSD_CHUNK = 128
CONV_CH = D_INNER + 2 * SSM_GROUPS * SSM_STATE
OD_IN = D_INNER + CONV_CH + SSM_HEADS
X_HEADS = 4
X_HEADDIM = 128
X_INNER = X_HEADS * X_HEADDIM
FF_DENSE = 2816
N_EXPERTS = 8
TOP_K = 2
FF_EXPERT = 3584

kernel_name = "hybrid_rglru_nsa_ssd_moe_block"


def rmsnorm(x, g):
    xf = x.astype(jnp.float32)
    y = xf * lax.rsqrt(jnp.mean(xf * xf, axis=-1, keepdims=True) + EPS)
    return (y * g.astype(jnp.float32)).astype(x.dtype)


def masked_softmax(s, mask):
    s = jnp.where(mask, s.astype(jnp.float32), -jnp.inf)
    m = jnp.max(s, axis=-1, keepdims=True)
    m = jnp.where(jnp.isfinite(m), m, 0.0)
    e = jnp.where(mask, jnp.exp(s - m), 0.0)
    return e / jnp.maximum(jnp.sum(e, axis=-1, keepdims=True), 1e-30)


def split_cols(x, sizes):
    return jnp.split(x, np.cumsum(sizes)[:-1].tolist(), axis=-1)


def causal_dwconv(x, w, b):
    k = w.shape[0]
    s = x.shape[1]
    xp = jnp.pad(x, ((0, 0), (k - 1, 0), (0, 0)))
    y = b
    for j in range(k):
        y = y + xp[:, j:j + s] * w[j]
    return y


def rglru(x, wa, ba, wx, bx, lam):
    b_, s_, c_ = x.shape
    xb = x.reshape(b_, s_, RG_BLOCKS, RG_BW)
    r = jax.nn.sigmoid(jnp.einsum("bshi,hij->bshj", xb, wa).reshape(b_, s_, c_) + ba)
    ig = jax.nn.sigmoid(jnp.einsum("bshi,hij->bshj", xb, wx).reshape(b_, s_, c_) + bx)
    log_a = -RG_C * r * jax.nn.softplus(-lam)
    a = jnp.exp(log_a)
    u = jnp.sqrt(-jnp.expm1(2.0 * log_a)) * (ig * x)

    def combine(lhs, rhs):
        a1, b1 = lhs
        a2, b2 = rhs
        return a1 * a2, a2 * b1 + b2

    _, h = lax.associative_scan(combine, (a, u), axis=1)
    return h


def nsa_compress(k, pos, w1, w2):
    b_, s_, g_, d_ = k.shape
    n_cmp = (s_ - CMP_LEN) // CMP_STRIDE + 1
    idx = jnp.arange(n_cmp)[:, None] * CMP_STRIDE + jnp.arange(CMP_LEN)[None, :]
    kb = k[:, idx] + pos[:, None, :]
    kb = kb.transpose(0, 1, 3, 2, 4).reshape(b_, n_cmp, g_, CMP_LEN * d_)
    return jax.nn.gelu(kb @ w1) @ w2


def nsa(q, kc, vc, ks, vs, kw, vw, gates, q_norm, k_norm, cmp_pos, ck_w1, ck_w2, cv_w1, cv_w2):
    b_, s_ = q.shape[:2]
    G, R, DK = NSA_KV_GROUPS, NSA_REP, NSA_DK
    scale = DK ** -0.5
    qr = rmsnorm(q, q_norm).reshape(b_, s_, G, R, DK)
    kcmp = rmsnorm(nsa_compress(kc, cmp_pos, ck_w1, ck_w2), k_norm[0])
    vcmp = nsa_compress(vc, cmp_pos, cv_w1, cv_w2)
    n_cmp = kcmp.shape[1]
    n_slc = s_ // SLC_LEN
    n_sel = min(SLC_TOPN, n_slc)
    ks_blk = rmsnorm(ks, k_norm[1]).reshape(b_, n_slc, SLC_LEN, G, DK).transpose(0, 3, 1, 2, 4)
    vs_blk = vs.reshape(b_, n_slc, SLC_LEN, G, NSA_DV).transpose(0, 3, 1, 2, 4)
    pad = ((0, 0), (WINDOW, 0), (0, 0), (0, 0))
    kw_pad = jnp.pad(rmsnorm(kw, k_norm[2]), pad)
    vw_pad = jnp.pad(vw, pad)
    g = jax.nn.sigmoid(gates).reshape(b_, s_, G, R, 3)
    cmp_start = jnp.arange(n_cmp) * CMP_STRIDE
    cmp_end = cmp_start + CMP_LEN - 1
    slc_start = jnp.arange(n_slc) * SLC_LEN
    overlap = ((cmp_start[:, None] <= slc_start[None, :] + SLC_LEN - 1)
               & (cmp_end[:, None] >= slc_start[None, :])).astype(jnp.float32)
    gather = jax.vmap(jax.vmap(lambda blocks, i: blocks[i]))
    j = jnp.arange(n_slc)

    def query_block(qb):
        t0 = qb * NSA_QBLOCK
        t = t0 + jnp.arange(NSA_QBLOCK)
        qblk = lax.dynamic_slice_in_dim(qr, t0, NSA_QBLOCK, axis=1)
        s = jnp.einsum("bqgrd,bngd->bgrqn", qblk, kcmp) * scale
        p_cmp = masked_softmax(s, cmp_end[None, :] <= t[:, None])
        o_cmp = jnp.einsum("bgrqn,bngd->bqgrd", p_cmp.astype(vcmp.dtype), vcmp)
        imp = jnp.einsum("bgrqn,nj->bgqj", p_cmp, overlap)
        cur = t // SLC_LEN
        forced = (j[None, :] == 0) | (j[None, :] == cur[:, None]) | (j[None, :] == cur[:, None] - 1)
        valid = slc_start[None, :] <= t[:, None]
        score = jnp.where(valid, imp + FORCE_BONUS * forced, -1.0)
        _, sel = lax.top_k(score, n_sel)
        ksel = gather(ks_blk, sel).reshape(b_, G, NSA_QBLOCK, n_sel * SLC_LEN, DK)
        vsel = gather(vs_blk, sel).reshape(b_, G, NSA_QBLOCK, n_sel * SLC_LEN, NSA_DV)
        key_pos = (sel[..., None] * SLC_LEN + jnp.arange(SLC_LEN)).reshape(b_, G, NSA_QBLOCK, n_sel * SLC_LEN)
        s = jnp.einsum("bqgrd,bgqkd->bgrqk", qblk, ksel) * scale
        p = masked_softmax(s, key_pos[:, :, None] <= t[:, None])
        o_slc = jnp.einsum("bgrqk,bgqkd->bqgrd", p.astype(vsel.dtype), vsel)
        kwb = lax.dynamic_slice_in_dim(kw_pad, t0, WINDOW + NSA_QBLOCK, axis=1)
        vwb = lax.dynamic_slice_in_dim(vw_pad, t0, WINDOW + NSA_QBLOCK, axis=1)
        kpos = t0 - WINDOW + jnp.arange(WINDOW + NSA_QBLOCK)
        diff = t[:, None] - kpos[None, :]
        s = jnp.einsum("bqgrd,bkgd->bgrqk", qblk, kwb) * scale
        p = masked_softmax(s, (diff >= 0) & (diff < WINDOW) & (kpos[None, :] >= 0))
        o_win = jnp.einsum("bgrqk,bkgd->bqgrd", p.astype(vwb.dtype), vwb)
        gb = lax.dynamic_slice_in_dim(g, t0, NSA_QBLOCK, axis=1)
        return gb[..., 0:1] * o_cmp + gb[..., 1:2] * o_slc + gb[..., 2:3] * o_win

    out = lax.map(query_block, jnp.arange(s_ // NSA_QBLOCK))
    return out.transpose(1, 0, 2, 3, 4, 5).reshape(b_, s_, NSA_HEADS * NSA_DV)


def even_mixer(h, w_in, conv_w, conv_b, wa, ba, wx, bx, lam, gate_b, q_norm, k_norm,
               cmp_pos, ck_w1, ck_w2, cv_w1, cv_w2, w_out):
    b_, s_, _ = h.shape
    gdk = NSA_KV_GROUPS * NSA_DK
    rg_x, rg_g, q, kc, vc, ks, vs, kw, vw, gates = split_cols(
        h @ w_in, [RG_WIDTH, RG_WIDTH, NSA_HEADS * NSA_DK] + [gdk] * 6 + [3 * NSA_HEADS])
    rg = jax.nn.gelu(rg_g) * rglru(causal_dwconv(rg_x, conv_w, conv_b), wa, ba, wx, bx, lam)

    def kv4(t):
        return t.reshape(b_, s_, NSA_KV_GROUPS, NSA_DK)

    att = nsa(q.reshape(b_, s_, NSA_HEADS, NSA_DK), kv4(kc), kv4(vc), kv4(ks), kv4(vs), kv4(kw), kv4(vw),
              gates + gate_b, q_norm, k_norm, cmp_pos, ck_w1, ck_w2, cv_w1, cv_w2)
    return jnp.concatenate([rg, att], axis=-1) @ w_out


def ssd(x, a, bm, cm):
    b_, s_, g_, r_, p_ = x.shape
    q_ = SSD_CHUNK
    c_ = s_ // q_
    x = x.reshape(b_, c_, q_, g_, r_, p_)
    bm = bm.reshape(b_, c_, q_, g_, -1)
    cm = cm.reshape(b_, c_, q_, g_, -1)
    a = a.reshape(b_, c_, q_, g_, r_).transpose(0, 3, 4, 1, 2)
    a_cs = jnp.cumsum(a, axis=-1)
    causal = jnp.tril(jnp.ones((q_, q_), dtype=bool))
    decay = jnp.exp(jnp.where(causal, a_cs[..., :, None] - a_cs[..., None, :], -jnp.inf))
    y_diag = jnp.einsum("bcqgn,bcsgn,bgrcqs,bcsgrp->bcqgrp", cm, bm, decay, x)
    decay_states = jnp.exp(a_cs[..., -1:] - a_cs)
    states = jnp.einsum("bcsgn,bgrcs,bcsgrp->bcgrpn", bm, decay_states, x)
    chunk_decay = jnp.exp(a_cs[..., -1])

    def step(hc, inp):
        st, dec = inp
        return dec[..., None, None] * hc + st, hc

    _, prev = lax.scan(step, jnp.zeros_like(states[:, 0]),
                       (jnp.moveaxis(states, 1, 0), jnp.moveaxis(chunk_decay, -1, 0)))
    y_off = jnp.einsum("bcqgn,cbgrpn,bgrcq->bcqgrp", cm, prev, jnp.exp(a_cs))
    return (y_diag + y_off).reshape(b_, s_, g_, r_, p_)


def mamba2(h, w_in, conv_w, conv_b, dt_bias, a_log, d_skip, norm_g, w_out):
    b_, s_, _ = h.shape
    gn = SSM_GROUPS * SSM_STATE
    z, xbc, dt = split_cols(h @ w_in, [D_INNER, CONV_CH, SSM_HEADS])
    xbc = jax.nn.silu(causal_dwconv(xbc, conv_w, conv_b))
    xs, bm, cm = split_cols(xbc, [D_INNER, gn, gn])
    xs = xs.reshape(b_, s_, SSM_GROUPS, SSM_REP, SSM_HEADDIM).astype(jnp.float32)
    bm = bm.reshape(b_, s_, SSM_GROUPS, SSM_STATE).astype(jnp.float32)
    cm = cm.reshape(b_, s_, SSM_GROUPS, SSM_STATE).astype(jnp.float32)
    dt = jax.nn.softplus(dt.astype(jnp.float32) + dt_bias.astype(jnp.float32)).reshape(b_, s_, SSM_GROUPS, SSM_REP)
    a = -jnp.exp(a_log.astype(jnp.float32)).reshape(SSM_GROUPS, SSM_REP)
    y = ssd(xs * dt[..., None], dt * a, bm, cm) + d_skip.astype(jnp.float32).reshape(SSM_GROUPS, SSM_REP, 1) * xs
    y = y.reshape(b_, s_, D_INNER).astype(h.dtype) * jax.nn.silu(z)
    y = rmsnorm(y.reshape(b_, s_, SSM_GROUPS, D_INNER // SSM_GROUPS),
                norm_g.reshape(SSM_GROUPS, D_INNER // SSM_GROUPS)).reshape(b_, s_, D_INNER)
    return y @ w_out


def cross_attn(xn, memn, wq, wkv, qn, kn, wo):
    b_, s_, _ = xn.shape
    m_ = memn.shape[1]
    q = rmsnorm((xn @ wq).reshape(b_, s_, X_HEADS, X_HEADDIM), qn)
    k, v = jnp.split(memn @ wkv, 2, axis=-1)
    k = rmsnorm(k.reshape(b_, m_, X_HEADS, X_HEADDIM), kn)
    v = v.reshape(b_, m_, X_HEADS, X_HEADDIM)
    s = jnp.einsum("bqhd,bmhd->bhqm", q, k).astype(jnp.float32) * (X_HEADDIM ** -0.5)
    p = jax.nn.softmax(s, axis=-1).astype(v.dtype)
    o = jnp.einsum("bhqm,bmhd->bqhd", p, v).reshape(b_, s_, X_INNER)
    return o @ wo


def swiglu(h, w13, w2):
    up, gate = jnp.split(h @ w13, 2, axis=-1)
    return (jax.nn.silu(gate) * up) @ w2


def moe(h, router, w13, w2):
    logits = (h @ router).astype(jnp.float32)
    top_v, top_i = lax.top_k(logits, TOP_K)
    w = jax.nn.softmax(top_v, axis=-1)
    gate = jnp.sum(jax.nn.one_hot(top_i, N_EXPERTS, dtype=jnp.float32) * w[..., None], axis=-2)
    y = jnp.zeros_like(h)
    for e in range(N_EXPERTS):
        y = y + gate[..., e:e + 1].astype(h.dtype) * swiglu(h, w13[e], w2[e])
    return y


def setup_inputs(seed: int = 0) -> dict:
    key = jax.random.key(seed)
    keys = iter(jax.random.split(key, 64))
    f32 = jnp.float32

    def nrm(shape, fan_in):
        return jax.random.normal(next(keys), shape, f32) * (fan_in ** -0.5)

    def gain(shape):
        return 1.0 + 0.02 * jax.random.normal(next(keys), shape, f32)

    def bias(shape):
        return 0.01 * jax.random.normal(next(keys), shape, f32)

    a0 = jax.random.uniform(next(keys), (N_EVEN, RG_WIDTH), f32, minval=0.9, maxval=0.999)
    s0 = a0 ** (1.0 / RG_C)
    lam = jnp.log(s0) - jnp.log1p(-s0)
    dt0 = jnp.exp(jax.random.uniform(next(keys), (N_ODD, SSM_HEADS), f32,
                                     minval=math.log(1e-3), maxval=math.log(1e-1)))
    dt_bias = dt0 + jnp.log(-jnp.expm1(-dt0))
    a_log = jnp.log(jax.random.uniform(next(keys), (N_ODD, SSM_HEADS), f32, minval=1.0, maxval=16.0))
    return {
        "x": jax.random.normal(next(keys), (BATCH, SEQ, D_MODEL), f32),
        "mem": jax.random.normal(next(keys), (BATCH, MEM_LEN, D_MODEL), f32),
        "norm_mix": gain((DEPTH, D_MODEL)),
        "norm_cross": gain((DEPTH, D_MODEL)),
        "norm_mem": gain((DEPTH, D_MODEL)),
        "norm_ffn": gain((DEPTH, D_MODEL)),
        "ev_w_in": nrm((N_EVEN, D_MODEL, EV_IN), D_MODEL),
        "ev_rg_conv_w": nrm((N_EVEN, CONV_W, RG_WIDTH), CONV_W),
        "ev_rg_conv_b": bias((N_EVEN, RG_WIDTH)),
        "ev_rg_wa": nrm((N_EVEN, RG_BLOCKS, RG_BW, RG_BW), RG_BW),
        "ev_rg_ba": bias((N_EVEN, RG_WIDTH)),
        "ev_rg_wx": nrm((N_EVEN, RG_BLOCKS, RG_BW, RG_BW), RG_BW),
        "ev_rg_bx": bias((N_EVEN, RG_WIDTH)),
        "ev_rg_lambda": lam,
        "ev_nsa_gate_b": bias((N_EVEN, 3 * NSA_HEADS)),
        "ev_q_norm": gain((N_EVEN, NSA_DK)),
        "ev_k_norm": gain((N_EVEN, 3, NSA_DK)),
        "ev_cmp_pos": 0.1 * jax.random.normal(next(keys), (N_EVEN, CMP_LEN, NSA_DK), f32),
        "ev_cmp_k_w1": nrm((N_EVEN, CMP_LEN * NSA_DK, CMP_HID), CMP_LEN * NSA_DK),
        "ev_cmp_k_w2": nrm((N_EVEN, CMP_HID, NSA_DK), CMP_HID),
        "ev_cmp_v_w1": nrm((N_EVEN, CMP_LEN * NSA_DK, CMP_HID), CMP_LEN * NSA_DK),
        "ev_cmp_v_w2": nrm((N_EVEN, CMP_HID, NSA_DV), CMP_HID),
        "ev_w_out": nrm((N_EVEN, EV_MIX, D_MODEL), EV_MIX),
        "od_w_in": nrm((N_ODD, D_MODEL, OD_IN), D_MODEL),
        "od_conv_w": nrm((N_ODD, CONV_W, CONV_CH), CONV_W),
        "od_conv_b": bias((N_ODD, CONV_CH)),
        "od_dt_bias": dt_bias,
        "od_a_log": a_log,
        "od_d_skip": gain((N_ODD, SSM_HEADS)),
        "od_norm": gain((N_ODD, D_INNER)),
        "od_w_out": nrm((N_ODD, D_INNER, D_MODEL), D_INNER),
        "x_wq": nrm((DEPTH, D_MODEL, X_INNER), D_MODEL),
        "x_wkv": nrm((DEPTH, D_MODEL, 2 * X_INNER), D_MODEL),
        "x_q_norm": gain((DEPTH, X_HEADDIM)),
        "x_k_norm": gain((DEPTH, X_HEADDIM)),
        "x_wo": nrm((DEPTH, X_INNER, D_MODEL), X_INNER),
        "ff_w13": nrm((N_EVEN, D_MODEL, 2 * FF_DENSE), D_MODEL),
        "ff_w2": nrm((N_EVEN, FF_DENSE, D_MODEL), FF_DENSE),
        "moe_router": nrm((N_ODD, D_MODEL, N_EXPERTS), D_MODEL),
        "moe_w13": nrm((N_ODD, N_EXPERTS, D_MODEL, 2 * FF_EXPERT), D_MODEL),
        "moe_w2": nrm((N_ODD, N_EXPERTS, FF_EXPERT, D_MODEL), FF_EXPERT),
    }


def reference(x, mem, norm_mix, norm_cross, norm_mem, norm_ffn,
              ev_w_in, ev_rg_conv_w, ev_rg_conv_b, ev_rg_wa, ev_rg_ba, ev_rg_wx, ev_rg_bx, ev_rg_lambda,
              ev_nsa_gate_b, ev_q_norm, ev_k_norm, ev_cmp_pos, ev_cmp_k_w1, ev_cmp_k_w2, ev_cmp_v_w1,
              ev_cmp_v_w2, ev_w_out,
              od_w_in, od_conv_w, od_conv_b, od_dt_bias, od_a_log, od_d_skip, od_norm, od_w_out,
              x_wq, x_wkv, x_q_norm, x_k_norm, x_wo,
              ff_w13, ff_w2, moe_router, moe_w13, moe_w2):
    for layer in range(DEPTH):
        i = layer // 2
        h = rmsnorm(x, norm_mix[layer])
        if layer % 2 == 0:
            mix = even_mixer(h, ev_w_in[i], ev_rg_conv_w[i], ev_rg_conv_b[i], ev_rg_wa[i], ev_rg_ba[i],
                             ev_rg_wx[i], ev_rg_bx[i], ev_rg_lambda[i], ev_nsa_gate_b[i], ev_q_norm[i],
                             ev_k_norm[i], ev_cmp_pos[i], ev_cmp_k_w1[i], ev_cmp_k_w2[i], ev_cmp_v_w1[i],
                             ev_cmp_v_w2[i], ev_w_out[i])
        else:
            mix = mamba2(h, od_w_in[i], od_conv_w[i], od_conv_b[i], od_dt_bias[i], od_a_log[i],
                         od_d_skip[i], od_norm[i], od_w_out[i])
        x = x + mix
        x = x + cross_attn(rmsnorm(x, norm_cross[layer]), rmsnorm(mem, norm_mem[layer]),
                           x_wq[layer], x_wkv[layer], x_q_norm[layer], x_k_norm[layer], x_wo[layer])
        h = rmsnorm(x, norm_ffn[layer])
        if layer % 2 == 0:
            x = x + swiglu(h, ff_w13[i], ff_w2[i])
        else:
            x = x + moe(h, moe_router[i], moe_w13[i], moe_w2[i])
    return x
```

```python
import functools
import math

import jax
import jax.numpy as jnp
import numpy as np
from jax import lax
from jax.experimental import pallas as pl
from jax.experimental.pallas import tpu as pltpu

F32 = jnp.float32
BF16 = jnp.bfloat16
I32 = jnp.int32

EPS = 1e-6
CONV_W = 4
RG_BLOCKS = 8
RG_C = 8.0
NSA_HEADS = 8
NSA_GROUPS = 2
NSA_REP = NSA_HEADS // NSA_GROUPS
NSA_DK = 128
CMP_LEN = 32
CMP_STRIDE = 16
SLC_LEN = 64
SLC_SHIFT = 6
SLC_TOPN = 16
WINDOW = 512
FORCE_BONUS = 100.0
SSM_HEADDIM = 64
SSM_GROUPS = 4
SSM_STATE = 128
SSD_CHUNK = 128
X_HEADS = 4
X_HEADDIM = 128
N_EXPERTS = 8

LANES = 128
VMEM_LIMIT_BYTES = 56 * 1024 * 1024
NEG = -1e30
SEL_BIAS = float(2 ** 20)

NT_DIMS = (((1,), (1,)), ((), ()))


def _params(*sem):
    return pltpu.CompilerParams(dimension_semantics=sem, vmem_limit_bytes=VMEM_LIMIT_BYTES)


def _dot(a, b):
    return jnp.dot(a, b, preferred_element_type=F32)


def _dot_nt(a, b):
    return lax.dot_general(a, b, NT_DIMS, preferred_element_type=F32)


def _rms(x, g):
    return x * lax.rsqrt(jnp.mean(x * x, axis=-1, keepdims=True) + EPS) * g


def _sigmoid(x):
    return 1.0 / (1.0 + jnp.exp(-x))


def _silu(x):
    return x * _sigmoid(x)


def _gelu_tanh(x):
    c = math.sqrt(2.0 / math.pi)
    return 0.5 * x * (1.0 + jnp.tanh(c * (x + 0.044715 * (x * x * x))))


def _softplus(x):
    return jnp.maximum(x, 0.0) + jnp.log(1.0 + jnp.exp(-jnp.abs(x)))


def _masked_softmax(s, mask):
    sm = jnp.where(mask, s, NEG)
    m = jnp.max(sm, axis=-1, keepdims=True)
    e = jnp.where(mask, jnp.exp(sm - m), 0.0)
    return e / jnp.maximum(jnp.sum(e, axis=-1, keepdims=True), 1e-30)


def _shift_rows(x, tail, k, rows):
    rolled = pltpu.roll(x, k, 0)
    head = pltpu.roll(tail, k, 0)
    pad = jnp.zeros((x.shape[0] - tail.shape[0], x.shape[1]), x.dtype)
    return jnp.where(rows < k, jnp.concatenate([head, pad], axis=0), rolled)


def _causal_conv(x, tail, w_ref, b_ref):
    rows = lax.broadcasted_iota(I32, (x.shape[0], 1), 0)
    y = b_ref[...] + w_ref[CONV_W - 1:CONV_W, :] * x
    for k in range(1, CONV_W):
        y = y + w_ref[CONV_W - 1 - k:CONV_W - k, :] * _shift_rows(x, tail, k, rows)
    return y


def _norm_proj_kernel(x_ref, g_ref, w_ref, eb_ref, *out_refs, n_main, n_planes, extra_sigmoid):
    h = _rms(x_ref[...], g_ref[...]).astype(BF16)
    main_ref = out_refs[0]
    for c0 in range(0, n_main, 512):
        main_ref[:, c0:c0 + 512] = _dot(h, w_ref[:, c0:c0 + 512]).astype(main_ref.dtype)
    col = n_main
    oi = 1
    if n_planes:
        kv_ref = out_refs[oi]
        oi += 1
        for p0 in range(0, n_planes, 4):
            r = _dot(h, w_ref[:, col:col + 512])
            for p in range(4):
                kv_ref[p0 + p] = r[:, p * LANES:(p + 1) * LANES].astype(kv_ref.dtype)
            col += 512
    ex_ref = out_refs[oi]
    n_extra = ex_ref.shape[1]
    e = _dot(h, w_ref[:, col:col + n_extra]) + eb_ref[...]
    ex_ref[...] = _sigmoid(e) if extra_sigmoid else e


def _norm_proj(x, g, w, eb, *, batch, n_main, n_planes, n_extra, extra_sigmoid, tm=512):
    t, d = x.shape
    seq = t // batch
    nt = seq // tm
    out_shape = [jax.ShapeDtypeStruct((t, n_main), BF16)]
    out_specs = [pl.BlockSpec((tm, n_main), lambda i: (i, 0))]
    if n_planes:
        out_shape.append(jax.ShapeDtypeStruct((batch, n_planes, seq, LANES), BF16))
        out_specs.append(pl.BlockSpec((None, n_planes, tm, LANES), lambda i: (i // nt, 0, i % nt, 0)))
    out_shape.append(jax.ShapeDtypeStruct((t, n_extra), F32))
    out_specs.append(pl.BlockSpec((tm, n_extra), lambda i: (i, 0)))
    kern = functools.partial(_norm_proj_kernel, n_main=n_main, n_planes=n_planes, extra_sigmoid=extra_sigmoid)
    return pl.pallas_call(
        kern,
        grid=(t // tm,),
        in_specs=[
            pl.BlockSpec((tm, d), lambda i: (i, 0)),
            pl.BlockSpec((1, d), lambda i: (0, 0)),
            pl.BlockSpec(w.shape, lambda i: (0, 0)),
            pl.BlockSpec((1, n_extra), lambda i: (0, 0)),
        ],
        out_specs=out_specs,
        out_shape=out_shape,
        compiler_params=_params("parallel"),
        name="norm_proj",
    )(x, g, w, eb)


def _rglru_kernel(rx_ref, rg_ref, cw_ref, cb_ref, wa_ref, ba_ref, wx_ref, bx_ref, lam_ref, o_ref, tail_ref, h_ref):
    tc, c = rx_ref.shape

    @pl.when(pl.program_id(1) == 0)
    def _():
        tail_ref[...] = jnp.zeros_like(tail_ref)
        h_ref[...] = jnp.zeros_like(h_ref)

    x = rx_ref[...].astype(F32)
    xc = _causal_conv(x, tail_ref[...], cw_ref, cb_ref)
    tail_ref[...] = x[tc - 8:tc, :]

    bw = c // RG_BLOCKS
    ra, rx = [], []
    for blk in range(RG_BLOCKS):
        xb = xc[:, blk * bw:(blk + 1) * bw].astype(BF16)
        ra.append(_dot(xb, wa_ref[blk]))
        rx.append(_dot(xb, wx_ref[blk]))
    r = _sigmoid(jnp.concatenate(ra, axis=1) + ba_ref[...])
    ig = _sigmoid(jnp.concatenate(rx, axis=1) + bx_ref[...])
    log_a = (-RG_C) * r * _softplus(-lam_ref[...])
    a = jnp.exp(log_a)
    u = jnp.sqrt(1.0 - a * a) * (ig * xc)

    rows = lax.broadcasted_iota(I32, (tc, 1), 0)
    d = 1
    while d < tc:
        keep = rows >= d
        a_sh = jnp.where(keep, pltpu.roll(a, d, 0), 1.0)
        u_sh = jnp.where(keep, pltpu.roll(u, d, 0), 0.0)
        u = a * u_sh + u
        a = a * a_sh
        d *= 2
    h = u + a * h_ref[7:8, :]
    h_ref[...] = h[tc - 8:tc, :]
    o_ref[...] = (_gelu_tanh(rg_ref[...].astype(F32)) * h).astype(o_ref.dtype)


def _rglru(main, cw, cb, wa, ba, wx, bx, lam, *, batch, tc=256):
    t = main.shape[0]
    c = cw.shape[1]
    nt = t // batch // tc
    vec = pl.BlockSpec((1, c), lambda b, i: (0, 0))
    blk = pl.BlockSpec(wa.shape, lambda b, i: (0, 0, 0))
    return pl.pallas_call(
        _rglru_kernel,
        grid=(batch, nt),
        in_specs=[
            pl.BlockSpec((tc, c), lambda b, i: (b * nt + i, 0)),
            pl.BlockSpec((tc, c), lambda b, i: (b * nt + i, 1)),
            pl.BlockSpec((CONV_W, c), lambda b, i: (0, 0)),
            vec, blk, vec, blk, vec, vec,
        ],
        out_specs=pl.BlockSpec((tc, c), lambda b, i: (b * nt + i, 0)),
        out_shape=jax.ShapeDtypeStruct((t, c), BF16),
        scratch_shapes=[pltpu.VMEM((8, c), F32), pltpu.VMEM((8, c), F32)],
        compiler_params=_params("parallel", "arbitrary"),
        name="rglru",
    )(main, main, cw, cb, wa, ba, wx, bx, lam)


def _nsa_kprep_kernel(ks_ref, kw_ref, kn_ref, kp_ref, kwn_ref):
    tk = ks_ref.shape[0]
    ks = _rms(ks_ref[...].astype(F32), kn_ref[1:2, :]).astype(BF16)
    t0 = pl.program_id(2) * tk
    blk = jnp.right_shift(t0 + lax.broadcasted_iota(I32, (tk, LANES), 0), SLC_SHIFT)
    onehot = jnp.where(blk == lax.broadcasted_iota(I32, (tk, LANES), 1), 1.0, 0.0).astype(BF16)
    kp_ref[...] = jnp.concatenate([ks, onehot], axis=1)
    kwn_ref[...] = _rms(kw_ref[...].astype(F32), kn_ref[2:3, :]).astype(BF16)


def _nsa_kprep(planes, k_norm, *, tk=512):
    b, _, seq, _ = planes.shape
    g = NSA_GROUPS
    return pl.pallas_call(
        _nsa_kprep_kernel,
        grid=(b, g, seq // tk),
        in_specs=[
            pl.BlockSpec((None, None, tk, LANES), lambda bi, gi, i: (bi, 4 + gi, i, 0)),
            pl.BlockSpec((None, None, tk, LANES), lambda bi, gi, i: (bi, 8 + gi, i, 0)),
            pl.BlockSpec((8, LANES), lambda bi, gi, i: (0, 0)),
        ],
        out_specs=[
            pl.BlockSpec((None, None, tk, 2 * LANES), lambda bi, gi, i: (bi, gi, i, 0)),
            pl.BlockSpec((None, None, tk, LANES), lambda bi, gi, i: (bi, gi, i, 0)),
        ],
        out_shape=[
            jax.ShapeDtypeStruct((b, g, seq, 2 * LANES), BF16),
            jax.ShapeDtypeStruct((b, g, seq, LANES), BF16),
        ],
        compiler_params=_params("parallel", "parallel", "parallel"),
        name="nsa_kprep",
    )(planes, planes, k_norm)


def _nsa_compress_kernel(xk_ref, xv_ref, pos_ref, kw1_ref, kw2_ref, vw1_ref, vw2_ref, kn_ref, kc_ref, vc_ref):
    n, half = xk_ref.shape
    last = lax.broadcasted_iota(I32, (n, 1), 0) == n - 1
    pos = pos_ref[...]

    def compress(x_ref, w1_ref, w2_ref):
        x = x_ref[...]
        y0 = _dot(x, w1_ref[0:half, :])
        y1 = _dot(x, w1_ref[half:2 * half, :])
        y1_next = jnp.where(last, 0.0, pltpu.roll(y1, n - 1, 0))
        const = _dot(pos, w1_ref[...])[0:1, :]
        hid = _gelu_tanh(y0 + y1_next + const)
        return _dot(hid.astype(BF16), w2_ref[...])

    kc_ref[...] = _rms(compress(xk_ref, kw1_ref, kw2_ref), kn_ref[0:1, :]).astype(BF16)
    vc_ref[...] = compress(xv_ref, vw1_ref, vw2_ref).astype(BF16)


def _nsa_compress(planes, pos_flat, kw1, kw2, vw1, vw2, k_norm):
    b, n_planes, seq, _ = planes.shape
    g = NSA_GROUPS
    n = seq // CMP_STRIDE
    half = CMP_STRIDE * LANES
    grouped = planes.reshape(b, n_planes, n, half)
    full = lambda a: pl.BlockSpec(a.shape, lambda bi, gi: (0,) * a.ndim)
    out_spec = pl.BlockSpec((None, None, n, LANES), lambda bi, gi: (bi, gi, 0, 0))
    out_shape = jax.ShapeDtypeStruct((b, g, n, LANES), BF16)
    return pl.pallas_call(
        _nsa_compress_kernel,
        grid=(b, g),
        in_specs=[
            pl.BlockSpec((None, None, n, half), lambda bi, gi: (bi, gi, 0, 0)),
            pl.BlockSpec((None, None, n, half), lambda bi, gi: (bi, 2 + gi, 0, 0)),
            full(pos_flat), full(kw1), full(kw2), full(vw1), full(vw2), full(k_norm),
        ],
        out_specs=[out_spec, out_spec],
        out_shape=[out_shape, out_shape],
        compiler_params=_params("parallel", "parallel"),
        name="nsa_compress",
    )(grouped, grouped, pos_flat, kw1, kw2, vw1, vw2, k_norm)


def _nsa_cw_kernel(q_ref, kc_ref, vc_ref, kw_ref, vw_ref, gt_ref, qn_ref, ov_ref, qp_ref, o_ref, *, qn, n_sel):
    t0 = pl.program_id(2) * qn
    rep = NSA_REP
    rows = rep * qn
    scale = NSA_DK ** -0.5
    qf = q_ref[...].astype(F32)
    heads = []
    for r in range(rep):
        qh = _rms(qf[:, r * LANES:(r + 1) * LANES], qn_ref[...]) * scale
        heads.append(qh.astype(BF16))
    qs = jnp.concatenate(heads, axis=0)
    trow = t0 + (lax.broadcasted_iota(I32, (rows, 1), 0) & (qn - 1))

    n_cmp = kc_ref.shape[0]
    s = _dot_nt(qs, kc_ref[...])
    ncol = lax.broadcasted_iota(I32, (1, n_cmp), 1)
    p = _masked_softmax(s, ncol * CMP_STRIDE + (CMP_LEN - 1) <= trow)
    o_cmp = _dot(p.astype(BF16), vc_ref[...])

    psum = p[0:qn]
    for r in range(1, rep):
        psum = psum + p[r * qn:(r + 1) * qn]
    p_hi = psum.astype(BF16)
    p_lo = (psum - p_hi.astype(F32)).astype(BF16)
    imp = _dot(p_hi, ov_ref[...]) + _dot(p_lo, ov_ref[...])

    imp_t = imp.T
    jj = lax.broadcasted_iota(I32, imp_t.shape, 0).astype(F32)
    cur = jnp.right_shift(t0 + lax.broadcasted_iota(I32, imp_t.shape, 1), SLC_SHIFT).astype(F32)
    forced = (jj == 0.0) | (jj == cur) | (jj == cur - 1.0)
    work = jnp.where(jj <= cur, imp_t + jnp.where(forced, FORCE_BONUS, 0.0), -1.0)
    bias_t = jnp.full(imp_t.shape, -SEL_BIAS, F32)
    for _ in range(n_sel):
        m = jnp.max(work, axis=0, keepdims=True)
        idx = jnp.min(jnp.where(work == m, jj, float(LANES)), axis=0, keepdims=True)
        pick = jj == idx
        bias_t = jnp.where(pick, 0.0, bias_t)
        work = jnp.where(pick, -2.0, work)
    bias = bias_t.T.astype(BF16)
    for r in range(rep):
        qp_ref[r] = jnp.concatenate([heads[r], bias], axis=1)

    span = WINDOW + qn
    start = pl.multiple_of(jnp.maximum(t0 - WINDOW, 0), qn)
    s = _dot_nt(qs, kw_ref[pl.ds(start, span), :])
    diff = trow - (start + lax.broadcasted_iota(I32, (1, span), 1))
    p = _masked_softmax(s, (diff >= 0) & (diff < WINDOW))
    o_win = _dot(p.astype(BF16), vw_ref[pl.ds(start, span), :])

    gt = gt_ref[...]
    for r in range(rep):
        sl = slice(r * qn, (r + 1) * qn)
        o = gt[:, 3 * r:3 * r + 1] * o_cmp[sl] + gt[:, 3 * r + 2:3 * r + 3] * o_win[sl]
        o_ref[:, r * LANES:(r + 1) * LANES] = o.astype(o_ref.dtype)


def _nsa_cw(main, kcmp, vcmp, kwn, planes, gates, q_norm, overlap, *, qn=256):
    b, g, seq, _ = kwn.shape
    t = main.shape[0]
    nq = seq // qn
    rep = NSA_REP
    gw = rep * LANES
    q_blk0 = (main.shape[1] - NSA_HEADS * NSA_DK) // gw
    n_cmp = kcmp.shape[2]
    kern = functools.partial(_nsa_cw_kernel, qn=qn, n_sel=min(SLC_TOPN, seq // SLC_LEN))
    return pl.pallas_call(
        kern,
        grid=(b, g, nq),
        in_specs=[
            pl.BlockSpec((qn, gw), lambda bi, gi, i: (bi * nq + i, q_blk0 + gi)),
            pl.BlockSpec((None, None, n_cmp, LANES), lambda bi, gi, i: (bi, gi, 0, 0)),
            pl.BlockSpec((None, None, n_cmp, LANES), lambda bi, gi, i: (bi, gi, 0, 0)),
            pl.BlockSpec((None, None, seq, LANES), lambda bi, gi, i: (bi, gi, 0, 0)),
            pl.BlockSpec((None, None, seq, LANES), lambda bi, gi, i: (bi, 10 + gi, 0, 0)),
            pl.BlockSpec((qn, LANES), lambda bi, gi, i: (bi * nq + i, gi)),
            pl.BlockSpec((1, LANES), lambda bi, gi, i: (0, 0)),
            pl.BlockSpec(overlap.shape, lambda bi, gi, i: (0, 0)),
        ],
        out_specs=[
            pl.BlockSpec((None, None, rep, qn, 2 * LANES), lambda bi, gi, i: (bi, gi, 0, i, 0)),
            pl.BlockSpec((qn, gw), lambda bi, gi, i: (bi * nq + i, gi)),
        ],
        out_shape=[
            jax.ShapeDtypeStruct((b, g, rep, seq, 2 * LANES), BF16),
            jax.ShapeDtypeStruct((t, NSA_HEADS * NSA_DK), BF16),
        ],
        compiler_params=_params("parallel", "parallel", "parallel"),
        name="nsa_cmp_win",
    )(main, kcmp, vcmp, kwn, planes, gates, q_norm, overlap)


def _nsa_slc_kernel(qp_ref, kp_ref, vs_ref, gt_ref, ocw_ref, o_ref, m_ref, l_ref, acc_ref, *, qn, tk):
    t0 = pl.program_id(2) * qn
    rep = NSA_REP
    rows = rep * qn
    qp = qp_ref[...].reshape(rows, qp_ref.shape[2])
    m_ref[...] = jnp.full(m_ref.shape, NEG, F32)
    l_ref[...] = jnp.zeros_like(l_ref)
    acc_ref[...] = jnp.zeros_like(acc_ref)

    def step(j, causal):
        k0 = pl.multiple_of(j * tk, tk)
        s = _dot_nt(qp, kp_ref[pl.ds(k0, tk), :])
        if causal:
            trow = t0 + (lax.broadcasted_iota(I32, (rows, 1), 0) & (qn - 1))
            s = jnp.where(k0 + lax.broadcasted_iota(I32, (1, tk), 1) <= trow, s, -SEL_BIAS)
        m_old = m_ref[...]
        m_new = jnp.maximum(m_old, jnp.max(s, axis=-1, keepdims=True))
        p = jnp.exp(s - m_new)
        alpha = jnp.exp(m_old - m_new)
        l_ref[...] = alpha * l_ref[...] + jnp.sum(p, axis=-1, keepdims=True)
        acc_ref[...] = alpha * acc_ref[...] + _dot(p.astype(BF16), vs_ref[pl.ds(k0, tk), :])
        m_ref[...] = m_new

    last = (t0 + qn - 1) // tk
    lax.fori_loop(0, last, lambda j, c: (step(j, False), c)[1], 0)
    step(last, True)

    o_slc = acc_ref[...] / jnp.maximum(l_ref[...], 1e-30)
    gt = gt_ref[...]
    for r in range(rep):
        o = ocw_ref[:, r * LANES:(r + 1) * LANES].astype(F32) + gt[:, 3 * r + 1:3 * r + 2] * o_slc[r * qn:(r + 1) * qn]
        o_ref[:, r * LANES:(r + 1) * LANES] = o.astype(o_ref.dtype)


def _nsa_slc(qp, kp, planes, gates, ocw, *, qn=256, tk=512):
    b, g, rep, seq, dqk = qp.shape
    t = ocw.shape[0]
    nq = seq // qn
    gw = rep * LANES
    tk = min(tk, seq)
    kern = functools.partial(_nsa_slc_kernel, qn=qn, tk=tk)
    return pl.pallas_call(
        kern,
        grid=(b, g, nq),
        in_specs=[
            pl.BlockSpec((None, None, rep, qn, dqk), lambda bi, gi, i: (bi, gi, 0, i, 0)),
            pl.BlockSpec((None, None, seq, dqk), lambda bi, gi, i: (bi, gi, 0, 0)),
            pl.BlockSpec((None, None, seq, LANES), lambda bi, gi, i: (bi, 6 + gi, 0, 0)),
            pl.BlockSpec((qn, LANES), lambda bi, gi, i: (bi * nq + i, gi)),
            pl.BlockSpec((qn, gw), lambda bi, gi, i: (bi * nq + i, gi)),
        ],
        out_specs=pl.BlockSpec((qn, gw), lambda bi, gi, i: (bi * nq + i, gi)),
        out_shape=jax.ShapeDtypeStruct((t, NSA_HEADS * NSA_DK), BF16),
        scratch_shapes=[
            pltpu.VMEM((rep * qn, 1), F32),
            pltpu.VMEM((rep * qn, 1), F32),
            pltpu.VMEM((rep * qn, LANES), F32),
        ],
        compiler_params=_params("parallel", "parallel", "parallel"),
        name="nsa_selected",
    )(qp, kp, planes, gates, ocw)


def _out_proj_kernel(*refs, n_in):
    a_refs = refs[:n_in]
    w_ref, x_ref, o_ref = refs[n_in:]
    acc = x_ref[...]
    k0 = 0
    for a_ref in a_refs:
        k = a_ref.shape[1]
        acc = acc + _dot(a_ref[...], w_ref[k0:k0 + k, :])
        k0 += k
    o_ref[...] = acc


def _out_proj(acts, w, x, *, tm=512):
    t, d = x.shape
    return pl.pallas_call(
        functools.partial(_out_proj_kernel, n_in=len(acts)),
        grid=(t // tm,),
        in_specs=[pl.BlockSpec((tm, a.shape[1]), lambda i: (i, 0)) for a in acts]
        + [pl.BlockSpec(w.shape, lambda i: (0, 0)), pl.BlockSpec((tm, d), lambda i: (i, 0))],
        out_specs=pl.BlockSpec((tm, d), lambda i: (i, 0)),
        out_shape=jax.ShapeDtypeStruct((t, d), F32),
        compiler_params=_params("parallel"),
        name="out_proj",
    )(*acts, w, x)


def _mem_kv_kernel(mem_ref, g_ref, wkv_ref, kn_ref, k_ref, v_ref):
    memn = _rms(mem_ref[...], g_ref[...]).astype(BF16)
    kv = _dot(memn, wkv_ref[...])
    inner = k_ref.shape[1]
    for h in range(inner // X_HEADDIM):
        sl = slice(h * X_HEADDIM, (h + 1) * X_HEADDIM)
        k_ref[:, sl] = _rms(kv[:, sl], kn_ref[...]).astype(BF16)
    v_ref[...] = kv[:, inner:].astype(BF16)


def _mem_kv(mem, g, wkv, kn):
    b, m, d = mem.shape
    inner = wkv.shape[1] // 2
    out_spec = pl.BlockSpec((None, m, inner), lambda bi: (bi, 0, 0))
    out_shape = jax.ShapeDtypeStruct((b, m, inner), BF16)
    return pl.pallas_call(
        _mem_kv_kernel,
        grid=(b,),
        in_specs=[
            pl.BlockSpec((None, m, d), lambda bi: (bi, 0, 0)),
            pl.BlockSpec((1, d), lambda bi: (0, 0)),
            pl.BlockSpec(wkv.shape, lambda bi: (0, 0)),
            pl.BlockSpec((1, X_HEADDIM), lambda bi: (0, 0)),
        ],
        out_specs=[out_spec, out_spec],
        out_shape=[out_shape, out_shape],
        compiler_params=_params("parallel"),
        name="mem_kv",
    )(mem, g, wkv, kn)


def _cross_attn_kernel(x_ref, g_ref, wq_ref, qn_ref, k_ref, v_ref, wo_ref, o_ref):
    x = x_ref[...]
    q = _dot(_rms(x, g_ref[...]).astype(BF16), wq_ref[...])
    scale = X_HEADDIM ** -0.5
    outs = []
    for h in range(q.shape[1] // X_HEADDIM):
        sl = slice(h * X_HEADDIM, (h + 1) * X_HEADDIM)
        qh = (_rms(q[:, sl], qn_ref[...]) * scale).astype(BF16)
        s = _dot_nt(qh, k_ref[:, sl])
        e = jnp.exp(s - jnp.max(s, axis=-1, keepdims=True))
        p = e / jnp.sum(e, axis=-1, keepdims=True)
        outs.append(_dot(p.astype(BF16), v_ref[:, sl]))
    o = jnp.concatenate(outs, axis=1).astype(BF16)
    o_ref[...] = x + _dot(o, wo_ref[...])


def _cross_attn(x, g, wq, qn, k, v, wo, *, tm=512):
    t, d = x.shape
    b, m, inner = k.shape
    nt = t // b // tm
    return pl.pallas_call(
        _cross_attn_kernel,
        grid=(b, nt),
        in_specs=[
            pl.BlockSpec((tm, d), lambda bi, i: (bi * nt + i, 0)),
            pl.BlockSpec((1, d), lambda bi, i: (0, 0)),
            pl.BlockSpec(wq.shape, lambda bi, i: (0, 0)),
            pl.BlockSpec((1, X_HEADDIM), lambda bi, i: (0, 0)),
            pl.BlockSpec((None, m, inner), lambda bi, i: (bi, 0, 0)),
            pl.BlockSpec((None, m, inner), lambda bi, i: (bi, 0, 0)),
            pl.BlockSpec(wo.shape, lambda bi, i: (0, 0)),
        ],
        out_specs=pl.BlockSpec((tm, d), lambda bi, i: (bi * nt + i, 0)),
        out_shape=jax.ShapeDtypeStruct((t, d), F32),
        compiler_params=_params("parallel", "parallel"),
        name="cross_attn",
    )(x, g, wq, qn, k, v, wo)


def _swiglu_kernel(x_ref, g_ref, wu_ref, wg_ref, w2_ref, o_ref, h_ref, acc_ref):
    f = pl.program_id(1)

    @pl.when(f == 0)
    def _():
        h_ref[...] = _rms(x_ref[...], g_ref[...]).astype(BF16)
        acc_ref[...] = x_ref[...]

    h = h_ref[...]
    act = _silu(_dot(h, wg_ref[...])) * _dot(h, wu_ref[...])
    acc_ref[...] += _dot(act.astype(BF16), w2_ref[...])

    @pl.when(f == pl.num_programs(1) - 1)
    def _():
        o_ref[...] = acc_ref[...]


def _swiglu(x, g, w13, w2, *, tm=1024, tf=256):
    t, d = x.shape
    ff = w2.shape[0]
    nf = ff // tf
    return pl.pallas_call(
        _swiglu_kernel,
        grid=(t // tm, nf),
        in_specs=[
            pl.BlockSpec((tm, d), lambda i, f: (i, 0)),
            pl.BlockSpec((1, d), lambda i, f: (0, 0)),
            pl.BlockSpec((d, tf), lambda i, f: (0, f)),
            pl.BlockSpec((d, tf), lambda i, f: (0, nf + f)),
            pl.BlockSpec((tf, d), lambda i, f: (f, 0)),
        ],
        out_specs=pl.BlockSpec((tm, d), lambda i, f: (i, 0)),
        out_shape=jax.ShapeDtypeStruct((t, d), F32),
        scratch_shapes=[pltpu.VMEM((tm, d), BF16), pltpu.VMEM((tm, d), F32)],
        compiler_params=_params("parallel", "arbitrary"),
        name="swiglu",
    )(x, g, w13, w13, w2)


def _split3(x):
    a = x.astype(BF16)
    r = x - a.astype(F32)
    b = r.astype(BF16)
    c = (r - b.astype(F32)).astype(BF16)
    return a, b, c


def _ssd_kernel(z0_ref, z1_ref, x0_ref, x1_ref, bc_ref, dt_ref, cw_ref, cb_ref, dtb_ref, alog_ref, dskip_ref,
                ng_ref, o_ref, tail_ref, state_ref):
    q = x0_ref.shape[0]
    d_inner = o_ref.shape[1]
    gn = SSM_GROUPS * SSM_STATE
    hpg = d_inner // SSM_HEADDIM // SSM_GROUPS

    @pl.when(pl.program_id(1) == 0)
    def _():
        tail_ref[...] = jnp.zeros_like(tail_ref)
        state_ref[...] = jnp.zeros_like(state_ref)

    raw = jnp.concatenate([x0_ref[...], x1_ref[...], bc_ref[...]], axis=1).astype(F32)
    xbc = _silu(_causal_conv(raw, tail_ref[...], cw_ref, cb_ref))
    tail_ref[...] = raw[q - 8:q, :]
    xs = xbc[:, :d_inner]
    bm = xbc[:, d_inner:d_inner + gn]
    cm = xbc[:, d_inner + gn:]

    dt = _softplus(dt_ref[...] + dtb_ref[...])
    a = dt * (-jnp.exp(alog_ref[...]))
    ri = lax.broadcasted_iota(I32, (q, q), 0)
    ci = lax.broadcasted_iota(I32, (q, q), 1)
    causal = ci <= ri
    tri = jnp.where(causal, 1.0, 0.0).astype(BF16)
    a_cs = sum(_dot(tri, part) for part in _split3(a))
    a_cs_t = a_cs.T
    dt_t = dt.T
    lane = lax.broadcasted_iota(I32, (1, LANES), 1)
    lo = lane < SSM_HEADDIM

    y_parts = []
    for g in range(SSM_GROUPS):
        cg = cm[:, g * SSM_STATE:(g + 1) * SSM_STATE].astype(BF16)
        bg = bm[:, g * SSM_STATE:(g + 1) * SSM_STATE]
        gmat = _dot_nt(cg, bg.astype(BF16))
        bg_t = bg.T
        gw = hpg * SSM_HEADDIM
        prev = state_ref[:, g * gw:(g + 1) * gw]
        y_off = _dot(cg, prev.astype(BF16))
        for pr in range(hpg // 2):
            c0 = g * gw + pr * LANES
            x_pair = xs[:, c0:c0 + LANES]
            y_pair = dskip_ref[:, c0:c0 + LANES] * x_pair
            st_pair = jnp.zeros((SSM_STATE, LANES), F32)
            decay_pair = jnp.zeros((1, LANES), F32)
            for half in range(2):
                h = g * hpg + pr * 2 + half
                sel = lo if half == 0 else jnp.logical_not(lo)
                xh = jnp.where(sel, x_pair, 0.0).astype(BF16)
                row_cs = a_cs_t[h:h + 1, :]
                col_cs = a_cs[:, h:h + 1]
                row_dt = dt_t[h:h + 1, :]
                a_last = a_cs_t[h:h + 1, q - 1:q]
                dec = jnp.exp(jnp.where(causal, col_cs - row_cs, NEG))
                y_pair = y_pair + _dot((gmat * dec * row_dt).astype(BF16), xh)
                w_row = jnp.exp(a_last - row_cs) * row_dt
                st_pair = st_pair + _dot((bg_t * w_row).astype(BF16), xh)
                y_pair = y_pair + jnp.where(sel, jnp.exp(col_cs) * y_off[:, pr * LANES:(pr + 1) * LANES], 0.0)
                decay_pair = jnp.where(sel, jnp.exp(a_last), decay_pair)
            state_ref[:, c0:c0 + LANES] = decay_pair * state_ref[:, c0:c0 + LANES] + st_pair
            y_parts.append(y_pair)
    y = jnp.concatenate(y_parts, axis=1)

    z = jnp.concatenate([z0_ref[...], z1_ref[...]], axis=1).astype(F32)
    y = y * _silu(z)
    gsz = d_inner // SSM_GROUPS
    for g in range(SSM_GROUPS):
        sl = slice(g * gsz, (g + 1) * gsz)
        o_ref[:, sl] = _rms(y[:, sl], ng_ref[:, sl]).astype(o_ref.dtype)


def _ssd(main, dt, cw, cb, dtb, alog, dskip, ng, *, batch, d_inner):
    t = main.shape[0]
    q = SSD_CHUNK
    nc = t // batch // q
    conv_ch = cw.shape[1]
    half = d_inner // 2
    col = lambda j: pl.BlockSpec((q, half), lambda b, i: (b * nc + i, j))
    vec = lambda n: pl.BlockSpec((1, n), lambda b, i: (0, 0))
    return pl.pallas_call(
        _ssd_kernel,
        grid=(batch, nc),
        in_specs=[
            col(0), col(1), col(2), col(3), col(4),
            pl.BlockSpec((q, LANES), lambda b, i: (b * nc + i, 0)),
            pl.BlockSpec((CONV_W, conv_ch), lambda b, i: (0, 0)),
            vec(conv_ch), vec(LANES), vec(LANES), vec(d_inner), vec(d_inner),
        ],
        out_specs=pl.BlockSpec((q, d_inner), lambda b, i: (b * nc + i, 0)),
        out_shape=jax.ShapeDtypeStruct((t, d_inner), BF16),
        scratch_shapes=[pltpu.VMEM((8, conv_ch), F32), pltpu.VMEM((SSM_STATE, d_inner), F32)],
        compiler_params=_params("parallel", "arbitrary"),
        name="ssd",
    )(main, main, main, main, main, dt, cw, cb, dtb, alog, dskip, ng)


def _router_kernel(x_ref, g_ref, wr_ref, h_ref, info_ref, cnt_ref, run_ref):
    tm = x_ref.shape[0]

    @pl.when(pl.program_id(0) == 0)
    def _():
        run_ref[...] = jnp.zeros_like(run_ref)

    h = _rms(x_ref[...], g_ref[...])
    h_ref[...] = h
    logits = jnp.dot(h, wr_ref[...], preferred_element_type=F32, precision=lax.Precision.HIGHEST)
    lane = lax.broadcasted_iota(I32, (tm, LANES), 1)
    lg = jnp.where(lane < N_EXPERTS, logits, NEG)
    m1 = jnp.max(lg, axis=-1, keepdims=True)
    i1 = jnp.min(jnp.where(lg == m1, lane, LANES), axis=-1, keepdims=True)
    lg2 = jnp.where(lane == i1, NEG, lg)
    m2 = jnp.max(lg2, axis=-1, keepdims=True)
    i2 = jnp.min(jnp.where(lg2 == m2, lane, LANES), axis=-1, keepdims=True)
    e2 = jnp.exp(m2 - m1)
    w1 = 1.0 / (1.0 + e2)
    w2 = e2 / (1.0 + e2)

    hot1 = lane == i1
    hot2 = lane == i2
    hot = jnp.where(hot1 | hot2, 1.0, 0.0)
    ri = lax.broadcasted_iota(I32, (tm, tm), 0)
    ci = lax.broadcasted_iota(I32, (tm, tm), 1)
    before = jnp.where(ci < ri, 1.0, 0.0).astype(BF16)
    seen = run_ref[0:1, :] + _dot(before, hot.astype(BF16))
    rank1 = jnp.sum(jnp.where(hot1, seen, 0.0), axis=-1, keepdims=True)
    rank2 = jnp.sum(jnp.where(hot2, seen, 0.0), axis=-1, keepdims=True)
    run_ref[...] = run_ref[...] + jnp.sum(hot, axis=0, keepdims=True)
    cnt_ref[...] = run_ref[...]

    cols = [i1.astype(F32), i2.astype(F32), w1, w2, rank1, rank2]
    info = jnp.zeros((tm, LANES), F32)
    for c, v in enumerate(cols):
        info = jnp.where(lane == c, v, info)
    info_ref[...] = info


def _router(x, g, wr, *, tm=256):
    t, d = x.shape
    return pl.pallas_call(
        _router_kernel,
        grid=(t // tm,),
        in_specs=[
            pl.BlockSpec((tm, d), lambda i: (i, 0)),
            pl.BlockSpec((1, d), lambda i: (0, 0)),
            pl.BlockSpec(wr.shape, lambda i: (0, 0)),
        ],
        out_specs=[
            pl.BlockSpec((tm, d), lambda i: (i, 0)),
            pl.BlockSpec((tm, LANES), lambda i: (i, 0)),
            pl.BlockSpec((8, LANES), lambda i: (0, 0)),
        ],
        out_shape=[
            jax.ShapeDtypeStruct((t, d), F32),
            jax.ShapeDtypeStruct((t, LANES), F32),
            jax.ShapeDtypeStruct((8, LANES), F32),
        ],
        scratch_shapes=[pltpu.VMEM((8, LANES), F32)],
        compiler_params=_params("arbitrary"),
        name="moe_router",
    )(x, g, wr)


def _row_copy(src_ref, src_row, dst_ref, dst_row, sem):
    return pltpu.make_async_copy(src_ref.at[pl.ds(src_row, 1)], dst_ref.at[pl.ds(dst_row, 1)], sem)


def _dispatch_kernel(pos_ref, h_ref, init_ref, xs_ref, sem):
    del init_ref
    tt = h_ref.shape[0]

    def issue(r, c):
        _row_copy(h_ref, r, xs_ref, pos_ref[0, r], sem).start()
        _row_copy(h_ref, r, xs_ref, pos_ref[1, r], sem).start()
        return c

    def drain(r, c):
        _row_copy(h_ref, 0, xs_ref, 0, sem).wait()
        _row_copy(h_ref, 0, xs_ref, 0, sem).wait()
        return c

    lax.fori_loop(0, tt, issue, 0)
    lax.fori_loop(0, tt, drain, 0)


def _dispatch(h, pos, rows, *, tt=256):
    t, d = h.shape
    init = jnp.zeros((rows, d), h.dtype)
    return pl.pallas_call(
        _dispatch_kernel,
        grid=(t // tt,),
        in_specs=[
            pl.BlockSpec((None, 2, tt), lambda i: (i, 0, 0), memory_space=pltpu.SMEM),
            pl.BlockSpec((tt, d), lambda i: (i, 0)),
            pl.BlockSpec(memory_space=pl.ANY),
        ],
        out_specs=pl.BlockSpec(memory_space=pl.ANY),
        out_shape=jax.ShapeDtypeStruct((rows, d), h.dtype),
        scratch_shapes=[pltpu.SemaphoreType.DMA(())],
        input_output_aliases={2: 0},
        compiler_params=_params("arbitrary"),
        name="moe_dispatch",
    )(pos, h, init)


def _expert_kernel(te_ref, tv_ref, xs_ref, wu_ref, wg_ref, w2_ref, o_ref, acc_ref):
    del te_ref
    i = pl.program_id(0)
    f = pl.program_id(1)
    last = pl.num_programs(1) - 1
    live = tv_ref[i] > 0

    @pl.when(live)
    def _():
        @pl.when(f == 0)
        def _():
            acc_ref[...] = jnp.zeros_like(acc_ref)

        x = xs_ref[...].astype(BF16)
        act = _silu(_dot(x, wg_ref[...].astype(BF16))) * _dot(x, wu_ref[...].astype(BF16))
        acc_ref[...] += _dot(act.astype(BF16), w2_ref[...].astype(BF16))

        @pl.when(f == last)
        def _():
            o_ref[...] = acc_ref[...]

    @pl.when(jnp.logical_not(live) & (f == last))
    def _():
        o_ref[...] = jnp.zeros_like(o_ref)


def _experts(xs, w13, w2, tile_expert, tile_live, *, tm, tf=512):
    rows, d = xs.shape
    ff = w2.shape[1]
    nf = ff // tf

    def f_of(i, f, te, tv):
        return jnp.where(tv[i] > 0, f, nf - 1)

    grid_spec = pltpu.PrefetchScalarGridSpec(
        num_scalar_prefetch=2,
        grid=(rows // tm, nf),
        in_specs=[
            pl.BlockSpec((tm, d), lambda i, f, te, tv: (i, 0)),
            pl.BlockSpec((None, d, tf), lambda i, f, te, tv: (te[i], 0, f_of(i, f, te, tv))),
            pl.BlockSpec((None, d, tf), lambda i, f, te, tv: (te[i], 0, nf + f_of(i, f, te, tv))),
            pl.BlockSpec((None, tf, d), lambda i, f, te, tv: (te[i], f_of(i, f, te, tv), 0)),
        ],
        out_specs=pl.BlockSpec((tm, d), lambda i, f, te, tv: (i, 0)),
        scratch_shapes=[pltpu.VMEM((tm, d), F32)],
    )
    return pl.pallas_call(
        _expert_kernel,
        grid_spec=grid_spec,
        out_shape=jax.ShapeDtypeStruct((rows, d), F32),
        compiler_params=_params("parallel", "arbitrary"),
        name="moe_experts",
    )(tile_expert, tile_live, xs, w13, w13, w2)


def _combine_kernel(pos_ref, x_ref, info_ref, ys_ref, o_ref, buf_ref, sem):
    tt = x_ref.shape[0]

    def issue(r, c):
        _row_copy(ys_ref, pos_ref[0, r], buf_ref.at[0], r, sem).start()
        _row_copy(ys_ref, pos_ref[1, r], buf_ref.at[1], r, sem).start()
        return c

    def drain(r, c):
        _row_copy(ys_ref, 0, buf_ref.at[0], 0, sem).wait()
        _row_copy(ys_ref, 0, buf_ref.at[1], 0, sem).wait()
        return c

    lax.fori_loop(0, tt, issue, 0)
    lax.fori_loop(0, tt, drain, 0)
    info = info_ref[...]
    o_ref[...] = x_ref[...] + info[:, 2:3] * buf_ref[0] + info[:, 3:4] * buf_ref[1]


def _combine(x, info, pos, ys, *, tt=256):
    t, d = x.shape
    return pl.pallas_call(
        _combine_kernel,
        grid=(t // tt,),
        in_specs=[
            pl.BlockSpec((None, 2, tt), lambda i: (i, 0, 0), memory_space=pltpu.SMEM),
            pl.BlockSpec((tt, d), lambda i: (i, 0)),
            pl.BlockSpec((tt, LANES), lambda i: (i, 0)),
            pl.BlockSpec(memory_space=pl.ANY),
        ],
        out_specs=pl.BlockSpec((tt, d), lambda i: (i, 0)),
        out_shape=jax.ShapeDtypeStruct((t, d), F32),
        scratch_shapes=[pltpu.VMEM((2, tt, d), F32), pltpu.SemaphoreType.DMA(())],
        compiler_params=_params("arbitrary"),
        name="moe_combine",
    )(pos, x, info, ys)


def _moe(x, g, router, w13, w2, *, tm=512, tt=256):
    t, d = x.shape
    n_exp = router.shape[1]
    wr = jnp.pad(router, ((0, 0), (0, LANES - n_exp)))
    h, info, counts = _router(x, g, wr, tm=tt)

    counts = counts[0, :n_exp].astype(I32)
    seg = (counts + tm - 1) // tm * tm
    seg_end = jnp.cumsum(seg)
    seg_start = seg_end - seg
    e1 = info[:, 0].astype(I32)
    e2 = info[:, 1].astype(I32)
    pos = jnp.stack([seg_start[e1] + info[:, 4].astype(I32), seg_start[e2] + info[:, 5].astype(I32)], axis=0)
    pos = pos.reshape(2, t // tt, tt).transpose(1, 0, 2)
    rows = 2 * t + n_exp * tm
    tile_row0 = jnp.arange(rows // tm, dtype=I32) * tm
    tile_live = (tile_row0 < seg_end[-1]).astype(I32)
    tile_expert = jnp.minimum(jnp.searchsorted(seg_end, tile_row0, side="right"), n_exp - 1).astype(I32)
    last_live = jnp.maximum(jnp.sum(tile_live) - 1, 0)
    tile_expert = jnp.where(tile_live > 0, tile_expert, tile_expert[last_live])

    xs = _dispatch(h, pos, rows, tt=tt)
    ys = _experts(xs, w13, w2, tile_expert, tile_live, tm=tm)
    return _combine(x, info, pos, ys, tt=tt)


def _row(v, n=None):
    v = v.reshape(1, -1).astype(F32)
    if n is not None and v.shape[1] < n:
        v = jnp.pad(v, ((0, 0), (0, n - v.shape[1])))
    return v


def _overlap_matrix(seq):
    n = seq // CMP_STRIDE
    cmp_start = np.arange(n) * CMP_STRIDE
    slc_start = np.arange(LANES) * SLC_LEN
    ov = (cmp_start[:, None] <= slc_start[None, :] + SLC_LEN - 1) & (cmp_start[:, None] + CMP_LEN - 1 >= slc_start[None, :])
    ov[n - 1] = False
    return jnp.asarray(ov, dtype=BF16)


def _even_layer(x, batch, norm_mix, w_in, conv_w, conv_b, wa, ba, wx, bx, lam, gate_b, q_norm, k_norm, cmp_pos,
                ck_w1, ck_w2, cv_w1, cv_w2, w_out):
    t, d = x.shape
    seq = t // batch
    rg = wa.shape[0] * wa.shape[1]
    gdk = NSA_GROUPS * NSA_DK
    n_main = 2 * rg + NSA_HEADS * NSA_DK
    n_planes = 6 * NSA_GROUPS
    n_kv = 6 * gdk
    per_group = 3 * NSA_REP
    gate_cols = w_in[:, n_main + n_kv:].reshape(d, NSA_GROUPS, per_group)
    gate_cols = jnp.pad(gate_cols, ((0, 0), (0, 0), (0, LANES - per_group))).reshape(d, NSA_GROUPS * LANES)
    gate_bias = jnp.pad(gate_b.reshape(NSA_GROUPS, per_group), ((0, 0), (0, LANES - per_group))).reshape(1, -1)
    w_all = jnp.concatenate([w_in[:, :n_main + n_kv], gate_cols], axis=1).astype(BF16)
    main, planes, gates = _norm_proj(x, _row(norm_mix), w_all, gate_bias, batch=batch, n_main=n_main,
                                     n_planes=n_planes, n_extra=NSA_GROUPS * LANES, extra_sigmoid=True)

    rg_out = _rglru(main, conv_w, _row(conv_b), wa.astype(BF16), _row(ba), wx.astype(BF16), _row(bx), _row(lam),
                    batch=batch)

    kn = jnp.pad(k_norm, ((0, 8 - k_norm.shape[0]), (0, 0)))
    kp, kwn = _nsa_kprep(planes, kn)
    pos_flat = jnp.broadcast_to(cmp_pos.reshape(1, -1), (8, CMP_LEN * NSA_DK)).astype(BF16)
    kcmp, vcmp = _nsa_compress(planes, pos_flat, ck_w1.astype(BF16), ck_w2.astype(BF16), cv_w1.astype(BF16),
                               cv_w2.astype(BF16), kn)
    qp, ocw = _nsa_cw(main, kcmp, vcmp, kwn, planes, gates, _row(q_norm), _overlap_matrix(seq))
    att = _nsa_slc(qp, kp, planes, gates, ocw)
    return _out_proj([rg_out, att], w_out.astype(BF16), x)


def _odd_layer(x, batch, norm_mix, w_in, conv_w, conv_b, dt_bias, a_log, d_skip, norm_g, w_out):
    t, d = x.shape
    d_inner = w_out.shape[0]
    conv_ch = conv_w.shape[1]
    n_main = d_inner + conv_ch
    heads = dt_bias.shape[0]
    w_all = jnp.concatenate([w_in[:, :n_main], jnp.pad(w_in[:, n_main:], ((0, 0), (0, LANES - heads)))], axis=1)
    main, dt = _norm_proj(x, _row(norm_mix), w_all.astype(BF16), jnp.zeros((1, LANES), F32), batch=batch,
                          n_main=n_main, n_planes=0, n_extra=LANES, extra_sigmoid=False)
    y = _ssd(main, dt, conv_w, _row(conv_b), _row(dt_bias, LANES), _row(a_log, LANES),
             _row(jnp.repeat(d_skip, SSM_HEADDIM)), _row(norm_g), batch=batch, d_inner=d_inner)
    return _out_proj([y], w_out.astype(BF16), x)


def kernel(x, mem, norm_mix, norm_cross, norm_mem, norm_ffn, ev_w_in, ev_rg_conv_w, ev_rg_conv_b, ev_rg_wa, ev_rg_ba, ev_rg_wx, ev_rg_bx, ev_rg_lambda, ev_nsa_gate_b, ev_q_norm, ev_k_norm, ev_cmp_pos, ev_cmp_k_w1, ev_cmp_k_w2, ev_cmp_v_w1, ev_cmp_v_w2, ev_w_out, od_w_in, od_conv_w, od_conv_b, od_dt_bias, od_a_log, od_d_skip, od_norm, od_w_out, x_wq, x_wkv, x_q_norm, x_k_norm, x_wo, ff_w13, ff_w2, moe_router, moe_w13, moe_w2):
    batch, seq, d = x.shape
    depth = norm_mix.shape[0]
    xf = x.reshape(batch * seq, d)
    for layer in range(depth):
        i = layer // 2
        if layer % 2 == 0:
            xf = _even_layer(xf, batch, norm_mix[layer], ev_w_in[i], ev_rg_conv_w[i], ev_rg_conv_b[i], ev_rg_wa[i],
                             ev_rg_ba[i], ev_rg_wx[i], ev_rg_bx[i], ev_rg_lambda[i], ev_nsa_gate_b[i], ev_q_norm[i],
                             ev_k_norm[i], ev_cmp_pos[i], ev_cmp_k_w1[i], ev_cmp_k_w2[i], ev_cmp_v_w1[i],
                             ev_cmp_v_w2[i], ev_w_out[i])
        else:
            xf = _odd_layer(xf, batch, norm_mix[layer], od_w_in[i], od_conv_w[i], od_conv_b[i], od_dt_bias[i],
                            od_a_log[i], od_d_skip[i], od_norm[i], od_w_out[i])
        k, v = _mem_kv(mem, _row(norm_mem[layer]), x_wkv[layer].astype(BF16), _row(x_k_norm[layer]))
        xf = _cross_attn(xf, _row(norm_cross[layer]), x_wq[layer].astype(BF16), _row(x_q_norm[layer]), k, v,
                         x_wo[layer].astype(BF16))
        if layer % 2 == 0:
            xf = _swiglu(xf, _row(norm_ffn[layer]), ff_w13[i].astype(BF16), ff_w2[i].astype(BF16))
        else:
            xf = _moe(xf, _row(norm_ffn[layer]), moe_router[i], moe_w13[i], moe_w2[i])
    return xf.reshape(batch, seq, d)
```

```python
import functools
import math

import jax
import jax.numpy as jnp
import numpy as np
from jax import lax
from jax.experimental import pallas as pl
from jax.experimental.pallas import tpu as pltpu

F32 = jnp.float32
BF16 = jnp.bfloat16
I32 = jnp.int32

EPS = 1e-6
CONV_W = 4
RG_BLOCKS = 8
RG_C = 8.0
NSA_HEADS = 8
NSA_GROUPS = 2
NSA_REP = NSA_HEADS // NSA_GROUPS
NSA_DK = 128
CMP_LEN = 32
CMP_STRIDE = 16
SLC_LEN = 64
SLC_SHIFT = 6
SLC_TOPN = 16
WINDOW = 512
FORCE_BONUS = 100.0
SSM_HEADDIM = 64
SSM_GROUPS = 4
SSM_STATE = 128
SSD_CHUNK = 128
X_HEADS = 4
X_HEADDIM = 128
N_EXPERTS = 8

LANES = 128
VMEM_LIMIT_BYTES = 56 * 1024 * 1024
NEG = -1e30
SEL_BIAS = float(2 ** 20)
LOG2E = math.log2(math.e)
FIXED_SHIFT_MAX = 56.0

NT_DIMS = (((1,), (1,)), ((), ()))


def _params(*sem):
    return pltpu.CompilerParams(dimension_semantics=sem, vmem_limit_bytes=VMEM_LIMIT_BYTES)


def _dot(a, b):
    return jnp.dot(a, b, preferred_element_type=F32)


def _dot_nt(a, b):
    return lax.dot_general(a, b, NT_DIMS, preferred_element_type=F32)


def _rms(x, g):
    return x * lax.rsqrt(jnp.mean(x * x, axis=-1, keepdims=True) + EPS) * g


def _sigmoid(x):
    return 1.0 / (1.0 + jnp.exp(-x))


def _silu(x):
    return x * _sigmoid(x)


def _gelu_tanh(x):
    c = math.sqrt(2.0 / math.pi)
    return 0.5 * x * (1.0 + jnp.tanh(c * (x + 0.044715 * (x * x * x))))


def _softplus(x):
    return jnp.maximum(x, 0.0) + jnp.log(1.0 + jnp.exp(-jnp.abs(x)))


def _masked_softmax2(s, mask):
    sm = jnp.where(mask, s, NEG)
    m = jnp.max(sm, axis=-1, keepdims=True)
    e = jnp.where(mask, jnp.exp2(sm - m), 0.0)
    return e / jnp.maximum(jnp.sum(e, axis=-1, keepdims=True), 1e-30)


def _shift_rows(x, tail, k, rows):
    rolled = pltpu.roll(x, k, 0)
    head = pltpu.roll(tail, k, 0)
    pad = jnp.zeros((x.shape[0] - tail.shape[0], x.shape[1]), x.dtype)
    return jnp.where(rows < k, jnp.concatenate([head, pad], axis=0), rolled)


def _causal_conv(x, tail, w_ref, b_ref):
    rows = lax.broadcasted_iota(I32, (x.shape[0], 1), 0)
    y = b_ref[...] + w_ref[CONV_W - 1:CONV_W, :] * x
    for k in range(1, CONV_W):
        y = y + w_ref[CONV_W - 1 - k:CONV_W - k, :] * _shift_rows(x, tail, k, rows)
    return y


def _norm_proj_kernel(x_ref, g_ref, w_ref, eb_ref, *out_refs, n_main, n_planes, extra_sigmoid):
    h = _rms(x_ref[...], g_ref[...]).astype(BF16)
    main_ref = out_refs[0]
    for c0 in range(0, n_main, 512):
        main_ref[:, c0:c0 + 512] = _dot(h, w_ref[:, c0:c0 + 512]).astype(main_ref.dtype)
    col = n_main
    oi = 1
    if n_planes:
        kv_ref = out_refs[oi]
        oi += 1
        for p0 in range(0, n_planes, 4):
            r = _dot(h, w_ref[:, col:col + 512])
            for p in range(4):
                kv_ref[p0 + p] = r[:, p * LANES:(p + 1) * LANES].astype(kv_ref.dtype)
            col += 512
    ex_ref = out_refs[oi]
    n_extra = ex_ref.shape[1]
    e = _dot(h, w_ref[:, col:col + n_extra]) + eb_ref[...]
    ex_ref[...] = _sigmoid(e) if extra_sigmoid else e


def _norm_proj(x, g, w, eb, *, batch, n_main, n_planes, n_extra, extra_sigmoid, tm=512):
    t, d = x.shape
    seq = t // batch
    nt = seq // tm
    out_shape = [jax.ShapeDtypeStruct((t, n_main), BF16)]
    out_specs = [pl.BlockSpec((tm, n_main), lambda i: (i, 0))]
    if n_planes:
        out_shape.append(jax.ShapeDtypeStruct((batch, n_planes, seq, LANES), BF16))
        out_specs.append(pl.BlockSpec((None, n_planes, tm, LANES), lambda i: (i // nt, 0, i % nt, 0)))
    out_shape.append(jax.ShapeDtypeStruct((t, n_extra), F32))
    out_specs.append(pl.BlockSpec((tm, n_extra), lambda i: (i, 0)))
    kern = functools.partial(_norm_proj_kernel, n_main=n_main, n_planes=n_planes, extra_sigmoid=extra_sigmoid)
    return pl.pallas_call(
        kern,
        grid=(t // tm,),
        in_specs=[
            pl.BlockSpec((tm, d), lambda i: (i, 0)),
            pl.BlockSpec((1, d), lambda i: (0, 0)),
            pl.BlockSpec(w.shape, lambda i: (0, 0)),
            pl.BlockSpec((1, n_extra), lambda i: (0, 0)),
        ],
        out_specs=out_specs,
        out_shape=out_shape,
        compiler_params=_params("parallel"),
        name="norm_proj",
    )(x, g, w, eb)


def _rglru_kernel(rx_ref, rg_ref, cw_ref, cb_ref, wa_ref, ba_ref, wx_ref, bx_ref, lam_ref, o_ref, tail_ref, h_ref):
    tc, c = rx_ref.shape

    @pl.when(pl.program_id(1) == 0)
    def _():
        tail_ref[...] = jnp.zeros_like(tail_ref)
        h_ref[...] = jnp.zeros_like(h_ref)

    x = rx_ref[...].astype(F32)
    xc = _causal_conv(x, tail_ref[...], cw_ref, cb_ref)
    tail_ref[...] = x[tc - 8:tc, :]

    bw = c // RG_BLOCKS
    ra, rx = [], []
    for blk in range(RG_BLOCKS):
        xb = xc[:, blk * bw:(blk + 1) * bw].astype(BF16)
        ra.append(_dot(xb, wa_ref[blk]))
        rx.append(_dot(xb, wx_ref[blk]))
    r = _sigmoid(jnp.concatenate(ra, axis=1) + ba_ref[...])
    ig = _sigmoid(jnp.concatenate(rx, axis=1) + bx_ref[...])
    log_a = (-RG_C) * r * _softplus(-lam_ref[...])
    a = jnp.exp(log_a)
    u = jnp.sqrt(1.0 - a * a) * (ig * xc)

    rows = lax.broadcasted_iota(I32, (tc, 1), 0)
    d = 1
    while d < tc:
        keep = rows >= d
        a_sh = jnp.where(keep, pltpu.roll(a, d, 0), 1.0)
        u_sh = jnp.where(keep, pltpu.roll(u, d, 0), 0.0)
        u = a * u_sh + u
        a = a * a_sh
        d *= 2
    h = u + a * h_ref[7:8, :]
    h_ref[...] = h[tc - 8:tc, :]
    o_ref[...] = (_gelu_tanh(rg_ref[...].astype(F32)) * h).astype(o_ref.dtype)


def _rglru(main, cw, cb, wa, ba, wx, bx, lam, *, batch, tc=256):
    t = main.shape[0]
    c = cw.shape[1]
    nt = t // batch // tc
    vec = pl.BlockSpec((1, c), lambda b, i: (0, 0))
    blk = pl.BlockSpec(wa.shape, lambda b, i: (0, 0, 0))
    return pl.pallas_call(
        _rglru_kernel,
        grid=(batch, nt),
        in_specs=[
            pl.BlockSpec((tc, c), lambda b, i: (b * nt + i, 0)),
            pl.BlockSpec((tc, c), lambda b, i: (b * nt + i, 1)),
            pl.BlockSpec((CONV_W, c), lambda b, i: (0, 0)),
            vec, blk, vec, blk, vec, vec,
        ],
        out_specs=pl.BlockSpec((tc, c), lambda b, i: (b * nt + i, 0)),
        out_shape=jax.ShapeDtypeStruct((t, c), BF16),
        scratch_shapes=[pltpu.VMEM((8, c), F32), pltpu.VMEM((8, c), F32)],
        compiler_params=_params("parallel", "arbitrary"),
        name="rglru",
    )(main, main, cw, cb, wa, ba, wx, bx, lam)


def _nsa_kprep_kernel(ks_ref, vs_ref, kw_ref, kn_ref, kp_ref, vp_ref, kwn_ref):
    tk = ks_ref.shape[0]
    ks = _rms(ks_ref[...].astype(F32), kn_ref[1:2, :]).astype(BF16)
    t0 = pl.program_id(2) * tk
    blk = jnp.right_shift(t0 + lax.broadcasted_iota(I32, (tk, LANES), 0), SLC_SHIFT)
    onehot = jnp.where(blk == lax.broadcasted_iota(I32, (tk, LANES), 1), 1.0, 0.0).astype(BF16)
    kp_ref[...] = jnp.concatenate([ks, onehot], axis=1)
    vp_ref[...] = jnp.concatenate([vs_ref[...], jnp.ones((tk, LANES), BF16)], axis=1)
    kwn_ref[...] = _rms(kw_ref[...].astype(F32), kn_ref[2:3, :]).astype(BF16)


def _nsa_kprep(planes, k_norm, *, tk=512):
    b, _, seq, _ = planes.shape
    g = NSA_GROUPS
    plane = lambda p0: pl.BlockSpec((None, None, tk, LANES), lambda bi, gi, i: (bi, p0 + gi, i, 0))
    wide = pl.BlockSpec((None, None, tk, 2 * LANES), lambda bi, gi, i: (bi, gi, i, 0))
    return pl.pallas_call(
        _nsa_kprep_kernel,
        grid=(b, g, seq // tk),
        in_specs=[plane(4), plane(6), plane(8), pl.BlockSpec((8, LANES), lambda bi, gi, i: (0, 0))],
        out_specs=[wide, wide, pl.BlockSpec((None, None, tk, LANES), lambda bi, gi, i: (bi, gi, i, 0))],
        out_shape=[
            jax.ShapeDtypeStruct((b, g, seq, 2 * LANES), BF16),
            jax.ShapeDtypeStruct((b, g, seq, 2 * LANES), BF16),
            jax.ShapeDtypeStruct((b, g, seq, LANES), BF16),
        ],
        compiler_params=_params("parallel", "parallel", "parallel"),
        name="nsa_kprep",
    )(planes, planes, planes, k_norm)


def _nsa_compress_kernel(xk_ref, xv_ref, pos_ref, kw1_ref, kw2_ref, vw1_ref, vw2_ref, kn_ref, kc_ref, vc_ref):
    n, half = xk_ref.shape
    last = lax.broadcasted_iota(I32, (n, 1), 0) == n - 1
    pos = pos_ref[...]

    def compress(x_ref, w1_ref, w2_ref):
        x = x_ref[...]
        y0 = _dot(x, w1_ref[0:half, :])
        y1 = _dot(x, w1_ref[half:2 * half, :])
        y1_next = jnp.where(last, 0.0, pltpu.roll(y1, n - 1, 0))
        const = _dot(pos, w1_ref[...])[0:1, :]
        hid = _gelu_tanh(y0 + y1_next + const)
        return _dot(hid.astype(BF16), w2_ref[...])

    kc_ref[...] = _rms(compress(xk_ref, kw1_ref, kw2_ref), kn_ref[0:1, :]).astype(BF16)
    vc_ref[...] = compress(xv_ref, vw1_ref, vw2_ref).astype(BF16)


def _nsa_compress(planes, pos_flat, kw1, kw2, vw1, vw2, k_norm):
    b, n_planes, seq, _ = planes.shape
    g = NSA_GROUPS
    n = seq // CMP_STRIDE
    half = CMP_STRIDE * LANES
    grouped = planes.reshape(b, n_planes, n, half)
    full = lambda a: pl.BlockSpec(a.shape, lambda bi, gi: (0,) * a.ndim)
    out_spec = pl.BlockSpec((None, None, n, LANES), lambda bi, gi: (bi, gi, 0, 0))
    out_shape = jax.ShapeDtypeStruct((b, g, n, LANES), BF16)
    return pl.pallas_call(
        _nsa_compress_kernel,
        grid=(b, g),
        in_specs=[
            pl.BlockSpec((None, None, n, half), lambda bi, gi: (bi, gi, 0, 0)),
            pl.BlockSpec((None, None, n, half), lambda bi, gi: (bi, 2 + gi, 0, 0)),
            full(pos_flat), full(kw1), full(kw2), full(vw1), full(vw2), full(k_norm),
        ],
        out_specs=[out_spec, out_spec],
        out_shape=[out_shape, out_shape],
        compiler_params=_params("parallel", "parallel"),
        name="nsa_compress",
    )(grouped, grouped, pos_flat, kw1, kw2, vw1, vw2, k_norm)


def _nsa_cw_kernel(shift_ref, q_ref, kc_ref, vc_ref, kw_ref, vw_ref, gt_ref, qn_ref, ov_ref, qp_ref, o_ref, *, qn,
                   n_sel):
    t0 = pl.program_id(2) * qn
    rep = NSA_REP
    rows = rep * qn
    scale = NSA_DK ** -0.5 * LOG2E
    qf = q_ref[...].astype(F32)
    heads = []
    for r in range(rep):
        qh = _rms(qf[:, r * LANES:(r + 1) * LANES], qn_ref[...]) * scale
        heads.append(qh.astype(BF16))
    qs = jnp.concatenate(heads, axis=0)
    trow = t0 + (lax.broadcasted_iota(I32, (rows, 1), 0) & (qn - 1))

    n_cmp = kc_ref.shape[0]
    s = _dot_nt(qs, kc_ref[...])
    ncol = lax.broadcasted_iota(I32, (1, n_cmp), 1)
    p = _masked_softmax2(s, ncol * CMP_STRIDE + (CMP_LEN - 1) <= trow)
    o_cmp = _dot(p.astype(BF16), vc_ref[...])

    psum = p[0:qn]
    for r in range(1, rep):
        psum = psum + p[r * qn:(r + 1) * qn]
    p_hi = psum.astype(BF16)
    p_lo = (psum - p_hi.astype(F32)).astype(BF16)
    imp = _dot(p_hi, ov_ref[...]) + _dot(p_lo, ov_ref[...])

    imp_t = imp.T
    jj = lax.broadcasted_iota(I32, imp_t.shape, 0).astype(F32)
    cur = jnp.right_shift(t0 + lax.broadcasted_iota(I32, imp_t.shape, 1), SLC_SHIFT).astype(F32)
    forced = (jj == 0.0) | (jj == cur) | (jj == cur - 1.0)
    work = jnp.where(jj <= cur, imp_t + jnp.where(forced, FORCE_BONUS, 0.0), -1.0)
    bias_t = jnp.full(imp_t.shape, -SEL_BIAS, F32)
    shift = -shift_ref[0]
    for _ in range(n_sel):
        m = jnp.max(work, axis=0, keepdims=True)
        idx = jnp.min(jnp.where(work == m, jj, float(LANES)), axis=0, keepdims=True)
        pick = jj == idx
        bias_t = jnp.where(pick, shift, bias_t)
        work = jnp.where(pick, -2.0, work)
    bias = bias_t.T.astype(BF16)
    for r in range(rep):
        qp_ref[r] = jnp.concatenate([heads[r], bias], axis=1)

    span = WINDOW + qn
    start = pl.multiple_of(jnp.maximum(t0 - WINDOW, 0), qn)
    s = _dot_nt(qs, kw_ref[pl.ds(start, span), :])
    diff = trow - (start + lax.broadcasted_iota(I32, (1, span), 1))
    p = _masked_softmax2(s, (diff >= 0) & (diff < WINDOW))
    o_win = _dot(p.astype(BF16), vw_ref[pl.ds(start, span), :])

    gt = gt_ref[...]
    for r in range(rep):
        sl = slice(r * qn, (r + 1) * qn)
        o = gt[:, 3 * r:3 * r + 1] * o_cmp[sl] + gt[:, 3 * r + 2:3 * r + 3] * o_win[sl]
        o_ref[:, r * LANES:(r + 1) * LANES] = o.astype(o_ref.dtype)


def _nsa_cw(shift, main, kcmp, vcmp, kwn, planes, gates, q_norm, overlap, *, qn=256):
    b, g, seq, _ = kwn.shape
    t = main.shape[0]
    nq = seq // qn
    rep = NSA_REP
    gw = rep * LANES
    q_blk0 = (main.shape[1] - NSA_HEADS * NSA_DK) // gw
    n_cmp = kcmp.shape[2]
    kern = functools.partial(_nsa_cw_kernel, qn=qn, n_sel=min(SLC_TOPN, seq // SLC_LEN))
    return pl.pallas_call(
        kern,
        grid=(b, g, nq),
        in_specs=[
            pl.BlockSpec(memory_space=pltpu.SMEM),
            pl.BlockSpec((qn, gw), lambda bi, gi, i: (bi * nq + i, q_blk0 + gi)),
            pl.BlockSpec((None, None, n_cmp, LANES), lambda bi, gi, i: (bi, gi, 0, 0)),
            pl.BlockSpec((None, None, n_cmp, LANES), lambda bi, gi, i: (bi, gi, 0, 0)),
            pl.BlockSpec((None, None, seq, LANES), lambda bi, gi, i: (bi, gi, 0, 0)),
            pl.BlockSpec((None, None, seq, LANES), lambda bi, gi, i: (bi, 10 + gi, 0, 0)),
            pl.BlockSpec((qn, LANES), lambda bi, gi, i: (bi * nq + i, gi)),
            pl.BlockSpec((1, LANES), lambda bi, gi, i: (0, 0)),
            pl.BlockSpec(overlap.shape, lambda bi, gi, i: (0, 0)),
        ],
        out_specs=[
            pl.BlockSpec((None, None, rep, qn, 2 * LANES), lambda bi, gi, i: (bi, gi, 0, i, 0)),
            pl.BlockSpec((qn, gw), lambda bi, gi, i: (bi * nq + i, gi)),
        ],
        out_shape=[
            jax.ShapeDtypeStruct((b, g, rep, seq, 2 * LANES), BF16),
            jax.ShapeDtypeStruct((t, NSA_HEADS * NSA_DK), BF16),
        ],
        compiler_params=_params("parallel", "parallel", "parallel"),
        name="nsa_cmp_win",
    )(shift, main, kcmp, vcmp, kwn, planes, gates, q_norm, overlap)


def _nsa_slc_kernel(shift_ref, qp_ref, kp_ref, vp_ref, gt_ref, ocw_ref, o_ref, m_ref, acc_ref, *, qn, tk):
    t0 = pl.program_id(2) * qn
    rep = NSA_REP
    rows = rep * qn
    last = (t0 + qn - 1) // tk
    acc_ref[...] = jnp.zeros_like(acc_ref)

    def scores(j, causal):
        k0 = pl.multiple_of(j * tk, tk)
        qp = qp_ref[...].reshape(rows, qp_ref.shape[2])
        s = _dot_nt(qp, kp_ref[pl.ds(k0, tk), :])
        if causal:
            trow = t0 + (lax.broadcasted_iota(I32, (rows, 1), 0) & (qn - 1))
            s = jnp.where(k0 + lax.broadcasted_iota(I32, (1, tk), 1) <= trow, s, -SEL_BIAS)
        return s, vp_ref[pl.ds(k0, tk), :]

    def fixed_shift_step(j, causal):
        s, v = scores(j, causal)
        acc_ref[...] += _dot(jnp.exp2(s).astype(BF16), v)

    def running_max_step(j, causal):
        s, v = scores(j, causal)
        m_old = m_ref[...]
        m_new = jnp.maximum(m_old, jnp.max(s, axis=-1, keepdims=True))
        acc_ref[...] = jnp.exp2(m_old - m_new) * acc_ref[...] + _dot(jnp.exp2(s - m_new).astype(BF16), v)
        m_ref[...] = m_new

    fixed = shift_ref[0] <= FIXED_SHIFT_MAX

    @pl.when(fixed)
    def _():
        lax.fori_loop(0, last, lambda j, c: (fixed_shift_step(j, False), c)[1], 0)
        fixed_shift_step(last, True)

    @pl.when(jnp.logical_not(fixed))
    def _():
        m_ref[...] = jnp.full(m_ref.shape, NEG, F32)
        lax.fori_loop(0, last, lambda j, c: (running_max_step(j, False), c)[1], 0)
        running_max_step(last, True)

    o_slc = acc_ref[:, 0:LANES] / jnp.maximum(acc_ref[:, LANES:2 * LANES], 1e-30)
    gt = gt_ref[...]
    for r in range(rep):
        o = ocw_ref[:, r * LANES:(r + 1) * LANES].astype(F32) + gt[:, 3 * r + 1:3 * r + 2] * o_slc[r * qn:(r + 1) * qn]
        o_ref[:, r * LANES:(r + 1) * LANES] = o.astype(o_ref.dtype)


def _nsa_slc(shift, qp, kp, vp, gates, ocw, *, qn=256, tk=512):
    b, g, rep, seq, dqk = qp.shape
    t = ocw.shape[0]
    nq = seq // qn
    gw = rep * LANES
    tk = min(tk, seq)
    kern = functools.partial(_nsa_slc_kernel, qn=qn, tk=tk)
    return pl.pallas_call(
        kern,
        grid=(b, g, nq),
        in_specs=[
            pl.BlockSpec(memory_space=pltpu.SMEM),
            pl.BlockSpec((None, None, rep, qn, dqk), lambda bi, gi, i: (bi, gi, 0, i, 0)),
            pl.BlockSpec((None, None, seq, dqk), lambda bi, gi, i: (bi, gi, 0, 0)),
            pl.BlockSpec((None, None, seq, 2 * LANES), lambda bi, gi, i: (bi, gi, 0, 0)),
            pl.BlockSpec((qn, LANES), lambda bi, gi, i: (bi * nq + i, gi)),
            pl.BlockSpec((qn, gw), lambda bi, gi, i: (bi * nq + i, gi)),
        ],
        out_specs=pl.BlockSpec((qn, gw), lambda bi, gi, i: (bi * nq + i, gi)),
        out_shape=jax.ShapeDtypeStruct((t, NSA_HEADS * NSA_DK), BF16),
        scratch_shapes=[pltpu.VMEM((rep * qn, 1), F32), pltpu.VMEM((rep * qn, 2 * LANES), F32)],
        compiler_params=_params("parallel", "parallel", "parallel"),
        name="nsa_selected",
    )(shift, qp, kp, vp, gates, ocw)


def _out_proj_kernel(*refs, n_in):
    a_refs = refs[:n_in]
    w_ref, x_ref, o_ref = refs[n_in:]
    acc = x_ref[...]
    k0 = 0
    for a_ref in a_refs:
        k = a_ref.shape[1]
        acc = acc + _dot(a_ref[...], w_ref[k0:k0 + k, :])
        k0 += k
    o_ref[...] = acc


def _out_proj(acts, w, x, *, tm=512):
    t, d = x.shape
    return pl.pallas_call(
        functools.partial(_out_proj_kernel, n_in=len(acts)),
        grid=(t // tm,),
        in_specs=[pl.BlockSpec((tm, a.shape[1]), lambda i: (i, 0)) for a in acts]
        + [pl.BlockSpec(w.shape, lambda i: (0, 0)), pl.BlockSpec((tm, d), lambda i: (i, 0))],
        out_specs=pl.BlockSpec((tm, d), lambda i: (i, 0)),
        out_shape=jax.ShapeDtypeStruct((t, d), F32),
        compiler_params=_params("parallel"),
        name="out_proj",
    )(*acts, w, x)


def _mem_kv_kernel(mem_ref, g_ref, wkv_ref, kn_ref, k_ref, v_ref):
    memn = _rms(mem_ref[...], g_ref[...]).astype(BF16)
    kv = _dot(memn, wkv_ref[...])
    inner = k_ref.shape[1]
    for h in range(inner // X_HEADDIM):
        sl = slice(h * X_HEADDIM, (h + 1) * X_HEADDIM)
        k_ref[:, sl] = _rms(kv[:, sl], kn_ref[...]).astype(BF16)
    v_ref[...] = kv[:, inner:].astype(BF16)


def _mem_kv(mem, g, wkv, kn):
    b, m, d = mem.shape
    inner = wkv.shape[1] // 2
    out_spec = pl.BlockSpec((None, m, inner), lambda bi: (bi, 0, 0))
    out_shape = jax.ShapeDtypeStruct((b, m, inner), BF16)
    return pl.pallas_call(
        _mem_kv_kernel,
        grid=(b,),
        in_specs=[
            pl.BlockSpec((None, m, d), lambda bi: (bi, 0, 0)),
            pl.BlockSpec((1, d), lambda bi: (0, 0)),
            pl.BlockSpec(wkv.shape, lambda bi: (0, 0)),
            pl.BlockSpec((1, X_HEADDIM), lambda bi: (0, 0)),
        ],
        out_specs=[out_spec, out_spec],
        out_shape=[out_shape, out_shape],
        compiler_params=_params("parallel"),
        name="mem_kv",
    )(mem, g, wkv, kn)


def _cross_attn_kernel(x_ref, g_ref, wq_ref, qn_ref, k_ref, v_ref, wo_ref, o_ref):
    x = x_ref[...]
    q = _dot(_rms(x, g_ref[...]).astype(BF16), wq_ref[...])
    scale = X_HEADDIM ** -0.5
    outs = []
    for h in range(q.shape[1] // X_HEADDIM):
        sl = slice(h * X_HEADDIM, (h + 1) * X_HEADDIM)
        qh = (_rms(q[:, sl], qn_ref[...]) * scale).astype(BF16)
        s = _dot_nt(qh, k_ref[:, sl])
        e = jnp.exp(s - jnp.max(s, axis=-1, keepdims=True))
        p = e / jnp.sum(e, axis=-1, keepdims=True)
        outs.append(_dot(p.astype(BF16), v_ref[:, sl]))
    o = jnp.concatenate(outs, axis=1).astype(BF16)
    o_ref[...] = x + _dot(o, wo_ref[...])


def _cross_attn(x, g, wq, qn, k, v, wo, *, tm=512):
    t, d = x.shape
    b, m, inner = k.shape
    nt = t // b // tm
    return pl.pallas_call(
        _cross_attn_kernel,
        grid=(b, nt),
        in_specs=[
            pl.BlockSpec((tm, d), lambda bi, i: (bi * nt + i, 0)),
            pl.BlockSpec((1, d), lambda bi, i: (0, 0)),
            pl.BlockSpec(wq.shape, lambda bi, i: (0, 0)),
            pl.BlockSpec((1, X_HEADDIM), lambda bi, i: (0, 0)),
            pl.BlockSpec((None, m, inner), lambda bi, i: (bi, 0, 0)),
            pl.BlockSpec((None, m, inner), lambda bi, i: (bi, 0, 0)),
            pl.BlockSpec(wo.shape, lambda bi, i: (0, 0)),
        ],
        out_specs=pl.BlockSpec((tm, d), lambda bi, i: (bi * nt + i, 0)),
        out_shape=jax.ShapeDtypeStruct((t, d), F32),
        compiler_params=_params("parallel", "parallel"),
        name="cross_attn",
    )(x, g, wq, qn, k, v, wo)


def _swiglu_kernel(x_ref, g_ref, wu_ref, wg_ref, w2_ref, o_ref, h_ref, acc_ref):
    f = pl.program_id(1)

    @pl.when(f == 0)
    def _():
        h_ref[...] = _rms(x_ref[...], g_ref[...]).astype(BF16)
        acc_ref[...] = x_ref[...]

    h = h_ref[...]
    act = _silu(_dot(h, wg_ref[...])) * _dot(h, wu_ref[...])
    acc_ref[...] += _dot(act.astype(BF16), w2_ref[...])

    @pl.when(f == pl.num_programs(1) - 1)
    def _():
        o_ref[...] = acc_ref[...]


def _swiglu(x, g, w13, w2, *, tm=1024, tf=256):
    t, d = x.shape
    ff = w2.shape[0]
    nf = ff // tf
    return pl.pallas_call(
        _swiglu_kernel,
        grid=(t // tm, nf),
        in_specs=[
            pl.BlockSpec((tm, d), lambda i, f: (i, 0)),
            pl.BlockSpec((1, d), lambda i, f: (0, 0)),
            pl.BlockSpec((d, tf), lambda i, f: (0, f)),
            pl.BlockSpec((d, tf), lambda i, f: (0, nf + f)),
            pl.BlockSpec((tf, d), lambda i, f: (f, 0)),
        ],
        out_specs=pl.BlockSpec((tm, d), lambda i, f: (i, 0)),
        out_shape=jax.ShapeDtypeStruct((t, d), F32),
        scratch_shapes=[pltpu.VMEM((tm, d), BF16), pltpu.VMEM((tm, d), F32)],
        compiler_params=_params("parallel", "arbitrary"),
        name="swiglu",
    )(x, g, w13, w13, w2)


def _split3(x):
    a = x.astype(BF16)
    r = x - a.astype(F32)
    b = r.astype(BF16)
    c = (r - b.astype(F32)).astype(BF16)
    return a, b, c


def _ssd_kernel(z0_ref, z1_ref, x0_ref, x1_ref, bc_ref, dt_ref, cw_ref, cb_ref, dtb_ref, alog_ref, dskip_ref,
                ng_ref, o_ref, tail_ref, state_ref):
    q = x0_ref.shape[0]
    d_inner = o_ref.shape[1]
    gn = SSM_GROUPS * SSM_STATE
    hpg = d_inner // SSM_HEADDIM // SSM_GROUPS

    @pl.when(pl.program_id(1) == 0)
    def _():
        tail_ref[...] = jnp.zeros_like(tail_ref)
        state_ref[...] = jnp.zeros_like(state_ref)

    raw = jnp.concatenate([x0_ref[...], x1_ref[...], bc_ref[...]], axis=1).astype(F32)
    xbc = _silu(_causal_conv(raw, tail_ref[...], cw_ref, cb_ref))
    tail_ref[...] = raw[q - 8:q, :]
    xs = xbc[:, :d_inner]
    bm = xbc[:, d_inner:d_inner + gn]
    cm = xbc[:, d_inner + gn:]

    dt = _softplus(dt_ref[...] + dtb_ref[...])
    a = dt * (-jnp.exp(alog_ref[...]))
    ri = lax.broadcasted_iota(I32, (q, q), 0)
    ci = lax.broadcasted_iota(I32, (q, q), 1)
    causal = ci <= ri
    tri = jnp.where(causal, 1.0, 0.0).astype(BF16)
    a_cs = sum(_dot(tri, part) for part in _split3(a))
    a_cs_t = a_cs.T
    dt_t = dt.T
    lane = lax.broadcasted_iota(I32, (1, LANES), 1)
    lo = lane < SSM_HEADDIM

    y_parts = []
    for g in range(SSM_GROUPS):
        cg = cm[:, g * SSM_STATE:(g + 1) * SSM_STATE].astype(BF16)
        bg = bm[:, g * SSM_STATE:(g + 1) * SSM_STATE]
        gmat = _dot_nt(cg, bg.astype(BF16))
        bg_t = bg.T
        gw = hpg * SSM_HEADDIM
        prev = state_ref[:, g * gw:(g + 1) * gw]
        y_off = _dot(cg, prev.astype(BF16))
        for pr in range(hpg // 2):
            c0 = g * gw + pr * LANES
            x_pair = xs[:, c0:c0 + LANES]
            y_pair = dskip_ref[:, c0:c0 + LANES] * x_pair
            st_pair = jnp.zeros((SSM_STATE, LANES), F32)
            decay_pair = jnp.zeros((1, LANES), F32)
            for half in range(2):
                h = g * hpg + pr * 2 + half
                sel = lo if half == 0 else jnp.logical_not(lo)
                xh = jnp.where(sel, x_pair, 0.0).astype(BF16)
                row_cs = a_cs_t[h:h + 1, :]
                col_cs = a_cs[:, h:h + 1]
                row_dt = dt_t[h:h + 1, :]
                a_last = a_cs_t[h:h + 1, q - 1:q]
                dec = jnp.exp(jnp.where(causal, col_cs - row_cs, NEG))
                y_pair = y_pair + _dot((gmat * dec * row_dt).astype(BF16), xh)
                w_row = jnp.exp(a_last - row_cs) * row_dt
                st_pair = st_pair + _dot((bg_t * w_row).astype(BF16), xh)
                y_pair = y_pair + jnp.where(sel, jnp.exp(col_cs) * y_off[:, pr * LANES:(pr + 1) * LANES], 0.0)
                decay_pair = jnp.where(sel, jnp.exp(a_last), decay_pair)
            state_ref[:, c0:c0 + LANES] = decay_pair * state_ref[:, c0:c0 + LANES] + st_pair
            y_parts.append(y_pair)
    y = jnp.concatenate(y_parts, axis=1)

    z = jnp.concatenate([z0_ref[...], z1_ref[...]], axis=1).astype(F32)
    y = y * _silu(z)
    gsz = d_inner // SSM_GROUPS
    for g in range(SSM_GROUPS):
        sl = slice(g * gsz, (g + 1) * gsz)
        o_ref[:, sl] = _rms(y[:, sl], ng_ref[:, sl]).astype(o_ref.dtype)


def _ssd(main, dt, cw, cb, dtb, alog, dskip, ng, *, batch, d_inner):
    t = main.shape[0]
    q = SSD_CHUNK
    nc = t // batch // q
    conv_ch = cw.shape[1]
    half = d_inner // 2
    col = lambda j: pl.BlockSpec((q, half), lambda b, i: (b * nc + i, j))
    vec = lambda n: pl.BlockSpec((1, n), lambda b, i: (0, 0))
    return pl.pallas_call(
        _ssd_kernel,
        grid=(batch, nc),
        in_specs=[
            col(0), col(1), col(2), col(3), col(4),
            pl.BlockSpec((q, LANES), lambda b, i: (b * nc + i, 0)),
            pl.BlockSpec((CONV_W, conv_ch), lambda b, i: (0, 0)),
            vec(conv_ch), vec(LANES), vec(LANES), vec(d_inner), vec(d_inner),
        ],
        out_specs=pl.BlockSpec((q, d_inner), lambda b, i: (b * nc + i, 0)),
        out_shape=jax.ShapeDtypeStruct((t, d_inner), BF16),
        scratch_shapes=[pltpu.VMEM((8, conv_ch), F32), pltpu.VMEM((SSM_STATE, d_inner), F32)],
        compiler_params=_params("parallel", "arbitrary"),
        name="ssd",
    )(main, main, main, main, main, dt, cw, cb, dtb, alog, dskip, ng)


def _router_kernel(x_ref, g_ref, wr_ref, h_ref, info_ref, cnt_ref, run_ref):
    tm = x_ref.shape[0]

    @pl.when(pl.program_id(0) == 0)
    def _():
        run_ref[...] = jnp.zeros_like(run_ref)

    h = _rms(x_ref[...], g_ref[...])
    h_ref[...] = h
    logits = jnp.dot(h, wr_ref[...], preferred_element_type=F32, precision=lax.Precision.HIGHEST)
    lane = lax.broadcasted_iota(I32, (tm, LANES), 1)
    lg = jnp.where(lane < N_EXPERTS, logits, NEG)
    m1 = jnp.max(lg, axis=-1, keepdims=True)
    i1 = jnp.min(jnp.where(lg == m1, lane, LANES), axis=-1, keepdims=True)
    lg2 = jnp.where(lane == i1, NEG, lg)
    m2 = jnp.max(lg2, axis=-1, keepdims=True)
    i2 = jnp.min(jnp.where(lg2 == m2, lane, LANES), axis=-1, keepdims=True)
    e2 = jnp.exp(m2 - m1)
    w1 = 1.0 / (1.0 + e2)
    w2 = e2 / (1.0 + e2)

    hot1 = lane == i1
    hot2 = lane == i2
    hot = jnp.where(hot1 | hot2, 1.0, 0.0)
    ri = lax.broadcasted_iota(I32, (tm, tm), 0)
    ci = lax.broadcasted_iota(I32, (tm, tm), 1)
    before = jnp.where(ci < ri, 1.0, 0.0).astype(BF16)
    seen = run_ref[0:1, :] + _dot(before, hot.astype(BF16))
    rank1 = jnp.sum(jnp.where(hot1, seen, 0.0), axis=-1, keepdims=True)
    rank2 = jnp.sum(jnp.where(hot2, seen, 0.0), axis=-1, keepdims=True)
    run_ref[...] = run_ref[...] + jnp.sum(hot, axis=0, keepdims=True)
    cnt_ref[...] = run_ref[...]

    cols = [i1.astype(F32), i2.astype(F32), w1, w2, rank1, rank2]
    info = jnp.zeros((tm, LANES), F32)
    for c, v in enumerate(cols):
        info = jnp.where(lane == c, v, info)
    info_ref[...] = info


def _router(x, g, wr, *, tm=256):
    t, d = x.shape
    return pl.pallas_call(
        _router_kernel,
        grid=(t // tm,),
        in_specs=[
            pl.BlockSpec((tm, d), lambda i: (i, 0)),
            pl.BlockSpec((1, d), lambda i: (0, 0)),
            pl.BlockSpec(wr.shape, lambda i: (0, 0)),
        ],
        out_specs=[
            pl.BlockSpec((tm, d), lambda i: (i, 0)),
            pl.BlockSpec((tm, LANES), lambda i: (i, 0)),
            pl.BlockSpec((8, LANES), lambda i: (0, 0)),
        ],
        out_shape=[
            jax.ShapeDtypeStruct((t, d), F32),
            jax.ShapeDtypeStruct((t, LANES), F32),
            jax.ShapeDtypeStruct((8, LANES), F32),
        ],
        scratch_shapes=[pltpu.VMEM((8, LANES), F32)],
        compiler_params=_params("arbitrary"),
        name="moe_router",
    )(x, g, wr)


def _row_copy(src_ref, src_row, dst_ref, dst_row, sem):
    return pltpu.make_async_copy(src_ref.at[pl.ds(src_row, 1)], dst_ref.at[pl.ds(dst_row, 1)], sem)


def _dispatch_kernel(pos_ref, h_ref, init_ref, xs_ref, sem):
    del init_ref
    tt = h_ref.shape[0]

    def issue(r, c):
        _row_copy(h_ref, r, xs_ref, pos_ref[0, r], sem).start()
        _row_copy(h_ref, r, xs_ref, pos_ref[1, r], sem).start()
        return c

    def drain(r, c):
        _row_copy(h_ref, 0, xs_ref, 0, sem).wait()
        _row_copy(h_ref, 0, xs_ref, 0, sem).wait()
        return c

    lax.fori_loop(0, tt, issue, 0)
    lax.fori_loop(0, tt, drain, 0)


def _dispatch(h, pos, rows, *, tt=256):
    t, d = h.shape
    init = jnp.zeros((rows, d), h.dtype)
    return pl.pallas_call(
        _dispatch_kernel,
        grid=(t // tt,),
        in_specs=[
            pl.BlockSpec((None, 2, tt), lambda i: (i, 0, 0), memory_space=pltpu.SMEM),
            pl.BlockSpec((tt, d), lambda i: (i, 0)),
            pl.BlockSpec(memory_space=pl.ANY),
        ],
        out_specs=pl.BlockSpec(memory_space=pl.ANY),
        out_shape=jax.ShapeDtypeStruct((rows, d), h.dtype),
        scratch_shapes=[pltpu.SemaphoreType.DMA(())],
        input_output_aliases={2: 0},
        compiler_params=_params("arbitrary"),
        name="moe_dispatch",
    )(pos, h, init)


def _expert_kernel(te_ref, tv_ref, xs_ref, wu_ref, wg_ref, w2_ref, o_ref, acc_ref):
    del te_ref
    i = pl.program_id(0)
    f = pl.program_id(1)
    last = pl.num_programs(1) - 1
    live = tv_ref[i] > 0

    @pl.when(live)
    def _():
        @pl.when(f == 0)
        def _():
            acc_ref[...] = jnp.zeros_like(acc_ref)

        x = xs_ref[...].astype(BF16)
        act = _silu(_dot(x, wg_ref[...].astype(BF16))) * _dot(x, wu_ref[...].astype(BF16))
        acc_ref[...] += _dot(act.astype(BF16), w2_ref[...].astype(BF16))

        @pl.when(f == last)
        def _():
            o_ref[...] = acc_ref[...]

    @pl.when(jnp.logical_not(live) & (f == last))
    def _():
        o_ref[...] = jnp.zeros_like(o_ref)


def _experts(xs, w13, w2, tile_expert, tile_live, *, tm, tf=512):
    rows, d = xs.shape
    ff = w2.shape[1]
    nf = ff // tf

    def f_of(i, f, te, tv):
        return jnp.where(tv[i] > 0, f, nf - 1)

    grid_spec = pltpu.PrefetchScalarGridSpec(
        num_scalar_prefetch=2,
        grid=(rows // tm, nf),
        in_specs=[
            pl.BlockSpec((tm, d), lambda i, f, te, tv: (i, 0)),
            pl.BlockSpec((None, d, tf), lambda i, f, te, tv: (te[i], 0, f_of(i, f, te, tv))),
            pl.BlockSpec((None, d, tf), lambda i, f, te, tv: (te[i], 0, nf + f_of(i, f, te, tv))),
            pl.BlockSpec((None, tf, d), lambda i, f, te, tv: (te[i], f_of(i, f, te, tv), 0)),
        ],
        out_specs=pl.BlockSpec((tm, d), lambda i, f, te, tv: (i, 0)),
        scratch_shapes=[pltpu.VMEM((tm, d), F32)],
    )
    return pl.pallas_call(
        _expert_kernel,
        grid_spec=grid_spec,
        out_shape=jax.ShapeDtypeStruct((rows, d), F32),
        compiler_params=_params("parallel", "arbitrary"),
        name="moe_experts",
    )(tile_expert, tile_live, xs, w13, w13, w2)


def _combine_kernel(pos_ref, x_ref, info_ref, ys_ref, o_ref, buf_ref, sem):
    tt = x_ref.shape[0]

    def issue(r, c):
        _row_copy(ys_ref, pos_ref[0, r], buf_ref.at[0], r, sem).start()
        _row_copy(ys_ref, pos_ref[1, r], buf_ref.at[1], r, sem).start()
        return c

    def drain(r, c):
        _row_copy(ys_ref, 0, buf_ref.at[0], 0, sem).wait()
        _row_copy(ys_ref, 0, buf_ref.at[1], 0, sem).wait()
        return c

    lax.fori_loop(0, tt, issue, 0)
    lax.fori_loop(0, tt, drain, 0)
    info = info_ref[...]
    o_ref[...] = x_ref[...] + info[:, 2:3] * buf_ref[0] + info[:, 3:4] * buf_ref[1]


def _combine(x, info, pos, ys, *, tt=256):
    t, d = x.shape
    return pl.pallas_call(
        _combine_kernel,
        grid=(t // tt,),
        in_specs=[
            pl.BlockSpec((None, 2, tt), lambda i: (i, 0, 0), memory_space=pltpu.SMEM),
            pl.BlockSpec((tt, d), lambda i: (i, 0)),
            pl.BlockSpec((tt, LANES), lambda i: (i, 0)),
            pl.BlockSpec(memory_space=pl.ANY),
        ],
        out_specs=pl.BlockSpec((tt, d), lambda i: (i, 0)),
        out_shape=jax.ShapeDtypeStruct((t, d), F32),
        scratch_shapes=[pltpu.VMEM((2, tt, d), F32), pltpu.SemaphoreType.DMA(())],
        compiler_params=_params("arbitrary"),
        name="moe_combine",
    )(pos, x, info, ys)


def _moe(x, g, router, w13, w2, *, tm=512, tt=256):
    t, d = x.shape
    n_exp = router.shape[1]
    wr = jnp.pad(router, ((0, 0), (0, LANES - n_exp)))
    h, info, counts = _router(x, g, wr, tm=tt)

    counts = counts[0, :n_exp].astype(I32)
    seg = (counts + tm - 1) // tm * tm
    seg_end = jnp.cumsum(seg)
    seg_start = seg_end - seg
    e1 = info[:, 0].astype(I32)
    e2 = info[:, 1].astype(I32)
    pos = jnp.stack([seg_start[e1] + info[:, 4].astype(I32), seg_start[e2] + info[:, 5].astype(I32)], axis=0)
    pos = pos.reshape(2, t // tt, tt).transpose(1, 0, 2)
    rows = 2 * t + n_exp * tm
    tile_row0 = jnp.arange(rows // tm, dtype=I32) * tm
    tile_live = (tile_row0 < seg_end[-1]).astype(I32)
    tile_expert = jnp.sum((seg_end[None, :] <= tile_row0[:, None]).astype(I32), axis=1)
    tile_expert = jnp.minimum(tile_expert, n_exp - 1)
    last_live = jnp.maximum(jnp.sum(tile_live) - 1, 0)
    tile_expert = jnp.where(tile_live > 0, tile_expert, tile_expert[last_live])

    xs = _dispatch(h, pos, rows, tt=tt)
    ys = _experts(xs, w13, w2, tile_expert, tile_live, tm=tm)
    return _combine(x, info, pos, ys, tt=tt)


def _row(v, n=None):
    v = v.reshape(1, -1).astype(F32)
    if n is not None and v.shape[1] < n:
        v = jnp.pad(v, ((0, 0), (0, n - v.shape[1])))
    return v


def _overlap_matrix(seq):
    n = seq // CMP_STRIDE
    cmp_start = np.arange(n) * CMP_STRIDE
    slc_start = np.arange(LANES) * SLC_LEN
    ov = (cmp_start[:, None] <= slc_start[None, :] + SLC_LEN - 1) & (cmp_start[:, None] + CMP_LEN - 1 >= slc_start[None, :])
    ov[n - 1] = False
    return jnp.asarray(ov, dtype=BF16)


def _even_layer(x, batch, norm_mix, w_in, conv_w, conv_b, wa, ba, wx, bx, lam, gate_b, q_norm, k_norm, cmp_pos,
                ck_w1, ck_w2, cv_w1, cv_w2, w_out):
    t, d = x.shape
    seq = t // batch
    rg = wa.shape[0] * wa.shape[1]
    gdk = NSA_GROUPS * NSA_DK
    n_main = 2 * rg + NSA_HEADS * NSA_DK
    n_planes = 6 * NSA_GROUPS
    n_kv = 6 * gdk
    per_group = 3 * NSA_REP
    gate_cols = w_in[:, n_main + n_kv:].reshape(d, NSA_GROUPS, per_group)
    gate_cols = jnp.pad(gate_cols, ((0, 0), (0, 0), (0, LANES - per_group))).reshape(d, NSA_GROUPS * LANES)
    gate_bias = jnp.pad(gate_b.reshape(NSA_GROUPS, per_group), ((0, 0), (0, LANES - per_group))).reshape(1, -1)
    w_all = jnp.concatenate([w_in[:, :n_main + n_kv], gate_cols], axis=1).astype(BF16)
    main, planes, gates = _norm_proj(x, _row(norm_mix), w_all, gate_bias, batch=batch, n_main=n_main,
                                     n_planes=n_planes, n_extra=NSA_GROUPS * LANES, extra_sigmoid=True)

    rg_out = _rglru(main, conv_w, _row(conv_b), wa.astype(BF16), _row(ba), wx.astype(BF16), _row(bx), _row(lam),
                    batch=batch)

    kn = jnp.pad(k_norm, ((0, 8 - k_norm.shape[0]), (0, 0)))
    kp, vp, kwn = _nsa_kprep(planes, kn)
    shift = (1.02 * LOG2E * math.sqrt(NSA_DK)) * jnp.max(jnp.abs(q_norm)) * jnp.max(jnp.abs(k_norm[1]))
    shift = shift.reshape(1).astype(F32)
    pos_flat = jnp.broadcast_to(cmp_pos.reshape(1, -1), (8, CMP_LEN * NSA_DK)).astype(BF16)
    kcmp, vcmp = _nsa_compress(planes, pos_flat, ck_w1.astype(BF16), ck_w2.astype(BF16), cv_w1.astype(BF16),
                               cv_w2.astype(BF16), kn)
    qp, ocw = _nsa_cw(shift, main, kcmp, vcmp, kwn, planes, gates, _row(q_norm), _overlap_matrix(seq))
    att = _nsa_slc(shift, qp, kp, vp, gates, ocw)
    return _out_proj([rg_out, att], w_out.astype(BF16), x)


def _odd_layer(x, batch, norm_mix, w_in, conv_w, conv_b, dt_bias, a_log, d_skip, norm_g, w_out):
    t, d = x.shape
    d_inner = w_out.shape[0]
    conv_ch = conv_w.shape[1]
    n_main = d_inner + conv_ch
    heads = dt_bias.shape[0]
    w_all = jnp.concatenate([w_in[:, :n_main], jnp.pad(w_in[:, n_main:], ((0, 0), (0, LANES - heads)))], axis=1)
    main, dt = _norm_proj(x, _row(norm_mix), w_all.astype(BF16), jnp.zeros((1, LANES), F32), batch=batch,
                          n_main=n_main, n_planes=0, n_extra=LANES, extra_sigmoid=False)
    y = _ssd(main, dt, conv_w, _row(conv_b), _row(dt_bias, LANES), _row(a_log, LANES),
             _row(jnp.repeat(d_skip, SSM_HEADDIM)), _row(norm_g), batch=batch, d_inner=d_inner)
    return _out_proj([y], w_out.astype(BF16), x)


def kernel(x, mem, norm_mix, norm_cross, norm_mem, norm_ffn, ev_w_in, ev_rg_conv_w, ev_rg_conv_b, ev_rg_wa, ev_rg_ba, ev_rg_wx, ev_rg_bx, ev_rg_lambda, ev_nsa_gate_b, ev_q_norm, ev_k_norm, ev_cmp_pos, ev_cmp_k_w1, ev_cmp_k_w2, ev_cmp_v_w1, ev_cmp_v_w2, ev_w_out, od_w_in, od_conv_w, od_conv_b, od_dt_bias, od_a_log, od_d_skip, od_norm, od_w_out, x_wq, x_wkv, x_q_norm, x_k_norm, x_wo, ff_w13, ff_w2, moe_router, moe_w13, moe_w2):
    batch, seq, d = x.shape
    depth = norm_mix.shape[0]
    xf = x.reshape(batch * seq, d)
    for layer in range(depth):
        i = layer // 2
        if layer % 2 == 0:
            xf = _even_layer(xf, batch, norm_mix[layer], ev_w_in[i], ev_rg_conv_w[i], ev_rg_conv_b[i], ev_rg_wa[i],
                             ev_rg_ba[i], ev_rg_wx[i], ev_rg_bx[i], ev_rg_lambda[i], ev_nsa_gate_b[i], ev_q_norm[i],
                             ev_k_norm[i], ev_cmp_pos[i], ev_cmp_k_w1[i], ev_cmp_k_w2[i], ev_cmp_v_w1[i],
                             ev_cmp_v_w2[i], ev_w_out[i])
        else:
            xf = _odd_layer(xf, batch, norm_mix[layer], od_w_in[i], od_conv_w[i], od_conv_b[i], od_dt_bias[i],
                            od_a_log[i], od_d_skip[i], od_norm[i], od_w_out[i])
        k, v = _mem_kv(mem, _row(norm_mem[layer]), x_wkv[layer].astype(BF16), _row(x_k_norm[layer]))
        xf = _cross_attn(xf, _row(norm_cross[layer]), x_wq[layer].astype(BF16), _row(x_q_norm[layer]), k, v,
                         x_wo[layer].astype(BF16))
        if layer % 2 == 0:
            xf = _swiglu(xf, _row(norm_ffn[layer]), ff_w13[i].astype(BF16), ff_w2[i].astype(BF16))
        else:
            xf = _moe(xf, _row(norm_ffn[layer]), moe_router[i], moe_w13[i], moe_w2[i])
    return xf.reshape(batch, seq, d)
```

```python
import functools
import math

import jax
import jax.numpy as jnp
import numpy as np
from jax import lax
from jax.experimental import pallas as pl
from jax.experimental.pallas import tpu as pltpu

F32 = jnp.float32
BF16 = jnp.bfloat16
I32 = jnp.int32

EPS = 1e-6
CONV_W = 4
RG_BLOCKS = 8
RG_C = 8.0
NSA_HEADS = 8
NSA_GROUPS = 2
NSA_REP = NSA_HEADS // NSA_GROUPS
NSA_DK = 128
CMP_LEN = 32
CMP_STRIDE = 16
SLC_LEN = 64
SLC_SHIFT = 6
SLC_TOPN = 16
WINDOW = 512
FORCE_BONUS = 100.0
SSM_HEADDIM = 64
SSM_GROUPS = 4
SSM_STATE = 128
SSD_CHUNK = 128
X_HEADS = 4
X_HEADDIM = 128
N_EXPERTS = 8
EXPERT_CHUNK = 256

LANES = 128
VMEM_LIMIT_BYTES = 56 * 1024 * 1024
NEG = -1e30
SEL_BIAS = float(2 ** 20)
LOG2E = math.log2(math.e)
FIXED_SHIFT_MAX = 56.0

NT_DIMS = (((1,), (1,)), ((), ()))


def _params(*sem):
    return pltpu.CompilerParams(dimension_semantics=sem, vmem_limit_bytes=VMEM_LIMIT_BYTES)


def _dot(a, b):
    return jnp.dot(a, b, preferred_element_type=F32)


def _dot_nt(a, b):
    return lax.dot_general(a, b, NT_DIMS, preferred_element_type=F32)


def _rms(x, g):
    return x * lax.rsqrt(jnp.mean(x * x, axis=-1, keepdims=True) + EPS) * g


def _sigmoid(x):
    return 1.0 / (1.0 + jnp.exp(-x))


def _silu(x):
    return x * _sigmoid(x)


def _gelu_tanh(x):
    c = math.sqrt(2.0 / math.pi)
    return 0.5 * x * (1.0 + jnp.tanh(c * (x + 0.044715 * (x * x * x))))


def _softplus(x):
    return jnp.maximum(x, 0.0) + jnp.log(1.0 + jnp.exp(-jnp.abs(x)))


def _shift_rows(x, tail, k, rows):
    rolled = pltpu.roll(x, k, 0)
    head = pltpu.roll(tail, k, 0)
    pad = jnp.zeros((x.shape[0] - tail.shape[0], x.shape[1]), x.dtype)
    return jnp.where(rows < k, jnp.concatenate([head, pad], axis=0), rolled)


def _causal_conv(x, tail, w_ref, b_ref):
    rows = lax.broadcasted_iota(I32, (x.shape[0], 1), 0)
    y = b_ref[...] + w_ref[CONV_W - 1:CONV_W, :] * x
    for k in range(1, CONV_W):
        y = y + w_ref[CONV_W - 1 - k:CONV_W - k, :] * _shift_rows(x, tail, k, rows)
    return y


def _norm_proj_kernel(x_ref, g_ref, w_ref, eb_ref, *out_refs, n_main, n_planes, extra_sigmoid):
    h = _rms(x_ref[...], g_ref[...]).astype(BF16)
    main_ref = out_refs[0]
    for c0 in range(0, n_main, 512):
        main_ref[:, c0:c0 + 512] = _dot(h, w_ref[:, c0:c0 + 512]).astype(main_ref.dtype)
    col = n_main
    oi = 1
    if n_planes:
        kv_ref = out_refs[oi]
        oi += 1
        for p0 in range(0, n_planes, 4):
            r = _dot(h, w_ref[:, col:col + 512])
            for p in range(4):
                kv_ref[p0 + p] = r[:, p * LANES:(p + 1) * LANES].astype(kv_ref.dtype)
            col += 512
    ex_ref = out_refs[oi]
    n_extra = ex_ref.shape[1]
    e = _dot(h, w_ref[:, col:col + n_extra]) + eb_ref[...]
    ex_ref[...] = _sigmoid(e) if extra_sigmoid else e


def _norm_proj(x, g, w, eb, *, batch, n_main, n_planes, n_extra, extra_sigmoid, tm=512):
    t, d = x.shape
    seq = t // batch
    nt = seq // tm
    out_shape = [jax.ShapeDtypeStruct((t, n_main), BF16)]
    out_specs = [pl.BlockSpec((tm, n_main), lambda i: (i, 0))]
    if n_planes:
        out_shape.append(jax.ShapeDtypeStruct((batch, n_planes, seq, LANES), BF16))
        out_specs.append(pl.BlockSpec((None, n_planes, tm, LANES), lambda i: (i // nt, 0, i % nt, 0)))
    out_shape.append(jax.ShapeDtypeStruct((t, n_extra), F32))
    out_specs.append(pl.BlockSpec((tm, n_extra), lambda i: (i, 0)))
    kern = functools.partial(_norm_proj_kernel, n_main=n_main, n_planes=n_planes, extra_sigmoid=extra_sigmoid)
    return pl.pallas_call(
        kern,
        grid=(t // tm,),
        in_specs=[
            pl.BlockSpec((tm, d), lambda i: (i, 0)),
            pl.BlockSpec((1, d), lambda i: (0, 0)),
            pl.BlockSpec(w.shape, lambda i: (0, 0)),
            pl.BlockSpec((1, n_extra), lambda i: (0, 0)),
        ],
        out_specs=out_specs,
        out_shape=out_shape,
        compiler_params=_params("parallel"),
        name="norm_proj",
    )(x, g, w, eb)


def _rglru_kernel(rx_ref, rg_ref, cw_ref, cb_ref, wa_ref, ba_ref, wx_ref, bx_ref, lam_ref, o_ref, tail_ref, h_ref):
    tc, c = rx_ref.shape

    @pl.when(pl.program_id(1) == 0)
    def _():
        tail_ref[...] = jnp.zeros_like(tail_ref)
        h_ref[...] = jnp.zeros_like(h_ref)

    x = rx_ref[...].astype(F32)
    xc = _causal_conv(x, tail_ref[...], cw_ref, cb_ref)
    tail_ref[...] = x[tc - 8:tc, :]

    bw = c // RG_BLOCKS
    ra, rx = [], []
    for blk in range(RG_BLOCKS):
        xb = xc[:, blk * bw:(blk + 1) * bw].astype(BF16)
        ra.append(_dot(xb, wa_ref[blk]))
        rx.append(_dot(xb, wx_ref[blk]))
    r = _sigmoid(jnp.concatenate(ra, axis=1) + ba_ref[...])
    ig = _sigmoid(jnp.concatenate(rx, axis=1) + bx_ref[...])
    log_a = (-RG_C) * r * _softplus(-lam_ref[...])
    a = jnp.exp(log_a)
    u = jnp.sqrt(1.0 - a * a) * (ig * xc)

    rows = lax.broadcasted_iota(I32, (tc, 1), 0)
    d = 1
    while d < tc:
        keep = rows >= d
        a_sh = jnp.where(keep, pltpu.roll(a, d, 0), 1.0)
        u_sh = jnp.where(keep, pltpu.roll(u, d, 0), 0.0)
        u = a * u_sh + u
        a = a * a_sh
        d *= 2
    h = u + a * h_ref[7:8, :]
    h_ref[...] = h[tc - 8:tc, :]
    o_ref[...] = (_gelu_tanh(rg_ref[...].astype(F32)) * h).astype(o_ref.dtype)


def _rglru(main, cw, cb, wa, ba, wx, bx, lam, *, batch, tc=256):
    t = main.shape[0]
    c = cw.shape[1]
    nt = t // batch // tc
    vec = pl.BlockSpec((1, c), lambda b, i: (0, 0))
    blk = pl.BlockSpec(wa.shape, lambda b, i: (0, 0, 0))
    return pl.pallas_call(
        _rglru_kernel,
        grid=(batch, nt),
        in_specs=[
            pl.BlockSpec((tc, c), lambda b, i: (b * nt + i, 0)),
            pl.BlockSpec((tc, c), lambda b, i: (b * nt + i, 1)),
            pl.BlockSpec((CONV_W, c), lambda b, i: (0, 0)),
            vec, blk, vec, blk, vec, vec,
        ],
        out_specs=pl.BlockSpec((tc, c), lambda b, i: (b * nt + i, 0)),
        out_shape=jax.ShapeDtypeStruct((t, c), BF16),
        scratch_shapes=[pltpu.VMEM((8, c), F32), pltpu.VMEM((8, c), F32)],
        compiler_params=_params("parallel", "arbitrary"),
        name="rglru",
    )(main, main, cw, cb, wa, ba, wx, bx, lam)


def _nsa_kprep_kernel(ks_ref, vs_ref, kw_ref, vw_ref, kn_ref, kp_ref, vp_ref, kwn_ref, vwp_ref):
    tk = ks_ref.shape[0]
    ks = _rms(ks_ref[...].astype(F32), kn_ref[1:2, :]).astype(BF16)
    t0 = pl.program_id(2) * tk
    blk = jnp.right_shift(t0 + lax.broadcasted_iota(I32, (tk, LANES), 0), SLC_SHIFT)
    onehot = jnp.where(blk == lax.broadcasted_iota(I32, (tk, LANES), 1), 1.0, 0.0).astype(BF16)
    kp_ref[...] = jnp.concatenate([ks, onehot], axis=1)
    ones = jnp.ones((tk, LANES), BF16)
    vp_ref[...] = jnp.concatenate([vs_ref[...], ones], axis=1)
    vwp_ref[...] = jnp.concatenate([vw_ref[...], ones], axis=1)
    kwn_ref[...] = _rms(kw_ref[...].astype(F32), kn_ref[2:3, :]).astype(BF16)


def _nsa_kprep(planes, k_norm, *, tk=512):
    b, _, seq, _ = planes.shape
    g = NSA_GROUPS
    plane = lambda p0: pl.BlockSpec((None, None, tk, LANES), lambda bi, gi, i: (bi, p0 + gi, i, 0))
    wide = pl.BlockSpec((None, None, tk, 2 * LANES), lambda bi, gi, i: (bi, gi, i, 0))
    return pl.pallas_call(
        _nsa_kprep_kernel,
        grid=(b, g, seq // tk),
        in_specs=[plane(4), plane(6), plane(8), plane(10), pl.BlockSpec((8, LANES), lambda bi, gi, i: (0, 0))],
        out_specs=[wide, wide, pl.BlockSpec((None, None, tk, LANES), lambda bi, gi, i: (bi, gi, i, 0)), wide],
        out_shape=[
            jax.ShapeDtypeStruct((b, g, seq, 2 * LANES), BF16),
            jax.ShapeDtypeStruct((b, g, seq, 2 * LANES), BF16),
            jax.ShapeDtypeStruct((b, g, seq, LANES), BF16),
            jax.ShapeDtypeStruct((b, g, seq, 2 * LANES), BF16),
        ],
        compiler_params=_params("parallel", "parallel", "parallel"),
        name="nsa_kprep",
    )(planes, planes, planes, planes, k_norm)


def _nsa_compress_kernel(xk_ref, xv_ref, pos_ref, kw1_ref, kw2_ref, vw1_ref, vw2_ref, kn_ref, ov_ref, kc_ref, vc_ref):
    n, half = xk_ref.shape
    last = lax.broadcasted_iota(I32, (n, 1), 0) == n - 1
    pos = pos_ref[...]

    def compress(x_ref, w1_ref, w2_ref):
        x = x_ref[...]
        y0 = _dot(x, w1_ref[0:half, :])
        y1 = _dot(x, w1_ref[half:2 * half, :])
        y1_next = jnp.where(last, 0.0, pltpu.roll(y1, n - 1, 0))
        const = _dot(pos, w1_ref[...])[0:1, :]
        hid = _gelu_tanh(y0 + y1_next + const)
        return _dot(hid.astype(BF16), w2_ref[...])

    kc_ref[...] = _rms(compress(xk_ref, kw1_ref, kw2_ref), kn_ref[0:1, :]).astype(BF16)
    vc = compress(xv_ref, vw1_ref, vw2_ref).astype(BF16)
    vc_ref[...] = jnp.concatenate([vc, jnp.ones((n, LANES), BF16), ov_ref[...]], axis=1)


def _nsa_compress(planes, pos_flat, kw1, kw2, vw1, vw2, k_norm, overlap):
    b, _, seq, _ = planes.shape
    g = NSA_GROUPS
    n = seq // CMP_STRIDE
    half = CMP_STRIDE * LANES
    grouped = planes[:, :2 * g].reshape(b, 2 * g, n, half)
    full = lambda a: pl.BlockSpec(a.shape, lambda bi, gi: (0,) * a.ndim)
    out_spec = lambda w: pl.BlockSpec((None, None, n, w), lambda bi, gi: (bi, gi, 0, 0))
    out_shape = lambda w: jax.ShapeDtypeStruct((b, g, n, w), BF16)
    return pl.pallas_call(
        _nsa_compress_kernel,
        grid=(b, g),
        in_specs=[
            pl.BlockSpec((None, None, n, half), lambda bi, gi: (bi, gi, 0, 0)),
            pl.BlockSpec((None, None, n, half), lambda bi, gi: (bi, 2 + gi, 0, 0)),
            full(pos_flat), full(kw1), full(kw2), full(vw1), full(vw2), full(k_norm), full(overlap),
        ],
        out_specs=[out_spec(LANES), out_spec(3 * LANES)],
        out_shape=[out_shape(LANES), out_shape(3 * LANES)],
        compiler_params=_params("parallel", "parallel"),
        name="nsa_compress",
    )(grouped, grouped, pos_flat, kw1, kw2, vw1, vw2, k_norm, overlap)


def _nsa_cw_kernel(shift_ref, *refs, qn, n_sel):
    fixed = jnp.maximum(shift_ref[1], shift_ref[2]) <= FIXED_SHIFT_MAX
    pl.when(fixed)(functools.partial(_nsa_cw_body, shift_ref, *refs, qn=qn, n_sel=n_sel, fixed=True))
    pl.when(jnp.logical_not(fixed))(functools.partial(_nsa_cw_body, shift_ref, *refs, qn=qn, n_sel=n_sel, fixed=False))


def _nsa_cw_body(shift_ref, q_ref, kc_ref, vc_ref, kw_ref, vw_ref, gt_ref, qn_ref, wb_ref, qp_ref, o_ref, *, qn, n_sel,
                 fixed):
    t0 = pl.program_id(2) * qn
    rep = NSA_REP
    rows = rep * qn
    scale = NSA_DK ** -0.5 * LOG2E
    qf = q_ref[...].astype(F32)
    heads = []
    for r in range(rep):
        qh = _rms(qf[:, r * LANES:(r + 1) * LANES], qn_ref[...]) * scale
        heads.append(qh.astype(BF16))
    qs = jnp.concatenate(heads, axis=0)
    trow = t0 + (lax.broadcasted_iota(I32, (rows, 1), 0) & (qn - 1))

    def attend(sm, vx, shift):
        if not fixed:
            sm = sm - jnp.maximum(jnp.max(sm, axis=-1, keepdims=True), -2.0 * shift)
        return _dot(jnp.exp2(sm).astype(BF16), vx)

    n_cmp = kc_ref.shape[0]
    visible = lax.broadcasted_iota(I32, (1, n_cmp), 1) * CMP_STRIDE + (CMP_LEN - 1) <= trow
    r_cmp = attend(jnp.where(visible, _dot_nt(qs, kc_ref[...]), NEG) - shift_ref[1], vc_ref[...], shift_ref[1])
    inv = 1.0 / jnp.maximum(r_cmp[:, LANES:2 * LANES], 1e-30)
    o_cmp = r_cmp[:, 0:LANES] * inv
    imp_h = r_cmp[:, 2 * LANES:3 * LANES] * inv
    imp = imp_h[0:qn]
    for r in range(1, rep):
        imp = imp + imp_h[r * qn:(r + 1) * qn]

    imp_t = imp.T
    jj = lax.broadcasted_iota(I32, imp_t.shape, 0).astype(F32)
    cur = jnp.right_shift(t0 + lax.broadcasted_iota(I32, imp_t.shape, 1), SLC_SHIFT).astype(F32)
    forced = (jj == 0.0) | (jj == cur) | (jj == cur - 1.0)
    work = jnp.where(jj <= cur, imp_t + jnp.where(forced, FORCE_BONUS, 0.0), -1.0)
    bias_t = jnp.full(imp_t.shape, -SEL_BIAS, F32)
    shift = -shift_ref[0]
    for _ in range(n_sel):
        m = jnp.max(work, axis=0, keepdims=True)
        idx = jnp.min(jnp.where(work == m, jj, float(LANES)), axis=0, keepdims=True)
        pick = jj == idx
        bias_t = jnp.where(pick, shift, bias_t)
        work = jnp.where(pick, -2.0, work)
    bias = bias_t.T.astype(BF16)
    for r in range(rep):
        qp_ref[r] = jnp.concatenate([heads[r], bias], axis=1)

    span = WINDOW + qn
    start = pl.multiple_of(jnp.maximum(t0 - WINDOW, 0), qn)
    s = _dot_nt(qs, kw_ref[pl.ds(start, span), :]).reshape(rep, qn, span) + wb_ref[...]
    r_win = attend(s.reshape(rows, span), vw_ref[pl.ds(start, span), :], shift_ref[2])
    o_win = r_win[:, 0:LANES] / jnp.maximum(r_win[:, LANES:2 * LANES], 1e-30)

    gt = gt_ref[...]
    for r in range(rep):
        sl = slice(r * qn, (r + 1) * qn)
        o = gt[:, 3 * r:3 * r + 1] * o_cmp[sl] + gt[:, 3 * r + 2:3 * r + 3] * o_win[sl]
        o_ref[:, r * LANES:(r + 1) * LANES] = o.astype(o_ref.dtype)


def _nsa_cw(shift, main, kcmp, vcmp, kwn, vwp, gates, q_norm, *, qn=256):
    b, g, seq, _ = kwn.shape
    t = main.shape[0]
    nq = seq // qn
    rep = NSA_REP
    gw = rep * LANES
    q_blk0 = (main.shape[1] - NSA_HEADS * NSA_DK) // gw
    n_cmp = kcmp.shape[2]
    n_case = WINDOW // qn + 1
    span = WINDOW + qn
    in_window = []
    for case in range(n_case):
        t0 = case * qn
        diff = (t0 + np.arange(qn)[:, None]) - (max(t0 - WINDOW, 0) + np.arange(span)[None, :])
        in_window.append((diff >= 0) & (diff < WINDOW))
    window_bias = jnp.where(jnp.asarray(np.stack(in_window)), -shift[2], NEG).astype(F32)
    kern = functools.partial(_nsa_cw_kernel, qn=qn, n_sel=min(SLC_TOPN, seq // SLC_LEN))
    return pl.pallas_call(
        kern,
        grid=(b, g, nq),
        in_specs=[
            pl.BlockSpec(memory_space=pltpu.SMEM),
            pl.BlockSpec((qn, gw), lambda bi, gi, i: (bi * nq + i, q_blk0 + gi)),
            pl.BlockSpec((None, None, n_cmp, LANES), lambda bi, gi, i: (bi, gi, 0, 0)),
            pl.BlockSpec((None, None, n_cmp, 3 * LANES), lambda bi, gi, i: (bi, gi, 0, 0)),
            pl.BlockSpec((None, None, seq, LANES), lambda bi, gi, i: (bi, gi, 0, 0)),
            pl.BlockSpec((None, None, seq, 2 * LANES), lambda bi, gi, i: (bi, gi, 0, 0)),
            pl.BlockSpec((qn, LANES), lambda bi, gi, i: (bi * nq + i, gi)),
            pl.BlockSpec((1, LANES), lambda bi, gi, i: (0, 0)),
            pl.BlockSpec((None, qn, span), lambda bi, gi, i: (jnp.minimum(i, n_case - 1), 0, 0)),
        ],
        out_specs=[
            pl.BlockSpec((None, None, rep, qn, 2 * LANES), lambda bi, gi, i: (bi, gi, 0, i, 0)),
            pl.BlockSpec((qn, gw), lambda bi, gi, i: (bi * nq + i, gi)),
        ],
        out_shape=[
            jax.ShapeDtypeStruct((b, g, rep, seq, 2 * LANES), BF16),
            jax.ShapeDtypeStruct((t, NSA_HEADS * NSA_DK), BF16),
        ],
        compiler_params=_params("parallel", "parallel", "parallel"),
        name="nsa_cmp_win",
    )(shift, main, kcmp, vcmp, kwn, vwp, gates, q_norm, window_bias)


def _nsa_slc_kernel(shift_ref, qp_ref, kp_ref, vp_ref, gt_ref, ocw_ref, o_ref, m_ref, acc_ref, *, qn, tk):
    t0 = pl.program_id(2) * qn
    rep = NSA_REP
    rows = rep * qn
    last = (t0 + qn - 1) // tk
    acc_ref[...] = jnp.zeros_like(acc_ref)

    def scores(j, causal):
        k0 = pl.multiple_of(j * tk, tk)
        qp = qp_ref[...].reshape(rows, qp_ref.shape[2])
        s = _dot_nt(qp, kp_ref[pl.ds(k0, tk), :])
        if causal:
            trow = t0 + (lax.broadcasted_iota(I32, (rows, 1), 0) & (qn - 1))
            s = jnp.where(k0 + lax.broadcasted_iota(I32, (1, tk), 1) <= trow, s, -SEL_BIAS)
        return s, vp_ref[pl.ds(k0, tk), :]

    def fixed_shift_step(j, causal):
        s, v = scores(j, causal)
        acc_ref[...] += _dot(jnp.exp2(s).astype(BF16), v)

    def running_max_step(j, causal):
        s, v = scores(j, causal)
        m_old = m_ref[...]
        m_new = jnp.maximum(m_old, jnp.max(s, axis=-1, keepdims=True))
        acc_ref[...] = jnp.exp2(m_old - m_new) * acc_ref[...] + _dot(jnp.exp2(s - m_new).astype(BF16), v)
        m_ref[...] = m_new

    fixed = shift_ref[0] <= FIXED_SHIFT_MAX

    @pl.when(fixed)
    def _():
        lax.fori_loop(0, last, lambda j, c: (fixed_shift_step(j, False), c)[1], 0)
        fixed_shift_step(last, True)

    @pl.when(jnp.logical_not(fixed))
    def _():
        m_ref[...] = jnp.full(m_ref.shape, NEG, F32)
        lax.fori_loop(0, last, lambda j, c: (running_max_step(j, False), c)[1], 0)
        running_max_step(last, True)

    o_slc = acc_ref[:, 0:LANES] / jnp.maximum(acc_ref[:, LANES:2 * LANES], 1e-30)
    gt = gt_ref[...]
    for r in range(rep):
        o = ocw_ref[:, r * LANES:(r + 1) * LANES].astype(F32) + gt[:, 3 * r + 1:3 * r + 2] * o_slc[r * qn:(r + 1) * qn]
        o_ref[:, r * LANES:(r + 1) * LANES] = o.astype(o_ref.dtype)


def _nsa_slc(shift, qp, kp, vp, gates, ocw, *, qn=256, tk=512):
    b, g, rep, seq, dqk = qp.shape
    t = ocw.shape[0]
    nq = seq // qn
    gw = rep * LANES
    tk = min(tk, seq)
    kern = functools.partial(_nsa_slc_kernel, qn=qn, tk=tk)
    return pl.pallas_call(
        kern,
        grid=(b, g, nq),
        in_specs=[
            pl.BlockSpec(memory_space=pltpu.SMEM),
            pl.BlockSpec((None, None, rep, qn, dqk), lambda bi, gi, i: (bi, gi, 0, i, 0)),
            pl.BlockSpec((None, None, seq, dqk), lambda bi, gi, i: (bi, gi, 0, 0)),
            pl.BlockSpec((None, None, seq, 2 * LANES), lambda bi, gi, i: (bi, gi, 0, 0)),
            pl.BlockSpec((qn, LANES), lambda bi, gi, i: (bi * nq + i, gi)),
            pl.BlockSpec((qn, gw), lambda bi, gi, i: (bi * nq + i, gi)),
        ],
        out_specs=pl.BlockSpec((qn, gw), lambda bi, gi, i: (bi * nq + i, gi)),
        out_shape=jax.ShapeDtypeStruct((t, NSA_HEADS * NSA_DK), BF16),
        scratch_shapes=[pltpu.VMEM((rep * qn, 1), F32), pltpu.VMEM((rep * qn, 2 * LANES), F32)],
        compiler_params=_params("parallel", "parallel", "parallel"),
        name="nsa_selected",
    )(shift, qp, kp, vp, gates, ocw)


def _out_proj_kernel(*refs, n_in):
    a_refs = refs[:n_in]
    w_ref, x_ref, o_ref = refs[n_in:]
    acc = x_ref[...]
    k0 = 0
    for a_ref in a_refs:
        k = a_ref.shape[1]
        acc = acc + _dot(a_ref[...], w_ref[k0:k0 + k, :])
        k0 += k
    o_ref[...] = acc


def _out_proj(acts, w, x, *, tm=512):
    t, d = x.shape
    return pl.pallas_call(
        functools.partial(_out_proj_kernel, n_in=len(acts)),
        grid=(t // tm,),
        in_specs=[pl.BlockSpec((tm, a.shape[1]), lambda i: (i, 0)) for a in acts]
        + [pl.BlockSpec(w.shape, lambda i: (0, 0)), pl.BlockSpec((tm, d), lambda i: (i, 0))],
        out_specs=pl.BlockSpec((tm, d), lambda i: (i, 0)),
        out_shape=jax.ShapeDtypeStruct((t, d), F32),
        compiler_params=_params("parallel"),
        name="out_proj",
    )(*acts, w, x)


def _mem_kv_kernel(mem_ref, g_ref, wkv_ref, kn_ref, k_ref, v_ref):
    memn = _rms(mem_ref[...], g_ref[...]).astype(BF16)
    kv = _dot(memn, wkv_ref[...])
    inner = k_ref.shape[1]
    for h in range(inner // X_HEADDIM):
        sl = slice(h * X_HEADDIM, (h + 1) * X_HEADDIM)
        k_ref[:, sl] = _rms(kv[:, sl], kn_ref[...]).astype(BF16)
    v_ref[...] = kv[:, inner:].astype(BF16)


def _mem_kv(mem, g, wkv, kn):
    b, m, d = mem.shape
    inner = wkv.shape[1] // 2
    out_spec = pl.BlockSpec((None, m, inner), lambda bi: (bi, 0, 0))
    out_shape = jax.ShapeDtypeStruct((b, m, inner), BF16)
    return pl.pallas_call(
        _mem_kv_kernel,
        grid=(b,),
        in_specs=[
            pl.BlockSpec((None, m, d), lambda bi: (bi, 0, 0)),
            pl.BlockSpec((1, d), lambda bi: (0, 0)),
            pl.BlockSpec(wkv.shape, lambda bi: (0, 0)),
            pl.BlockSpec((1, X_HEADDIM), lambda bi: (0, 0)),
        ],
        out_specs=[out_spec, out_spec],
        out_shape=[out_shape, out_shape],
        compiler_params=_params("parallel"),
        name="mem_kv",
    )(mem, g, wkv, kn)


def _cross_attn_kernel(x_ref, g_ref, wq_ref, qn_ref, k_ref, v_ref, wo_ref, o_ref):
    x = x_ref[...]
    q = _dot(_rms(x, g_ref[...]).astype(BF16), wq_ref[...])
    scale = X_HEADDIM ** -0.5
    outs = []
    for h in range(q.shape[1] // X_HEADDIM):
        sl = slice(h * X_HEADDIM, (h + 1) * X_HEADDIM)
        qh = (_rms(q[:, sl], qn_ref[...]) * scale).astype(BF16)
        s = _dot_nt(qh, k_ref[:, sl])
        e = jnp.exp(s - jnp.max(s, axis=-1, keepdims=True))
        p = e / jnp.sum(e, axis=-1, keepdims=True)
        outs.append(_dot(p.astype(BF16), v_ref[:, sl]))
    o = jnp.concatenate(outs, axis=1).astype(BF16)
    o_ref[...] = x + _dot(o, wo_ref[...])


def _cross_attn(x, g, wq, qn, k, v, wo, *, tm=512):
    t, d = x.shape
    b, m, inner = k.shape
    nt = t // b // tm
    return pl.pallas_call(
        _cross_attn_kernel,
        grid=(b, nt),
        in_specs=[
            pl.BlockSpec((tm, d), lambda bi, i: (bi * nt + i, 0)),
            pl.BlockSpec((1, d), lambda bi, i: (0, 0)),
            pl.BlockSpec(wq.shape, lambda bi, i: (0, 0)),
            pl.BlockSpec((1, X_HEADDIM), lambda bi, i: (0, 0)),
            pl.BlockSpec((None, m, inner), lambda bi, i: (bi, 0, 0)),
            pl.BlockSpec((None, m, inner), lambda bi, i: (bi, 0, 0)),
            pl.BlockSpec(wo.shape, lambda bi, i: (0, 0)),
        ],
        out_specs=pl.BlockSpec((tm, d), lambda bi, i: (bi * nt + i, 0)),
        out_shape=jax.ShapeDtypeStruct((t, d), F32),
        compiler_params=_params("parallel", "parallel"),
        name="cross_attn",
    )(x, g, wq, qn, k, v, wo)


def _swiglu_kernel(x_ref, g_ref, wu_ref, wg_ref, w2_ref, o_ref, h_ref, acc_ref):
    f = pl.program_id(1)

    @pl.when(f == 0)
    def _():
        h_ref[...] = _rms(x_ref[...], g_ref[...]).astype(BF16)
        acc_ref[...] = x_ref[...]

    h = h_ref[...]
    act = _silu(_dot(h, wg_ref[...])) * _dot(h, wu_ref[...])
    acc_ref[...] += _dot(act.astype(BF16), w2_ref[...])

    @pl.when(f == pl.num_programs(1) - 1)
    def _():
        o_ref[...] = acc_ref[...]


def _swiglu(x, g, w13, w2, *, tm=1024, tf=256):
    t, d = x.shape
    ff = w2.shape[0]
    nf = ff // tf
    return pl.pallas_call(
        _swiglu_kernel,
        grid=(t // tm, nf),
        in_specs=[
            pl.BlockSpec((tm, d), lambda i, f: (i, 0)),
            pl.BlockSpec((1, d), lambda i, f: (0, 0)),
            pl.BlockSpec((d, tf), lambda i, f: (0, f)),
            pl.BlockSpec((d, tf), lambda i, f: (0, nf + f)),
            pl.BlockSpec((tf, d), lambda i, f: (f, 0)),
        ],
        out_specs=pl.BlockSpec((tm, d), lambda i, f: (i, 0)),
        out_shape=jax.ShapeDtypeStruct((t, d), F32),
        scratch_shapes=[pltpu.VMEM((tm, d), BF16), pltpu.VMEM((tm, d), F32)],
        compiler_params=_params("parallel", "arbitrary"),
        name="swiglu",
    )(x, g, w13, w13, w2)


def _split3(x):
    a = x.astype(BF16)
    r = x - a.astype(F32)
    b = r.astype(BF16)
    c = (r - b.astype(F32)).astype(BF16)
    return a, b, c


def _ssd_kernel(z0_ref, z1_ref, x0_ref, x1_ref, bc_ref, dt_ref, cw_ref, cb_ref, dtb_ref, alog_ref, dskip_ref,
                ng_ref, o_ref, tail_ref, state_ref):
    q = x0_ref.shape[0]
    d_inner = o_ref.shape[1]
    gn = SSM_GROUPS * SSM_STATE
    hpg = d_inner // SSM_HEADDIM // SSM_GROUPS

    @pl.when(pl.program_id(1) == 0)
    def _():
        tail_ref[...] = jnp.zeros_like(tail_ref)
        state_ref[...] = jnp.zeros_like(state_ref)

    raw = jnp.concatenate([x0_ref[...], x1_ref[...], bc_ref[...]], axis=1).astype(F32)
    xbc = _silu(_causal_conv(raw, tail_ref[...], cw_ref, cb_ref))
    tail_ref[...] = raw[q - 8:q, :]
    xs = xbc[:, :d_inner]
    bm = xbc[:, d_inner:d_inner + gn]
    cm = xbc[:, d_inner + gn:]

    dt = _softplus(dt_ref[...] + dtb_ref[...])
    a = dt * (-jnp.exp(alog_ref[...]))
    ri = lax.broadcasted_iota(I32, (q, q), 0)
    ci = lax.broadcasted_iota(I32, (q, q), 1)
    causal = ci <= ri
    tri = jnp.where(causal, 1.0, 0.0).astype(BF16)
    a_cs = sum(_dot(tri, part) for part in _split3(a))
    a_cs_t = a_cs.T
    dt_t = dt.T
    lane = lax.broadcasted_iota(I32, (1, LANES), 1)
    lo = lane < SSM_HEADDIM

    y_parts = []
    for g in range(SSM_GROUPS):
        cg = cm[:, g * SSM_STATE:(g + 1) * SSM_STATE].astype(BF16)
        bg = bm[:, g * SSM_STATE:(g + 1) * SSM_STATE]
        gmat = _dot_nt(cg, bg.astype(BF16))
        bg_t = bg.T
        gw = hpg * SSM_HEADDIM
        prev = state_ref[:, g * gw:(g + 1) * gw]
        y_off = _dot(cg, prev.astype(BF16))
        for pr in range(hpg // 2):
            c0 = g * gw + pr * LANES
            x_pair = xs[:, c0:c0 + LANES]
            y_pair = dskip_ref[:, c0:c0 + LANES] * x_pair
            st_pair = jnp.zeros((SSM_STATE, LANES), F32)
            decay_pair = jnp.zeros((1, LANES), F32)
            for half in range(2):
                h = g * hpg + pr * 2 + half
                sel = lo if half == 0 else jnp.logical_not(lo)
                xh = jnp.where(sel, x_pair, 0.0).astype(BF16)
                row_cs = a_cs_t[h:h + 1, :]
                col_cs = a_cs[:, h:h + 1]
                row_dt = dt_t[h:h + 1, :]
                a_last = a_cs_t[h:h + 1, q - 1:q]
                dec = jnp.exp(jnp.where(causal, col_cs - row_cs, NEG))
                y_pair = y_pair + _dot((gmat * dec * row_dt).astype(BF16), xh)
                w_row = jnp.exp(a_last - row_cs) * row_dt
                st_pair = st_pair + _dot((bg_t * w_row).astype(BF16), xh)
                y_pair = y_pair + jnp.where(sel, jnp.exp(col_cs) * y_off[:, pr * LANES:(pr + 1) * LANES], 0.0)
                decay_pair = jnp.where(sel, jnp.exp(a_last), decay_pair)
            state_ref[:, c0:c0 + LANES] = decay_pair * state_ref[:, c0:c0 + LANES] + st_pair
            y_parts.append(y_pair)
    y = jnp.concatenate(y_parts, axis=1)

    z = jnp.concatenate([z0_ref[...], z1_ref[...]], axis=1).astype(F32)
    y = y * _silu(z)
    gsz = d_inner // SSM_GROUPS
    for g in range(SSM_GROUPS):
        sl = slice(g * gsz, (g + 1) * gsz)
        o_ref[:, sl] = _rms(y[:, sl], ng_ref[:, sl]).astype(o_ref.dtype)


def _ssd(main, dt, cw, cb, dtb, alog, dskip, ng, *, batch, d_inner):
    t = main.shape[0]
    q = SSD_CHUNK
    nc = t // batch // q
    conv_ch = cw.shape[1]
    half = d_inner // 2
    col = lambda j: pl.BlockSpec((q, half), lambda b, i: (b * nc + i, j))
    vec = lambda n: pl.BlockSpec((1, n), lambda b, i: (0, 0))
    return pl.pallas_call(
        _ssd_kernel,
        grid=(batch, nc),
        in_specs=[
            col(0), col(1), col(2), col(3), col(4),
            pl.BlockSpec((q, LANES), lambda b, i: (b * nc + i, 0)),
            pl.BlockSpec((CONV_W, conv_ch), lambda b, i: (0, 0)),
            vec(conv_ch), vec(LANES), vec(LANES), vec(d_inner), vec(d_inner),
        ],
        out_specs=pl.BlockSpec((q, d_inner), lambda b, i: (b * nc + i, 0)),
        out_shape=jax.ShapeDtypeStruct((t, d_inner), BF16),
        scratch_shapes=[pltpu.VMEM((8, conv_ch), F32), pltpu.VMEM((SSM_STATE, d_inner), F32)],
        compiler_params=_params("parallel", "arbitrary"),
        name="ssd",
    )(main, main, main, main, main, dt, cw, cb, dtb, alog, dskip, ng)


def _router_kernel(x_ref, g_ref, wr_ref, h_ref, info_ref, cnt_ref, run_ref):
    tm = x_ref.shape[0]

    @pl.when(pl.program_id(0) == 0)
    def _():
        run_ref[...] = jnp.zeros_like(run_ref)

    h = _rms(x_ref[...], g_ref[...])
    h_ref[...] = h
    logits = jnp.dot(h, wr_ref[...], preferred_element_type=F32, precision=lax.Precision.HIGHEST)
    lane = lax.broadcasted_iota(I32, (tm, LANES), 1)
    lg = jnp.where(lane < N_EXPERTS, logits, NEG)
    m1 = jnp.max(lg, axis=-1, keepdims=True)
    i1 = jnp.min(jnp.where(lg == m1, lane, LANES), axis=-1, keepdims=True)
    lg2 = jnp.where(lane == i1, NEG, lg)
    m2 = jnp.max(lg2, axis=-1, keepdims=True)
    i2 = jnp.min(jnp.where(lg2 == m2, lane, LANES), axis=-1, keepdims=True)
    e2 = jnp.exp(m2 - m1)
    w1 = 1.0 / (1.0 + e2)
    w2 = e2 / (1.0 + e2)

    hot1 = lane == i1
    hot2 = lane == i2
    hot = jnp.where(hot1 | hot2, 1.0, 0.0)
    ri = lax.broadcasted_iota(I32, (tm, tm), 0)
    ci = lax.broadcasted_iota(I32, (tm, tm), 1)
    before = jnp.where(ci < ri, 1.0, 0.0).astype(BF16)
    seen = run_ref[0:1, :] + _dot(before, hot.astype(BF16))
    rank1 = jnp.sum(jnp.where(hot1, seen, 0.0), axis=-1, keepdims=True)
    rank2 = jnp.sum(jnp.where(hot2, seen, 0.0), axis=-1, keepdims=True)
    run_ref[...] = run_ref[...] + jnp.sum(hot, axis=0, keepdims=True)
    cnt_ref[...] = run_ref[...]

    cols = [i1.astype(F32), i2.astype(F32), w1, w2, rank1, rank2]
    info = jnp.zeros((tm, LANES), F32)
    for c, v in enumerate(cols):
        info = jnp.where(lane == c, v, info)
    info_ref[...] = info


def _router(x, g, wr, *, tm=256):
    t, d = x.shape
    return pl.pallas_call(
        _router_kernel,
        grid=(t // tm,),
        in_specs=[
            pl.BlockSpec((tm, d), lambda i: (i, 0)),
            pl.BlockSpec((1, d), lambda i: (0, 0)),
            pl.BlockSpec(wr.shape, lambda i: (0, 0)),
        ],
        out_specs=[
            pl.BlockSpec((tm, d), lambda i: (i, 0)),
            pl.BlockSpec((tm, LANES), lambda i: (i, 0)),
            pl.BlockSpec((8, LANES), lambda i: (0, 0)),
        ],
        out_shape=[
            jax.ShapeDtypeStruct((t, d), F32),
            jax.ShapeDtypeStruct((t, LANES), F32),
            jax.ShapeDtypeStruct((8, LANES), F32),
        ],
        scratch_shapes=[pltpu.VMEM((8, LANES), F32)],
        compiler_params=_params("arbitrary"),
        name="moe_router",
    )(x, g, wr)


def _row_copy(src_ref, src_row, dst_ref, dst_row, sem):
    return pltpu.make_async_copy(src_ref.at[pl.ds(src_row, 1)], dst_ref.at[pl.ds(dst_row, 1)], sem)


def _dispatch_kernel(pos_ref, h_ref, init_ref, xs_ref, sem):
    del init_ref
    tt = h_ref.shape[0]

    def issue(r, c):
        _row_copy(h_ref, r, xs_ref, pos_ref[0, r], sem).start()
        _row_copy(h_ref, r, xs_ref, pos_ref[1, r], sem).start(priority=1)
        return c

    def drain(r, c):
        _row_copy(h_ref, 0, xs_ref, 0, sem).wait()
        _row_copy(h_ref, 0, xs_ref, 0, sem).wait()
        return c

    lax.fori_loop(0, tt, issue, 0)
    lax.fori_loop(0, tt, drain, 0)


def _dispatch(h, pos, rows, *, tt=256):
    t, d = h.shape
    init = jnp.zeros((rows, d), h.dtype)
    return pl.pallas_call(
        _dispatch_kernel,
        grid=(t // tt,),
        in_specs=[
            pl.BlockSpec((None, 2, tt), lambda i: (i, 0, 0), memory_space=pltpu.SMEM),
            pl.BlockSpec((tt, d), lambda i: (i, 0)),
            pl.BlockSpec(memory_space=pl.ANY),
        ],
        out_specs=pl.BlockSpec(memory_space=pl.ANY),
        out_shape=jax.ShapeDtypeStruct((rows, d), h.dtype),
        scratch_shapes=[pltpu.SemaphoreType.DMA(())],
        input_output_aliases={2: 0},
        compiler_params=_params("arbitrary"),
        name="moe_dispatch",
    )(pos, h, init)


def _expert_kernel(te_ref, tv_ref, xs_ref, wu_ref, wg_ref, w2_ref, o_ref, xb_ref):
    del te_ref
    i = pl.program_id(0)
    f = pl.program_id(1)
    live = tv_ref[i] > 0

    @pl.when(f == 0)
    def _():
        o_ref[...] = jnp.zeros_like(o_ref)
        xb_ref[...] = xs_ref[...].astype(BF16)

    @pl.when(live)
    def _():
        x = xb_ref[...]
        act = (_silu(_dot(x, wg_ref[...].astype(BF16))) * _dot(x, wu_ref[...].astype(BF16))).astype(BF16)
        for c0 in range(0, o_ref.shape[1], EXPERT_CHUNK):
            c = slice(c0, c0 + EXPERT_CHUNK)
            o_ref[:, c] += _dot(act, w2_ref[:, c].astype(BF16))


def _experts(xs, w13, w2, tile_expert, tile_live, *, tm, tf=512):
    rows, d = xs.shape
    ff = w2.shape[1]
    nf = ff // tf

    def f_of(i, f, te, tv):
        return jnp.where(tv[i] > 0, f, nf - 1)

    grid_spec = pltpu.PrefetchScalarGridSpec(
        num_scalar_prefetch=2,
        grid=(rows // tm, nf),
        in_specs=[
            pl.BlockSpec((tm, d), lambda i, f, te, tv: (i, 0)),
            pl.BlockSpec((None, d, tf), lambda i, f, te, tv: (te[i], 0, f_of(i, f, te, tv))),
            pl.BlockSpec((None, d, tf), lambda i, f, te, tv: (te[i], 0, nf + f_of(i, f, te, tv))),
            pl.BlockSpec((None, tf, d), lambda i, f, te, tv: (te[i], f_of(i, f, te, tv), 0)),
        ],
        out_specs=pl.BlockSpec((tm, d), lambda i, f, te, tv: (i, 0)),
        scratch_shapes=[pltpu.VMEM((tm, d), BF16)],
    )
    return pl.pallas_call(
        _expert_kernel,
        grid_spec=grid_spec,
        out_shape=jax.ShapeDtypeStruct((rows, d), F32),
        compiler_params=_params("parallel", "arbitrary"),
        name="moe_experts",
    )(tile_expert, tile_live, xs, w13, w13, w2)


def _combine_kernel(pos_ref, x_ref, info_ref, ys_ref, o_ref, buf_ref, sem):
    tt = x_ref.shape[0]

    def issue(r, c):
        _row_copy(ys_ref, pos_ref[0, r], buf_ref.at[0], r, sem).start()
        _row_copy(ys_ref, pos_ref[1, r], buf_ref.at[1], r, sem).start(priority=1)
        return c

    def drain(r, c):
        _row_copy(ys_ref, 0, buf_ref.at[0], 0, sem).wait()
        _row_copy(ys_ref, 0, buf_ref.at[1], 0, sem).wait()
        return c

    lax.fori_loop(0, tt, issue, 0)
    lax.fori_loop(0, tt, drain, 0)
    info = info_ref[...]
    o_ref[...] = x_ref[...] + info[:, 2:3] * buf_ref[0] + info[:, 3:4] * buf_ref[1]


def _combine(x, info, pos, ys, *, tt=256):
    t, d = x.shape
    return pl.pallas_call(
        _combine_kernel,
        grid=(t // tt,),
        in_specs=[
            pl.BlockSpec((None, 2, tt), lambda i: (i, 0, 0), memory_space=pltpu.SMEM),
            pl.BlockSpec((tt, d), lambda i: (i, 0)),
            pl.BlockSpec((tt, LANES), lambda i: (i, 0)),
            pl.BlockSpec(memory_space=pl.ANY),
        ],
        out_specs=pl.BlockSpec((tt, d), lambda i: (i, 0)),
        out_shape=jax.ShapeDtypeStruct((t, d), F32),
        scratch_shapes=[pltpu.VMEM((2, tt, d), F32), pltpu.SemaphoreType.DMA(())],
        compiler_params=_params("arbitrary"),
        name="moe_combine",
    )(pos, x, info, ys)


def _moe(x, g, router, w13, w2, *, tm=1024, tt=256):
    t, d = x.shape
    n_exp = router.shape[1]
    wr = jnp.pad(router, ((0, 0), (0, LANES - n_exp)))
    h, info, counts = _router(x, g, wr, tm=tt)

    counts = counts[0, :n_exp].astype(I32)
    seg = (counts + tm - 1) // tm * tm
    seg_end = jnp.cumsum(seg)
    seg_start = seg_end - seg
    e1 = info[:, 0].astype(I32)
    e2 = info[:, 1].astype(I32)
    pos = jnp.stack([seg_start[e1] + info[:, 4].astype(I32), seg_start[e2] + info[:, 5].astype(I32)], axis=0)
    pos = pos.reshape(2, t // tt, tt).transpose(1, 0, 2)
    rows = 2 * t + n_exp * tm
    tile_row0 = jnp.arange(rows // tm, dtype=I32) * tm
    tile_live = (tile_row0 < seg_end[-1]).astype(I32)
    tile_expert = jnp.sum((seg_end[None, :] <= tile_row0[:, None]).astype(I32), axis=1)
    tile_expert = jnp.minimum(tile_expert, n_exp - 1)
    last_live = jnp.maximum(jnp.sum(tile_live) - 1, 0)
    tile_expert = jnp.where(tile_live > 0, tile_expert, tile_expert[last_live])

    xs = _dispatch(h, pos, rows, tt=tt)
    ys = _experts(xs, w13, w2, tile_expert, tile_live, tm=tm)
    return _combine(x, info, pos, ys, tt=tt)


def _row(v, n=None):
    v = v.reshape(1, -1).astype(F32)
    if n is not None and v.shape[1] < n:
        v = jnp.pad(v, ((0, 0), (0, n - v.shape[1])))
    return v


def _overlap_matrix(seq):
    n = seq // CMP_STRIDE
    cmp_start = np.arange(n) * CMP_STRIDE
    slc_start = np.arange(LANES) * SLC_LEN
    ov = (cmp_start[:, None] <= slc_start[None, :] + SLC_LEN - 1) & (cmp_start[:, None] + CMP_LEN - 1 >= slc_start[None, :])
    ov[n - 1] = False
    return jnp.asarray(ov, dtype=BF16)


def _even_layer(x, batch, norm_mix, w_in, conv_w, conv_b, wa, ba, wx, bx, lam, gate_b, q_norm, k_norm, cmp_pos,
                ck_w1, ck_w2, cv_w1, cv_w2, w_out):
    t, d = x.shape
    seq = t // batch
    rg = wa.shape[0] * wa.shape[1]
    gdk = NSA_GROUPS * NSA_DK
    n_main = 2 * rg + NSA_HEADS * NSA_DK
    n_planes = 6 * NSA_GROUPS
    n_kv = 6 * gdk
    per_group = 3 * NSA_REP
    gate_cols = w_in[:, n_main + n_kv:].reshape(d, NSA_GROUPS, per_group)
    gate_cols = jnp.pad(gate_cols, ((0, 0), (0, 0), (0, LANES - per_group))).reshape(d, NSA_GROUPS * LANES)
    gate_bias = jnp.pad(gate_b.reshape(NSA_GROUPS, per_group), ((0, 0), (0, LANES - per_group))).reshape(1, -1)
    w_all = jnp.concatenate([w_in[:, :n_main + n_kv], gate_cols], axis=1).astype(BF16)
    main, planes, gates = _norm_proj(x, _row(norm_mix), w_all, gate_bias, batch=batch, n_main=n_main,
                                     n_planes=n_planes, n_extra=NSA_GROUPS * LANES, extra_sigmoid=True)

    rg_out = _rglru(main, conv_w, _row(conv_b), wa.astype(BF16), _row(ba), wx.astype(BF16), _row(bx), _row(lam),
                    batch=batch)

    kn = jnp.pad(k_norm, ((0, 8 - k_norm.shape[0]), (0, 0)))
    kp, vp, kwn, vwp = _nsa_kprep(planes, kn)
    k_gain = jnp.max(jnp.abs(k_norm), axis=1)[jnp.array([1, 0, 2])]
    shift = ((1.02 * LOG2E * math.sqrt(NSA_DK)) * jnp.max(jnp.abs(q_norm)) * k_gain).astype(F32)
    pos_flat = jnp.broadcast_to(cmp_pos.reshape(1, -1), (8, CMP_LEN * NSA_DK)).astype(BF16)
    kcmp, vcmp = _nsa_compress(planes, pos_flat, ck_w1.astype(BF16), ck_w2.astype(BF16), cv_w1.astype(BF16),
                               cv_w2.astype(BF16), kn, _overlap_matrix(seq))
    qp, ocw = _nsa_cw(shift, main, kcmp, vcmp, kwn, vwp, gates, _row(q_norm))
    att = _nsa_slc(shift, qp, kp, vp, gates, ocw)
    return _out_proj([rg_out, att], w_out.astype(BF16), x)


def _odd_layer(x, batch, norm_mix, w_in, conv_w, conv_b, dt_bias, a_log, d_skip, norm_g, w_out):
    t, d = x.shape
    d_inner = w_out.shape[0]
    conv_ch = conv_w.shape[1]
    n_main = d_inner + conv_ch
    heads = dt_bias.shape[0]
    w_all = jnp.concatenate([w_in[:, :n_main], jnp.pad(w_in[:, n_main:], ((0, 0), (0, LANES - heads)))], axis=1)
    main, dt = _norm_proj(x, _row(norm_mix), w_all.astype(BF16), jnp.zeros((1, LANES), F32), batch=batch,
                          n_main=n_main, n_planes=0, n_extra=LANES, extra_sigmoid=False)
    y = _ssd(main, dt, conv_w, _row(conv_b), _row(dt_bias, LANES), _row(a_log, LANES),
             _row(jnp.repeat(d_skip, SSM_HEADDIM)), _row(norm_g), batch=batch, d_inner=d_inner)
    return _out_proj([y], w_out.astype(BF16), x)


def kernel(x, mem, norm_mix, norm_cross, norm_mem, norm_ffn, ev_w_in, ev_rg_conv_w, ev_rg_conv_b, ev_rg_wa, ev_rg_ba, ev_rg_wx, ev_rg_bx, ev_rg_lambda, ev_nsa_gate_b, ev_q_norm, ev_k_norm, ev_cmp_pos, ev_cmp_k_w1, ev_cmp_k_w2, ev_cmp_v_w1, ev_cmp_v_w2, ev_w_out, od_w_in, od_conv_w, od_conv_b, od_dt_bias, od_a_log, od_d_skip, od_norm, od_w_out, x_wq, x_wkv, x_q_norm, x_k_norm, x_wo, ff_w13, ff_w2, moe_router, moe_w13, moe_w2):
    batch, seq, d = x.shape
    depth = norm_mix.shape[0]
    xf = x.reshape(batch * seq, d)
    for layer in range(depth):
        i = layer // 2
        if layer % 2 == 0:
            xf = _even_layer(xf, batch, norm_mix[layer], ev_w_in[i], ev_rg_conv_w[i], ev_rg_conv_b[i], ev_rg_wa[i],
                             ev_rg_ba[i], ev_rg_wx[i], ev_rg_bx[i], ev_rg_lambda[i], ev_nsa_gate_b[i], ev_q_norm[i],
                             ev_k_norm[i], ev_cmp_pos[i], ev_cmp_k_w1[i], ev_cmp_k_w2[i], ev_cmp_v_w1[i],
                             ev_cmp_v_w2[i], ev_w_out[i])
        else:
            xf = _odd_layer(xf, batch, norm_mix[layer], od_w_in[i], od_conv_w[i], od_conv_b[i], od_dt_bias[i],
                            od_a_log[i], od_d_skip[i], od_norm[i], od_w_out[i])
        k, v = _mem_kv(mem, _row(norm_mem[layer]), x_wkv[layer].astype(BF16), _row(x_k_norm[layer]))
        xf = _cross_attn(xf, _row(norm_cross[layer]), x_wq[layer].astype(BF16), _row(x_q_norm[layer]), k, v,
                         x_wo[layer].astype(BF16))
        if layer % 2 == 0:
            xf = _swiglu(xf, _row(norm_ffn[layer]), ff_w13[i].astype(BF16), ff_w2[i].astype(BF16))
        else:
            xf = _moe(xf, _row(norm_ffn[layer]), moe_router[i], moe_w13[i], moe_w2[i])
    return xf.reshape(batch, seq, d)
```

```python
import functools
import math

import jax
import jax.numpy as jnp
import numpy as np
from jax import lax
from jax.experimental import pallas as pl
from jax.experimental.pallas import tpu as pltpu

F32 = jnp.float32
BF16 = jnp.bfloat16
I32 = jnp.int32

EPS = 1e-6
CONV_W = 4
RG_BLOCKS = 8
RG_C = 8.0
NSA_HEADS = 8
NSA_GROUPS = 2
NSA_REP = NSA_HEADS // NSA_GROUPS
NSA_DK = 128
CMP_LEN = 32
CMP_STRIDE = 16
SLC_LEN = 64
SLC_SHIFT = 6
SLC_TOPN = 16
WINDOW = 512
FORCE_BONUS = 100.0
SSM_HEADDIM = 64
SSM_GROUPS = 4
SSM_STATE = 128
SSD_CHUNK = 128
X_HEADS = 4
X_HEADDIM = 128
N_EXPERTS = 8
EXPERT_CHUNK = 256
DMA_UNROLL = 8

LANES = 128
VMEM_LIMIT_BYTES = 56 * 1024 * 1024
NEG = -1e30
SEL_BIAS = float(2 ** 20)
LOG2E = math.log2(math.e)
FIXED_SHIFT_MAX = 56.0

NT_DIMS = (((1,), (1,)), ((), ()))


def _params(*sem):
    return pltpu.CompilerParams(dimension_semantics=sem, vmem_limit_bytes=VMEM_LIMIT_BYTES)


def _dot(a, b):
    return jnp.dot(a, b, preferred_element_type=F32)


def _dot_nt(a, b):
    return lax.dot_general(a, b, NT_DIMS, preferred_element_type=F32)


def _rms(x, g):
    return x * lax.rsqrt(jnp.mean(x * x, axis=-1, keepdims=True) + EPS) * g


def _sigmoid(x):
    return 1.0 / (1.0 + jnp.exp(-x))


def _silu(x):
    return x * _sigmoid(x)


def _gelu_tanh(x):
    c = math.sqrt(2.0 / math.pi)
    return 0.5 * x * (1.0 + jnp.tanh(c * (x + 0.044715 * (x * x * x))))


def _softplus(x):
    return jnp.maximum(x, 0.0) + jnp.log(1.0 + jnp.exp(-jnp.abs(x)))


def _shift_rows(x, tail, k, rows):
    rolled = pltpu.roll(x, k, 0)
    head = pltpu.roll(tail, k, 0)
    pad = jnp.zeros((x.shape[0] - tail.shape[0], x.shape[1]), x.dtype)
    return jnp.where(rows < k, jnp.concatenate([head, pad], axis=0), rolled)


def _causal_conv(x, tail, w_ref, b_ref):
    rows = lax.broadcasted_iota(I32, (x.shape[0], 1), 0)
    y = b_ref[...] + w_ref[CONV_W - 1:CONV_W, :] * x
    for k in range(1, CONV_W):
        y = y + w_ref[CONV_W - 1 - k:CONV_W - k, :] * _shift_rows(x, tail, k, rows)
    return y


def _norm_proj_kernel(x_ref, g_ref, w_ref, eb_ref, *out_refs, n_main, n_planes, extra_sigmoid):
    h = _rms(x_ref[...], g_ref[...]).astype(BF16)
    main_ref = out_refs[0]
    for c0 in range(0, n_main, 512):
        main_ref[:, c0:c0 + 512] = _dot(h, w_ref[:, c0:c0 + 512]).astype(main_ref.dtype)
    col = n_main
    oi = 1
    if n_planes:
        kv_ref = out_refs[oi]
        oi += 1
        for p0 in range(0, n_planes, 4):
            r = _dot(h, w_ref[:, col:col + 512])
            for p in range(4):
                kv_ref[p0 + p] = r[:, p * LANES:(p + 1) * LANES].astype(kv_ref.dtype)
            col += 512
    ex_ref = out_refs[oi]
    n_extra = ex_ref.shape[1]
    e = _dot(h, w_ref[:, col:col + n_extra]) + eb_ref[...]
    ex_ref[...] = _sigmoid(e) if extra_sigmoid else e


def _norm_proj(x, g, w, eb, *, batch, n_main, n_planes, n_extra, extra_sigmoid, tm=512):
    t, d = x.shape
    seq = t // batch
    nt = seq // tm
    out_shape = [jax.ShapeDtypeStruct((t, n_main), BF16)]
    out_specs = [pl.BlockSpec((tm, n_main), lambda i: (i, 0))]
    if n_planes:
        out_shape.append(jax.ShapeDtypeStruct((batch, n_planes, seq, LANES), BF16))
        out_specs.append(pl.BlockSpec((None, n_planes, tm, LANES), lambda i: (i // nt, 0, i % nt, 0)))
    out_shape.append(jax.ShapeDtypeStruct((t, n_extra), F32))
    out_specs.append(pl.BlockSpec((tm, n_extra), lambda i: (i, 0)))
    kern = functools.partial(_norm_proj_kernel, n_main=n_main, n_planes=n_planes, extra_sigmoid=extra_sigmoid)
    return pl.pallas_call(
        kern,
        grid=(t // tm,),
        in_specs=[
            pl.BlockSpec((tm, d), lambda i: (i, 0)),
            pl.BlockSpec((1, d), lambda i: (0, 0)),
            pl.BlockSpec(w.shape, lambda i: (0, 0)),
            pl.BlockSpec((1, n_extra), lambda i: (0, 0)),
        ],
        out_specs=out_specs,
        out_shape=out_shape,
        compiler_params=_params("parallel"),
        name="norm_proj",
    )(x, g, w, eb)


def _rglru_kernel(rx_ref, rg_ref, cw_ref, cb_ref, wa_ref, ba_ref, wx_ref, bx_ref, lam_ref, o_ref, tail_ref, h_ref):
    tc, c = rx_ref.shape

    @pl.when(pl.program_id(1) == 0)
    def _():
        tail_ref[...] = jnp.zeros_like(tail_ref)
        h_ref[...] = jnp.zeros_like(h_ref)

    x = rx_ref[...].astype(F32)
    xc = _causal_conv(x, tail_ref[...], cw_ref, cb_ref)
    tail_ref[...] = x[tc - 8:tc, :]

    bw = c // RG_BLOCKS
    ra, rx = [], []
    for blk in range(RG_BLOCKS):
        xb = xc[:, blk * bw:(blk + 1) * bw].astype(BF16)
        ra.append(_dot(xb, wa_ref[blk]))
        rx.append(_dot(xb, wx_ref[blk]))
    r = _sigmoid(jnp.concatenate(ra, axis=1) + ba_ref[...])
    ig = _sigmoid(jnp.concatenate(rx, axis=1) + bx_ref[...])
    log_a = (-RG_C) * r * _softplus(-lam_ref[...])
    a = jnp.exp(log_a)
    u = jnp.sqrt(1.0 - a * a) * (ig * xc)

    rows = lax.broadcasted_iota(I32, (tc, 1), 0)
    d = 1
    while d < tc:
        keep = rows >= d
        a_sh = jnp.where(keep, pltpu.roll(a, d, 0), 1.0)
        u_sh = jnp.where(keep, pltpu.roll(u, d, 0), 0.0)
        u = a * u_sh + u
        a = a * a_sh
        d *= 2
    h = u + a * h_ref[7:8, :]
    h_ref[...] = h[tc - 8:tc, :]
    o_ref[...] = (_gelu_tanh(rg_ref[...].astype(F32)) * h).astype(o_ref.dtype)


def _rglru(main, cw, cb, wa, ba, wx, bx, lam, *, batch, tc=256):
    t = main.shape[0]
    c = cw.shape[1]
    nt = t // batch // tc
    vec = pl.BlockSpec((1, c), lambda b, i: (0, 0))
    blk = pl.BlockSpec(wa.shape, lambda b, i: (0, 0, 0))
    return pl.pallas_call(
        _rglru_kernel,
        grid=(batch, nt),
        in_specs=[
            pl.BlockSpec((tc, c), lambda b, i: (b * nt + i, 0)),
            pl.BlockSpec((tc, c), lambda b, i: (b * nt + i, 1)),
            pl.BlockSpec((CONV_W, c), lambda b, i: (0, 0)),
            vec, blk, vec, blk, vec, vec,
        ],
        out_specs=pl.BlockSpec((tc, c), lambda b, i: (b * nt + i, 0)),
        out_shape=jax.ShapeDtypeStruct((t, c), BF16),
        scratch_shapes=[pltpu.VMEM((8, c), F32), pltpu.VMEM((8, c), F32)],
        compiler_params=_params("parallel", "arbitrary"),
        name="rglru",
    )(main, main, cw, cb, wa, ba, wx, bx, lam)


def _nsa_kprep_kernel(ks_ref, vs_ref, kw_ref, vw_ref, kn_ref, kp_ref, vp_ref, kwn_ref, vwp_ref):
    tk = ks_ref.shape[0]
    ks = _rms(ks_ref[...].astype(F32), kn_ref[1:2, :]).astype(BF16)
    t0 = pl.program_id(2) * tk
    blk = jnp.right_shift(t0 + lax.broadcasted_iota(I32, (tk, LANES), 0), SLC_SHIFT)
    onehot = jnp.where(blk == lax.broadcasted_iota(I32, (tk, LANES), 1), 1.0, 0.0).astype(BF16)
    kp_ref[...] = jnp.concatenate([ks, onehot], axis=1)
    ones = jnp.ones((tk, LANES), BF16)
    vp_ref[...] = jnp.concatenate([vs_ref[...], ones], axis=1)
    vwp_ref[...] = jnp.concatenate([vw_ref[...], ones], axis=1)
    kwn_ref[...] = _rms(kw_ref[...].astype(F32), kn_ref[2:3, :]).astype(BF16)


def _nsa_kprep(planes, k_norm, *, tk=512):
    b, _, seq, _ = planes.shape
    g = NSA_GROUPS
    plane = lambda p0: pl.BlockSpec((None, None, tk, LANES), lambda bi, gi, i: (bi, p0 + gi, i, 0))
    wide = pl.BlockSpec((None, None, tk, 2 * LANES), lambda bi, gi, i: (bi, gi, i, 0))
    return pl.pallas_call(
        _nsa_kprep_kernel,
        grid=(b, g, seq // tk),
        in_specs=[plane(4), plane(6), plane(8), plane(10), pl.BlockSpec((8, LANES), lambda bi, gi, i: (0, 0))],
        out_specs=[wide, wide, pl.BlockSpec((None, None, tk, LANES), lambda bi, gi, i: (bi, gi, i, 0)), wide],
        out_shape=[
            jax.ShapeDtypeStruct((b, g, seq, 2 * LANES), BF16),
            jax.ShapeDtypeStruct((b, g, seq, 2 * LANES), BF16),
            jax.ShapeDtypeStruct((b, g, seq, LANES), BF16),
            jax.ShapeDtypeStruct((b, g, seq, 2 * LANES), BF16),
        ],
        compiler_params=_params("parallel", "parallel", "parallel"),
        name="nsa_kprep",
    )(planes, planes, planes, planes, k_norm)


def _nsa_compress_kernel(xk_ref, xv_ref, pos_ref, kw1_ref, kw2_ref, vw1_ref, vw2_ref, kn_ref, ov_ref, kc_ref, vc_ref):
    n, half = xk_ref.shape
    last = lax.broadcasted_iota(I32, (n, 1), 0) == n - 1
    pos = pos_ref[...]

    def compress(x_ref, w1_ref, w2_ref):
        x = x_ref[...]
        y0 = _dot(x, w1_ref[0:half, :])
        y1 = _dot(x, w1_ref[half:2 * half, :])
        y1_next = jnp.where(last, 0.0, pltpu.roll(y1, n - 1, 0))
        const = _dot(pos, w1_ref[...])[0:1, :]
        hid = _gelu_tanh(y0 + y1_next + const)
        return _dot(hid.astype(BF16), w2_ref[...])

    kc_ref[...] = _rms(compress(xk_ref, kw1_ref, kw2_ref), kn_ref[0:1, :]).astype(BF16)
    vc = compress(xv_ref, vw1_ref, vw2_ref).astype(BF16)
    vc_ref[...] = jnp.concatenate([vc, jnp.ones((n, LANES), BF16), ov_ref[...]], axis=1)


def _nsa_compress(planes, pos_flat, kw1, kw2, vw1, vw2, k_norm, overlap):
    b, _, seq, _ = planes.shape
    g = NSA_GROUPS
    n = seq // CMP_STRIDE
    half = CMP_STRIDE * LANES
    grouped = planes[:, :2 * g].reshape(b, 2 * g, n, half)
    full = lambda a: pl.BlockSpec(a.shape, lambda bi, gi: (0,) * a.ndim)
    out_spec = lambda w: pl.BlockSpec((None, None, n, w), lambda bi, gi: (bi, gi, 0, 0))
    out_shape = lambda w: jax.ShapeDtypeStruct((b, g, n, w), BF16)
    return pl.pallas_call(
        _nsa_compress_kernel,
        grid=(b, g),
        in_specs=[
            pl.BlockSpec((None, None, n, half), lambda bi, gi: (bi, gi, 0, 0)),
            pl.BlockSpec((None, None, n, half), lambda bi, gi: (bi, 2 + gi, 0, 0)),
            full(pos_flat), full(kw1), full(kw2), full(vw1), full(vw2), full(k_norm), full(overlap),
        ],
        out_specs=[out_spec(LANES), out_spec(3 * LANES)],
        out_shape=[out_shape(LANES), out_shape(3 * LANES)],
        compiler_params=_params("parallel", "parallel"),
        name="nsa_compress",
    )(grouped, grouped, pos_flat, kw1, kw2, vw1, vw2, k_norm, overlap)


def _nsa_cw_kernel(shift_ref, *refs, qn, n_sel):
    fixed = jnp.maximum(shift_ref[1], shift_ref[2]) <= FIXED_SHIFT_MAX
    pl.when(fixed)(functools.partial(_nsa_cw_body, shift_ref, *refs, qn=qn, n_sel=n_sel, fixed=True))
    pl.when(jnp.logical_not(fixed))(functools.partial(_nsa_cw_body, shift_ref, *refs, qn=qn, n_sel=n_sel, fixed=False))


def _nsa_cw_body(shift_ref, q_ref, kc_ref, vc_ref, kw_ref, vw_ref, gt_ref, qn_ref, wb_ref, qp_ref, o_ref, *, qn, n_sel,
                 fixed):
    t0 = pl.program_id(2) * qn
    rep = NSA_REP
    rows = rep * qn
    scale = NSA_DK ** -0.5 * LOG2E
    qf = q_ref[...].astype(F32)
    heads = []
    for r in range(rep):
        qh = _rms(qf[:, r * LANES:(r + 1) * LANES], qn_ref[...]) * scale
        heads.append(qh.astype(BF16))
    qs = jnp.concatenate(heads, axis=0)
    trow = t0 + (lax.broadcasted_iota(I32, (rows, 1), 0) & (qn - 1))

    def attend(sm, vx, shift):
        if not fixed:
            sm = sm - jnp.maximum(jnp.max(sm, axis=-1, keepdims=True), -2.0 * shift)
        return _dot(jnp.exp2(sm).astype(BF16), vx)

    n_cmp = kc_ref.shape[0]
    visible = lax.broadcasted_iota(I32, (1, n_cmp), 1) * CMP_STRIDE + (CMP_LEN - 1) <= trow
    r_cmp = attend(jnp.where(visible, _dot_nt(qs, kc_ref[...]), NEG) - shift_ref[1], vc_ref[...], shift_ref[1])
    inv = 1.0 / jnp.maximum(r_cmp[:, LANES:2 * LANES], 1e-30)
    o_cmp = r_cmp[:, 0:LANES] * inv
    imp_h = r_cmp[:, 2 * LANES:3 * LANES] * inv
    imp = imp_h[0:qn]
    for r in range(1, rep):
        imp = imp + imp_h[r * qn:(r + 1) * qn]

    imp_t = imp.T
    jj = lax.broadcasted_iota(I32, imp_t.shape, 0).astype(F32)
    cur = jnp.right_shift(t0 + lax.broadcasted_iota(I32, imp_t.shape, 1), SLC_SHIFT).astype(F32)
    forced = (jj == 0.0) | (jj == cur) | (jj == cur - 1.0)
    work = jnp.where(jj <= cur, imp_t + jnp.where(forced, FORCE_BONUS, 0.0), -1.0)
    bias_t = jnp.full(imp_t.shape, -SEL_BIAS, F32)
    shift = -shift_ref[0]
    for _ in range(n_sel):
        m = jnp.max(work, axis=0, keepdims=True)
        idx = jnp.min(jnp.where(work == m, jj, float(LANES)), axis=0, keepdims=True)
        pick = jj == idx
        bias_t = jnp.where(pick, shift, bias_t)
        work = jnp.where(pick, -2.0, work)
    bias = bias_t.T.astype(BF16)
    for r in range(rep):
        qp_ref[r] = jnp.concatenate([heads[r], bias], axis=1)

    span = WINDOW + qn
    start = pl.multiple_of(jnp.maximum(t0 - WINDOW, 0), qn)
    s = _dot_nt(qs, kw_ref[pl.ds(start, span), :]).reshape(rep, qn, span) + wb_ref[...]
    r_win = attend(s.reshape(rows, span), vw_ref[pl.ds(start, span), :], shift_ref[2])
    o_win = r_win[:, 0:LANES] / jnp.maximum(r_win[:, LANES:2 * LANES], 1e-30)

    gt = gt_ref[...]
    for r in range(rep):
        sl = slice(r * qn, (r + 1) * qn)
        o = gt[:, 3 * r:3 * r + 1] * o_cmp[sl] + gt[:, 3 * r + 2:3 * r + 3] * o_win[sl]
        o_ref[:, r * LANES:(r + 1) * LANES] = o.astype(o_ref.dtype)


def _nsa_cw(shift, main, kcmp, vcmp, kwn, vwp, gates, q_norm, *, qn=256):
    b, g, seq, _ = kwn.shape
    t = main.shape[0]
    nq = seq // qn
    rep = NSA_REP
    gw = rep * LANES
    q_blk0 = (main.shape[1] - NSA_HEADS * NSA_DK) // gw
    n_cmp = kcmp.shape[2]
    n_case = WINDOW // qn + 1
    span = WINDOW + qn
    in_window = []
    for case in range(n_case):
        t0 = case * qn
        diff = (t0 + np.arange(qn)[:, None]) - (max(t0 - WINDOW, 0) + np.arange(span)[None, :])
        in_window.append((diff >= 0) & (diff < WINDOW))
    window_bias = jnp.where(jnp.asarray(np.stack(in_window)), -shift[2], NEG).astype(F32)
    kern = functools.partial(_nsa_cw_kernel, qn=qn, n_sel=min(SLC_TOPN, seq // SLC_LEN))
    return pl.pallas_call(
        kern,
        grid=(b, g, nq),
        in_specs=[
            pl.BlockSpec(memory_space=pltpu.SMEM),
            pl.BlockSpec((qn, gw), lambda bi, gi, i: (bi * nq + i, q_blk0 + gi)),
            pl.BlockSpec((None, None, n_cmp, LANES), lambda bi, gi, i: (bi, gi, 0, 0)),
            pl.BlockSpec((None, None, n_cmp, 3 * LANES), lambda bi, gi, i: (bi, gi, 0, 0)),
            pl.BlockSpec((None, None, seq, LANES), lambda bi, gi, i: (bi, gi, 0, 0)),
            pl.BlockSpec((None, None, seq, 2 * LANES), lambda bi, gi, i: (bi, gi, 0, 0)),
            pl.BlockSpec((qn, LANES), lambda bi, gi, i: (bi * nq + i, gi)),
            pl.BlockSpec((1, LANES), lambda bi, gi, i: (0, 0)),
            pl.BlockSpec((None, qn, span), lambda bi, gi, i: (jnp.minimum(i, n_case - 1), 0, 0)),
        ],
        out_specs=[
            pl.BlockSpec((None, None, rep, qn, 2 * LANES), lambda bi, gi, i: (bi, gi, 0, i, 0)),
            pl.BlockSpec((qn, gw), lambda bi, gi, i: (bi * nq + i, gi)),
        ],
        out_shape=[
            jax.ShapeDtypeStruct((b, g, rep, seq, 2 * LANES), BF16),
            jax.ShapeDtypeStruct((t, NSA_HEADS * NSA_DK), BF16),
        ],
        compiler_params=_params("parallel", "parallel", "parallel"),
        name="nsa_cmp_win",
    )(shift, main, kcmp, vcmp, kwn, vwp, gates, q_norm, window_bias)


def _nsa_slc_kernel(shift_ref, qp_ref, kp_ref, vp_ref, gt_ref, ocw_ref, o_ref, m_ref, acc_ref, *, qn, tk):
    t0 = pl.program_id(2) * qn
    rep = NSA_REP
    rows = rep * qn
    last = (t0 + qn - 1) // tk
    acc_ref[...] = jnp.zeros_like(acc_ref)

    def scores(k0, width, causal):
        qp = qp_ref[...].reshape(rows, qp_ref.shape[2])
        s = _dot_nt(qp, kp_ref[pl.ds(k0, width), :])
        if causal:
            trow = t0 + (lax.broadcasted_iota(I32, (rows, 1), 0) & (qn - 1))
            s = jnp.where(k0 + lax.broadcasted_iota(I32, (1, width), 1) <= trow, s, -SEL_BIAS)
        return s, vp_ref[pl.ds(k0, width), :]

    def fixed_shift_step(k0, width, causal):
        s, v = scores(k0, width, causal)
        acc_ref[...] += _dot(jnp.exp2(s).astype(BF16), v)

    def running_max_step(k0, width, causal):
        s, v = scores(k0, width, causal)
        m_old = m_ref[...]
        m_new = jnp.maximum(m_old, jnp.max(s, axis=-1, keepdims=True))
        acc_ref[...] = jnp.exp2(m_old - m_new) * acc_ref[...] + _dot(jnp.exp2(s - m_new).astype(BF16), v)
        m_ref[...] = m_new

    def sweep(step):
        wide = 2 * tk
        lax.fori_loop(0, last // 2, lambda j, c: (step(pl.multiple_of(j * wide, wide), wide, False), c)[1], 0)
        pl.when(last % 2 == 1)(lambda: step(pl.multiple_of((last - 1) * tk, tk), tk, False))
        step(pl.multiple_of(last * tk, tk), tk, True)

    fixed = shift_ref[0] <= FIXED_SHIFT_MAX

    @pl.when(fixed)
    def _():
        sweep(fixed_shift_step)

    @pl.when(jnp.logical_not(fixed))
    def _():
        m_ref[...] = jnp.full(m_ref.shape, NEG, F32)
        sweep(running_max_step)

    o_slc = acc_ref[:, 0:LANES] / jnp.maximum(acc_ref[:, LANES:2 * LANES], 1e-30)
    gt = gt_ref[...]
    for r in range(rep):
        o = ocw_ref[:, r * LANES:(r + 1) * LANES].astype(F32) + gt[:, 3 * r + 1:3 * r + 2] * o_slc[r * qn:(r + 1) * qn]
        o_ref[:, r * LANES:(r + 1) * LANES] = o.astype(o_ref.dtype)


def _nsa_slc(shift, qp, kp, vp, gates, ocw, *, qn=256, tk=512):
    b, g, rep, seq, dqk = qp.shape
    t = ocw.shape[0]
    nq = seq // qn
    gw = rep * LANES
    tk = min(tk, seq)
    kern = functools.partial(_nsa_slc_kernel, qn=qn, tk=tk)
    return pl.pallas_call(
        kern,
        grid=(b, g, nq),
        in_specs=[
            pl.BlockSpec(memory_space=pltpu.SMEM),
            pl.BlockSpec((None, None, rep, qn, dqk), lambda bi, gi, i: (bi, gi, 0, i, 0)),
            pl.BlockSpec((None, None, seq, dqk), lambda bi, gi, i: (bi, gi, 0, 0)),
            pl.BlockSpec((None, None, seq, 2 * LANES), lambda bi, gi, i: (bi, gi, 0, 0)),
            pl.BlockSpec((qn, LANES), lambda bi, gi, i: (bi * nq + i, gi)),
            pl.BlockSpec((qn, gw), lambda bi, gi, i: (bi * nq + i, gi)),
        ],
        out_specs=pl.BlockSpec((qn, gw), lambda bi, gi, i: (bi * nq + i, gi)),
        out_shape=jax.ShapeDtypeStruct((t, NSA_HEADS * NSA_DK), BF16),
        scratch_shapes=[pltpu.VMEM((rep * qn, 1), F32), pltpu.VMEM((rep * qn, 2 * LANES), F32)],
        compiler_params=_params("parallel", "parallel", "parallel"),
        name="nsa_selected",
    )(shift, qp, kp, vp, gates, ocw)


def _out_proj_kernel(*refs, n_in):
    a_refs = refs[:n_in]
    w_ref, x_ref, o_ref = refs[n_in:]
    acc = x_ref[...]
    k0 = 0
    for a_ref in a_refs:
        k = a_ref.shape[1]
        acc = acc + _dot(a_ref[...], w_ref[k0:k0 + k, :])
        k0 += k
    o_ref[...] = acc


def _out_proj(acts, w, x, *, tm=512):
    t, d = x.shape
    return pl.pallas_call(
        functools.partial(_out_proj_kernel, n_in=len(acts)),
        grid=(t // tm,),
        in_specs=[pl.BlockSpec((tm, a.shape[1]), lambda i: (i, 0)) for a in acts]
        + [pl.BlockSpec(w.shape, lambda i: (0, 0)), pl.BlockSpec((tm, d), lambda i: (i, 0))],
        out_specs=pl.BlockSpec((tm, d), lambda i: (i, 0)),
        out_shape=jax.ShapeDtypeStruct((t, d), F32),
        compiler_params=_params("parallel"),
        name="out_proj",
    )(*acts, w, x)


def _mem_kv_kernel(mem_ref, g_ref, wkv_ref, kn_ref, k_ref, v_ref):
    memn = _rms(mem_ref[...], g_ref[...]).astype(BF16)
    kv = _dot(memn, wkv_ref[...])
    inner = k_ref.shape[1]
    for h in range(inner // X_HEADDIM):
        sl = slice(h * X_HEADDIM, (h + 1) * X_HEADDIM)
        k_ref[:, sl] = _rms(kv[:, sl], kn_ref[...]).astype(BF16)
    v_ref[...] = kv[:, inner:].astype(BF16)


def _mem_kv(mem, g, wkv, kn):
    b, m, d = mem.shape
    inner = wkv.shape[1] // 2
    out_spec = pl.BlockSpec((None, m, inner), lambda bi: (bi, 0, 0))
    out_shape = jax.ShapeDtypeStruct((b, m, inner), BF16)
    return pl.pallas_call(
        _mem_kv_kernel,
        grid=(b,),
        in_specs=[
            pl.BlockSpec((None, m, d), lambda bi: (bi, 0, 0)),
            pl.BlockSpec((1, d), lambda bi: (0, 0)),
            pl.BlockSpec(wkv.shape, lambda bi: (0, 0)),
            pl.BlockSpec((1, X_HEADDIM), lambda bi: (0, 0)),
        ],
        out_specs=[out_spec, out_spec],
        out_shape=[out_shape, out_shape],
        compiler_params=_params("parallel"),
        name="mem_kv",
    )(mem, g, wkv, kn)


def _cross_attn_kernel(x_ref, g_ref, wq_ref, qn_ref, k_ref, v_ref, wo_ref, o_ref):
    x = x_ref[...]
    q = _dot(_rms(x, g_ref[...]).astype(BF16), wq_ref[...])
    scale = X_HEADDIM ** -0.5
    outs = []
    for h in range(q.shape[1] // X_HEADDIM):
        sl = slice(h * X_HEADDIM, (h + 1) * X_HEADDIM)
        qh = (_rms(q[:, sl], qn_ref[...]) * scale).astype(BF16)
        s = _dot_nt(qh, k_ref[:, sl])
        e = jnp.exp(s - jnp.max(s, axis=-1, keepdims=True))
        p = e / jnp.sum(e, axis=-1, keepdims=True)
        outs.append(_dot(p.astype(BF16), v_ref[:, sl]))
    o = jnp.concatenate(outs, axis=1).astype(BF16)
    o_ref[...] = x + _dot(o, wo_ref[...])


def _cross_attn(x, g, wq, qn, k, v, wo, *, tm=512):
    t, d = x.shape
    b, m, inner = k.shape
    nt = t // b // tm
    return pl.pallas_call(
        _cross_attn_kernel,
        grid=(b, nt),
        in_specs=[
            pl.BlockSpec((tm, d), lambda bi, i: (bi * nt + i, 0)),
            pl.BlockSpec((1, d), lambda bi, i: (0, 0)),
            pl.BlockSpec(wq.shape, lambda bi, i: (0, 0)),
            pl.BlockSpec((1, X_HEADDIM), lambda bi, i: (0, 0)),
            pl.BlockSpec((None, m, inner), lambda bi, i: (bi, 0, 0)),
            pl.BlockSpec((None, m, inner), lambda bi, i: (bi, 0, 0)),
            pl.BlockSpec(wo.shape, lambda bi, i: (0, 0)),
        ],
        out_specs=pl.BlockSpec((tm, d), lambda bi, i: (bi * nt + i, 0)),
        out_shape=jax.ShapeDtypeStruct((t, d), F32),
        compiler_params=_params("parallel", "parallel"),
        name="cross_attn",
    )(x, g, wq, qn, k, v, wo)


def _swiglu_kernel(x_ref, g_ref, wu_ref, wg_ref, w2_ref, o_ref, h_ref):
    @pl.when(pl.program_id(1) == 0)
    def _():
        h_ref[...] = _rms(x_ref[...], g_ref[...]).astype(BF16)
        o_ref[...] = x_ref[...]

    h = h_ref[...]
    act = (_silu(_dot(h, wg_ref[...])) * _dot(h, wu_ref[...])).astype(BF16)
    for c0 in range(0, o_ref.shape[1], EXPERT_CHUNK):
        c = slice(c0, c0 + EXPERT_CHUNK)
        o_ref[:, c] += _dot(act, w2_ref[:, c])


def _swiglu(x, g, w13, w2, *, tm=512, tf=1408):
    t, d = x.shape
    ff = w2.shape[0]
    nf = ff // tf
    return pl.pallas_call(
        _swiglu_kernel,
        grid=(t // tm, nf),
        in_specs=[
            pl.BlockSpec((tm, d), lambda i, f: (i, 0)),
            pl.BlockSpec((1, d), lambda i, f: (0, 0)),
            pl.BlockSpec((d, tf), lambda i, f: (0, f)),
            pl.BlockSpec((d, tf), lambda i, f: (0, nf + f)),
            pl.BlockSpec((tf, d), lambda i, f: (f, 0)),
        ],
        out_specs=pl.BlockSpec((tm, d), lambda i, f: (i, 0)),
        out_shape=jax.ShapeDtypeStruct((t, d), F32),
        scratch_shapes=[pltpu.VMEM((tm, d), BF16)],
        compiler_params=_params("parallel", "arbitrary"),
        name="swiglu",
    )(x, g, w13, w13, w2)


def _split3(x):
    a = x.astype(BF16)
    r = x - a.astype(F32)
    b = r.astype(BF16)
    c = (r - b.astype(F32)).astype(BF16)
    return a, b, c


def _ssd_kernel(z0_ref, z1_ref, x0_ref, x1_ref, bc_ref, dt_ref, cw_ref, cb_ref, dtb_ref, alog_ref, dskip_ref,
                ng_ref, o_ref, tail_ref, state_ref):
    q = x0_ref.shape[0]
    d_inner = o_ref.shape[1]
    gn = SSM_GROUPS * SSM_STATE
    hpg = d_inner // SSM_HEADDIM // SSM_GROUPS

    @pl.when(pl.program_id(1) == 0)
    def _():
        tail_ref[...] = jnp.zeros_like(tail_ref)
        state_ref[...] = jnp.zeros_like(state_ref)

    raw = jnp.concatenate([x0_ref[...], x1_ref[...], bc_ref[...]], axis=1).astype(F32)
    xbc = _silu(_causal_conv(raw, tail_ref[...], cw_ref, cb_ref))
    tail_ref[...] = raw[q - 8:q, :]
    xs = xbc[:, :d_inner]
    bm = xbc[:, d_inner:d_inner + gn]
    cm = xbc[:, d_inner + gn:]

    dt = _softplus(dt_ref[...] + dtb_ref[...])
    a = dt * (-jnp.exp(alog_ref[...]))
    ri = lax.broadcasted_iota(I32, (q, q), 0)
    ci = lax.broadcasted_iota(I32, (q, q), 1)
    causal = ci <= ri
    tri = jnp.where(causal, 1.0, 0.0).astype(BF16)
    a_cs = sum(_dot(tri, part) for part in _split3(a))
    a_cs_t = a_cs.T
    dt_t = dt.T
    lane = lax.broadcasted_iota(I32, (1, LANES), 1)
    lo = lane < SSM_HEADDIM

    y_parts = []
    for g in range(SSM_GROUPS):
        cg = cm[:, g * SSM_STATE:(g + 1) * SSM_STATE].astype(BF16)
        bg = bm[:, g * SSM_STATE:(g + 1) * SSM_STATE]
        gmat = _dot_nt(cg, bg.astype(BF16))
        bg_t = bg.T
        gw = hpg * SSM_HEADDIM
        prev = state_ref[:, g * gw:(g + 1) * gw]
        y_off = _dot(cg, prev.astype(BF16))
        for pr in range(hpg // 2):
            c0 = g * gw + pr * LANES
            x_pair = xs[:, c0:c0 + LANES]
            y_pair = dskip_ref[:, c0:c0 + LANES] * x_pair
            st_pair = jnp.zeros((SSM_STATE, LANES), F32)
            decay_pair = jnp.zeros((1, LANES), F32)
            for half in range(2):
                h = g * hpg + pr * 2 + half
                sel = lo if half == 0 else jnp.logical_not(lo)
                xh = jnp.where(sel, x_pair, 0.0).astype(BF16)
                row_cs = a_cs_t[h:h + 1, :]
                col_cs = a_cs[:, h:h + 1]
                row_dt = dt_t[h:h + 1, :]
                a_last = a_cs_t[h:h + 1, q - 1:q]
                dec = jnp.exp(jnp.where(causal, col_cs - row_cs, NEG))
                y_pair = y_pair + _dot((gmat * dec * row_dt).astype(BF16), xh)
                w_row = jnp.exp(a_last - row_cs) * row_dt
                st_pair = st_pair + _dot((bg_t * w_row).astype(BF16), xh)
                y_pair = y_pair + jnp.where(sel, jnp.exp(col_cs) * y_off[:, pr * LANES:(pr + 1) * LANES], 0.0)
                decay_pair = jnp.where(sel, jnp.exp(a_last), decay_pair)
            state_ref[:, c0:c0 + LANES] = decay_pair * state_ref[:, c0:c0 + LANES] + st_pair
            y_parts.append(y_pair)
    y = jnp.concatenate(y_parts, axis=1)

    z = jnp.concatenate([z0_ref[...], z1_ref[...]], axis=1).astype(F32)
    y = y * _silu(z)
    gsz = d_inner // SSM_GROUPS
    for g in range(SSM_GROUPS):
        sl = slice(g * gsz, (g + 1) * gsz)
        o_ref[:, sl] = _rms(y[:, sl], ng_ref[:, sl]).astype(o_ref.dtype)


def _ssd(main, dt, cw, cb, dtb, alog, dskip, ng, *, batch, d_inner):
    t = main.shape[0]
    q = SSD_CHUNK
    nc = t // batch // q
    conv_ch = cw.shape[1]
    half = d_inner // 2
    col = lambda j: pl.BlockSpec((q, half), lambda b, i: (b * nc + i, j))
    vec = lambda n: pl.BlockSpec((1, n), lambda b, i: (0, 0))
    return pl.pallas_call(
        _ssd_kernel,
        grid=(batch, nc),
        in_specs=[
            col(0), col(1), col(2), col(3), col(4),
            pl.BlockSpec((q, LANES), lambda b, i: (b * nc + i, 0)),
            pl.BlockSpec((CONV_W, conv_ch), lambda b, i: (0, 0)),
            vec(conv_ch), vec(LANES), vec(LANES), vec(d_inner), vec(d_inner),
        ],
        out_specs=pl.BlockSpec((q, d_inner), lambda b, i: (b * nc + i, 0)),
        out_shape=jax.ShapeDtypeStruct((t, d_inner), BF16),
        scratch_shapes=[pltpu.VMEM((8, conv_ch), F32), pltpu.VMEM((SSM_STATE, d_inner), F32)],
        compiler_params=_params("parallel", "arbitrary"),
        name="ssd",
    )(main, main, main, main, main, dt, cw, cb, dtb, alog, dskip, ng)


def _router_kernel(x_ref, g_ref, wr_ref, h_ref, info_ref, cnt_ref, run_ref):
    tm = x_ref.shape[0]

    @pl.when(pl.program_id(0) == 0)
    def _():
        run_ref[...] = jnp.zeros_like(run_ref)

    h = _rms(x_ref[...], g_ref[...])
    h_ref[...] = h
    h_hi = h.astype(BF16)
    h_lo = (h - h_hi.astype(F32)).astype(BF16)
    w = wr_ref[...]
    w_hi = w.astype(BF16)
    w_lo = (w - w_hi.astype(F32)).astype(BF16)
    logits = _dot(h_hi, w_hi) + _dot(h_lo, w_hi) + _dot(h_hi, w_lo)
    lane = lax.broadcasted_iota(I32, (tm, LANES), 1)
    lg = jnp.where(lane < N_EXPERTS, logits, NEG)
    m1 = jnp.max(lg, axis=-1, keepdims=True)
    i1 = jnp.min(jnp.where(lg == m1, lane, LANES), axis=-1, keepdims=True)
    lg2 = jnp.where(lane == i1, NEG, lg)
    m2 = jnp.max(lg2, axis=-1, keepdims=True)
    i2 = jnp.min(jnp.where(lg2 == m2, lane, LANES), axis=-1, keepdims=True)
    e2 = jnp.exp(m2 - m1)
    w1 = 1.0 / (1.0 + e2)
    w2 = e2 / (1.0 + e2)

    hot1 = lane == i1
    hot2 = lane == i2
    hot = jnp.where(hot1 | hot2, 1.0, 0.0)
    ri = lax.broadcasted_iota(I32, (tm, tm), 0)
    ci = lax.broadcasted_iota(I32, (tm, tm), 1)
    before = jnp.where(ci < ri, 1.0, 0.0).astype(BF16)
    seen = run_ref[0:1, :] + _dot(before, hot.astype(BF16))
    rank1 = jnp.sum(jnp.where(hot1, seen, 0.0), axis=-1, keepdims=True)
    rank2 = jnp.sum(jnp.where(hot2, seen, 0.0), axis=-1, keepdims=True)
    run_ref[...] = run_ref[...] + jnp.sum(hot, axis=0, keepdims=True)
    cnt_ref[...] = run_ref[...]

    cols = [i1.astype(F32), i2.astype(F32), w1, w2, rank1, rank2]
    info = jnp.zeros((tm, LANES), F32)
    for c, v in enumerate(cols):
        info = jnp.where(lane == c, v, info)
    info_ref[...] = info


def _router(x, g, wr, *, tm=256):
    t, d = x.shape
    return pl.pallas_call(
        _router_kernel,
        grid=(t // tm,),
        in_specs=[
            pl.BlockSpec((tm, d), lambda i: (i, 0)),
            pl.BlockSpec((1, d), lambda i: (0, 0)),
            pl.BlockSpec(wr.shape, lambda i: (0, 0)),
        ],
        out_specs=[
            pl.BlockSpec((tm, d), lambda i: (i, 0)),
            pl.BlockSpec((tm, LANES), lambda i: (i, 0)),
            pl.BlockSpec((8, LANES), lambda i: (0, 0)),
        ],
        out_shape=[
            jax.ShapeDtypeStruct((t, d), F32),
            jax.ShapeDtypeStruct((t, LANES), F32),
            jax.ShapeDtypeStruct((8, LANES), F32),
        ],
        scratch_shapes=[pltpu.VMEM((8, LANES), F32)],
        compiler_params=_params("arbitrary"),
        name="moe_router",
    )(x, g, wr)


def _row_copy(src_ref, src_row, dst_ref, dst_row, sem):
    return pltpu.make_async_copy(src_ref.at[pl.ds(src_row, 1)], dst_ref.at[pl.ds(dst_row, 1)], sem)


def _dispatch_kernel(pos_ref, h_ref, init_ref, xs_ref, sem):
    del init_ref
    tt = h_ref.shape[0]

    def issue(r, c):
        _row_copy(h_ref, r, xs_ref, pos_ref[0, r], sem).start()
        _row_copy(h_ref, r, xs_ref, pos_ref[1, r], sem).start(priority=1)
        return c

    lax.fori_loop(0, tt, issue, 0, unroll=DMA_UNROLL)
    for _ in range(2):
        pltpu.make_async_copy(h_ref, xs_ref.at[pl.ds(0, tt)], sem).wait()


def _dispatch(h, pos, rows, *, tt=256):
    t, d = h.shape
    init = jnp.zeros((rows, d), h.dtype)
    return pl.pallas_call(
        _dispatch_kernel,
        grid=(t // tt,),
        in_specs=[
            pl.BlockSpec((None, 2, tt), lambda i: (i, 0, 0), memory_space=pltpu.SMEM),
            pl.BlockSpec((tt, d), lambda i: (i, 0)),
            pl.BlockSpec(memory_space=pl.ANY),
        ],
        out_specs=pl.BlockSpec(memory_space=pl.ANY),
        out_shape=jax.ShapeDtypeStruct((rows, d), h.dtype),
        scratch_shapes=[pltpu.SemaphoreType.DMA(())],
        input_output_aliases={2: 0},
        compiler_params=_params("arbitrary"),
        name="moe_dispatch",
    )(pos, h, init)


def _expert_kernel(te_ref, tv_ref, xs_ref, wu_ref, wg_ref, w2_ref, o_ref, xb_ref):
    del te_ref
    i = pl.program_id(0)
    f = pl.program_id(1)
    live = tv_ref[i] > 0

    @pl.when(f == 0)
    def _():
        o_ref[...] = jnp.zeros_like(o_ref)
        xb_ref[...] = xs_ref[...].astype(BF16)

    @pl.when(live)
    def _():
        x = xb_ref[...]
        act = (_silu(_dot(x, wg_ref[...].astype(BF16))) * _dot(x, wu_ref[...].astype(BF16))).astype(BF16)
        for c0 in range(0, o_ref.shape[1], EXPERT_CHUNK):
            c = slice(c0, c0 + EXPERT_CHUNK)
            o_ref[:, c] += _dot(act, w2_ref[:, c].astype(BF16))


def _experts(xs, w13, w2, tile_expert, tile_live, *, tm, tf=512):
    rows, d = xs.shape
    ff = w2.shape[1]
    nf = ff // tf

    def f_of(i, f, te, tv):
        return jnp.where(tv[i] > 0, f, nf - 1)

    grid_spec = pltpu.PrefetchScalarGridSpec(
        num_scalar_prefetch=2,
        grid=(rows // tm, nf),
        in_specs=[
            pl.BlockSpec((tm, d), lambda i, f, te, tv: (i, 0)),
            pl.BlockSpec((None, d, tf), lambda i, f, te, tv: (te[i], 0, f_of(i, f, te, tv))),
            pl.BlockSpec((None, d, tf), lambda i, f, te, tv: (te[i], 0, nf + f_of(i, f, te, tv))),
            pl.BlockSpec((None, tf, d), lambda i, f, te, tv: (te[i], f_of(i, f, te, tv), 0)),
        ],
        out_specs=pl.BlockSpec((tm, d), lambda i, f, te, tv: (i, 0)),
        scratch_shapes=[pltpu.VMEM((tm, d), BF16)],
    )
    return pl.pallas_call(
        _expert_kernel,
        grid_spec=grid_spec,
        out_shape=jax.ShapeDtypeStruct((rows, d), F32),
        compiler_params=_params("parallel", "arbitrary"),
        name="moe_experts",
    )(tile_expert, tile_live, xs, w13, w13, w2)


def _combine_kernel(pos_ref, x_ref, info_ref, ys_ref, o_ref, buf_ref, sem):
    tt = x_ref.shape[0]

    def issue(r, c):
        _row_copy(ys_ref, pos_ref[0, r], buf_ref.at[0], r, sem).start()
        _row_copy(ys_ref, pos_ref[1, r], buf_ref.at[1], r, sem).start(priority=1)
        return c

    lax.fori_loop(0, tt, issue, 0, unroll=DMA_UNROLL)
    for k in range(2):
        pltpu.make_async_copy(ys_ref.at[pl.ds(0, tt)], buf_ref.at[k], sem).wait()
    info = info_ref[...]
    o_ref[...] = x_ref[...] + info[:, 2:3] * buf_ref[0] + info[:, 3:4] * buf_ref[1]


def _combine(x, info, pos, ys, *, tt=256):
    t, d = x.shape
    return pl.pallas_call(
        _combine_kernel,
        grid=(t // tt,),
        in_specs=[
            pl.BlockSpec((None, 2, tt), lambda i: (i, 0, 0), memory_space=pltpu.SMEM),
            pl.BlockSpec((tt, d), lambda i: (i, 0)),
            pl.BlockSpec((tt, LANES), lambda i: (i, 0)),
            pl.BlockSpec(memory_space=pl.ANY),
        ],
        out_specs=pl.BlockSpec((tt, d), lambda i: (i, 0)),
        out_shape=jax.ShapeDtypeStruct((t, d), F32),
        scratch_shapes=[pltpu.VMEM((2, tt, d), F32), pltpu.SemaphoreType.DMA(())],
        compiler_params=_params("arbitrary"),
        name="moe_combine",
    )(pos, x, info, ys)


def _moe(x, g, router, w13, w2, *, tm=1024, tt=256):
    t, d = x.shape
    n_exp = router.shape[1]
    wr = jnp.pad(router, ((0, 0), (0, LANES - n_exp)))
    h, info, counts = _router(x, g, wr, tm=tt)

    counts = counts[0, :n_exp].astype(I32)
    seg = (counts + tm - 1) // tm * tm
    seg_end = jnp.cumsum(seg)
    seg_start = seg_end - seg
    e1 = info[:, 0].astype(I32)
    e2 = info[:, 1].astype(I32)
    pos = jnp.stack([seg_start[e1] + info[:, 4].astype(I32), seg_start[e2] + info[:, 5].astype(I32)], axis=0)
    pos = pos.reshape(2, t // tt, tt).transpose(1, 0, 2)
    rows = 2 * t + n_exp * tm
    tile_row0 = jnp.arange(rows // tm, dtype=I32) * tm
    tile_live = (tile_row0 < seg_end[-1]).astype(I32)
    tile_expert = jnp.sum((seg_end[None, :] <= tile_row0[:, None]).astype(I32), axis=1)
    tile_expert = jnp.minimum(tile_expert, n_exp - 1)
    last_live = jnp.maximum(jnp.sum(tile_live) - 1, 0)
    tile_expert = jnp.where(tile_live > 0, tile_expert, tile_expert[last_live])

    xs = _dispatch(h, pos, rows, tt=tt)
    ys = _experts(xs, w13, w2, tile_expert, tile_live, tm=tm)
    return _combine(x, info, pos, ys, tt=tt)


def _row(v, n=None):
    v = v.reshape(1, -1).astype(F32)
    if n is not None and v.shape[1] < n:
        v = jnp.pad(v, ((0, 0), (0, n - v.shape[1])))
    return v


def _overlap_matrix(seq):
    n = seq // CMP_STRIDE
    cmp_start = np.arange(n) * CMP_STRIDE
    slc_start = np.arange(LANES) * SLC_LEN
    ov = (cmp_start[:, None] <= slc_start[None, :] + SLC_LEN - 1) & (cmp_start[:, None] + CMP_LEN - 1 >= slc_start[None, :])
    ov[n - 1] = False
    return jnp.asarray(ov, dtype=BF16)


def _even_layer(x, batch, norm_mix, w_in, conv_w, conv_b, wa, ba, wx, bx, lam, gate_b, q_norm, k_norm, cmp_pos,
                ck_w1, ck_w2, cv_w1, cv_w2, w_out):
    t, d = x.shape
    seq = t // batch
    rg = wa.shape[0] * wa.shape[1]
    gdk = NSA_GROUPS * NSA_DK
    n_main = 2 * rg + NSA_HEADS * NSA_DK
    n_planes = 6 * NSA_GROUPS
    n_kv = 6 * gdk
    per_group = 3 * NSA_REP
    gate_cols = w_in[:, n_main + n_kv:].reshape(d, NSA_GROUPS, per_group)
    gate_cols = jnp.pad(gate_cols, ((0, 0), (0, 0), (0, LANES - per_group))).reshape(d, NSA_GROUPS * LANES)
    gate_bias = jnp.pad(gate_b.reshape(NSA_GROUPS, per_group), ((0, 0), (0, LANES - per_group))).reshape(1, -1)
    w_all = jnp.concatenate([w_in[:, :n_main + n_kv], gate_cols], axis=1).astype(BF16)
    main, planes, gates = _norm_proj(x, _row(norm_mix), w_all, gate_bias, batch=batch, n_main=n_main,
                                     n_planes=n_planes, n_extra=NSA_GROUPS * LANES, extra_sigmoid=True)

    rg_out = _rglru(main, conv_w, _row(conv_b), wa.astype(BF16), _row(ba), wx.astype(BF16), _row(bx), _row(lam),
                    batch=batch)

    kn = jnp.pad(k_norm, ((0, 8 - k_norm.shape[0]), (0, 0)))
    kp, vp, kwn, vwp = _nsa_kprep(planes, kn)
    k_gain = jnp.max(jnp.abs(k_norm), axis=1)[jnp.array([1, 0, 2])]
    shift = ((1.02 * LOG2E * math.sqrt(NSA_DK)) * jnp.max(jnp.abs(q_norm)) * k_gain).astype(F32)
    pos_flat = jnp.broadcast_to(cmp_pos.reshape(1, -1), (8, CMP_LEN * NSA_DK)).astype(BF16)
    kcmp, vcmp = _nsa_compress(planes, pos_flat, ck_w1.astype(BF16), ck_w2.astype(BF16), cv_w1.astype(BF16),
                               cv_w2.astype(BF16), kn, _overlap_matrix(seq))
    qp, ocw = _nsa_cw(shift, main, kcmp, vcmp, kwn, vwp, gates, _row(q_norm))
    att = _nsa_slc(shift, qp, kp, vp, gates, ocw)
    return _out_proj([rg_out, att], w_out.astype(BF16), x)


def _odd_layer(x, batch, norm_mix, w_in, conv_w, conv_b, dt_bias, a_log, d_skip, norm_g, w_out):
    t, d = x.shape
    d_inner = w_out.shape[0]
    conv_ch = conv_w.shape[1]
    n_main = d_inner + conv_ch
    heads = dt_bias.shape[0]
    w_all = jnp.concatenate([w_in[:, :n_main], jnp.pad(w_in[:, n_main:], ((0, 0), (0, LANES - heads)))], axis=1)
    main, dt = _norm_proj(x, _row(norm_mix), w_all.astype(BF16), jnp.zeros((1, LANES), F32), batch=batch,
                          n_main=n_main, n_planes=0, n_extra=LANES, extra_sigmoid=False)
    y = _ssd(main, dt, conv_w, _row(conv_b), _row(dt_bias, LANES), _row(a_log, LANES),
             _row(jnp.repeat(d_skip, SSM_HEADDIM)), _row(norm_g), batch=batch, d_inner=d_inner)
    return _out_proj([y], w_out.astype(BF16), x)


def kernel(x, mem, norm_mix, norm_cross, norm_mem, norm_ffn, ev_w_in, ev_rg_conv_w, ev_rg_conv_b, ev_rg_wa, ev_rg_ba, ev_rg_wx, ev_rg_bx, ev_rg_lambda, ev_nsa_gate_b, ev_q_norm, ev_k_norm, ev_cmp_pos, ev_cmp_k_w1, ev_cmp_k_w2, ev_cmp_v_w1, ev_cmp_v_w2, ev_w_out, od_w_in, od_conv_w, od_conv_b, od_dt_bias, od_a_log, od_d_skip, od_norm, od_w_out, x_wq, x_wkv, x_q_norm, x_k_norm, x_wo, ff_w13, ff_w2, moe_router, moe_w13, moe_w2):
    batch, seq, d = x.shape
    depth = norm_mix.shape[0]
    xf = x.reshape(batch * seq, d)
    for layer in range(depth):
        i = layer // 2
        if layer % 2 == 0:
            xf = _even_layer(xf, batch, norm_mix[layer], ev_w_in[i], ev_rg_conv_w[i], ev_rg_conv_b[i], ev_rg_wa[i],
                             ev_rg_ba[i], ev_rg_wx[i], ev_rg_bx[i], ev_rg_lambda[i], ev_nsa_gate_b[i], ev_q_norm[i],
                             ev_k_norm[i], ev_cmp_pos[i], ev_cmp_k_w1[i], ev_cmp_k_w2[i], ev_cmp_v_w1[i],
                             ev_cmp_v_w2[i], ev_w_out[i])
        else:
            xf = _odd_layer(xf, batch, norm_mix[layer], od_w_in[i], od_conv_w[i], od_conv_b[i], od_dt_bias[i],
                            od_a_log[i], od_d_skip[i], od_norm[i], od_w_out[i])
        k, v = _mem_kv(mem, _row(norm_mem[layer]), x_wkv[layer].astype(BF16), _row(x_k_norm[layer]))
        xf = _cross_attn(xf, _row(norm_cross[layer]), x_wq[layer].astype(BF16), _row(x_q_norm[layer]), k, v,
                         x_wo[layer].astype(BF16))
        if layer % 2 == 0:
            xf = _swiglu(xf, _row(norm_ffn[layer]), ff_w13[i].astype(BF16), ff_w2[i].astype(BF16))
        else:
            xf = _moe(xf, _row(norm_ffn[layer]), moe_router[i], moe_w13[i], moe_w2[i])
    return xf.reshape(batch, seq, d)
```

```python
import functools
import math

import jax
import jax.numpy as jnp
import numpy as np
from jax import lax
from jax.experimental import pallas as pl
from jax.experimental.pallas import tpu as pltpu

F32 = jnp.float32
BF16 = jnp.bfloat16
I32 = jnp.int32

EPS = 1e-6
CONV_W = 4
RG_BLOCKS = 8
RG_C = 8.0
NSA_HEADS = 8
NSA_GROUPS = 2
NSA_REP = NSA_HEADS // NSA_GROUPS
NSA_DK = 128
CMP_LEN = 32
CMP_STRIDE = 16
SLC_LEN = 64
SLC_SHIFT = 6
SLC_WIDE = 4
SLC_TOPN = 16
WINDOW = 512
FORCE_BONUS = 100.0
SSM_HEADDIM = 64
SSM_GROUPS = 4
SSM_STATE = 128
SSD_CHUNK = 128
X_HEADS = 4
X_HEADDIM = 128
N_EXPERTS = 8
EXPERT_CHUNK = 256
DMA_UNROLL = 8

LANES = 128
SUBLANES = 8
VMEM_LIMIT_BYTES = 56 * 1024 * 1024
NEG = -1e30
SEL_BIAS = float(2 ** 20)
LOG2E = math.log2(math.e)
FIXED_SHIFT_MAX = 56.0

NT_DIMS = (((1,), (1,)), ((), ()))


def _params(*sem):
    return pltpu.CompilerParams(dimension_semantics=sem, vmem_limit_bytes=VMEM_LIMIT_BYTES)


def _dot(a, b):
    return jnp.dot(a, b, preferred_element_type=F32)


def _dot_nt(a, b):
    return lax.dot_general(a, b, NT_DIMS, preferred_element_type=F32)


def _rms(x, g):
    return x * lax.rsqrt(jnp.mean(x * x, axis=-1, keepdims=True) + EPS) * g


def _sigmoid(x):
    return 1.0 / (1.0 + jnp.exp(-x))


def _silu(x):
    return x * _sigmoid(x)


def _gelu_tanh(x):
    c = math.sqrt(2.0 / math.pi)
    return 0.5 * x * (1.0 + jnp.tanh(c * (x + 0.044715 * (x * x * x))))


def _softplus(x):
    return jnp.maximum(x, 0.0) + jnp.log(1.0 + jnp.exp(-jnp.abs(x)))


def _causal_conv(xb, x, tail, w_ref, b_ref):
    n = xb.shape[0]
    delay = lax.broadcasted_iota(I32, (n, n), 0) - lax.broadcasted_iota(I32, (n, n), 1)
    r8 = lax.broadcasted_iota(I32, (SUBLANES, 1), 0)
    y = b_ref[...] + w_ref[CONV_W - 1:CONV_W, :] * x
    head = jnp.zeros(tail.shape, F32)
    for k in range(1, CONV_W):
        wk = w_ref[CONV_W - 1 - k:CONV_W - k, :]
        y = y + wk * _dot(jnp.where(delay == k, 1.0, 0.0).astype(BF16), xb)
        head = head + wk * jnp.where(r8 < k, pltpu.roll(tail, k, 0), 0.0)
    return jnp.concatenate([y[0:SUBLANES] + head, y[SUBLANES:]], axis=0)


def _norm_proj_kernel(x_ref, g_ref, w_ref, eb_ref, *out_refs, n_main, n_planes, extra_sigmoid):
    h = _rms(x_ref[...], g_ref[...]).astype(BF16)
    main_ref = out_refs[0]
    for c0 in range(0, n_main, 512):
        main_ref[:, c0:c0 + 512] = _dot(h, w_ref[:, c0:c0 + 512]).astype(main_ref.dtype)
    col = n_main
    oi = 1
    if n_planes:
        kv_ref = out_refs[oi]
        oi += 1
        for p0 in range(0, n_planes, 4):
            r = _dot(h, w_ref[:, col:col + 512])
            for p in range(4):
                kv_ref[p0 + p] = r[:, p * LANES:(p + 1) * LANES].astype(kv_ref.dtype)
            col += 512
    ex_ref = out_refs[oi]
    n_extra = ex_ref.shape[1]
    e = _dot(h, w_ref[:, col:col + n_extra]) + eb_ref[...]
    ex_ref[...] = _sigmoid(e) if extra_sigmoid else e


def _norm_proj(x, g, w, eb, *, batch, n_main, n_planes, n_extra, extra_sigmoid, tm=512):
    t, d = x.shape
    seq = t // batch
    nt = seq // tm
    out_shape = [jax.ShapeDtypeStruct((t, n_main), BF16)]
    out_specs = [pl.BlockSpec((tm, n_main), lambda i: (i, 0))]
    if n_planes:
        out_shape.append(jax.ShapeDtypeStruct((batch, n_planes, seq, LANES), BF16))
        out_specs.append(pl.BlockSpec((None, n_planes, tm, LANES), lambda i: (i // nt, 0, i % nt, 0)))
    out_shape.append(jax.ShapeDtypeStruct((t, n_extra), F32))
    out_specs.append(pl.BlockSpec((tm, n_extra), lambda i: (i, 0)))
    kern = functools.partial(_norm_proj_kernel, n_main=n_main, n_planes=n_planes, extra_sigmoid=extra_sigmoid)
    return pl.pallas_call(
        kern,
        grid=(t // tm,),
        in_specs=[
            pl.BlockSpec((tm, d), lambda i: (i, 0)),
            pl.BlockSpec((1, d), lambda i: (0, 0)),
            pl.BlockSpec(w.shape, lambda i: (0, 0)),
            pl.BlockSpec((1, n_extra), lambda i: (0, 0)),
        ],
        out_specs=out_specs,
        out_shape=out_shape,
        compiler_params=_params("parallel"),
        name="norm_proj",
    )(x, g, w, eb)


def _rglru_kernel(rx_ref, rg_ref, cw_ref, cb_ref, wa_ref, ba_ref, wx_ref, bx_ref, lam_ref, o_ref, tail_ref, h_ref):
    tc, c = rx_ref.shape

    @pl.when(pl.program_id(1) == 0)
    def _():
        tail_ref[...] = jnp.zeros_like(tail_ref)
        h_ref[...] = jnp.zeros_like(h_ref)

    xb = rx_ref[...]
    x = xb.astype(F32)
    xc = _causal_conv(xb, x, tail_ref[...], cw_ref, cb_ref)
    tail_ref[...] = x[tc - 8:tc, :]

    bw = c // RG_BLOCKS
    ra, rx = [], []
    for blk in range(RG_BLOCKS):
        xb = xc[:, blk * bw:(blk + 1) * bw].astype(BF16)
        ra.append(_dot(xb, wa_ref[blk]))
        rx.append(_dot(xb, wx_ref[blk]))
    r = _sigmoid(jnp.concatenate(ra, axis=1) + ba_ref[...])
    ig = _sigmoid(jnp.concatenate(rx, axis=1) + bx_ref[...])
    log_a = (-RG_C) * r * _softplus(-lam_ref[...])
    a = jnp.exp(log_a)
    z = 1.0 - a * a
    u = jnp.where(z > 0.0, z * lax.rsqrt(z), 0.0) * (ig * xc)

    in_group = lax.broadcasted_iota(I32, (tc, 1), 0) & (SUBLANES - 1)
    d = 1
    while d < SUBLANES:
        keep = in_group >= d
        a_sh = jnp.where(keep, pltpu.roll(a, d, 0), 1.0)
        u_sh = jnp.where(keep, pltpu.roll(u, d, 0), 0.0)
        u = a * u_sh + u
        a = a * a_sh
        d *= 2
    carry = h_ref[SUBLANES - 1:SUBLANES, :]
    groups = []
    for g0 in range(0, tc, SUBLANES):
        hg = u[g0:g0 + SUBLANES, :] + a[g0:g0 + SUBLANES, :] * carry
        carry = hg[SUBLANES - 1:SUBLANES, :]
        groups.append(hg)
    h_ref[...] = groups[-1]
    o_ref[...] = (_gelu_tanh(rg_ref[...].astype(F32)) * jnp.concatenate(groups, axis=0)).astype(o_ref.dtype)


def _rglru(main, cw, cb, wa, ba, wx, bx, lam, *, batch, tc=256):
    t = main.shape[0]
    c = cw.shape[1]
    nt = t // batch // tc
    vec = pl.BlockSpec((1, c), lambda b, i: (0, 0))
    blk = pl.BlockSpec(wa.shape, lambda b, i: (0, 0, 0))
    return pl.pallas_call(
        _rglru_kernel,
        grid=(batch, nt),
        in_specs=[
            pl.BlockSpec((tc, c), lambda b, i: (b * nt + i, 0)),
            pl.BlockSpec((tc, c), lambda b, i: (b * nt + i, 1)),
            pl.BlockSpec((CONV_W, c), lambda b, i: (0, 0)),
            vec, blk, vec, blk, vec, vec,
        ],
        out_specs=pl.BlockSpec((tc, c), lambda b, i: (b * nt + i, 0)),
        out_shape=jax.ShapeDtypeStruct((t, c), BF16),
        scratch_shapes=[pltpu.VMEM((8, c), F32), pltpu.VMEM((8, c), F32)],
        compiler_params=_params("parallel", "arbitrary"),
        name="rglru",
    )(main, main, cw, cb, wa, ba, wx, bx, lam)


def _nsa_kprep_kernel(ks_ref, vs_ref, kw_ref, vw_ref, kn_ref, kp_ref, vp_ref, kwn_ref, vwp_ref):
    tk = ks_ref.shape[0]
    ks = _rms(ks_ref[...].astype(F32), kn_ref[1:2, :]).astype(BF16)
    t0 = pl.program_id(2) * tk
    blk = jnp.right_shift(t0 + lax.broadcasted_iota(I32, (tk, LANES), 0), SLC_SHIFT)
    onehot = jnp.where(blk == lax.broadcasted_iota(I32, (tk, LANES), 1), 1.0, 0.0).astype(BF16)
    kp_ref[...] = jnp.concatenate([ks, onehot], axis=1)
    ones = jnp.ones((tk, LANES), BF16)
    vp_ref[...] = jnp.concatenate([vs_ref[...], ones], axis=1)
    vwp_ref[...] = jnp.concatenate([vw_ref[...], ones], axis=1)
    kwn_ref[...] = _rms(kw_ref[...].astype(F32), kn_ref[2:3, :]).astype(BF16)


def _nsa_kprep(planes, k_norm, *, tk=512):
    b, _, seq, _ = planes.shape
    g = NSA_GROUPS
    plane = lambda p0: pl.BlockSpec((None, None, tk, LANES), lambda bi, gi, i: (bi, p0 + gi, i, 0))
    wide = pl.BlockSpec((None, None, tk, 2 * LANES), lambda bi, gi, i: (bi, gi, i, 0))
    return pl.pallas_call(
        _nsa_kprep_kernel,
        grid=(b, g, seq // tk),
        in_specs=[plane(4), plane(6), plane(8), plane(10), pl.BlockSpec((8, LANES), lambda bi, gi, i: (0, 0))],
        out_specs=[wide, wide, pl.BlockSpec((None, None, tk, LANES), lambda bi, gi, i: (bi, gi, i, 0)), wide],
        out_shape=[
            jax.ShapeDtypeStruct((b, g, seq, 2 * LANES), BF16),
            jax.ShapeDtypeStruct((b, g, seq, 2 * LANES), BF16),
            jax.ShapeDtypeStruct((b, g, seq, LANES), BF16),
            jax.ShapeDtypeStruct((b, g, seq, 2 * LANES), BF16),
        ],
        compiler_params=_params("parallel", "parallel", "parallel"),
        name="nsa_kprep",
    )(planes, planes, planes, planes, k_norm)


def _nsa_compress_kernel(xk_ref, xv_ref, pos_ref, kw1_ref, kw2_ref, vw1_ref, vw2_ref, kn_ref, ov_ref, kc_ref, vc_ref):
    n, half = xk_ref.shape
    last = lax.broadcasted_iota(I32, (n, 1), 0) == n - 1
    pos = pos_ref[...]

    def compress(x_ref, w1_ref, w2_ref):
        x = x_ref[...]
        y0 = _dot(x, w1_ref[0:half, :])
        y1 = _dot(x, w1_ref[half:2 * half, :])
        y1_next = jnp.where(last, 0.0, pltpu.roll(y1, n - 1, 0))
        const = _dot(pos, w1_ref[...])[0:1, :]
        hid = _gelu_tanh(y0 + y1_next + const)
        return _dot(hid.astype(BF16), w2_ref[...])

    kc_ref[...] = _rms(compress(xk_ref, kw1_ref, kw2_ref), kn_ref[0:1, :]).astype(BF16)
    vc = compress(xv_ref, vw1_ref, vw2_ref).astype(BF16)
    vc_ref[...] = jnp.concatenate([vc, jnp.ones((n, LANES), BF16), ov_ref[...]], axis=1)


def _nsa_compress(planes, pos_flat, kw1, kw2, vw1, vw2, k_norm, overlap):
    b, _, seq, _ = planes.shape
    g = NSA_GROUPS
    n = seq // CMP_STRIDE
    half = CMP_STRIDE * LANES
    grouped = planes[:, :2 * g].reshape(b, 2 * g, n, half)
    full = lambda a: pl.BlockSpec(a.shape, lambda bi, gi: (0,) * a.ndim)
    out_spec = lambda w: pl.BlockSpec((None, None, n, w), lambda bi, gi: (bi, gi, 0, 0))
    out_shape = lambda w: jax.ShapeDtypeStruct((b, g, n, w), BF16)
    return pl.pallas_call(
        _nsa_compress_kernel,
        grid=(b, g),
        in_specs=[
            pl.BlockSpec((None, None, n, half), lambda bi, gi: (bi, gi, 0, 0)),
            pl.BlockSpec((None, None, n, half), lambda bi, gi: (bi, 2 + gi, 0, 0)),
            full(pos_flat), full(kw1), full(kw2), full(vw1), full(vw2), full(k_norm), full(overlap),
        ],
        out_specs=[out_spec(LANES), out_spec(3 * LANES)],
        out_shape=[out_shape(LANES), out_shape(3 * LANES)],
        compiler_params=_params("parallel", "parallel"),
        name="nsa_compress",
    )(grouped, grouped, pos_flat, kw1, kw2, vw1, vw2, k_norm, overlap)


def _nsa_cw_kernel(shift_ref, *refs, qn, n_sel):
    fixed = jnp.maximum(shift_ref[1], shift_ref[2]) <= FIXED_SHIFT_MAX
    pl.when(fixed)(functools.partial(_nsa_cw_body, shift_ref, *refs, qn=qn, n_sel=n_sel, fixed=True))
    pl.when(jnp.logical_not(fixed))(functools.partial(_nsa_cw_body, shift_ref, *refs, qn=qn, n_sel=n_sel, fixed=False))


def _nsa_cw_body(shift_ref, q_ref, kc_ref, vc_ref, kw_ref, vw_ref, gt_ref, qn_ref, wb_ref, qp_ref, o_ref, *, qn, n_sel,
                 fixed):
    t0 = pl.program_id(2) * qn
    rep = NSA_REP
    rows = rep * qn
    scale = NSA_DK ** -0.5 * LOG2E
    qf = q_ref[...].astype(F32)
    heads = []
    for r in range(rep):
        qh = _rms(qf[:, r * LANES:(r + 1) * LANES], qn_ref[...]) * scale
        heads.append(qh.astype(BF16))
    qs = jnp.concatenate(heads, axis=0)
    trow = t0 + (lax.broadcasted_iota(I32, (rows, 1), 0) & (qn - 1))

    def attend(sm, vx, shift):
        if not fixed:
            sm = sm - jnp.maximum(jnp.max(sm, axis=-1, keepdims=True), -2.0 * shift)
        return _dot(jnp.exp2(sm).astype(BF16), vx)

    n_cmp = kc_ref.shape[0]
    visible = lax.broadcasted_iota(I32, (1, n_cmp), 1) * CMP_STRIDE + (CMP_LEN - 1) <= trow
    r_cmp = attend(jnp.where(visible, _dot_nt(qs, kc_ref[...]), NEG) - shift_ref[1], vc_ref[...], shift_ref[1])
    inv = 1.0 / jnp.maximum(r_cmp[:, LANES:2 * LANES], 1e-30)
    o_cmp = r_cmp[:, 0:LANES] * inv
    imp_h = r_cmp[:, 2 * LANES:3 * LANES] * inv
    imp = imp_h[0:qn]
    for r in range(1, rep):
        imp = imp + imp_h[r * qn:(r + 1) * qn]

    imp_t = imp.T
    jj = lax.broadcasted_iota(I32, imp_t.shape, 0).astype(F32)
    cur = jnp.right_shift(t0 + lax.broadcasted_iota(I32, imp_t.shape, 1), SLC_SHIFT).astype(F32)
    forced = (jj == 0.0) | (jj == cur) | (jj == cur - 1.0)
    work = jnp.where(jj <= cur, imp_t + jnp.where(forced, FORCE_BONUS, 0.0), -1.0)
    bias_t = jnp.full(imp_t.shape, -SEL_BIAS, F32)
    shift = -shift_ref[0]
    for _ in range(n_sel):
        m = jnp.max(work, axis=0, keepdims=True)
        idx = jnp.min(jnp.where(work == m, jj, float(LANES)), axis=0, keepdims=True)
        pick = jj == idx
        bias_t = jnp.where(pick, shift, bias_t)
        work = jnp.where(pick, -2.0, work)
    bias = bias_t.T.astype(BF16)
    for r in range(rep):
        qp_ref[r] = jnp.concatenate([heads[r], bias], axis=1)

    span = WINDOW + qn
    start = pl.multiple_of(jnp.maximum(t0 - WINDOW, 0), qn)
    s = _dot_nt(qs, kw_ref[pl.ds(start, span), :]).reshape(rep, qn, span) + wb_ref[...]
    r_win = attend(s.reshape(rows, span), vw_ref[pl.ds(start, span), :], shift_ref[2])
    o_win = r_win[:, 0:LANES] / jnp.maximum(r_win[:, LANES:2 * LANES], 1e-30)

    gt = gt_ref[...]
    for r in range(rep):
        sl = slice(r * qn, (r + 1) * qn)
        o = gt[:, 3 * r:3 * r + 1] * o_cmp[sl] + gt[:, 3 * r + 2:3 * r + 3] * o_win[sl]
        o_ref[:, r * LANES:(r + 1) * LANES] = o.astype(o_ref.dtype)


def _nsa_cw(shift, main, kcmp, vcmp, kwn, vwp, gates, q_norm, *, qn=256):
    b, g, seq, _ = kwn.shape
    t = main.shape[0]
    nq = seq // qn
    rep = NSA_REP
    gw = rep * LANES
    q_blk0 = (main.shape[1] - NSA_HEADS * NSA_DK) // gw
    n_cmp = kcmp.shape[2]
    n_case = WINDOW // qn + 1
    span = WINDOW + qn
    in_window = []
    for case in range(n_case):
        t0 = case * qn
        diff = (t0 + np.arange(qn)[:, None]) - (max(t0 - WINDOW, 0) + np.arange(span)[None, :])
        in_window.append((diff >= 0) & (diff < WINDOW))
    window_bias = jnp.where(jnp.asarray(np.stack(in_window)), -shift[2], NEG).astype(F32)
    kern = functools.partial(_nsa_cw_kernel, qn=qn, n_sel=min(SLC_TOPN, seq // SLC_LEN))
    return pl.pallas_call(
        kern,
        grid=(b, g, nq),
        in_specs=[
            pl.BlockSpec(memory_space=pltpu.SMEM),
            pl.BlockSpec((qn, gw), lambda bi, gi, i: (bi * nq + i, q_blk0 + gi)),
            pl.BlockSpec((None, None, n_cmp, LANES), lambda bi, gi, i: (bi, gi, 0, 0)),
            pl.BlockSpec((None, None, n_cmp, 3 * LANES), lambda bi, gi, i: (bi, gi, 0, 0)),
            pl.BlockSpec((None, None, seq, LANES), lambda bi, gi, i: (bi, gi, 0, 0)),
            pl.BlockSpec((None, None, seq, 2 * LANES), lambda bi, gi, i: (bi, gi, 0, 0)),
            pl.BlockSpec((qn, LANES), lambda bi, gi, i: (bi * nq + i, gi)),
            pl.BlockSpec((1, LANES), lambda bi, gi, i: (0, 0)),
            pl.BlockSpec((None, qn, span), lambda bi, gi, i: (jnp.minimum(i, n_case - 1), 0, 0)),
        ],
        out_specs=[
            pl.BlockSpec((None, None, rep, qn, 2 * LANES), lambda bi, gi, i: (bi, gi, 0, i, 0)),
            pl.BlockSpec((qn, gw), lambda bi, gi, i: (bi * nq + i, gi)),
        ],
        out_shape=[
            jax.ShapeDtypeStruct((b, g, rep, seq, 2 * LANES), BF16),
            jax.ShapeDtypeStruct((t, NSA_HEADS * NSA_DK), BF16),
        ],
        compiler_params=_params("parallel", "parallel", "parallel"),
        name="nsa_cmp_win",
    )(shift, main, kcmp, vcmp, kwn, vwp, gates, q_norm, window_bias)


def _nsa_slc_kernel(shift_ref, qp_ref, kp_ref, vp_ref, gt_ref, ocw_ref, o_ref, m_ref, acc_ref, *, qn, tk):
    t0 = pl.program_id(2) * qn
    rep = NSA_REP
    rows = rep * qn
    last = (t0 + qn - 1) // tk
    acc_ref[...] = jnp.zeros_like(acc_ref)

    def scores(k0, width, causal):
        qp = qp_ref[...].reshape(rows, qp_ref.shape[2])
        s = _dot_nt(qp, kp_ref[pl.ds(k0, width), :])
        if causal:
            trow = t0 + (lax.broadcasted_iota(I32, (rows, 1), 0) & (qn - 1))
            s = jnp.where(k0 + lax.broadcasted_iota(I32, (1, width), 1) <= trow, s, -SEL_BIAS)
        return s, vp_ref[pl.ds(k0, width), :]

    def fixed_shift_step(k0, width, causal):
        s, v = scores(k0, width, causal)
        acc_ref[...] += _dot(jnp.exp2(s).astype(BF16), v)

    def running_max_step(k0, width, causal):
        s, v = scores(k0, width, causal)
        m_old = m_ref[...]
        m_new = jnp.maximum(m_old, jnp.max(s, axis=-1, keepdims=True))
        acc_ref[...] = jnp.exp2(m_old - m_new) * acc_ref[...] + _dot(jnp.exp2(s - m_new).astype(BF16), v)
        m_ref[...] = m_new

    def sweep(step):
        wide = SLC_WIDE * tk
        n_wide = last // SLC_WIDE
        lax.fori_loop(0, n_wide, lambda j, c: (step(pl.multiple_of(j * wide, wide), wide, False), c)[1], 0)
        lax.fori_loop(n_wide * SLC_WIDE, last, lambda j, c: (step(pl.multiple_of(j * tk, tk), tk, False), c)[1], 0)
        step(pl.multiple_of(last * tk, tk), tk, True)

    fixed = shift_ref[0] <= FIXED_SHIFT_MAX

    @pl.when(fixed)
    def _():
        sweep(fixed_shift_step)

    @pl.when(jnp.logical_not(fixed))
    def _():
        m_ref[...] = jnp.full(m_ref.shape, NEG, F32)
        sweep(running_max_step)

    o_slc = acc_ref[:, 0:LANES] / jnp.maximum(acc_ref[:, LANES:2 * LANES], 1e-30)
    gt = gt_ref[...]
    for r in range(rep):
        o = ocw_ref[:, r * LANES:(r + 1) * LANES].astype(F32) + gt[:, 3 * r + 1:3 * r + 2] * o_slc[r * qn:(r + 1) * qn]
        o_ref[:, r * LANES:(r + 1) * LANES] = o.astype(o_ref.dtype)


def _nsa_slc(shift, qp, kp, vp, gates, ocw, *, qn=256, tk=512):
    b, g, rep, seq, dqk = qp.shape
    t = ocw.shape[0]
    nq = seq // qn
    gw = rep * LANES
    tk = min(tk, seq)
    kern = functools.partial(_nsa_slc_kernel, qn=qn, tk=tk)
    return pl.pallas_call(
        kern,
        grid=(b, g, nq),
        in_specs=[
            pl.BlockSpec(memory_space=pltpu.SMEM),
            pl.BlockSpec((None, None, rep, qn, dqk), lambda bi, gi, i: (bi, gi, 0, i, 0)),
            pl.BlockSpec((None, None, seq, dqk), lambda bi, gi, i: (bi, gi, 0, 0)),
            pl.BlockSpec((None, None, seq, 2 * LANES), lambda bi, gi, i: (bi, gi, 0, 0)),
            pl.BlockSpec((qn, LANES), lambda bi, gi, i: (bi * nq + i, gi)),
            pl.BlockSpec((qn, gw), lambda bi, gi, i: (bi * nq + i, gi)),
        ],
        out_specs=pl.BlockSpec((qn, gw), lambda bi, gi, i: (bi * nq + i, gi)),
        out_shape=jax.ShapeDtypeStruct((t, NSA_HEADS * NSA_DK), BF16),
        scratch_shapes=[pltpu.VMEM((rep * qn, 1), F32), pltpu.VMEM((rep * qn, 2 * LANES), F32)],
        compiler_params=_params("parallel", "parallel", "parallel"),
        name="nsa_selected",
    )(shift, qp, kp, vp, gates, ocw)


def _mem_kv_kernel(mem_ref, g_ref, wkv_ref, kn_ref, k_ref, v_ref):
    memn = _rms(mem_ref[...], g_ref[...]).astype(BF16)
    kv = _dot(memn, wkv_ref[...])
    inner = k_ref.shape[1]
    for h in range(inner // X_HEADDIM):
        sl = slice(h * X_HEADDIM, (h + 1) * X_HEADDIM)
        k_ref[:, sl] = _rms(kv[:, sl], kn_ref[...]).astype(BF16)
    v_ref[...] = kv[:, inner:].astype(BF16)


def _mem_kv(mem, g, wkv, kn):
    b, m, d = mem.shape
    inner = wkv.shape[1] // 2
    out_spec = pl.BlockSpec((None, m, inner), lambda bi: (bi, 0, 0))
    out_shape = jax.ShapeDtypeStruct((b, m, inner), BF16)
    return pl.pallas_call(
        _mem_kv_kernel,
        grid=(b,),
        in_specs=[
            pl.BlockSpec((None, m, d), lambda bi: (bi, 0, 0)),
            pl.BlockSpec((1, d), lambda bi: (0, 0)),
            pl.BlockSpec(wkv.shape, lambda bi: (0, 0)),
            pl.BlockSpec((1, X_HEADDIM), lambda bi: (0, 0)),
        ],
        out_specs=[out_spec, out_spec],
        out_shape=[out_shape, out_shape],
        compiler_params=_params("parallel"),
        name="mem_kv",
    )(mem, g, wkv, kn)


def _cross_attn_kernel(*refs, n_in):
    a_refs = refs[:n_in]
    w_ref, x_ref, g_ref, wq_ref, qn_ref, k_ref, v_ref, wo_ref, o_ref = refs[n_in:]
    x = x_ref[...]
    k0 = 0
    for a_ref in a_refs:
        x = x + _dot(a_ref[...], w_ref[k0:k0 + a_ref.shape[1], :])
        k0 += a_ref.shape[1]
    q = _dot(_rms(x, g_ref[...]).astype(BF16), wq_ref[...])
    scale = X_HEADDIM ** -0.5
    outs = []
    for h in range(q.shape[1] // X_HEADDIM):
        sl = slice(h * X_HEADDIM, (h + 1) * X_HEADDIM)
        qh = (_rms(q[:, sl], qn_ref[...]) * scale).astype(BF16)
        s = _dot_nt(qh, k_ref[:, sl])
        e = jnp.exp(s - jnp.max(s, axis=-1, keepdims=True))
        p = e / jnp.sum(e, axis=-1, keepdims=True)
        outs.append(_dot(p.astype(BF16), v_ref[:, sl]))
    o = jnp.concatenate(outs, axis=1).astype(BF16)
    o_ref[...] = x + _dot(o, wo_ref[...])


def _cross_attn(acts, w_out, x, g, wq, qn, k, v, wo, *, tm=512):
    t, d = x.shape
    b, m, inner = k.shape
    nt = t // b // tm
    return pl.pallas_call(
        functools.partial(_cross_attn_kernel, n_in=len(acts)),
        grid=(b, nt),
        in_specs=[pl.BlockSpec((tm, a.shape[1]), lambda bi, i: (bi * nt + i, 0)) for a in acts] + [
            pl.BlockSpec(w_out.shape, lambda bi, i: (0, 0)),
            pl.BlockSpec((tm, d), lambda bi, i: (bi * nt + i, 0)),
            pl.BlockSpec((1, d), lambda bi, i: (0, 0)),
            pl.BlockSpec(wq.shape, lambda bi, i: (0, 0)),
            pl.BlockSpec((1, X_HEADDIM), lambda bi, i: (0, 0)),
            pl.BlockSpec((None, m, inner), lambda bi, i: (bi, 0, 0)),
            pl.BlockSpec((None, m, inner), lambda bi, i: (bi, 0, 0)),
            pl.BlockSpec(wo.shape, lambda bi, i: (0, 0)),
        ],
        out_specs=pl.BlockSpec((tm, d), lambda bi, i: (bi * nt + i, 0)),
        out_shape=jax.ShapeDtypeStruct((t, d), F32),
        compiler_params=_params("parallel", "parallel"),
        name="cross_attn",
    )(*acts, w_out, x, g, wq, qn, k, v, wo)


def _swiglu_kernel(x_ref, g_ref, wu_ref, wg_ref, w2_ref, o_ref, h_ref):
    @pl.when(pl.program_id(1) == 0)
    def _():
        h_ref[...] = _rms(x_ref[...], g_ref[...]).astype(BF16)
        o_ref[...] = x_ref[...]

    h = h_ref[...]
    act = (_silu(_dot(h, wg_ref[...])) * _dot(h, wu_ref[...])).astype(BF16)
    for c0 in range(0, o_ref.shape[1], EXPERT_CHUNK):
        c = slice(c0, c0 + EXPERT_CHUNK)
        o_ref[:, c] += _dot(act, w2_ref[:, c])


def _swiglu(x, g, w13, w2, *, tm=512, tf=1408):
    t, d = x.shape
    ff = w2.shape[0]
    nf = ff // tf
    return pl.pallas_call(
        _swiglu_kernel,
        grid=(t // tm, nf),
        in_specs=[
            pl.BlockSpec((tm, d), lambda i, f: (i, 0)),
            pl.BlockSpec((1, d), lambda i, f: (0, 0)),
            pl.BlockSpec((d, tf), lambda i, f: (0, f)),
            pl.BlockSpec((d, tf), lambda i, f: (0, nf + f)),
            pl.BlockSpec((tf, d), lambda i, f: (f, 0)),
        ],
        out_specs=pl.BlockSpec((tm, d), lambda i, f: (i, 0)),
        out_shape=jax.ShapeDtypeStruct((t, d), F32),
        scratch_shapes=[pltpu.VMEM((tm, d), BF16)],
        compiler_params=_params("parallel", "arbitrary"),
        name="swiglu",
    )(x, g, w13, w13, w2)


def _split3(x):
    a = x.astype(BF16)
    r = x - a.astype(F32)
    b = r.astype(BF16)
    c = (r - b.astype(F32)).astype(BF16)
    return a, b, c


def _ssd_kernel(z0_ref, z1_ref, x0_ref, x1_ref, bc_ref, dt_ref, cw_ref, cb_ref, dtb_ref, alog_ref, dskip_ref,
                ng_ref, o_ref, tail_ref, state_ref):
    q = x0_ref.shape[0]
    d_inner = o_ref.shape[1]
    gn = SSM_GROUPS * SSM_STATE
    hpg = d_inner // SSM_HEADDIM // SSM_GROUPS

    @pl.when(pl.program_id(1) == 0)
    def _():
        tail_ref[...] = jnp.zeros_like(tail_ref)
        state_ref[...] = jnp.zeros_like(state_ref)

    raw_b = jnp.concatenate([x0_ref[...], x1_ref[...], bc_ref[...]], axis=1)
    raw = raw_b.astype(F32)
    xbc = _silu(_causal_conv(raw_b, raw, tail_ref[...], cw_ref, cb_ref))
    tail_ref[...] = raw[q - 8:q, :]
    xs = xbc[:, :d_inner]
    bm = xbc[:, d_inner:d_inner + gn]
    cm = xbc[:, d_inner + gn:]

    dt = _softplus(dt_ref[...] + dtb_ref[...])
    a = dt * (-jnp.exp(alog_ref[...]))
    ri = lax.broadcasted_iota(I32, (q, q), 0)
    ci = lax.broadcasted_iota(I32, (q, q), 1)
    causal = ci <= ri
    tri = jnp.where(causal, 1.0, 0.0).astype(BF16)
    a_cs = sum(_dot(tri, part) for part in _split3(a))
    a_cs_t = a_cs.T
    dt_t = dt.T
    lane = lax.broadcasted_iota(I32, (1, LANES), 1)
    lo = lane < SSM_HEADDIM

    y_parts = []
    for g in range(SSM_GROUPS):
        cg = cm[:, g * SSM_STATE:(g + 1) * SSM_STATE].astype(BF16)
        bg = bm[:, g * SSM_STATE:(g + 1) * SSM_STATE]
        gmat = _dot_nt(cg, bg.astype(BF16))
        bg_t = bg.T
        gw = hpg * SSM_HEADDIM
        prev = state_ref[:, g * gw:(g + 1) * gw]
        y_off = _dot(cg, prev.astype(BF16))
        for pr in range(hpg // 2):
            c0 = g * gw + pr * LANES
            x_pair = xs[:, c0:c0 + LANES]
            y_pair = dskip_ref[:, c0:c0 + LANES] * x_pair
            st_pair = jnp.zeros((SSM_STATE, LANES), F32)
            decay_pair = jnp.zeros((1, LANES), F32)
            for half in range(2):
                h = g * hpg + pr * 2 + half
                sel = lo if half == 0 else jnp.logical_not(lo)
                xh = jnp.where(sel, x_pair, 0.0).astype(BF16)
                row_cs = a_cs_t[h:h + 1, :]
                col_cs = a_cs[:, h:h + 1]
                row_dt = dt_t[h:h + 1, :]
                a_last = a_cs_t[h:h + 1, q - 1:q]
                dec = jnp.exp(jnp.where(causal, col_cs - row_cs, NEG))
                y_pair = y_pair + _dot((gmat * dec * row_dt).astype(BF16), xh)
                w_row = jnp.exp(a_last - row_cs) * row_dt
                st_pair = st_pair + _dot((bg_t * w_row).astype(BF16), xh)
                y_pair = y_pair + jnp.where(sel, jnp.exp(col_cs) * y_off[:, pr * LANES:(pr + 1) * LANES], 0.0)
                decay_pair = jnp.where(sel, jnp.exp(a_last), decay_pair)
            state_ref[:, c0:c0 + LANES] = decay_pair * state_ref[:, c0:c0 + LANES] + st_pair
            y_parts.append(y_pair)
    y = jnp.concatenate(y_parts, axis=1)

    z = jnp.concatenate([z0_ref[...], z1_ref[...]], axis=1).astype(F32)
    y = y * _silu(z)
    gsz = d_inner // SSM_GROUPS
    for g in range(SSM_GROUPS):
        sl = slice(g * gsz, (g + 1) * gsz)
        o_ref[:, sl] = _rms(y[:, sl], ng_ref[:, sl]).astype(o_ref.dtype)


def _ssd(main, dt, cw, cb, dtb, alog, dskip, ng, *, batch, d_inner):
    t = main.shape[0]
    q = SSD_CHUNK
    nc = t // batch // q
    conv_ch = cw.shape[1]
    half = d_inner // 2
    col = lambda j: pl.BlockSpec((q, half), lambda b, i: (b * nc + i, j))
    vec = lambda n: pl.BlockSpec((1, n), lambda b, i: (0, 0))
    return pl.pallas_call(
        _ssd_kernel,
        grid=(batch, nc),
        in_specs=[
            col(0), col(1), col(2), col(3), col(4),
            pl.BlockSpec((q, LANES), lambda b, i: (b * nc + i, 0)),
            pl.BlockSpec((CONV_W, conv_ch), lambda b, i: (0, 0)),
            vec(conv_ch), vec(LANES), vec(LANES), vec(d_inner), vec(d_inner),
        ],
        out_specs=pl.BlockSpec((q, d_inner), lambda b, i: (b * nc + i, 0)),
        out_shape=jax.ShapeDtypeStruct((t, d_inner), BF16),
        scratch_shapes=[pltpu.VMEM((8, conv_ch), F32), pltpu.VMEM((SSM_STATE, d_inner), F32)],
        compiler_params=_params("parallel", "arbitrary"),
        name="ssd",
    )(main, main, main, main, main, dt, cw, cb, dtb, alog, dskip, ng)


def _router_kernel(x_ref, g_ref, wr_ref, h_ref, info_ref, cnt_ref, run_ref):
    tm = x_ref.shape[0]

    @pl.when(pl.program_id(0) == 0)
    def _():
        run_ref[...] = jnp.zeros_like(run_ref)

    h = _rms(x_ref[...], g_ref[...])
    h_ref[...] = h
    h_hi = h.astype(BF16)
    h_lo = (h - h_hi.astype(F32)).astype(BF16)
    w = wr_ref[...]
    w_hi = w.astype(BF16)
    w_lo = (w - w_hi.astype(F32)).astype(BF16)
    logits = _dot(h_hi, w_hi) + _dot(h_lo, w_hi) + _dot(h_hi, w_lo)
    lane = lax.broadcasted_iota(I32, (tm, LANES), 1)
    lg = jnp.where(lane < N_EXPERTS, logits, NEG)
    m1 = jnp.max(lg, axis=-1, keepdims=True)
    i1 = jnp.min(jnp.where(lg == m1, lane, LANES), axis=-1, keepdims=True)
    lg2 = jnp.where(lane == i1, NEG, lg)
    m2 = jnp.max(lg2, axis=-1, keepdims=True)
    i2 = jnp.min(jnp.where(lg2 == m2, lane, LANES), axis=-1, keepdims=True)
    e2 = jnp.exp(m2 - m1)
    w1 = 1.0 / (1.0 + e2)
    w2 = e2 / (1.0 + e2)

    hot1 = lane == i1
    hot2 = lane == i2
    hot = jnp.where(hot1 | hot2, 1.0, 0.0)
    ri = lax.broadcasted_iota(I32, (tm, tm), 0)
    ci = lax.broadcasted_iota(I32, (tm, tm), 1)
    before = jnp.where(ci < ri, 1.0, 0.0).astype(BF16)
    seen = run_ref[0:1, :] + _dot(before, hot.astype(BF16))
    rank1 = jnp.sum(jnp.where(hot1, seen, 0.0), axis=-1, keepdims=True)
    rank2 = jnp.sum(jnp.where(hot2, seen, 0.0), axis=-1, keepdims=True)
    run_ref[...] = run_ref[...] + jnp.sum(hot, axis=0, keepdims=True)
    cnt_ref[...] = run_ref[...]

    cols = [i1.astype(F32), i2.astype(F32), w1, w2, rank1, rank2]
    info = jnp.zeros((tm, LANES), F32)
    for c, v in enumerate(cols):
        info = jnp.where(lane == c, v, info)
    info_ref[...] = info


def _router(x, g, wr, *, tm=256):
    t, d = x.shape
    return pl.pallas_call(
        _router_kernel,
        grid=(t // tm,),
        in_specs=[
            pl.BlockSpec((tm, d), lambda i: (i, 0)),
            pl.BlockSpec((1, d), lambda i: (0, 0)),
            pl.BlockSpec(wr.shape, lambda i: (0, 0)),
        ],
        out_specs=[
            pl.BlockSpec((tm, d), lambda i: (i, 0)),
            pl.BlockSpec((tm, LANES), lambda i: (i, 0)),
            pl.BlockSpec((8, LANES), lambda i: (0, 0)),
        ],
        out_shape=[
            jax.ShapeDtypeStruct((t, d), F32),
            jax.ShapeDtypeStruct((t, LANES), F32),
            jax.ShapeDtypeStruct((8, LANES), F32),
        ],
        scratch_shapes=[pltpu.VMEM((8, LANES), F32)],
        compiler_params=_params("arbitrary"),
        name="moe_router",
    )(x, g, wr)


def _row_copy(src_ref, src_row, dst_ref, dst_row, sem):
    return pltpu.make_async_copy(src_ref.at[pl.ds(src_row, 1)], dst_ref.at[pl.ds(dst_row, 1)], sem)


def _dispatch_kernel(pos_ref, h_ref, init_ref, xs_ref, sem):
    del init_ref
    tt = h_ref.shape[0]

    def issue(r, c):
        _row_copy(h_ref, r, xs_ref, pos_ref[0, r], sem).start()
        _row_copy(h_ref, r, xs_ref, pos_ref[1, r], sem).start(priority=1)
        return c

    lax.fori_loop(0, tt, issue, 0, unroll=DMA_UNROLL)
    for _ in range(2):
        pltpu.make_async_copy(h_ref, xs_ref.at[pl.ds(0, tt)], sem).wait()


def _dispatch(h, pos, rows, *, tt=256):
    t, d = h.shape
    init = jnp.zeros((rows, d), h.dtype)
    return pl.pallas_call(
        _dispatch_kernel,
        grid=(t // tt,),
        in_specs=[
            pl.BlockSpec((None, 2, tt), lambda i: (i, 0, 0), memory_space=pltpu.SMEM),
            pl.BlockSpec((tt, d), lambda i: (i, 0)),
            pl.BlockSpec(memory_space=pl.ANY),
        ],
        out_specs=pl.BlockSpec(memory_space=pl.ANY),
        out_shape=jax.ShapeDtypeStruct((rows, d), h.dtype),
        scratch_shapes=[pltpu.SemaphoreType.DMA(())],
        input_output_aliases={2: 0},
        compiler_params=_params("arbitrary"),
        name="moe_dispatch",
    )(pos, h, init)


def _expert_kernel(te_ref, tv_ref, xs_ref, wu_ref, wg_ref, w2_ref, o_ref, xb_ref):
    del te_ref
    i = pl.program_id(0)
    f = pl.program_id(1)
    live = tv_ref[i] > 0

    @pl.when(f == 0)
    def _():
        o_ref[...] = jnp.zeros_like(o_ref)
        xb_ref[...] = xs_ref[...].astype(BF16)

    @pl.when(live)
    def _():
        x = xb_ref[...]
        act = (_silu(_dot(x, wg_ref[...].astype(BF16))) * _dot(x, wu_ref[...].astype(BF16))).astype(BF16)
        for c0 in range(0, o_ref.shape[1], EXPERT_CHUNK):
            c = slice(c0, c0 + EXPERT_CHUNK)
            o_ref[:, c] += _dot(act, w2_ref[:, c].astype(BF16))


def _experts(xs, w13, w2, tile_expert, tile_live, *, tm, tf=512):
    rows, d = xs.shape
    ff = w2.shape[1]
    nf = ff // tf

    def f_of(i, f, te, tv):
        return jnp.where(tv[i] > 0, f, nf - 1)

    grid_spec = pltpu.PrefetchScalarGridSpec(
        num_scalar_prefetch=2,
        grid=(rows // tm, nf),
        in_specs=[
            pl.BlockSpec((tm, d), lambda i, f, te, tv: (i, 0)),
            pl.BlockSpec((None, d, tf), lambda i, f, te, tv: (te[i], 0, f_of(i, f, te, tv))),
            pl.BlockSpec((None, d, tf), lambda i, f, te, tv: (te[i], 0, nf + f_of(i, f, te, tv))),
            pl.BlockSpec((None, tf, d), lambda i, f, te, tv: (te[i], f_of(i, f, te, tv), 0)),
        ],
        out_specs=pl.BlockSpec((tm, d), lambda i, f, te, tv: (i, 0)),
        scratch_shapes=[pltpu.VMEM((tm, d), BF16)],
    )
    return pl.pallas_call(
        _expert_kernel,
        grid_spec=grid_spec,
        out_shape=jax.ShapeDtypeStruct((rows, d), F32),
        compiler_params=_params("parallel", "arbitrary"),
        name="moe_experts",
    )(tile_expert, tile_live, xs, w13, w13, w2)


def _combine_kernel(pos_ref, x_ref, info_ref, ys_ref, o_ref, buf_ref, sem):
    tt = x_ref.shape[0]

    def issue(r, c):
        _row_copy(ys_ref, pos_ref[0, r], buf_ref.at[0], r, sem).start()
        _row_copy(ys_ref, pos_ref[1, r], buf_ref.at[1], r, sem).start(priority=1)
        return c

    lax.fori_loop(0, tt, issue, 0, unroll=DMA_UNROLL)
    for k in range(2):
        pltpu.make_async_copy(ys_ref.at[pl.ds(0, tt)], buf_ref.at[k], sem).wait()
    info = info_ref[...]
    o_ref[...] = x_ref[...] + info[:, 2:3] * buf_ref[0] + info[:, 3:4] * buf_ref[1]


def _combine(x, info, pos, ys, *, tt=256):
    t, d = x.shape
    return pl.pallas_call(
        _combine_kernel,
        grid=(t // tt,),
        in_specs=[
            pl.BlockSpec((None, 2, tt), lambda i: (i, 0, 0), memory_space=pltpu.SMEM),
            pl.BlockSpec((tt, d), lambda i: (i, 0)),
            pl.BlockSpec((tt, LANES), lambda i: (i, 0)),
            pl.BlockSpec(memory_space=pl.ANY),
        ],
        out_specs=pl.BlockSpec((tt, d), lambda i: (i, 0)),
        out_shape=jax.ShapeDtypeStruct((t, d), F32),
        scratch_shapes=[pltpu.VMEM((2, tt, d), F32), pltpu.SemaphoreType.DMA(())],
        compiler_params=_params("arbitrary"),
        name="moe_combine",
    )(pos, x, info, ys)


def _moe(x, g, router, w13, w2, *, tm=1024, tt=256):
    t, d = x.shape
    n_exp = router.shape[1]
    wr = jnp.pad(router, ((0, 0), (0, LANES - n_exp)))
    h, info, counts = _router(x, g, wr, tm=tt)

    counts = counts[0, :n_exp].astype(I32)
    seg = (counts + tm - 1) // tm * tm
    seg_end = jnp.cumsum(seg)
    seg_start = seg_end - seg
    e1 = info[:, 0].astype(I32)
    e2 = info[:, 1].astype(I32)
    pos = jnp.stack([seg_start[e1] + info[:, 4].astype(I32), seg_start[e2] + info[:, 5].astype(I32)], axis=0)
    pos = pos.reshape(2, t // tt, tt).transpose(1, 0, 2)
    rows = 2 * t + n_exp * tm
    tile_row0 = jnp.arange(rows // tm, dtype=I32) * tm
    tile_live = (tile_row0 < seg_end[-1]).astype(I32)
    tile_expert = jnp.sum((seg_end[None, :] <= tile_row0[:, None]).astype(I32), axis=1)
    tile_expert = jnp.minimum(tile_expert, n_exp - 1)
    last_live = jnp.maximum(jnp.sum(tile_live) - 1, 0)
    tile_expert = jnp.where(tile_live > 0, tile_expert, tile_expert[last_live])

    xs = _dispatch(h, pos, rows, tt=tt)
    ys = _experts(xs, w13, w2, tile_expert, tile_live, tm=tm)
    return _combine(x, info, pos, ys, tt=tt)


def _row(v, n=None):
    v = v.reshape(1, -1).astype(F32)
    if n is not None and v.shape[1] < n:
        v = jnp.pad(v, ((0, 0), (0, n - v.shape[1])))
    return v


def _overlap_matrix(seq):
    n = seq // CMP_STRIDE
    cmp_start = np.arange(n) * CMP_STRIDE
    slc_start = np.arange(LANES) * SLC_LEN
    ov = (cmp_start[:, None] <= slc_start[None, :] + SLC_LEN - 1) & (cmp_start[:, None] + CMP_LEN - 1 >= slc_start[None, :])
    ov[n - 1] = False
    return jnp.asarray(ov, dtype=BF16)


def _even_layer(x, batch, norm_mix, w_in, conv_w, conv_b, wa, ba, wx, bx, lam, gate_b, q_norm, k_norm, cmp_pos,
                ck_w1, ck_w2, cv_w1, cv_w2):
    t, d = x.shape
    seq = t // batch
    rg = wa.shape[0] * wa.shape[1]
    gdk = NSA_GROUPS * NSA_DK
    n_main = 2 * rg + NSA_HEADS * NSA_DK
    n_planes = 6 * NSA_GROUPS
    n_kv = 6 * gdk
    per_group = 3 * NSA_REP
    gate_cols = w_in[:, n_main + n_kv:].reshape(d, NSA_GROUPS, per_group)
    gate_cols = jnp.pad(gate_cols, ((0, 0), (0, 0), (0, LANES - per_group))).reshape(d, NSA_GROUPS * LANES)
    gate_bias = jnp.pad(gate_b.reshape(NSA_GROUPS, per_group), ((0, 0), (0, LANES - per_group))).reshape(1, -1)
    w_all = jnp.concatenate([w_in[:, :n_main + n_kv], gate_cols], axis=1).astype(BF16)
    main, planes, gates = _norm_proj(x, _row(norm_mix), w_all, gate_bias, batch=batch, n_main=n_main,
                                     n_planes=n_planes, n_extra=NSA_GROUPS * LANES, extra_sigmoid=True)

    rg_out = _rglru(main, conv_w, _row(conv_b), wa.astype(BF16), _row(ba), wx.astype(BF16), _row(bx), _row(lam),
                    batch=batch)

    kn = jnp.pad(k_norm, ((0, 8 - k_norm.shape[0]), (0, 0)))
    kp, vp, kwn, vwp = _nsa_kprep(planes, kn)
    k_gain = jnp.max(jnp.abs(k_norm), axis=1)[jnp.array([1, 0, 2])]
    shift = ((1.02 * LOG2E * math.sqrt(NSA_DK)) * jnp.max(jnp.abs(q_norm)) * k_gain).astype(F32)
    pos_flat = jnp.broadcast_to(cmp_pos.reshape(1, -1), (8, CMP_LEN * NSA_DK)).astype(BF16)
    kcmp, vcmp = _nsa_compress(planes, pos_flat, ck_w1.astype(BF16), ck_w2.astype(BF16), cv_w1.astype(BF16),
                               cv_w2.astype(BF16), kn, _overlap_matrix(seq))
    qp, ocw = _nsa_cw(shift, main, kcmp, vcmp, kwn, vwp, gates, _row(q_norm))
    att = _nsa_slc(shift, qp, kp, vp, gates, ocw)
    return [rg_out, att]


def _odd_layer(x, batch, norm_mix, w_in, conv_w, conv_b, dt_bias, a_log, d_skip, norm_g, d_inner):
    conv_ch = conv_w.shape[1]
    n_main = d_inner + conv_ch
    heads = dt_bias.shape[0]
    w_all = jnp.concatenate([w_in[:, :n_main], jnp.pad(w_in[:, n_main:], ((0, 0), (0, LANES - heads)))], axis=1)
    main, dt = _norm_proj(x, _row(norm_mix), w_all.astype(BF16), jnp.zeros((1, LANES), F32), batch=batch,
                          n_main=n_main, n_planes=0, n_extra=LANES, extra_sigmoid=False)
    y = _ssd(main, dt, conv_w, _row(conv_b), _row(dt_bias, LANES), _row(a_log, LANES),
             _row(jnp.repeat(d_skip, SSM_HEADDIM)), _row(norm_g), batch=batch, d_inner=d_inner)
    return [y]


def kernel(x, mem, norm_mix, norm_cross, norm_mem, norm_ffn, ev_w_in, ev_rg_conv_w, ev_rg_conv_b, ev_rg_wa, ev_rg_ba, ev_rg_wx, ev_rg_bx, ev_rg_lambda, ev_nsa_gate_b, ev_q_norm, ev_k_norm, ev_cmp_pos, ev_cmp_k_w1, ev_cmp_k_w2, ev_cmp_v_w1, ev_cmp_v_w2, ev_w_out, od_w_in, od_conv_w, od_conv_b, od_dt_bias, od_a_log, od_d_skip, od_norm, od_w_out, x_wq, x_wkv, x_q_norm, x_k_norm, x_wo, ff_w13, ff_w2, moe_router, moe_w13, moe_w2):
    batch, seq, d = x.shape
    depth = norm_mix.shape[0]
    xf = x.reshape(batch * seq, d)
    for layer in range(depth):
        i = layer // 2
        if layer % 2 == 0:
            w_out = ev_w_out[i]
            mixed = _even_layer(xf, batch, norm_mix[layer], ev_w_in[i], ev_rg_conv_w[i], ev_rg_conv_b[i], ev_rg_wa[i],
                                ev_rg_ba[i], ev_rg_wx[i], ev_rg_bx[i], ev_rg_lambda[i], ev_nsa_gate_b[i],
                                ev_q_norm[i], ev_k_norm[i], ev_cmp_pos[i], ev_cmp_k_w1[i], ev_cmp_k_w2[i],
                                ev_cmp_v_w1[i], ev_cmp_v_w2[i])
        else:
            w_out = od_w_out[i]
            mixed = _odd_layer(xf, batch, norm_mix[layer], od_w_in[i], od_conv_w[i], od_conv_b[i], od_dt_bias[i],
                               od_a_log[i], od_d_skip[i], od_norm[i], w_out.shape[0])
        k, v = _mem_kv(mem, _row(norm_mem[layer]), x_wkv[layer].astype(BF16), _row(x_k_norm[layer]))
        xf = _cross_attn(mixed, w_out.astype(BF16), xf, _row(norm_cross[layer]), x_wq[layer].astype(BF16),
                         _row(x_q_norm[layer]), k, v, x_wo[layer].astype(BF16))
        if layer % 2 == 0:
            xf = _swiglu(xf, _row(norm_ffn[layer]), ff_w13[i].astype(BF16), ff_w2[i].astype(BF16))
        else:
            xf = _moe(xf, _row(norm_ffn[layer]), moe_router[i], moe_w13[i], moe_w2[i])
    return xf.reshape(batch, seq, d)
```

```python
import functools
import math

import jax
import jax.numpy as jnp
import numpy as np
from jax import lax
from jax.experimental import pallas as pl
from jax.experimental.pallas import tpu as pltpu

F32 = jnp.float32
BF16 = jnp.bfloat16
I32 = jnp.int32

EPS = 1e-6
CONV_W = 4
RG_BLOCKS = 8
RG_C = 8.0
NSA_HEADS = 8
NSA_GROUPS = 2
NSA_REP = NSA_HEADS // NSA_GROUPS
NSA_DK = 128
CMP_LEN = 32
CMP_STRIDE = 16
SLC_LEN = 64
SLC_SHIFT = 6
SLC_WIDE = 4
N_FORCED = 3
SLC_TOPN = 16
WINDOW = 512
SSM_HEADDIM = 64
SSM_GROUPS = 4
SSM_STATE = 128
SSD_CHUNK = 128
X_HEADS = 4
X_HEADDIM = 128
N_EXPERTS = 8
EXPERT_CHUNK = 256
DMA_UNROLL = 8

LANES = 128
SUBLANES = 8
VMEM_LIMIT_BYTES = 56 * 1024 * 1024
NEG = -1e30
SEL_BIAS = float(2 ** 20)
LOG2E = math.log2(math.e)
FIXED_SHIFT_MAX = 56.0

NT_DIMS = (((1,), (1,)), ((), ()))


def _params(*sem):
    return pltpu.CompilerParams(dimension_semantics=sem, vmem_limit_bytes=VMEM_LIMIT_BYTES)


def _dot(a, b):
    return jnp.dot(a, b, preferred_element_type=F32)


def _dot_nt(a, b):
    return lax.dot_general(a, b, NT_DIMS, preferred_element_type=F32)


def _rms(x, g):
    return x * lax.rsqrt(jnp.mean(x * x, axis=-1, keepdims=True) + EPS) * g


def _sigmoid(x):
    return 1.0 / (1.0 + jnp.exp(-x))


def _silu(x):
    return x * _sigmoid(x)


def _gelu_tanh(x):
    c = math.sqrt(2.0 / math.pi)
    return 0.5 * x * (1.0 + jnp.tanh(c * (x + 0.044715 * (x * x * x))))


def _softplus(x):
    return jnp.maximum(x, 0.0) + jnp.log(1.0 + jnp.exp(-jnp.abs(x)))


def _causal_conv(xb, x, tail, w_ref, b_ref):
    n = xb.shape[0]
    delay = lax.broadcasted_iota(I32, (n, n), 0) - lax.broadcasted_iota(I32, (n, n), 1)
    r8 = lax.broadcasted_iota(I32, (SUBLANES, 1), 0)
    y = b_ref[...] + w_ref[CONV_W - 1:CONV_W, :] * x
    head = jnp.zeros(tail.shape, F32)
    for k in range(1, CONV_W):
        wk = w_ref[CONV_W - 1 - k:CONV_W - k, :]
        y = y + wk * _dot(jnp.where(delay == k, 1.0, 0.0).astype(BF16), xb)
        head = head + wk * jnp.where(r8 < k, pltpu.roll(tail, k, 0), 0.0)
    return jnp.concatenate([y[0:SUBLANES] + head, y[SUBLANES:]], axis=0)


def _norm_proj_kernel(x_ref, g_ref, w_ref, eb_ref, kn_ref, *out_refs, n_main, nsa_keys, tiles_per_seq,
                      extra_sigmoid):
    h = _rms(x_ref[...], g_ref[...]).astype(BF16)
    main_ref = out_refs[0]
    for c0 in range(0, n_main, 512):
        main_ref[:, c0:c0 + 512] = _dot(h, w_ref[:, c0:c0 + 512]).astype(main_ref.dtype)
    col = n_main
    ex_ref = out_refs[-1]
    if nsa_keys:
        cmp_ref, kp_ref, vp_ref, kwn_ref, vwp_ref = out_refs[1:6]
        tm = x_ref.shape[0]
        g = NSA_GROUPS
        gsl = lambda r, p: r[:, p * LANES:(p + 1) * LANES]
        r = _dot(h, w_ref[:, col:col + 2 * g * LANES])
        for p in range(2 * g):
            cmp_ref[p] = gsl(r, p).astype(BF16)
        col += 2 * g * LANES
        t0 = (pl.program_id(0) % tiles_per_seq) * tm
        blk = jnp.right_shift(t0 + lax.broadcasted_iota(I32, (tm, LANES), 0), SLC_SHIFT)
        onehot = jnp.where(blk == lax.broadcasted_iota(I32, (tm, LANES), 1), 1.0, 0.0).astype(BF16)
        ones = jnp.ones((tm, LANES), BF16)
        r = _dot(h, w_ref[:, col:col + 2 * g * LANES])
        for gi in range(g):
            kp_ref[gi] = jnp.concatenate([_rms(gsl(r, gi), kn_ref[1:2, :]).astype(BF16), onehot], axis=1)
            vp_ref[gi] = jnp.concatenate([gsl(r, g + gi).astype(BF16), ones], axis=1)
        col += 2 * g * LANES
        r = _dot(h, w_ref[:, col:col + 2 * g * LANES])
        for gi in range(g):
            kwn_ref[gi] = _rms(gsl(r, gi), kn_ref[2:3, :]).astype(BF16)
            vwp_ref[gi] = jnp.concatenate([gsl(r, g + gi).astype(BF16), ones], axis=1)
        col += 2 * g * LANES
    n_extra = ex_ref.shape[1]
    e = _dot(h, w_ref[:, col:col + n_extra]) + eb_ref[...]
    ex_ref[...] = _sigmoid(e) if extra_sigmoid else e


def _norm_proj(x, g, w, eb, kn, *, batch, n_main, nsa_keys, n_extra, extra_sigmoid, tm=512):
    t, d = x.shape
    seq = t // batch
    nt = seq // tm
    out_shape = [jax.ShapeDtypeStruct((t, n_main), BF16)]
    out_specs = [pl.BlockSpec((tm, n_main), lambda i: (i, 0))]
    if nsa_keys:
        for planes, width in ((2 * NSA_GROUPS, LANES), (NSA_GROUPS, 2 * LANES), (NSA_GROUPS, 2 * LANES),
                              (NSA_GROUPS, LANES), (NSA_GROUPS, 2 * LANES)):
            out_shape.append(jax.ShapeDtypeStruct((batch, planes, seq, width), BF16))
            out_specs.append(pl.BlockSpec((None, planes, tm, width), lambda i: (i // nt, 0, i % nt, 0)))
    out_shape.append(jax.ShapeDtypeStruct((t, n_extra), F32))
    out_specs.append(pl.BlockSpec((tm, n_extra), lambda i: (i, 0)))
    kern = functools.partial(_norm_proj_kernel, n_main=n_main, nsa_keys=nsa_keys, tiles_per_seq=nt,
                             extra_sigmoid=extra_sigmoid)
    return pl.pallas_call(
        kern,
        grid=(t // tm,),
        in_specs=[
            pl.BlockSpec((tm, d), lambda i: (i, 0)),
            pl.BlockSpec((1, d), lambda i: (0, 0)),
            pl.BlockSpec(w.shape, lambda i: (0, 0)),
            pl.BlockSpec((1, n_extra), lambda i: (0, 0)),
            pl.BlockSpec(kn.shape, lambda i: (0, 0)),
        ],
        out_specs=out_specs,
        out_shape=out_shape,
        compiler_params=_params("parallel"),
        name="norm_proj",
    )(x, g, w, eb, kn)


def _rglru_kernel(rx_ref, rg_ref, cw_ref, cb_ref, wa_ref, ba_ref, wx_ref, bx_ref, lam_ref, o_ref, tail_ref, h_ref):
    tc, c = rx_ref.shape

    @pl.when(pl.program_id(1) == 0)
    def _():
        tail_ref[...] = jnp.zeros_like(tail_ref)
        h_ref[...] = jnp.zeros_like(h_ref)

    xb = rx_ref[...]
    x = xb.astype(F32)
    xc = _causal_conv(xb, x, tail_ref[...], cw_ref, cb_ref)
    tail_ref[...] = x[tc - 8:tc, :]

    bw = c // RG_BLOCKS
    ra, rx = [], []
    for blk in range(RG_BLOCKS):
        xb = xc[:, blk * bw:(blk + 1) * bw].astype(BF16)
        ra.append(_dot(xb, wa_ref[blk]))
        rx.append(_dot(xb, wx_ref[blk]))
    r = _sigmoid(jnp.concatenate(ra, axis=1) + ba_ref[...])
    ig = _sigmoid(jnp.concatenate(rx, axis=1) + bx_ref[...])
    log_a = (-RG_C) * r * _softplus(-lam_ref[...])
    a = jnp.exp(log_a)
    z = 1.0 - a * a
    u = jnp.where(z > 0.0, z * lax.rsqrt(z), 0.0) * (ig * xc)

    in_group = lax.broadcasted_iota(I32, (tc, 1), 0) & (SUBLANES - 1)
    d = 1
    while d < SUBLANES:
        keep = in_group >= d
        a_sh = jnp.where(keep, pltpu.roll(a, d, 0), 1.0)
        u_sh = jnp.where(keep, pltpu.roll(u, d, 0), 0.0)
        u = a * u_sh + u
        a = a * a_sh
        d *= 2
    carry = h_ref[SUBLANES - 1:SUBLANES, :]
    groups = []
    for g0 in range(0, tc, SUBLANES):
        hg = u[g0:g0 + SUBLANES, :] + a[g0:g0 + SUBLANES, :] * carry
        carry = hg[SUBLANES - 1:SUBLANES, :]
        groups.append(hg)
    h_ref[...] = groups[-1]
    o_ref[...] = (_gelu_tanh(rg_ref[...].astype(F32)) * jnp.concatenate(groups, axis=0)).astype(o_ref.dtype)


def _rglru(main, cw, cb, wa, ba, wx, bx, lam, *, batch, tc=256):
    t = main.shape[0]
    c = cw.shape[1]
    nt = t // batch // tc
    vec = pl.BlockSpec((1, c), lambda b, i: (0, 0))
    blk = pl.BlockSpec(wa.shape, lambda b, i: (0, 0, 0))
    return pl.pallas_call(
        _rglru_kernel,
        grid=(batch, nt),
        in_specs=[
            pl.BlockSpec((tc, c), lambda b, i: (b * nt + i, 0)),
            pl.BlockSpec((tc, c), lambda b, i: (b * nt + i, 1)),
            pl.BlockSpec((CONV_W, c), lambda b, i: (0, 0)),
            vec, blk, vec, blk, vec, vec,
        ],
        out_specs=pl.BlockSpec((tc, c), lambda b, i: (b * nt + i, 0)),
        out_shape=jax.ShapeDtypeStruct((t, c), BF16),
        scratch_shapes=[pltpu.VMEM((8, c), F32), pltpu.VMEM((8, c), F32)],
        compiler_params=_params("parallel", "arbitrary"),
        name="rglru",
    )(main, main, cw, cb, wa, ba, wx, bx, lam)


def _nsa_compress_kernel(xk_ref, xv_ref, pos_ref, kw1_ref, kw2_ref, vw1_ref, vw2_ref, kn_ref, ov_ref, kc_ref, vc_ref):
    n, half = xk_ref.shape
    last = lax.broadcasted_iota(I32, (n, 1), 0) == n - 1
    pos = pos_ref[...]

    def compress(x_ref, w1_ref, w2_ref):
        x = x_ref[...]
        y0 = _dot(x, w1_ref[0:half, :])
        y1 = _dot(x, w1_ref[half:2 * half, :])
        y1_next = jnp.where(last, 0.0, pltpu.roll(y1, n - 1, 0))
        const = _dot(pos, w1_ref[...])[0:1, :]
        hid = _gelu_tanh(y0 + y1_next + const)
        return _dot(hid.astype(BF16), w2_ref[...])

    kc_ref[...] = _rms(compress(xk_ref, kw1_ref, kw2_ref), kn_ref[0:1, :]).astype(BF16)
    vc = compress(xv_ref, vw1_ref, vw2_ref).astype(BF16)
    vc_ref[...] = jnp.concatenate([vc, jnp.ones((n, LANES), BF16), ov_ref[...]], axis=1)


def _nsa_compress(planes, pos_flat, kw1, kw2, vw1, vw2, k_norm, overlap):
    b, _, seq, _ = planes.shape
    g = NSA_GROUPS
    n = seq // CMP_STRIDE
    half = CMP_STRIDE * LANES
    grouped = planes.reshape(b, 2 * g, n, half)
    full = lambda a: pl.BlockSpec(a.shape, lambda bi, gi: (0,) * a.ndim)
    out_spec = lambda w: pl.BlockSpec((None, None, n, w), lambda bi, gi: (bi, gi, 0, 0))
    out_shape = lambda w: jax.ShapeDtypeStruct((b, g, n, w), BF16)
    return pl.pallas_call(
        _nsa_compress_kernel,
        grid=(b, g),
        in_specs=[
            pl.BlockSpec((None, None, n, half), lambda bi, gi: (bi, gi, 0, 0)),
            pl.BlockSpec((None, None, n, half), lambda bi, gi: (bi, 2 + gi, 0, 0)),
            full(pos_flat), full(kw1), full(kw2), full(vw1), full(vw2), full(k_norm), full(overlap),
        ],
        out_specs=[out_spec(LANES), out_spec(3 * LANES)],
        out_shape=[out_shape(LANES), out_shape(3 * LANES)],
        compiler_params=_params("parallel", "parallel"),
        name="nsa_compress",
    )(grouped, grouped, pos_flat, kw1, kw2, vw1, vw2, k_norm, overlap)


def _nsa_cw_kernel(shift_ref, *refs, qn, n_sel):
    fixed = jnp.maximum(shift_ref[1], shift_ref[2]) <= FIXED_SHIFT_MAX
    pl.when(fixed)(functools.partial(_nsa_cw_body, shift_ref, *refs, qn=qn, n_sel=n_sel, fixed=True))
    pl.when(jnp.logical_not(fixed))(functools.partial(_nsa_cw_body, shift_ref, *refs, qn=qn, n_sel=n_sel, fixed=False))


def _nsa_cw_body(shift_ref, q_ref, kc_ref, vc_ref, kw_ref, vw_ref, gt_ref, qn_ref, wb_ref, qp_ref, o_ref, *, qn, n_sel,
                 fixed):
    t0 = pl.program_id(2) * qn
    rep = NSA_REP
    rows = rep * qn
    scale = NSA_DK ** -0.5 * LOG2E
    qf = q_ref[...].astype(F32)
    heads = []
    for r in range(rep):
        qh = _rms(qf[:, r * LANES:(r + 1) * LANES], qn_ref[...]) * scale
        heads.append(qh.astype(BF16))
    qs = jnp.concatenate(heads, axis=0)
    trow = t0 + (lax.broadcasted_iota(I32, (rows, 1), 0) & (qn - 1))

    def attend(sm, vx, shift):
        if not fixed:
            sm = sm - jnp.maximum(jnp.max(sm, axis=-1, keepdims=True), -2.0 * shift)
        return _dot(jnp.exp2(sm).astype(BF16), vx)

    n_cmp = kc_ref.shape[0]
    visible = lax.broadcasted_iota(I32, (1, n_cmp), 1) * CMP_STRIDE + (CMP_LEN - 1) <= trow
    r_cmp = attend(jnp.where(visible, _dot_nt(qs, kc_ref[...]), NEG) - shift_ref[1], vc_ref[...], shift_ref[1])
    inv = 1.0 / jnp.maximum(r_cmp[:, LANES:2 * LANES], 1e-30)
    o_cmp = r_cmp[:, 0:LANES] * inv
    imp_h = r_cmp[:, 2 * LANES:3 * LANES] * inv
    imp = imp_h[0:qn]
    for r in range(1, rep):
        imp = imp + imp_h[r * qn:(r + 1) * qn]

    imp_t = imp.T
    jj = lax.broadcasted_iota(I32, imp_t.shape, 0).astype(F32)
    cur = jnp.right_shift(t0 + lax.broadcasted_iota(I32, imp_t.shape, 1), SLC_SHIFT).astype(F32)
    forced = (jj == 0.0) | (jj == cur) | (jj == cur - 1.0)
    shift = -shift_ref[0]
    taken = forced & (jj <= cur)
    work = jnp.where(taken, -2.0, jnp.where(jj <= cur, imp_t, -1.0))
    bias_t = jnp.where(taken, shift, -SEL_BIAS)
    for _ in range(n_sel - N_FORCED):
        m = jnp.max(work, axis=0, keepdims=True)
        idx = jnp.min(jnp.where(work == m, jj, float(LANES)), axis=0, keepdims=True)
        pick = jj == idx
        bias_t = jnp.where(pick, shift, bias_t)
        work = jnp.where(pick, -2.0, work)
    bias = bias_t.T.astype(BF16)
    for r in range(rep):
        qp_ref[r] = jnp.concatenate([heads[r], bias], axis=1)

    span = WINDOW + qn
    start = pl.multiple_of(jnp.maximum(t0 - WINDOW, 0), qn)
    s = _dot_nt(qs, kw_ref[pl.ds(start, span), :]).reshape(rep, qn, span) + wb_ref[...]
    r_win = attend(s.reshape(rows, span), vw_ref[pl.ds(start, span), :], shift_ref[2])
    o_win = r_win[:, 0:LANES] / jnp.maximum(r_win[:, LANES:2 * LANES], 1e-30)

    gt = gt_ref[...]
    for r in range(rep):
        sl = slice(r * qn, (r + 1) * qn)
        o = gt[:, 3 * r:3 * r + 1] * o_cmp[sl] + gt[:, 3 * r + 2:3 * r + 3] * o_win[sl]
        o_ref[:, r * LANES:(r + 1) * LANES] = o.astype(o_ref.dtype)


def _nsa_cw(shift, main, kcmp, vcmp, kwn, vwp, gates, q_norm, *, qn=256):
    b, g, seq, _ = kwn.shape
    t = main.shape[0]
    nq = seq // qn
    rep = NSA_REP
    gw = rep * LANES
    q_blk0 = (main.shape[1] - NSA_HEADS * NSA_DK) // gw
    n_cmp = kcmp.shape[2]
    n_case = WINDOW // qn + 1
    span = WINDOW + qn
    in_window = []
    for case in range(n_case):
        t0 = case * qn
        diff = (t0 + np.arange(qn)[:, None]) - (max(t0 - WINDOW, 0) + np.arange(span)[None, :])
        in_window.append((diff >= 0) & (diff < WINDOW))
    window_bias = jnp.where(jnp.asarray(np.stack(in_window)), -shift[2], NEG).astype(F32)
    kern = functools.partial(_nsa_cw_kernel, qn=qn, n_sel=min(SLC_TOPN, seq // SLC_LEN))
    return pl.pallas_call(
        kern,
        grid=(b, g, nq),
        in_specs=[
            pl.BlockSpec(memory_space=pltpu.SMEM),
            pl.BlockSpec((qn, gw), lambda bi, gi, i: (bi * nq + i, q_blk0 + gi)),
            pl.BlockSpec((None, None, n_cmp, LANES), lambda bi, gi, i: (bi, gi, 0, 0)),
            pl.BlockSpec((None, None, n_cmp, 3 * LANES), lambda bi, gi, i: (bi, gi, 0, 0)),
            pl.BlockSpec((None, None, seq, LANES), lambda bi, gi, i: (bi, gi, 0, 0)),
            pl.BlockSpec((None, None, seq, 2 * LANES), lambda bi, gi, i: (bi, gi, 0, 0)),
            pl.BlockSpec((qn, LANES), lambda bi, gi, i: (bi * nq + i, gi)),
            pl.BlockSpec((1, LANES), lambda bi, gi, i: (0, 0)),
            pl.BlockSpec((None, qn, span), lambda bi, gi, i: (jnp.minimum(i, n_case - 1), 0, 0)),
        ],
        out_specs=[
            pl.BlockSpec((None, None, rep, qn, 2 * LANES), lambda bi, gi, i: (bi, gi, 0, i, 0)),
            pl.BlockSpec((qn, gw), lambda bi, gi, i: (bi * nq + i, gi)),
        ],
        out_shape=[
            jax.ShapeDtypeStruct((b, g, rep, seq, 2 * LANES), BF16),
            jax.ShapeDtypeStruct((t, NSA_HEADS * NSA_DK), BF16),
        ],
        compiler_params=_params("parallel", "parallel", "parallel"),
        name="nsa_cmp_win",
    )(shift, main, kcmp, vcmp, kwn, vwp, gates, q_norm, window_bias)


def _nsa_slc_kernel(shift_ref, qp_ref, kp_ref, vp_ref, gt_ref, ocw_ref, o_ref, m_ref, acc_ref, *, qn, tk):
    t0 = pl.program_id(2) * qn
    rep = NSA_REP
    rows = rep * qn
    last = (t0 + qn - 1) // tk
    acc_ref[...] = jnp.zeros_like(acc_ref)

    def scores(k0, width, causal):
        qp = qp_ref[...].reshape(rows, qp_ref.shape[2])
        s = _dot_nt(qp, kp_ref[pl.ds(k0, width), :])
        if causal:
            trow = t0 + (lax.broadcasted_iota(I32, (rows, 1), 0) & (qn - 1))
            s = jnp.where(k0 + lax.broadcasted_iota(I32, (1, width), 1) <= trow, s, -SEL_BIAS)
        return s, vp_ref[pl.ds(k0, width), :]

    def fixed_shift_step(k0, width, causal):
        s, v = scores(k0, width, causal)
        acc_ref[...] += _dot(jnp.exp2(s).astype(BF16), v)

    def running_max_step(k0, width, causal):
        s, v = scores(k0, width, causal)
        m_old = m_ref[...]
        m_new = jnp.maximum(m_old, jnp.max(s, axis=-1, keepdims=True))
        acc_ref[...] = jnp.exp2(m_old - m_new) * acc_ref[...] + _dot(jnp.exp2(s - m_new).astype(BF16), v)
        m_ref[...] = m_new

    def sweep(step):
        wide = SLC_WIDE * tk
        n_wide = last // SLC_WIDE
        lax.fori_loop(0, n_wide, lambda j, c: (step(pl.multiple_of(j * wide, wide), wide, False), c)[1], 0)
        for left in range(SLC_WIDE):
            pl.when(last - n_wide * SLC_WIDE == left)(
                functools.partial(step, pl.multiple_of(n_wide * wide, tk), (left + 1) * tk, True))

    fixed = shift_ref[0] <= FIXED_SHIFT_MAX

    @pl.when(fixed)
    def _():
        sweep(fixed_shift_step)

    @pl.when(jnp.logical_not(fixed))
    def _():
        m_ref[...] = jnp.full(m_ref.shape, NEG, F32)
        sweep(running_max_step)

    o_slc = acc_ref[:, 0:LANES] / jnp.maximum(acc_ref[:, LANES:2 * LANES], 1e-30)
    gt = gt_ref[...]
    for r in range(rep):
        o = ocw_ref[:, r * LANES:(r + 1) * LANES].astype(F32) + gt[:, 3 * r + 1:3 * r + 2] * o_slc[r * qn:(r + 1) * qn]
        o_ref[:, r * LANES:(r + 1) * LANES] = o.astype(o_ref.dtype)


def _nsa_slc(shift, qp, kp, vp, gates, ocw, *, qn=256, tk=512):
    b, g, rep, seq, dqk = qp.shape
    t = ocw.shape[0]
    nq = seq // qn
    gw = rep * LANES
    tk = min(tk, seq)
    kern = functools.partial(_nsa_slc_kernel, qn=qn, tk=tk)
    return pl.pallas_call(
        kern,
        grid=(b, g, nq),
        in_specs=[
            pl.BlockSpec(memory_space=pltpu.SMEM),
            pl.BlockSpec((None, None, rep, qn, dqk), lambda bi, gi, i: (bi, gi, 0, i, 0)),
            pl.BlockSpec((None, None, seq, dqk), lambda bi, gi, i: (bi, gi, 0, 0)),
            pl.BlockSpec((None, None, seq, 2 * LANES), lambda bi, gi, i: (bi, gi, 0, 0)),
            pl.BlockSpec((qn, LANES), lambda bi, gi, i: (bi * nq + i, gi)),
            pl.BlockSpec((qn, gw), lambda bi, gi, i: (bi * nq + i, gi)),
        ],
        out_specs=pl.BlockSpec((qn, gw), lambda bi, gi, i: (bi * nq + i, gi)),
        out_shape=jax.ShapeDtypeStruct((t, NSA_HEADS * NSA_DK), BF16),
        scratch_shapes=[pltpu.VMEM((rep * qn, 1), F32), pltpu.VMEM((rep * qn, 2 * LANES), F32)],
        compiler_params=_params("parallel", "parallel", "parallel"),
        name="nsa_selected",
    )(shift, qp, kp, vp, gates, ocw)


def _mem_kv_kernel(mem_ref, g_ref, wkv_ref, kn_ref, k_ref, v_ref):
    memn = _rms(mem_ref[...], g_ref[...]).astype(BF16)
    kv = _dot(memn, wkv_ref[...])
    inner = k_ref.shape[1]
    for h in range(inner // X_HEADDIM):
        sl = slice(h * X_HEADDIM, (h + 1) * X_HEADDIM)
        k_ref[:, sl] = _rms(kv[:, sl], kn_ref[...]).astype(BF16)
    v_ref[...] = kv[:, inner:].astype(BF16)


def _mem_kv(mem, g, wkv, kn):
    b, m, d = mem.shape
    inner = wkv.shape[1] // 2
    out_spec = pl.BlockSpec((None, m, inner), lambda bi: (bi, 0, 0))
    out_shape = jax.ShapeDtypeStruct((b, m, inner), BF16)
    return pl.pallas_call(
        _mem_kv_kernel,
        grid=(b,),
        in_specs=[
            pl.BlockSpec((None, m, d), lambda bi: (bi, 0, 0)),
            pl.BlockSpec((1, d), lambda bi: (0, 0)),
            pl.BlockSpec(wkv.shape, lambda bi: (0, 0)),
            pl.BlockSpec((1, X_HEADDIM), lambda bi: (0, 0)),
        ],
        out_specs=[out_spec, out_spec],
        out_shape=[out_shape, out_shape],
        compiler_params=_params("parallel"),
        name="mem_kv",
    )(mem, g, wkv, kn)


def _cross_attn_kernel(*refs, n_in):
    a_refs = refs[:n_in]
    w_ref, x_ref, g_ref, wq_ref, qn_ref, k_ref, v_ref, wo_ref, o_ref = refs[n_in:]
    x = x_ref[...]
    k0 = 0
    for a_ref in a_refs:
        x = x + _dot(a_ref[...], w_ref[k0:k0 + a_ref.shape[1], :])
        k0 += a_ref.shape[1]
    q = _dot(_rms(x, g_ref[...]).astype(BF16), wq_ref[...])
    scale = X_HEADDIM ** -0.5
    outs = []
    for h in range(q.shape[1] // X_HEADDIM):
        sl = slice(h * X_HEADDIM, (h + 1) * X_HEADDIM)
        qh = (_rms(q[:, sl], qn_ref[...]) * scale).astype(BF16)
        s = _dot_nt(qh, k_ref[:, sl])
        e = jnp.exp(s - jnp.max(s, axis=-1, keepdims=True))
        p = e / jnp.sum(e, axis=-1, keepdims=True)
        outs.append(_dot(p.astype(BF16), v_ref[:, sl]))
    o = jnp.concatenate(outs, axis=1).astype(BF16)
    o_ref[...] = x + _dot(o, wo_ref[...])


def _cross_attn(acts, w_out, x, g, wq, qn, k, v, wo, *, tm=512):
    t, d = x.shape
    b, m, inner = k.shape
    nt = t // b // tm
    return pl.pallas_call(
        functools.partial(_cross_attn_kernel, n_in=len(acts)),
        grid=(b, nt),
        in_specs=[pl.BlockSpec((tm, a.shape[1]), lambda bi, i: (bi * nt + i, 0)) for a in acts] + [
            pl.BlockSpec(w_out.shape, lambda bi, i: (0, 0)),
            pl.BlockSpec((tm, d), lambda bi, i: (bi * nt + i, 0)),
            pl.BlockSpec((1, d), lambda bi, i: (0, 0)),
            pl.BlockSpec(wq.shape, lambda bi, i: (0, 0)),
            pl.BlockSpec((1, X_HEADDIM), lambda bi, i: (0, 0)),
            pl.BlockSpec((None, m, inner), lambda bi, i: (bi, 0, 0)),
            pl.BlockSpec((None, m, inner), lambda bi, i: (bi, 0, 0)),
            pl.BlockSpec(wo.shape, lambda bi, i: (0, 0)),
        ],
        out_specs=pl.BlockSpec((tm, d), lambda bi, i: (bi * nt + i, 0)),
        out_shape=jax.ShapeDtypeStruct((t, d), F32),
        compiler_params=_params("parallel", "parallel"),
        name="cross_attn",
    )(*acts, w_out, x, g, wq, qn, k, v, wo)


def _swiglu_kernel(x_ref, g_ref, wu_ref, wg_ref, w2_ref, o_ref, h_ref):
    @pl.when(pl.program_id(1) == 0)
    def _():
        h_ref[...] = _rms(x_ref[...], g_ref[...]).astype(BF16)
        o_ref[...] = x_ref[...]

    h = h_ref[...]
    act = (_silu(_dot(h, wg_ref[...])) * _dot(h, wu_ref[...])).astype(BF16)
    for c0 in range(0, o_ref.shape[1], EXPERT_CHUNK):
        c = slice(c0, c0 + EXPERT_CHUNK)
        o_ref[:, c] += _dot(act, w2_ref[:, c])


def _swiglu(x, g, w13, w2, *, tm=512, tf=1408):
    t, d = x.shape
    ff = w2.shape[0]
    nf = ff // tf
    return pl.pallas_call(
        _swiglu_kernel,
        grid=(t // tm, nf),
        in_specs=[
            pl.BlockSpec((tm, d), lambda i, f: (i, 0)),
            pl.BlockSpec((1, d), lambda i, f: (0, 0)),
            pl.BlockSpec((d, tf), lambda i, f: (0, f)),
            pl.BlockSpec((d, tf), lambda i, f: (0, nf + f)),
            pl.BlockSpec((tf, d), lambda i, f: (f, 0)),
        ],
        out_specs=pl.BlockSpec((tm, d), lambda i, f: (i, 0)),
        out_shape=jax.ShapeDtypeStruct((t, d), F32),
        scratch_shapes=[pltpu.VMEM((tm, d), BF16)],
        compiler_params=_params("parallel", "arbitrary"),
        name="swiglu",
    )(x, g, w13, w13, w2)


def _split3(x):
    a = x.astype(BF16)
    r = x - a.astype(F32)
    b = r.astype(BF16)
    c = (r - b.astype(F32)).astype(BF16)
    return a, b, c


def _ssd_kernel(z0_ref, z1_ref, x0_ref, x1_ref, bc_ref, dt_ref, cw_ref, cb_ref, dtb_ref, alog_ref, dskip_ref,
                ng_ref, o_ref, tail_ref, state_ref):
    q = x0_ref.shape[0]
    d_inner = o_ref.shape[1]
    gn = SSM_GROUPS * SSM_STATE
    hpg = d_inner // SSM_HEADDIM // SSM_GROUPS

    @pl.when(pl.program_id(1) == 0)
    def _():
        tail_ref[...] = jnp.zeros_like(tail_ref)
        state_ref[...] = jnp.zeros_like(state_ref)

    raw_b = jnp.concatenate([x0_ref[...], x1_ref[...], bc_ref[...]], axis=1)
    raw = raw_b.astype(F32)
    xbc = _silu(_causal_conv(raw_b, raw, tail_ref[...], cw_ref, cb_ref))
    tail_ref[...] = raw[q - 8:q, :]
    xs = xbc[:, :d_inner]
    bm = xbc[:, d_inner:d_inner + gn]
    cm = xbc[:, d_inner + gn:]

    dt = _softplus(dt_ref[...] + dtb_ref[...])
    a = dt * (-jnp.exp(alog_ref[...]))
    ri = lax.broadcasted_iota(I32, (q, q), 0)
    ci = lax.broadcasted_iota(I32, (q, q), 1)
    causal = ci <= ri
    tri = jnp.where(causal, 1.0, 0.0).astype(BF16)
    a_cs = sum(_dot(tri, part) for part in _split3(a))
    a_cs_t = a_cs.T
    dt_t = dt.T
    lane = lax.broadcasted_iota(I32, (1, LANES), 1)
    lo = lane < SSM_HEADDIM

    y_parts = []
    for g in range(SSM_GROUPS):
        cg = cm[:, g * SSM_STATE:(g + 1) * SSM_STATE].astype(BF16)
        bg = bm[:, g * SSM_STATE:(g + 1) * SSM_STATE]
        gmat = _dot_nt(cg, bg.astype(BF16))
        bg_t = bg.T
        gw = hpg * SSM_HEADDIM
        prev = state_ref[:, g * gw:(g + 1) * gw]
        y_off = _dot(cg, prev.astype(BF16))
        for pr in range(hpg // 2):
            c0 = g * gw + pr * LANES
            x_pair = xs[:, c0:c0 + LANES]
            y_pair = dskip_ref[:, c0:c0 + LANES] * x_pair
            st_pair = jnp.zeros((SSM_STATE, LANES), F32)
            decay_pair = jnp.zeros((1, LANES), F32)
            for half in range(2):
                h = g * hpg + pr * 2 + half
                sel = lo if half == 0 else jnp.logical_not(lo)
                xh = jnp.where(sel, x_pair, 0.0).astype(BF16)
                row_cs = a_cs_t[h:h + 1, :]
                col_cs = a_cs[:, h:h + 1]
                row_dt = dt_t[h:h + 1, :]
                a_last = a_cs_t[h:h + 1, q - 1:q]
                dec = jnp.exp(jnp.where(causal, col_cs - row_cs, NEG))
                y_pair = y_pair + _dot((gmat * dec * row_dt).astype(BF16), xh)
                w_row = jnp.exp(a_last - row_cs) * row_dt
                st_pair = st_pair + _dot((bg_t * w_row).astype(BF16), xh)
                y_pair = y_pair + jnp.where(sel, jnp.exp(col_cs) * y_off[:, pr * LANES:(pr + 1) * LANES], 0.0)
                decay_pair = jnp.where(sel, jnp.exp(a_last), decay_pair)
            state_ref[:, c0:c0 + LANES] = decay_pair * state_ref[:, c0:c0 + LANES] + st_pair
            y_parts.append(y_pair)
    y = jnp.concatenate(y_parts, axis=1)

    z = jnp.concatenate([z0_ref[...], z1_ref[...]], axis=1).astype(F32)
    y = y * _silu(z)
    gsz = d_inner // SSM_GROUPS
    for g in range(SSM_GROUPS):
        sl = slice(g * gsz, (g + 1) * gsz)
        o_ref[:, sl] = _rms(y[:, sl], ng_ref[:, sl]).astype(o_ref.dtype)


def _ssd(main, dt, cw, cb, dtb, alog, dskip, ng, *, batch, d_inner):
    t = main.shape[0]
    q = SSD_CHUNK
    nc = t // batch // q
    conv_ch = cw.shape[1]
    half = d_inner // 2
    col = lambda j: pl.BlockSpec((q, half), lambda b, i: (b * nc + i, j))
    vec = lambda n: pl.BlockSpec((1, n), lambda b, i: (0, 0))
    return pl.pallas_call(
        _ssd_kernel,
        grid=(batch, nc),
        in_specs=[
            col(0), col(1), col(2), col(3), col(4),
            pl.BlockSpec((q, LANES), lambda b, i: (b * nc + i, 0)),
            pl.BlockSpec((CONV_W, conv_ch), lambda b, i: (0, 0)),
            vec(conv_ch), vec(LANES), vec(LANES), vec(d_inner), vec(d_inner),
        ],
        out_specs=pl.BlockSpec((q, d_inner), lambda b, i: (b * nc + i, 0)),
        out_shape=jax.ShapeDtypeStruct((t, d_inner), BF16),
        scratch_shapes=[pltpu.VMEM((8, conv_ch), F32), pltpu.VMEM((SSM_STATE, d_inner), F32)],
        compiler_params=_params("parallel", "arbitrary"),
        name="ssd",
    )(main, main, main, main, main, dt, cw, cb, dtb, alog, dskip, ng)


def _router_kernel(x_ref, g_ref, wr_ref, h_ref, info_ref, cnt_ref, run_ref):
    tm = x_ref.shape[0]

    @pl.when(pl.program_id(0) == 0)
    def _():
        run_ref[...] = jnp.zeros_like(run_ref)

    h = _rms(x_ref[...], g_ref[...])
    hb = h.astype(BF16).astype(F32)
    half = h.shape[1] // 2
    hi = lax.bitcast_convert_type(hb[:, :half], jnp.uint32)
    lo = lax.bitcast_convert_type(hb[:, half:], jnp.uint32)
    h_ref[...] = hi | (lo >> 16)
    h_hi = h.astype(BF16)
    h_lo = (h - h_hi.astype(F32)).astype(BF16)
    w = wr_ref[...]
    w_hi = w.astype(BF16)
    w_lo = (w - w_hi.astype(F32)).astype(BF16)
    logits = _dot(h_hi, w_hi) + _dot(h_lo, w_hi) + _dot(h_hi, w_lo)
    lane = lax.broadcasted_iota(I32, (tm, LANES), 1)
    lg = jnp.where(lane < N_EXPERTS, logits, NEG)
    m1 = jnp.max(lg, axis=-1, keepdims=True)
    i1 = jnp.min(jnp.where(lg == m1, lane, LANES), axis=-1, keepdims=True)
    lg2 = jnp.where(lane == i1, NEG, lg)
    m2 = jnp.max(lg2, axis=-1, keepdims=True)
    i2 = jnp.min(jnp.where(lg2 == m2, lane, LANES), axis=-1, keepdims=True)
    e2 = jnp.exp(m2 - m1)
    w1 = 1.0 / (1.0 + e2)
    w2 = e2 / (1.0 + e2)

    hot1 = lane == i1
    hot2 = lane == i2
    hot = jnp.where(hot1 | hot2, 1.0, 0.0)
    ri = lax.broadcasted_iota(I32, (tm, tm), 0)
    ci = lax.broadcasted_iota(I32, (tm, tm), 1)
    before = jnp.where(ci < ri, 1.0, 0.0).astype(BF16)
    seen = run_ref[0:1, :] + _dot(before, hot.astype(BF16))
    rank1 = jnp.sum(jnp.where(hot1, seen, 0.0), axis=-1, keepdims=True)
    rank2 = jnp.sum(jnp.where(hot2, seen, 0.0), axis=-1, keepdims=True)
    run_ref[...] = run_ref[...] + jnp.sum(hot, axis=0, keepdims=True)
    cnt_ref[...] = run_ref[...]

    cols = [i1.astype(F32), i2.astype(F32), w1, w2, rank1, rank2]
    info = jnp.zeros((tm, LANES), F32)
    for c, v in enumerate(cols):
        info = jnp.where(lane == c, v, info)
    info_ref[...] = info


def _router(x, g, wr, *, tm=256):
    t, d = x.shape
    return pl.pallas_call(
        _router_kernel,
        grid=(t // tm,),
        in_specs=[
            pl.BlockSpec((tm, d), lambda i: (i, 0)),
            pl.BlockSpec((1, d), lambda i: (0, 0)),
            pl.BlockSpec(wr.shape, lambda i: (0, 0)),
        ],
        out_specs=[
            pl.BlockSpec((tm, d // 2), lambda i: (i, 0)),
            pl.BlockSpec((tm, LANES), lambda i: (i, 0)),
            pl.BlockSpec((8, LANES), lambda i: (0, 0)),
        ],
        out_shape=[
            jax.ShapeDtypeStruct((t, d // 2), jnp.uint32),
            jax.ShapeDtypeStruct((t, LANES), F32),
            jax.ShapeDtypeStruct((8, LANES), F32),
        ],
        scratch_shapes=[pltpu.VMEM((8, LANES), F32)],
        compiler_params=_params("arbitrary"),
        name="moe_router",
    )(x, g, wr)


def _row_copy(src_ref, src_row, dst_ref, dst_row, sem):
    return pltpu.make_async_copy(src_ref.at[pl.ds(src_row, 1)], dst_ref.at[pl.ds(dst_row, 1)], sem)


def _dispatch_kernel(pos_ref, h_ref, init_ref, xs_ref, sem):
    del init_ref
    tt = h_ref.shape[0]

    def issue(r, c):
        _row_copy(h_ref, r, xs_ref, pos_ref[0, r], sem).start()
        _row_copy(h_ref, r, xs_ref, pos_ref[1, r], sem).start(priority=1)
        return c

    lax.fori_loop(0, tt, issue, 0, unroll=DMA_UNROLL)
    for _ in range(2):
        pltpu.make_async_copy(h_ref, xs_ref.at[pl.ds(0, tt)], sem).wait()


def _dispatch(h, pos, rows, *, tt=256):
    t, d = h.shape
    init = jnp.zeros((rows, d), h.dtype)
    return pl.pallas_call(
        _dispatch_kernel,
        grid=(t // tt,),
        in_specs=[
            pl.BlockSpec((None, 2, tt), lambda i: (i, 0, 0), memory_space=pltpu.SMEM),
            pl.BlockSpec((tt, d), lambda i: (i, 0)),
            pl.BlockSpec(memory_space=pl.ANY),
        ],
        out_specs=pl.BlockSpec(memory_space=pl.ANY),
        out_shape=jax.ShapeDtypeStruct((rows, d), h.dtype),
        scratch_shapes=[pltpu.SemaphoreType.DMA(())],
        input_output_aliases={2: 0},
        compiler_params=_params("arbitrary"),
        name="moe_dispatch",
    )(pos, h, init)


def _expert_kernel(te_ref, tv_ref, xs_ref, wu_ref, wg_ref, w2_ref, o_ref, xb_ref):
    del te_ref
    i = pl.program_id(0)
    f = pl.program_id(1)
    live = tv_ref[i] > 0

    @pl.when(f == 0)
    def _():
        o_ref[...] = jnp.zeros_like(o_ref)
        packed = xs_ref[...]
        hi = lax.bitcast_convert_type(packed & jnp.uint32(0xFFFF0000), F32)
        lo = lax.bitcast_convert_type(packed << 16, F32)
        xb_ref[...] = jnp.concatenate([hi, lo], axis=1).astype(BF16)

    @pl.when(live)
    def _():
        x = xb_ref[...]
        act = (_silu(_dot(x, wg_ref[...].astype(BF16))) * _dot(x, wu_ref[...].astype(BF16))).astype(BF16)
        for c0 in range(0, o_ref.shape[1], EXPERT_CHUNK):
            c = slice(c0, c0 + EXPERT_CHUNK)
            o_ref[:, c] += _dot(act, w2_ref[:, c].astype(BF16))


def _experts(xs, w13, w2, tile_expert, tile_live, *, tm, tf=512):
    rows = xs.shape[0]
    ff, d = w2.shape[1:]
    nf = ff // tf

    def f_of(i, f, te, tv):
        return jnp.where(tv[i] > 0, f, nf - 1)

    grid_spec = pltpu.PrefetchScalarGridSpec(
        num_scalar_prefetch=2,
        grid=(rows // tm, nf),
        in_specs=[
            pl.BlockSpec((tm, xs.shape[1]), lambda i, f, te, tv: (i, 0)),
            pl.BlockSpec((None, d, tf), lambda i, f, te, tv: (te[i], 0, f_of(i, f, te, tv))),
            pl.BlockSpec((None, d, tf), lambda i, f, te, tv: (te[i], 0, nf + f_of(i, f, te, tv))),
            pl.BlockSpec((None, tf, d), lambda i, f, te, tv: (te[i], f_of(i, f, te, tv), 0)),
        ],
        out_specs=pl.BlockSpec((tm, d), lambda i, f, te, tv: (i, 0)),
        scratch_shapes=[pltpu.VMEM((tm, d), BF16)],
    )
    return pl.pallas_call(
        _expert_kernel,
        grid_spec=grid_spec,
        out_shape=jax.ShapeDtypeStruct((rows, d), F32),
        compiler_params=_params("parallel", "arbitrary"),
        name="moe_experts",
    )(tile_expert, tile_live, xs, w13, w13, w2)


def _combine_kernel(pos_ref, x_ref, info_ref, ys_ref, o_ref, buf_ref, sem):
    tt = x_ref.shape[0]

    def issue(r, c):
        _row_copy(ys_ref, pos_ref[0, r], buf_ref.at[0], r, sem).start()
        _row_copy(ys_ref, pos_ref[1, r], buf_ref.at[1], r, sem).start(priority=1)
        return c

    lax.fori_loop(0, tt, issue, 0, unroll=DMA_UNROLL)
    for k in range(2):
        pltpu.make_async_copy(ys_ref.at[pl.ds(0, tt)], buf_ref.at[k], sem).wait()
    info = info_ref[...]
    o_ref[...] = x_ref[...] + info[:, 2:3] * buf_ref[0] + info[:, 3:4] * buf_ref[1]


def _combine(x, info, pos, ys, *, tt=256):
    t, d = x.shape
    return pl.pallas_call(
        _combine_kernel,
        grid=(t // tt,),
        in_specs=[
            pl.BlockSpec((None, 2, tt), lambda i: (i, 0, 0), memory_space=pltpu.SMEM),
            pl.BlockSpec((tt, d), lambda i: (i, 0)),
            pl.BlockSpec((tt, LANES), lambda i: (i, 0)),
            pl.BlockSpec(memory_space=pl.ANY),
        ],
        out_specs=pl.BlockSpec((tt, d), lambda i: (i, 0)),
        out_shape=jax.ShapeDtypeStruct((t, d), F32),
        scratch_shapes=[pltpu.VMEM((2, tt, d), F32), pltpu.SemaphoreType.DMA(())],
        compiler_params=_params("arbitrary"),
        name="moe_combine",
    )(pos, x, info, ys)


def _moe(x, g, router, w13, w2, *, tm=1024, tt=256):
    t, d = x.shape
    n_exp = router.shape[1]
    wr = jnp.pad(router, ((0, 0), (0, LANES - n_exp)))
    h, info, counts = _router(x, g, wr, tm=tt)

    counts = counts[0, :n_exp].astype(I32)
    seg = (counts + tm - 1) // tm * tm
    seg_end = jnp.cumsum(seg)
    seg_start = seg_end - seg
    e1 = info[:, 0].astype(I32)
    e2 = info[:, 1].astype(I32)
    pos = jnp.stack([seg_start[e1] + info[:, 4].astype(I32), seg_start[e2] + info[:, 5].astype(I32)], axis=0)
    pos = pos.reshape(2, t // tt, tt).transpose(1, 0, 2)
    rows = 2 * t + n_exp * tm
    tile_row0 = jnp.arange(rows // tm, dtype=I32) * tm
    tile_live = (tile_row0 < seg_end[-1]).astype(I32)
    tile_expert = jnp.sum((seg_end[None, :] <= tile_row0[:, None]).astype(I32), axis=1)
    tile_expert = jnp.minimum(tile_expert, n_exp - 1)
    last_live = jnp.maximum(jnp.sum(tile_live) - 1, 0)
    tile_expert = jnp.where(tile_live > 0, tile_expert, tile_expert[last_live])

    xs = _dispatch(h, pos, rows, tt=tt)
    ys = _experts(xs, w13, w2, tile_expert, tile_live, tm=tm)
    return _combine(x, info, pos, ys, tt=tt)


def _row(v, n=None):
    v = v.reshape(1, -1).astype(F32)
    if n is not None and v.shape[1] < n:
        v = jnp.pad(v, ((0, 0), (0, n - v.shape[1])))
    return v


def _overlap_matrix(seq):
    n = seq // CMP_STRIDE
    cmp_start = np.arange(n) * CMP_STRIDE
    slc_start = np.arange(LANES) * SLC_LEN
    ov = (cmp_start[:, None] <= slc_start[None, :] + SLC_LEN - 1) & (cmp_start[:, None] + CMP_LEN - 1 >= slc_start[None, :])
    ov[n - 1] = False
    return jnp.asarray(ov, dtype=BF16)


def _even_layer(x, batch, norm_mix, w_in, conv_w, conv_b, wa, ba, wx, bx, lam, gate_b, q_norm, k_norm, cmp_pos,
                ck_w1, ck_w2, cv_w1, cv_w2):
    t, d = x.shape
    seq = t // batch
    rg = wa.shape[0] * wa.shape[1]
    gdk = NSA_GROUPS * NSA_DK
    n_main = 2 * rg + NSA_HEADS * NSA_DK
    n_kv = 6 * gdk
    per_group = 3 * NSA_REP
    gate_cols = w_in[:, n_main + n_kv:].reshape(d, NSA_GROUPS, per_group)
    gate_cols = jnp.pad(gate_cols, ((0, 0), (0, 0), (0, LANES - per_group))).reshape(d, NSA_GROUPS * LANES)
    gate_bias = jnp.pad(gate_b.reshape(NSA_GROUPS, per_group), ((0, 0), (0, LANES - per_group))).reshape(1, -1)
    w_all = jnp.concatenate([w_in[:, :n_main + n_kv], gate_cols], axis=1).astype(BF16)
    kn = jnp.pad(k_norm, ((0, 8 - k_norm.shape[0]), (0, 0)))
    main, planes, kp, vp, kwn, vwp, gates = _norm_proj(
        x, _row(norm_mix), w_all, gate_bias, kn, batch=batch, n_main=n_main, nsa_keys=True,
        n_extra=NSA_GROUPS * LANES, extra_sigmoid=True)

    rg_out = _rglru(main, conv_w, _row(conv_b), wa.astype(BF16), _row(ba), wx.astype(BF16), _row(bx), _row(lam),
                    batch=batch)

    k_gain = jnp.max(jnp.abs(k_norm), axis=1)[jnp.array([1, 0, 2])]
    shift = ((1.02 * LOG2E * math.sqrt(NSA_DK)) * jnp.max(jnp.abs(q_norm)) * k_gain).astype(F32)
    pos_flat = jnp.broadcast_to(cmp_pos.reshape(1, -1), (8, CMP_LEN * NSA_DK)).astype(BF16)
    kcmp, vcmp = _nsa_compress(planes, pos_flat, ck_w1.astype(BF16), ck_w2.astype(BF16), cv_w1.astype(BF16),
                               cv_w2.astype(BF16), kn, _overlap_matrix(seq))
    qp, ocw = _nsa_cw(shift, main, kcmp, vcmp, kwn, vwp, gates, _row(q_norm))
    att = _nsa_slc(shift, qp, kp, vp, gates, ocw)
    return [rg_out, att]


def _odd_layer(x, batch, norm_mix, w_in, conv_w, conv_b, dt_bias, a_log, d_skip, norm_g, d_inner):
    conv_ch = conv_w.shape[1]
    n_main = d_inner + conv_ch
    heads = dt_bias.shape[0]
    w_all = jnp.concatenate([w_in[:, :n_main], jnp.pad(w_in[:, n_main:], ((0, 0), (0, LANES - heads)))], axis=1)
    main, dt = _norm_proj(x, _row(norm_mix), w_all.astype(BF16), jnp.zeros((1, LANES), F32),
                          jnp.zeros((8, LANES), F32), batch=batch, n_main=n_main, nsa_keys=False, n_extra=LANES,
                          extra_sigmoid=False)
    y = _ssd(main, dt, conv_w, _row(conv_b), _row(dt_bias, LANES), _row(a_log, LANES),
             _row(jnp.repeat(d_skip, SSM_HEADDIM)), _row(norm_g), batch=batch, d_inner=d_inner)
    return [y]


def kernel(x, mem, norm_mix, norm_cross, norm_mem, norm_ffn, ev_w_in, ev_rg_conv_w, ev_rg_conv_b, ev_rg_wa, ev_rg_ba, ev_rg_wx, ev_rg_bx, ev_rg_lambda, ev_nsa_gate_b, ev_q_norm, ev_k_norm, ev_cmp_pos, ev_cmp_k_w1, ev_cmp_k_w2, ev_cmp_v_w1, ev_cmp_v_w2, ev_w_out, od_w_in, od_conv_w, od_conv_b, od_dt_bias, od_a_log, od_d_skip, od_norm, od_w_out, x_wq, x_wkv, x_q_norm, x_k_norm, x_wo, ff_w13, ff_w2, moe_router, moe_w13, moe_w2):
    batch, seq, d = x.shape
    depth = norm_mix.shape[0]
    xf = x.reshape(batch * seq, d)
    for layer in range(depth):
        i = layer // 2
        if layer % 2 == 0:
            w_out = ev_w_out[i]
            mixed = _even_layer(xf, batch, norm_mix[layer], ev_w_in[i], ev_rg_conv_w[i], ev_rg_conv_b[i], ev_rg_wa[i],
                                ev_rg_ba[i], ev_rg_wx[i], ev_rg_bx[i], ev_rg_lambda[i], ev_nsa_gate_b[i],
                                ev_q_norm[i], ev_k_norm[i], ev_cmp_pos[i], ev_cmp_k_w1[i], ev_cmp_k_w2[i],
                                ev_cmp_v_w1[i], ev_cmp_v_w2[i])
        else:
            w_out = od_w_out[i]
            mixed = _odd_layer(xf, batch, norm_mix[layer], od_w_in[i], od_conv_w[i], od_conv_b[i], od_dt_bias[i],
                               od_a_log[i], od_d_skip[i], od_norm[i], w_out.shape[0])
        k, v = _mem_kv(mem, _row(norm_mem[layer]), x_wkv[layer].astype(BF16), _row(x_k_norm[layer]))
        xf = _cross_attn(mixed, w_out.astype(BF16), xf, _row(norm_cross[layer]), x_wq[layer].astype(BF16),
                         _row(x_q_norm[layer]), k, v, x_wo[layer].astype(BF16))
        if layer % 2 == 0:
            xf = _swiglu(xf, _row(norm_ffn[layer]), ff_w13[i].astype(BF16), ff_w2[i].astype(BF16))
        else:
            xf = _moe(xf, _row(norm_ffn[layer]), moe_router[i], moe_w13[i], moe_w2[i])
    return xf.reshape(batch, seq, d)
```

```python
import functools
import math

import jax
import jax.numpy as jnp
import numpy as np
from jax import lax
from jax.experimental import pallas as pl
from jax.experimental.pallas import tpu as pltpu

F32 = jnp.float32
BF16 = jnp.bfloat16
I32 = jnp.int32

EPS = 1e-6
CONV_W = 4
RG_BLOCKS = 8
RG_C = 8.0
NSA_HEADS = 8
NSA_GROUPS = 2
NSA_REP = NSA_HEADS // NSA_GROUPS
NSA_DK = 128
CMP_LEN = 32
CMP_STRIDE = 16
SLC_LEN = 64
SLC_SHIFT = 6
SLC_WIDE = 4
N_FORCED = 3
SLC_TOPN = 16
WINDOW = 512
SSM_HEADDIM = 64
SSM_GROUPS = 4
SSM_STATE = 128
SSD_CHUNK = 128
X_HEADS = 4
X_HEADDIM = 128
N_EXPERTS = 8
EXPERT_CHUNK = 256
DMA_UNROLL = 8

LANES = 128
SUBLANES = 8
VMEM_LIMIT_BYTES = 56 * 1024 * 1024
NEG = -1e30
SEL_BIAS = float(2 ** 20)
LOG2E = math.log2(math.e)
FIXED_SHIFT_MAX = 56.0

NT_DIMS = (((1,), (1,)), ((), ()))


def _params(*sem):
    return pltpu.CompilerParams(dimension_semantics=sem, vmem_limit_bytes=VMEM_LIMIT_BYTES)


def _dot(a, b):
    return jnp.dot(a, b, preferred_element_type=F32)


def _dot_nt(a, b):
    return lax.dot_general(a, b, NT_DIMS, preferred_element_type=F32)


def _rms(x, g):
    return x * lax.rsqrt(jnp.mean(x * x, axis=-1, keepdims=True) + EPS) * g


def _sigmoid(x):
    return 1.0 / (1.0 + jnp.exp(-x))


def _silu(x):
    return x * _sigmoid(x)


def _gelu_tanh(x):
    c = math.sqrt(2.0 / math.pi)
    return 0.5 * x * (1.0 + jnp.tanh(c * (x + 0.044715 * (x * x * x))))


def _softplus(x):
    return jnp.maximum(x, 0.0) + jnp.log(1.0 + jnp.exp(-jnp.abs(x)))


def _causal_conv(xb, x, tail, w_ref, b_ref):
    n = xb.shape[0]
    delay = lax.broadcasted_iota(I32, (n, n), 0) - lax.broadcasted_iota(I32, (n, n), 1)
    r8 = lax.broadcasted_iota(I32, (SUBLANES, 1), 0)
    y = b_ref[...] + w_ref[CONV_W - 1:CONV_W, :] * x
    head = jnp.zeros(tail.shape, F32)
    for k in range(1, CONV_W):
        wk = w_ref[CONV_W - 1 - k:CONV_W - k, :]
        y = y + wk * _dot(jnp.where(delay == k, 1.0, 0.0).astype(BF16), xb)
        head = head + wk * jnp.where(r8 < k, pltpu.roll(tail, k, 0), 0.0)
    return jnp.concatenate([y[0:SUBLANES] + head, y[SUBLANES:]], axis=0)


def _norm_proj_kernel(x_ref, g_ref, w_ref, eb_ref, kn_ref, *out_refs, n_main, nsa_keys, tiles_per_seq,
                      extra_sigmoid):
    h = _rms(x_ref[...], g_ref[...]).astype(BF16)
    main_ref = out_refs[0]
    for c0 in range(0, n_main, 512):
        main_ref[:, c0:c0 + 512] = _dot(h, w_ref[:, c0:c0 + 512]).astype(main_ref.dtype)
    col = n_main
    ex_ref = out_refs[-1]
    if nsa_keys:
        cmp_ref, kp_ref, vp_ref, kwn_ref, vwp_ref = out_refs[1:6]
        tm = x_ref.shape[0]
        g = NSA_GROUPS
        gsl = lambda r, p: r[:, p * LANES:(p + 1) * LANES]
        r = _dot(h, w_ref[:, col:col + 2 * g * LANES])
        for p in range(2 * g):
            cmp_ref[p] = gsl(r, p).astype(BF16)
        col += 2 * g * LANES
        t0 = (pl.program_id(0) % tiles_per_seq) * tm
        blk = jnp.right_shift(t0 + lax.broadcasted_iota(I32, (tm, LANES), 0), SLC_SHIFT)
        onehot = jnp.where(blk == lax.broadcasted_iota(I32, (tm, LANES), 1), 1.0, 0.0).astype(BF16)
        ones = jnp.ones((tm, LANES), BF16)
        r = _dot(h, w_ref[:, col:col + 2 * g * LANES])
        for gi in range(g):
            kp_ref[gi] = jnp.concatenate([_rms(gsl(r, gi), kn_ref[1:2, :]).astype(BF16), onehot], axis=1)
            vp_ref[gi] = jnp.concatenate([gsl(r, g + gi).astype(BF16), ones], axis=1)
        col += 2 * g * LANES
        r = _dot(h, w_ref[:, col:col + 2 * g * LANES])
        for gi in range(g):
            kwn_ref[gi] = _rms(gsl(r, gi), kn_ref[2:3, :]).astype(BF16)
            vwp_ref[gi] = jnp.concatenate([gsl(r, g + gi).astype(BF16), ones], axis=1)
        col += 2 * g * LANES
    n_extra = ex_ref.shape[1]
    e = _dot(h, w_ref[:, col:col + n_extra]) + eb_ref[...]
    ex_ref[...] = _sigmoid(e) if extra_sigmoid else e


def _norm_proj(x, g, w, eb, kn, *, batch, n_main, nsa_keys, n_extra, extra_sigmoid, tm=512):
    t, d = x.shape
    seq = t // batch
    nt = seq // tm
    out_shape = [jax.ShapeDtypeStruct((t, n_main), BF16)]
    out_specs = [pl.BlockSpec((tm, n_main), lambda i: (i, 0))]
    if nsa_keys:
        for planes, width in ((2 * NSA_GROUPS, LANES), (NSA_GROUPS, 2 * LANES), (NSA_GROUPS, 2 * LANES),
                              (NSA_GROUPS, LANES), (NSA_GROUPS, 2 * LANES)):
            out_shape.append(jax.ShapeDtypeStruct((batch, planes, seq, width), BF16))
            out_specs.append(pl.BlockSpec((None, planes, tm, width), lambda i: (i // nt, 0, i % nt, 0)))
    out_shape.append(jax.ShapeDtypeStruct((t, n_extra), F32))
    out_specs.append(pl.BlockSpec((tm, n_extra), lambda i: (i, 0)))
    kern = functools.partial(_norm_proj_kernel, n_main=n_main, nsa_keys=nsa_keys, tiles_per_seq=nt,
                             extra_sigmoid=extra_sigmoid)
    return pl.pallas_call(
        kern,
        grid=(t // tm,),
        in_specs=[
            pl.BlockSpec((tm, d), lambda i: (i, 0)),
            pl.BlockSpec((1, d), lambda i: (0, 0)),
            pl.BlockSpec(w.shape, lambda i: (0, 0)),
            pl.BlockSpec((1, n_extra), lambda i: (0, 0)),
            pl.BlockSpec(kn.shape, lambda i: (0, 0)),
        ],
        out_specs=out_specs,
        out_shape=out_shape,
        compiler_params=_params("parallel"),
        name="norm_proj",
    )(x, g, w, eb, kn)


def _rglru_kernel(rx_ref, rg_ref, cw_ref, cb_ref, wa_ref, ba_ref, wx_ref, bx_ref, lam_ref, o_ref, tail_ref, h_ref):
    tc, c = rx_ref.shape

    @pl.when(pl.program_id(1) == 0)
    def _():
        tail_ref[...] = jnp.zeros_like(tail_ref)
        h_ref[...] = jnp.zeros_like(h_ref)

    xb = rx_ref[...]
    x = xb.astype(F32)
    xc = _causal_conv(xb, x, tail_ref[...], cw_ref, cb_ref)
    tail_ref[...] = x[tc - 8:tc, :]

    bw = c // RG_BLOCKS
    ra, rx = [], []
    for blk in range(RG_BLOCKS):
        xb = xc[:, blk * bw:(blk + 1) * bw].astype(BF16)
        ra.append(_dot(xb, wa_ref[blk]))
        rx.append(_dot(xb, wx_ref[blk]))
    r = _sigmoid(jnp.concatenate(ra, axis=1) + ba_ref[...])
    ig = _sigmoid(jnp.concatenate(rx, axis=1) + bx_ref[...])
    log_a = (-RG_C) * r * _softplus(-lam_ref[...])
    a = jnp.exp(log_a)
    z = 1.0 - a * a
    u = jnp.where(z > 0.0, z * lax.rsqrt(z), 0.0) * (ig * xc)

    in_group = lax.broadcasted_iota(I32, (tc, 1), 0) & (SUBLANES - 1)
    d = 1
    while d < SUBLANES:
        keep = in_group >= d
        a_sh = jnp.where(keep, pltpu.roll(a, d, 0), 1.0)
        u_sh = jnp.where(keep, pltpu.roll(u, d, 0), 0.0)
        u = a * u_sh + u
        a = a * a_sh
        d *= 2
    carry = h_ref[SUBLANES - 1:SUBLANES, :]
    groups = []
    for g0 in range(0, tc, SUBLANES):
        hg = u[g0:g0 + SUBLANES, :] + a[g0:g0 + SUBLANES, :] * carry
        carry = hg[SUBLANES - 1:SUBLANES, :]
        groups.append(hg)
    h_ref[...] = groups[-1]
    o_ref[...] = (_gelu_tanh(rg_ref[...].astype(F32)) * jnp.concatenate(groups, axis=0)).astype(o_ref.dtype)


def _rglru(main, cw, cb, wa, ba, wx, bx, lam, *, batch, tc=256):
    t = main.shape[0]
    c = cw.shape[1]
    nt = t // batch // tc
    vec = pl.BlockSpec((1, c), lambda b, i: (0, 0))
    blk = pl.BlockSpec(wa.shape, lambda b, i: (0, 0, 0))
    return pl.pallas_call(
        _rglru_kernel,
        grid=(batch, nt),
        in_specs=[
            pl.BlockSpec((tc, c), lambda b, i: (b * nt + i, 0)),
            pl.BlockSpec((tc, c), lambda b, i: (b * nt + i, 1)),
            pl.BlockSpec((CONV_W, c), lambda b, i: (0, 0)),
            vec, blk, vec, blk, vec, vec,
        ],
        out_specs=pl.BlockSpec((tc, c), lambda b, i: (b * nt + i, 0)),
        out_shape=jax.ShapeDtypeStruct((t, c), BF16),
        scratch_shapes=[pltpu.VMEM((8, c), F32), pltpu.VMEM((8, c), F32)],
        compiler_params=_params("parallel", "arbitrary"),
        name="rglru",
    )(main, main, cw, cb, wa, ba, wx, bx, lam)


def _nsa_compress_kernel(xk_ref, xv_ref, pos_ref, kw1_ref, kw2_ref, vw1_ref, vw2_ref, kn_ref, ov_ref, kc_ref, vc_ref):
    n, half = xk_ref.shape
    last = lax.broadcasted_iota(I32, (n, 1), 0) == n - 1
    pos = pos_ref[...]

    def compress(x_ref, w1_ref, w2_ref):
        x = x_ref[...]
        y0 = _dot(x, w1_ref[0:half, :])
        y1 = _dot(x, w1_ref[half:2 * half, :])
        y1_next = jnp.where(last, 0.0, pltpu.roll(y1, n - 1, 0))
        const = _dot(pos, w1_ref[...])[0:1, :]
        hid = _gelu_tanh(y0 + y1_next + const)
        return _dot(hid.astype(BF16), w2_ref[...])

    kc_ref[...] = _rms(compress(xk_ref, kw1_ref, kw2_ref), kn_ref[0:1, :]).astype(BF16)
    vc = compress(xv_ref, vw1_ref, vw2_ref).astype(BF16)
    vc_ref[...] = jnp.concatenate([vc, jnp.ones((n, LANES), BF16), ov_ref[...]], axis=1)


def _nsa_compress(planes, pos_flat, kw1, kw2, vw1, vw2, k_norm, overlap):
    b, _, seq, _ = planes.shape
    g = NSA_GROUPS
    n = seq // CMP_STRIDE
    half = CMP_STRIDE * LANES
    grouped = planes.reshape(b, 2 * g, n, half)
    full = lambda a: pl.BlockSpec(a.shape, lambda bi, gi: (0,) * a.ndim)
    out_spec = lambda w: pl.BlockSpec((None, None, n, w), lambda bi, gi: (bi, gi, 0, 0))
    out_shape = lambda w: jax.ShapeDtypeStruct((b, g, n, w), BF16)
    return pl.pallas_call(
        _nsa_compress_kernel,
        grid=(b, g),
        in_specs=[
            pl.BlockSpec((None, None, n, half), lambda bi, gi: (bi, gi, 0, 0)),
            pl.BlockSpec((None, None, n, half), lambda bi, gi: (bi, 2 + gi, 0, 0)),
            full(pos_flat), full(kw1), full(kw2), full(vw1), full(vw2), full(k_norm), full(overlap),
        ],
        out_specs=[out_spec(LANES), out_spec(3 * LANES)],
        out_shape=[out_shape(LANES), out_shape(3 * LANES)],
        compiler_params=_params("parallel", "parallel"),
        name="nsa_compress",
    )(grouped, grouped, pos_flat, kw1, kw2, vw1, vw2, k_norm, overlap)


def _nsa_cw_kernel(shift_ref, *refs, qn, n_sel):
    fixed = jnp.maximum(shift_ref[1], shift_ref[2]) <= FIXED_SHIFT_MAX
    pl.when(fixed)(functools.partial(_nsa_cw_body, shift_ref, *refs, qn=qn, n_sel=n_sel, fixed=True))
    pl.when(jnp.logical_not(fixed))(functools.partial(_nsa_cw_body, shift_ref, *refs, qn=qn, n_sel=n_sel, fixed=False))


def _nsa_cw_body(shift_ref, q_ref, kc_ref, vc_ref, kw_ref, vw_ref, gt_ref, qn_ref, wb_ref, qp_ref, o_ref, *, qn, n_sel,
                 fixed):
    t0 = pl.program_id(2) * qn
    rep = NSA_REP
    rows = rep * qn
    scale = NSA_DK ** -0.5 * LOG2E
    qf = q_ref[...].astype(F32)
    heads = []
    for r in range(rep):
        qh = _rms(qf[:, r * LANES:(r + 1) * LANES], qn_ref[...]) * scale
        heads.append(qh.astype(BF16))
    qs = jnp.concatenate(heads, axis=0)
    trow = t0 + (lax.broadcasted_iota(I32, (rows, 1), 0) & (qn - 1))

    def attend(sm, vx, shift):
        if not fixed:
            sm = sm - jnp.maximum(jnp.max(sm, axis=-1, keepdims=True), -2.0 * shift)
        return _dot(jnp.exp2(sm).astype(BF16), vx)

    n_cmp = kc_ref.shape[0]
    visible = lax.broadcasted_iota(I32, (1, n_cmp), 1) * CMP_STRIDE + (CMP_LEN - 1) <= trow
    r_cmp = attend(jnp.where(visible, _dot_nt(qs, kc_ref[...]), NEG) - shift_ref[1], vc_ref[...], shift_ref[1])
    inv = 1.0 / jnp.maximum(r_cmp[:, LANES:2 * LANES], 1e-30)
    o_cmp = r_cmp[:, 0:LANES] * inv
    imp_h = r_cmp[:, 2 * LANES:3 * LANES] * inv
    imp = imp_h[0:qn]
    for r in range(1, rep):
        imp = imp + imp_h[r * qn:(r + 1) * qn]

    imp_t = imp.T
    jj = lax.broadcasted_iota(I32, imp_t.shape, 0).astype(F32)
    cur = jnp.right_shift(t0 + lax.broadcasted_iota(I32, imp_t.shape, 1), SLC_SHIFT).astype(F32)
    forced = (jj == 0.0) | (jj == cur) | (jj == cur - 1.0)
    shift = -shift_ref[0]
    taken = forced & (jj <= cur)
    work = jnp.where(taken, -2.0, jnp.where(jj <= cur, imp_t, -1.0))
    bias_t = jnp.where(taken, shift, -SEL_BIAS)
    for _ in range(n_sel - N_FORCED):
        m = jnp.max(work, axis=0, keepdims=True)
        idx = jnp.min(jnp.where(work == m, jj, float(LANES)), axis=0, keepdims=True)
        pick = jj == idx
        bias_t = jnp.where(pick, shift, bias_t)
        work = jnp.where(pick, -2.0, work)
    bias = bias_t.T.astype(BF16)
    for r in range(rep):
        qp_ref[r] = jnp.concatenate([heads[r], bias], axis=1)

    span = WINDOW + qn
    start = pl.multiple_of(jnp.maximum(t0 - WINDOW, 0), qn)
    s = _dot_nt(qs, kw_ref[pl.ds(start, span), :]).reshape(rep, qn, span) + wb_ref[...]
    r_win = attend(s.reshape(rows, span), vw_ref[pl.ds(start, span), :], shift_ref[2])
    o_win = r_win[:, 0:LANES] / jnp.maximum(r_win[:, LANES:2 * LANES], 1e-30)

    gt = gt_ref[...]
    for r in range(rep):
        sl = slice(r * qn, (r + 1) * qn)
        o = gt[:, 3 * r:3 * r + 1] * o_cmp[sl] + gt[:, 3 * r + 2:3 * r + 3] * o_win[sl]
        o_ref[:, r * LANES:(r + 1) * LANES] = o.astype(o_ref.dtype)


def _nsa_cw(shift, main, kcmp, vcmp, kwn, vwp, gates, q_norm, *, qn=256):
    b, g, seq, _ = kwn.shape
    t = main.shape[0]
    nq = seq // qn
    rep = NSA_REP
    gw = rep * LANES
    q_blk0 = (main.shape[1] - NSA_HEADS * NSA_DK) // gw
    n_cmp = kcmp.shape[2]
    n_case = WINDOW // qn + 1
    span = WINDOW + qn
    in_window = []
    for case in range(n_case):
        t0 = case * qn
        diff = (t0 + np.arange(qn)[:, None]) - (max(t0 - WINDOW, 0) + np.arange(span)[None, :])
        in_window.append((diff >= 0) & (diff < WINDOW))
    window_bias = jnp.where(jnp.asarray(np.stack(in_window)), -shift[2], NEG).astype(F32)
    kern = functools.partial(_nsa_cw_kernel, qn=qn, n_sel=min(SLC_TOPN, seq // SLC_LEN))
    return pl.pallas_call(
        kern,
        grid=(b, g, nq),
        in_specs=[
            pl.BlockSpec(memory_space=pltpu.SMEM),
            pl.BlockSpec((qn, gw), lambda bi, gi, i: (bi * nq + i, q_blk0 + gi)),
            pl.BlockSpec((None, None, n_cmp, LANES), lambda bi, gi, i: (bi, gi, 0, 0)),
            pl.BlockSpec((None, None, n_cmp, 3 * LANES), lambda bi, gi, i: (bi, gi, 0, 0)),
            pl.BlockSpec((None, None, seq, LANES), lambda bi, gi, i: (bi, gi, 0, 0)),
            pl.BlockSpec((None, None, seq, 2 * LANES), lambda bi, gi, i: (bi, gi, 0, 0)),
            pl.BlockSpec((qn, LANES), lambda bi, gi, i: (bi * nq + i, gi)),
            pl.BlockSpec((1, LANES), lambda bi, gi, i: (0, 0)),
            pl.BlockSpec((None, qn, span), lambda bi, gi, i: (jnp.minimum(i, n_case - 1), 0, 0)),
        ],
        out_specs=[
            pl.BlockSpec((None, None, rep, qn, 2 * LANES), lambda bi, gi, i: (bi, gi, 0, i, 0)),
            pl.BlockSpec((qn, gw), lambda bi, gi, i: (bi * nq + i, gi)),
        ],
        out_shape=[
            jax.ShapeDtypeStruct((b, g, rep, seq, 2 * LANES), BF16),
            jax.ShapeDtypeStruct((t, NSA_HEADS * NSA_DK), BF16),
        ],
        compiler_params=_params("parallel", "parallel", "parallel"),
        name="nsa_cmp_win",
    )(shift, main, kcmp, vcmp, kwn, vwp, gates, q_norm, window_bias)


def _nsa_slc_kernel(shift_ref, qp_ref, kp_ref, vp_ref, gt_ref, ocw_ref, o_ref, m_ref, acc_ref, *, qn, tk):
    t0 = pl.program_id(2) * qn
    rep = NSA_REP
    rows = rep * qn
    last = (t0 + qn - 1) // tk
    acc_ref[...] = jnp.zeros_like(acc_ref)

    def scores(k0, width, causal):
        qp = qp_ref[...].reshape(rows, qp_ref.shape[2])
        s = _dot_nt(qp, kp_ref[pl.ds(k0, width), :])
        if causal:
            trow = t0 + (lax.broadcasted_iota(I32, (rows, 1), 0) & (qn - 1))
            s = jnp.where(k0 + lax.broadcasted_iota(I32, (1, width), 1) <= trow, s, -SEL_BIAS)
        return s, vp_ref[pl.ds(k0, width), :]

    def fixed_shift_step(k0, width, causal):
        s, v = scores(k0, width, causal)
        acc_ref[...] += _dot(jnp.exp2(s).astype(BF16), v)

    def running_max_step(k0, width, causal):
        s, v = scores(k0, width, causal)
        m_old = m_ref[...]
        m_new = jnp.maximum(m_old, jnp.max(s, axis=-1, keepdims=True))
        acc_ref[...] = jnp.exp2(m_old - m_new) * acc_ref[...] + _dot(jnp.exp2(s - m_new).astype(BF16), v)
        m_ref[...] = m_new

    def sweep(step):
        wide = SLC_WIDE * tk
        n_wide = last // SLC_WIDE
        lax.fori_loop(0, n_wide, lambda j, c: (step(pl.multiple_of(j * wide, wide), wide, False), c)[1], 0)
        diag0 = last * tk
        for left in range(SLC_WIDE):
            for sub in range(1, tk // qn + 1):
                pl.when((last - n_wide * SLC_WIDE == left) & (t0 + qn - diag0 == sub * qn))(
                    functools.partial(step, pl.multiple_of(n_wide * wide, tk), left * tk + sub * qn, True))

    fixed = shift_ref[0] <= FIXED_SHIFT_MAX

    @pl.when(fixed)
    def _():
        sweep(fixed_shift_step)

    @pl.when(jnp.logical_not(fixed))
    def _():
        m_ref[...] = jnp.full(m_ref.shape, NEG, F32)
        sweep(running_max_step)

    o_slc = acc_ref[:, 0:LANES] / jnp.maximum(acc_ref[:, LANES:2 * LANES], 1e-30)
    gt = gt_ref[...]
    for r in range(rep):
        o = ocw_ref[:, r * LANES:(r + 1) * LANES].astype(F32) + gt[:, 3 * r + 1:3 * r + 2] * o_slc[r * qn:(r + 1) * qn]
        o_ref[:, r * LANES:(r + 1) * LANES] = o.astype(o_ref.dtype)


def _nsa_slc(shift, qp, kp, vp, gates, ocw, *, qn=256, tk=512):
    b, g, rep, seq, dqk = qp.shape
    t = ocw.shape[0]
    nq = seq // qn
    gw = rep * LANES
    tk = min(tk, seq)
    kern = functools.partial(_nsa_slc_kernel, qn=qn, tk=tk)
    return pl.pallas_call(
        kern,
        grid=(b, g, nq),
        in_specs=[
            pl.BlockSpec(memory_space=pltpu.SMEM),
            pl.BlockSpec((None, None, rep, qn, dqk), lambda bi, gi, i: (bi, gi, 0, i, 0)),
            pl.BlockSpec((None, None, seq, dqk), lambda bi, gi, i: (bi, gi, 0, 0)),
            pl.BlockSpec((None, None, seq, 2 * LANES), lambda bi, gi, i: (bi, gi, 0, 0)),
            pl.BlockSpec((qn, LANES), lambda bi, gi, i: (bi * nq + i, gi)),
            pl.BlockSpec((qn, gw), lambda bi, gi, i: (bi * nq + i, gi)),
        ],
        out_specs=pl.BlockSpec((qn, gw), lambda bi, gi, i: (bi * nq + i, gi)),
        out_shape=jax.ShapeDtypeStruct((t, NSA_HEADS * NSA_DK), BF16),
        scratch_shapes=[pltpu.VMEM((rep * qn, 1), F32), pltpu.VMEM((rep * qn, 2 * LANES), F32)],
        compiler_params=_params("parallel", "parallel", "parallel"),
        name="nsa_selected",
    )(shift, qp, kp, vp, gates, ocw)


def _mem_kv_kernel(mem_ref, g_ref, wkv_ref, kn_ref, k_ref, v_ref):
    memn = _rms(mem_ref[...], g_ref[...]).astype(BF16)
    kv = _dot(memn, wkv_ref[...])
    inner = k_ref.shape[1]
    for h in range(inner // X_HEADDIM):
        sl = slice(h * X_HEADDIM, (h + 1) * X_HEADDIM)
        k_ref[:, sl] = _rms(kv[:, sl], kn_ref[...]).astype(BF16)
    v_ref[...] = kv[:, inner:].astype(BF16)


def _mem_kv(mem, g, wkv, kn):
    b, m, d = mem.shape
    inner = wkv.shape[1] // 2
    out_spec = pl.BlockSpec((None, m, inner), lambda bi: (bi, 0, 0))
    out_shape = jax.ShapeDtypeStruct((b, m, inner), BF16)
    return pl.pallas_call(
        _mem_kv_kernel,
        grid=(b,),
        in_specs=[
            pl.BlockSpec((None, m, d), lambda bi: (bi, 0, 0)),
            pl.BlockSpec((1, d), lambda bi: (0, 0)),
            pl.BlockSpec(wkv.shape, lambda bi: (0, 0)),
            pl.BlockSpec((1, X_HEADDIM), lambda bi: (0, 0)),
        ],
        out_specs=[out_spec, out_spec],
        out_shape=[out_shape, out_shape],
        compiler_params=_params("parallel"),
        name="mem_kv",
    )(mem, g, wkv, kn)


def _cross_attn_kernel(*refs, n_in):
    a_refs = refs[:n_in]
    w_ref, x_ref, g_ref, wq_ref, qn_ref, k_ref, v_ref, wo_ref, o_ref = refs[n_in:]
    x = x_ref[...]
    k0 = 0
    for a_ref in a_refs:
        x = x + _dot(a_ref[...], w_ref[k0:k0 + a_ref.shape[1], :])
        k0 += a_ref.shape[1]
    q = _dot(_rms(x, g_ref[...]).astype(BF16), wq_ref[...])
    scale = X_HEADDIM ** -0.5
    outs = []
    for h in range(q.shape[1] // X_HEADDIM):
        sl = slice(h * X_HEADDIM, (h + 1) * X_HEADDIM)
        qh = (_rms(q[:, sl], qn_ref[...]) * scale).astype(BF16)
        s = _dot_nt(qh, k_ref[:, sl])
        e = jnp.exp(s - jnp.max(s, axis=-1, keepdims=True))
        p = e / jnp.sum(e, axis=-1, keepdims=True)
        outs.append(_dot(p.astype(BF16), v_ref[:, sl]))
    o = jnp.concatenate(outs, axis=1).astype(BF16)
    o_ref[...] = x + _dot(o, wo_ref[...])


def _cross_attn(acts, w_out, x, g, wq, qn, k, v, wo, *, tm=512):
    t, d = x.shape
    b, m, inner = k.shape
    nt = t // b // tm
    return pl.pallas_call(
        functools.partial(_cross_attn_kernel, n_in=len(acts)),
        grid=(b, nt),
        in_specs=[pl.BlockSpec((tm, a.shape[1]), lambda bi, i: (bi * nt + i, 0)) for a in acts] + [
            pl.BlockSpec(w_out.shape, lambda bi, i: (0, 0)),
            pl.BlockSpec((tm, d), lambda bi, i: (bi * nt + i, 0)),
            pl.BlockSpec((1, d), lambda bi, i: (0, 0)),
            pl.BlockSpec(wq.shape, lambda bi, i: (0, 0)),
            pl.BlockSpec((1, X_HEADDIM), lambda bi, i: (0, 0)),
            pl.BlockSpec((None, m, inner), lambda bi, i: (bi, 0, 0)),
            pl.BlockSpec((None, m, inner), lambda bi, i: (bi, 0, 0)),
            pl.BlockSpec(wo.shape, lambda bi, i: (0, 0)),
        ],
        out_specs=pl.BlockSpec((tm, d), lambda bi, i: (bi * nt + i, 0)),
        out_shape=jax.ShapeDtypeStruct((t, d), F32),
        compiler_params=_params("parallel", "parallel"),
        name="cross_attn",
    )(*acts, w_out, x, g, wq, qn, k, v, wo)


def _swiglu_kernel(x_ref, g_ref, wu_ref, wg_ref, w2_ref, o_ref, h_ref):
    @pl.when(pl.program_id(1) == 0)
    def _():
        h_ref[...] = _rms(x_ref[...], g_ref[...]).astype(BF16)
        o_ref[...] = x_ref[...]

    h = h_ref[...]
    tf = wu_ref.shape[1]
    half = tf // 2 // LANES * LANES
    acts = []
    for c in (slice(0, half), slice(half, tf)):
        acts.append((_silu(_dot(h, wg_ref[:, c])) * _dot(h, wu_ref[:, c])).astype(BF16))
    act = jnp.concatenate(acts, axis=1)
    for c0 in range(0, o_ref.shape[1], EXPERT_CHUNK):
        c = slice(c0, c0 + EXPERT_CHUNK)
        o_ref[:, c] += _dot(act, w2_ref[:, c])


def _swiglu(x, g, w13, w2, *, tm=1024, tf=1408):
    t, d = x.shape
    ff = w2.shape[0]
    nf = ff // tf
    return pl.pallas_call(
        _swiglu_kernel,
        grid=(t // tm, nf),
        in_specs=[
            pl.BlockSpec((tm, d), lambda i, f: (i, 0)),
            pl.BlockSpec((1, d), lambda i, f: (0, 0)),
            pl.BlockSpec((d, tf), lambda i, f: (0, f)),
            pl.BlockSpec((d, tf), lambda i, f: (0, nf + f)),
            pl.BlockSpec((tf, d), lambda i, f: (f, 0)),
        ],
        out_specs=pl.BlockSpec((tm, d), lambda i, f: (i, 0)),
        out_shape=jax.ShapeDtypeStruct((t, d), F32),
        scratch_shapes=[pltpu.VMEM((tm, d), BF16)],
        compiler_params=_params("parallel", "arbitrary"),
        name="swiglu",
    )(x, g, w13, w13, w2)


def _split3(x):
    a = x.astype(BF16)
    r = x - a.astype(F32)
    b = r.astype(BF16)
    c = (r - b.astype(F32)).astype(BF16)
    return a, b, c


def _ssd_kernel(z0_ref, z1_ref, x0_ref, x1_ref, bc_ref, dt_ref, cw_ref, cb_ref, dtb_ref, alog_ref, dskip_ref,
                ng_ref, o_ref, tail_ref, state_ref):
    q = x0_ref.shape[0]
    d_inner = o_ref.shape[1]
    gn = SSM_GROUPS * SSM_STATE
    hpg = d_inner // SSM_HEADDIM // SSM_GROUPS

    @pl.when(pl.program_id(1) == 0)
    def _():
        tail_ref[...] = jnp.zeros_like(tail_ref)
        state_ref[...] = jnp.zeros_like(state_ref)

    raw_b = jnp.concatenate([x0_ref[...], x1_ref[...], bc_ref[...]], axis=1)
    raw = raw_b.astype(F32)
    xbc = _silu(_causal_conv(raw_b, raw, tail_ref[...], cw_ref, cb_ref))
    tail_ref[...] = raw[q - 8:q, :]
    xs = xbc[:, :d_inner]
    bm = xbc[:, d_inner:d_inner + gn]
    cm = xbc[:, d_inner + gn:]

    dt = _softplus(dt_ref[...] + dtb_ref[...])
    a = dt * (-jnp.exp(alog_ref[...]))
    ri = lax.broadcasted_iota(I32, (q, q), 0)
    ci = lax.broadcasted_iota(I32, (q, q), 1)
    causal = ci <= ri
    tri = jnp.where(causal, 1.0, 0.0).astype(BF16)
    a_cs = sum(_dot(tri, part) for part in _split3(a))
    a_cs_t = a_cs.T
    dt_t = dt.T
    lane = lax.broadcasted_iota(I32, (1, LANES), 1)
    lo = lane < SSM_HEADDIM

    y_parts = []
    for g in range(SSM_GROUPS):
        cg = cm[:, g * SSM_STATE:(g + 1) * SSM_STATE].astype(BF16)
        bg = bm[:, g * SSM_STATE:(g + 1) * SSM_STATE]
        gmat = _dot_nt(cg, bg.astype(BF16))
        bg_t = bg.T
        gw = hpg * SSM_HEADDIM
        prev = state_ref[:, g * gw:(g + 1) * gw]
        y_off = _dot(cg, prev.astype(BF16))
        for pr in range(hpg // 2):
            c0 = g * gw + pr * LANES
            x_pair = xs[:, c0:c0 + LANES]
            y_pair = dskip_ref[:, c0:c0 + LANES] * x_pair
            st_pair = jnp.zeros((SSM_STATE, LANES), F32)
            decay_pair = jnp.zeros((1, LANES), F32)
            for half in range(2):
                h = g * hpg + pr * 2 + half
                sel = lo if half == 0 else jnp.logical_not(lo)
                xh = jnp.where(sel, x_pair, 0.0).astype(BF16)
                row_cs = a_cs_t[h:h + 1, :]
                col_cs = a_cs[:, h:h + 1]
                row_dt = dt_t[h:h + 1, :]
                a_last = a_cs_t[h:h + 1, q - 1:q]
                dec = jnp.exp(jnp.where(causal, col_cs - row_cs, NEG))
                y_pair = y_pair + _dot((gmat * dec * row_dt).astype(BF16), xh)
                w_row = jnp.exp(a_last - row_cs) * row_dt
                st_pair = st_pair + _dot((bg_t * w_row).astype(BF16), xh)
                y_pair = y_pair + jnp.where(sel, jnp.exp(col_cs) * y_off[:, pr * LANES:(pr + 1) * LANES], 0.0)
                decay_pair = jnp.where(sel, jnp.exp(a_last), decay_pair)
            state_ref[:, c0:c0 + LANES] = decay_pair * state_ref[:, c0:c0 + LANES] + st_pair
            y_parts.append(y_pair)
    y = jnp.concatenate(y_parts, axis=1)

    z = jnp.concatenate([z0_ref[...], z1_ref[...]], axis=1).astype(F32)
    y = y * _silu(z)
    gsz = d_inner // SSM_GROUPS
    for g in range(SSM_GROUPS):
        sl = slice(g * gsz, (g + 1) * gsz)
        o_ref[:, sl] = _rms(y[:, sl], ng_ref[:, sl]).astype(o_ref.dtype)


def _ssd(main, dt, cw, cb, dtb, alog, dskip, ng, *, batch, d_inner):
    t = main.shape[0]
    q = SSD_CHUNK
    nc = t // batch // q
    conv_ch = cw.shape[1]
    half = d_inner // 2
    col = lambda j: pl.BlockSpec((q, half), lambda b, i: (b * nc + i, j))
    vec = lambda n: pl.BlockSpec((1, n), lambda b, i: (0, 0))
    return pl.pallas_call(
        _ssd_kernel,
        grid=(batch, nc),
        in_specs=[
            col(0), col(1), col(2), col(3), col(4),
            pl.BlockSpec((q, LANES), lambda b, i: (b * nc + i, 0)),
            pl.BlockSpec((CONV_W, conv_ch), lambda b, i: (0, 0)),
            vec(conv_ch), vec(LANES), vec(LANES), vec(d_inner), vec(d_inner),
        ],
        out_specs=pl.BlockSpec((q, d_inner), lambda b, i: (b * nc + i, 0)),
        out_shape=jax.ShapeDtypeStruct((t, d_inner), BF16),
        scratch_shapes=[pltpu.VMEM((8, conv_ch), F32), pltpu.VMEM((SSM_STATE, d_inner), F32)],
        compiler_params=_params("parallel", "arbitrary"),
        name="ssd",
    )(main, main, main, main, main, dt, cw, cb, dtb, alog, dskip, ng)


def _router_kernel(x_ref, g_ref, wr_ref, h_ref, info_ref, cnt_ref, run_ref):
    tm = x_ref.shape[0]

    @pl.when(pl.program_id(0) == 0)
    def _():
        run_ref[...] = jnp.zeros_like(run_ref)

    h = _rms(x_ref[...], g_ref[...])
    h_ref[...] = h
    h_hi = h.astype(BF16)
    h_lo = (h - h_hi.astype(F32)).astype(BF16)
    w = wr_ref[...]
    w_hi = w.astype(BF16)
    w_lo = (w - w_hi.astype(F32)).astype(BF16)
    logits = _dot(h_hi, w_hi) + _dot(h_lo, w_hi) + _dot(h_hi, w_lo)
    lane = lax.broadcasted_iota(I32, (tm, LANES), 1)
    lg = jnp.where(lane < N_EXPERTS, logits, NEG)
    m1 = jnp.max(lg, axis=-1, keepdims=True)
    i1 = jnp.min(jnp.where(lg == m1, lane, LANES), axis=-1, keepdims=True)
    lg2 = jnp.where(lane == i1, NEG, lg)
    m2 = jnp.max(lg2, axis=-1, keepdims=True)
    i2 = jnp.min(jnp.where(lg2 == m2, lane, LANES), axis=-1, keepdims=True)
    e2 = jnp.exp(m2 - m1)
    w1 = 1.0 / (1.0 + e2)
    w2 = e2 / (1.0 + e2)

    hot1 = lane == i1
    hot2 = lane == i2
    hot = jnp.where(hot1 | hot2, 1.0, 0.0)
    ri = lax.broadcasted_iota(I32, (tm, tm), 0)
    ci = lax.broadcasted_iota(I32, (tm, tm), 1)
    before = jnp.where(ci < ri, 1.0, 0.0).astype(BF16)
    seen = run_ref[0:1, :] + _dot(before, hot.astype(BF16))
    rank1 = jnp.sum(jnp.where(hot1, seen, 0.0), axis=-1, keepdims=True)
    rank2 = jnp.sum(jnp.where(hot2, seen, 0.0), axis=-1, keepdims=True)
    run_ref[...] = run_ref[...] + jnp.sum(hot, axis=0, keepdims=True)
    cnt_ref[...] = run_ref[...]

    cols = [i1.astype(F32), i2.astype(F32), w1, w2, rank1, rank2]
    info = jnp.zeros((tm, LANES), F32)
    for c, v in enumerate(cols):
        info = jnp.where(lane == c, v, info)
    info_ref[...] = info


def _router(x, g, wr, *, tm=256):
    t, d = x.shape
    return pl.pallas_call(
        _router_kernel,
        grid=(t // tm,),
        in_specs=[
            pl.BlockSpec((tm, d), lambda i: (i, 0)),
            pl.BlockSpec((1, d), lambda i: (0, 0)),
            pl.BlockSpec(wr.shape, lambda i: (0, 0)),
        ],
        out_specs=[
            pl.BlockSpec((tm, d), lambda i: (i, 0)),
            pl.BlockSpec((tm, LANES), lambda i: (i, 0)),
            pl.BlockSpec((8, LANES), lambda i: (0, 0)),
        ],
        out_shape=[
            jax.ShapeDtypeStruct((t, d), F32),
            jax.ShapeDtypeStruct((t, LANES), F32),
            jax.ShapeDtypeStruct((8, LANES), F32),
        ],
        scratch_shapes=[pltpu.VMEM((8, LANES), F32)],
        compiler_params=_params("arbitrary"),
        name="moe_router",
    )(x, g, wr)


def _row_copy(src_ref, src_row, dst_ref, dst_row, sem):
    return pltpu.make_async_copy(src_ref.at[pl.ds(src_row, 1)], dst_ref.at[pl.ds(dst_row, 1)], sem)


def _dispatch_kernel(pos_ref, h_ref, init_ref, xs_ref, sem):
    del init_ref
    tt = h_ref.shape[0]

    def issue(r, c):
        _row_copy(h_ref, r, xs_ref, pos_ref[0, r], sem).start()
        _row_copy(h_ref, r, xs_ref, pos_ref[1, r], sem).start(priority=1)
        return c

    lax.fori_loop(0, tt, issue, 0, unroll=DMA_UNROLL)
    for _ in range(2):
        pltpu.make_async_copy(h_ref, xs_ref.at[pl.ds(0, tt)], sem).wait()


def _dispatch(h, pos, rows, *, tt=256):
    t, d = h.shape
    init = jnp.zeros((rows, d), h.dtype)
    return pl.pallas_call(
        _dispatch_kernel,
        grid=(t // tt,),
        in_specs=[
            pl.BlockSpec((None, 2, tt), lambda i: (i, 0, 0), memory_space=pltpu.SMEM),
            pl.BlockSpec((tt, d), lambda i: (i, 0)),
            pl.BlockSpec(memory_space=pl.ANY),
        ],
        out_specs=pl.BlockSpec(memory_space=pl.ANY),
        out_shape=jax.ShapeDtypeStruct((rows, d), h.dtype),
        scratch_shapes=[pltpu.SemaphoreType.DMA(())],
        input_output_aliases={2: 0},
        compiler_params=_params("arbitrary"),
        name="moe_dispatch",
    )(pos, h, init)


def _expert_kernel(te_ref, tv_ref, xs_ref, wu_ref, wg_ref, w2_ref, o_ref, xb_ref):
    del te_ref
    i = pl.program_id(0)
    f = pl.program_id(1)
    live = tv_ref[i] > 0

    @pl.when(f == 0)
    def _():
        o_ref[...] = jnp.zeros_like(o_ref)
        xb_ref[...] = xs_ref[...].astype(BF16)

    @pl.when(live)
    def _():
        x = xb_ref[...]
        act = (_silu(_dot(x, wg_ref[...].astype(BF16))) * _dot(x, wu_ref[...].astype(BF16))).astype(BF16)
        for c0 in range(0, o_ref.shape[1], EXPERT_CHUNK):
            c = slice(c0, c0 + EXPERT_CHUNK)
            o_ref[:, c] += _dot(act, w2_ref[:, c].astype(BF16))


def _experts(xs, w13, w2, tile_expert, tile_live, *, tm, tf=512):
    rows = xs.shape[0]
    ff, d = w2.shape[1:]
    nf = ff // tf

    def f_of(i, f, te, tv):
        return jnp.where(tv[i] > 0, f, nf - 1)

    grid_spec = pltpu.PrefetchScalarGridSpec(
        num_scalar_prefetch=2,
        grid=(rows // tm, nf),
        in_specs=[
            pl.BlockSpec((tm, xs.shape[1]), lambda i, f, te, tv: (i, 0)),
            pl.BlockSpec((None, d, tf), lambda i, f, te, tv: (te[i], 0, f_of(i, f, te, tv))),
            pl.BlockSpec((None, d, tf), lambda i, f, te, tv: (te[i], 0, nf + f_of(i, f, te, tv))),
            pl.BlockSpec((None, tf, d), lambda i, f, te, tv: (te[i], f_of(i, f, te, tv), 0)),
        ],
        out_specs=pl.BlockSpec((tm, d), lambda i, f, te, tv: (i, 0)),
        scratch_shapes=[pltpu.VMEM((tm, d), BF16)],
    )
    return pl.pallas_call(
        _expert_kernel,
        grid_spec=grid_spec,
        out_shape=jax.ShapeDtypeStruct((rows, d), F32),
        compiler_params=_params("parallel", "arbitrary"),
        name="moe_experts",
    )(tile_expert, tile_live, xs, w13, w13, w2)


def _combine_kernel(pos_ref, x_ref, info_ref, ys_ref, o_ref, buf_ref, sem):
    tt = x_ref.shape[0]

    def issue(r, c):
        _row_copy(ys_ref, pos_ref[0, r], buf_ref.at[0], r, sem).start()
        _row_copy(ys_ref, pos_ref[1, r], buf_ref.at[1], r, sem).start(priority=1)
        return c

    lax.fori_loop(0, tt, issue, 0, unroll=DMA_UNROLL)
    for k in range(2):
        pltpu.make_async_copy(ys_ref.at[pl.ds(0, tt)], buf_ref.at[k], sem).wait()
    info = info_ref[...]
    o_ref[...] = x_ref[...] + info[:, 2:3] * buf_ref[0] + info[:, 3:4] * buf_ref[1]


def _combine(x, info, pos, ys, *, tt=256):
    t, d = x.shape
    return pl.pallas_call(
        _combine_kernel,
        grid=(t // tt,),
        in_specs=[
            pl.BlockSpec((None, 2, tt), lambda i: (i, 0, 0), memory_space=pltpu.SMEM),
            pl.BlockSpec((tt, d), lambda i: (i, 0)),
            pl.BlockSpec((tt, LANES), lambda i: (i, 0)),
            pl.BlockSpec(memory_space=pl.ANY),
        ],
        out_specs=pl.BlockSpec((tt, d), lambda i: (i, 0)),
        out_shape=jax.ShapeDtypeStruct((t, d), F32),
        scratch_shapes=[pltpu.VMEM((2, tt, d), F32), pltpu.SemaphoreType.DMA(())],
        compiler_params=_params("arbitrary"),
        name="moe_combine",
    )(pos, x, info, ys)


def _moe(x, g, router, w13, w2, *, tm=1024, tt=256):
    t, d = x.shape
    n_exp = router.shape[1]
    wr = jnp.pad(router, ((0, 0), (0, LANES - n_exp)))
    h, info, counts = _router(x, g, wr, tm=tt)

    counts = counts[0, :n_exp].astype(I32)
    seg = (counts + tm - 1) // tm * tm
    seg_end = jnp.cumsum(seg)
    seg_start = seg_end - seg
    e1 = info[:, 0].astype(I32)
    e2 = info[:, 1].astype(I32)
    pos = jnp.stack([seg_start[e1] + info[:, 4].astype(I32), seg_start[e2] + info[:, 5].astype(I32)], axis=0)
    pos = pos.reshape(2, t // tt, tt).transpose(1, 0, 2)
    rows = 2 * t + n_exp * tm
    tile_row0 = jnp.arange(rows // tm, dtype=I32) * tm
    tile_live = (tile_row0 < seg_end[-1]).astype(I32)
    tile_expert = jnp.sum((seg_end[None, :] <= tile_row0[:, None]).astype(I32), axis=1)
    tile_expert = jnp.minimum(tile_expert, n_exp - 1)
    last_live = jnp.maximum(jnp.sum(tile_live) - 1, 0)
    tile_expert = jnp.where(tile_live > 0, tile_expert, tile_expert[last_live])

    xs = _dispatch(h, pos, rows, tt=tt)
    ys = _experts(xs, w13, w2, tile_expert, tile_live, tm=tm)
    return _combine(x, info, pos, ys, tt=tt)


def _row(v, n=None):
    v = v.reshape(1, -1).astype(F32)
    if n is not None and v.shape[1] < n:
        v = jnp.pad(v, ((0, 0), (0, n - v.shape[1])))
    return v


def _overlap_matrix(seq):
    n = seq // CMP_STRIDE
    cmp_start = np.arange(n) * CMP_STRIDE
    slc_start = np.arange(LANES) * SLC_LEN
    ov = (cmp_start[:, None] <= slc_start[None, :] + SLC_LEN - 1) & (cmp_start[:, None] + CMP_LEN - 1 >= slc_start[None, :])
    ov[n - 1] = False
    return jnp.asarray(ov, dtype=BF16)


def _even_layer(x, batch, norm_mix, w_in, conv_w, conv_b, wa, ba, wx, bx, lam, gate_b, q_norm, k_norm, cmp_pos,
                ck_w1, ck_w2, cv_w1, cv_w2):
    t, d = x.shape
    seq = t // batch
    rg = wa.shape[0] * wa.shape[1]
    gdk = NSA_GROUPS * NSA_DK
    n_main = 2 * rg + NSA_HEADS * NSA_DK
    n_kv = 6 * gdk
    per_group = 3 * NSA_REP
    gate_cols = w_in[:, n_main + n_kv:].reshape(d, NSA_GROUPS, per_group)
    gate_cols = jnp.pad(gate_cols, ((0, 0), (0, 0), (0, LANES - per_group))).reshape(d, NSA_GROUPS * LANES)
    gate_bias = jnp.pad(gate_b.reshape(NSA_GROUPS, per_group), ((0, 0), (0, LANES - per_group))).reshape(1, -1)
    w_all = jnp.concatenate([w_in[:, :n_main + n_kv], gate_cols], axis=1).astype(BF16)
    kn = jnp.pad(k_norm, ((0, 8 - k_norm.shape[0]), (0, 0)))
    main, planes, kp, vp, kwn, vwp, gates = _norm_proj(
        x, _row(norm_mix), w_all, gate_bias, kn, batch=batch, n_main=n_main, nsa_keys=True,
        n_extra=NSA_GROUPS * LANES, extra_sigmoid=True)

    rg_out = _rglru(main, conv_w, _row(conv_b), wa.astype(BF16), _row(ba), wx.astype(BF16), _row(bx), _row(lam),
                    batch=batch)

    k_gain = jnp.max(jnp.abs(k_norm), axis=1)[jnp.array([1, 0, 2])]
    shift = ((1.02 * LOG2E * math.sqrt(NSA_DK)) * jnp.max(jnp.abs(q_norm)) * k_gain).astype(F32)
    pos_flat = jnp.broadcast_to(cmp_pos.reshape(1, -1), (8, CMP_LEN * NSA_DK)).astype(BF16)
    kcmp, vcmp = _nsa_compress(planes, pos_flat, ck_w1.astype(BF16), ck_w2.astype(BF16), cv_w1.astype(BF16),
                               cv_w2.astype(BF16), kn, _overlap_matrix(seq))
    qp, ocw = _nsa_cw(shift, main, kcmp, vcmp, kwn, vwp, gates, _row(q_norm))
    att = _nsa_slc(shift, qp, kp, vp, gates, ocw)
    return [rg_out, att]


def _odd_layer(x, batch, norm_mix, w_in, conv_w, conv_b, dt_bias, a_log, d_skip, norm_g, d_inner):
    conv_ch = conv_w.shape[1]
    n_main = d_inner + conv_ch
    heads = dt_bias.shape[0]
    w_all = jnp.concatenate([w_in[:, :n_main], jnp.pad(w_in[:, n_main:], ((0, 0), (0, LANES - heads)))], axis=1)
    main, dt = _norm_proj(x, _row(norm_mix), w_all.astype(BF16), jnp.zeros((1, LANES), F32),
                          jnp.zeros((8, LANES), F32), batch=batch, n_main=n_main, nsa_keys=False, n_extra=LANES,
                          extra_sigmoid=False)
    y = _ssd(main, dt, conv_w, _row(conv_b), _row(dt_bias, LANES), _row(a_log, LANES),
             _row(jnp.repeat(d_skip, SSM_HEADDIM)), _row(norm_g), batch=batch, d_inner=d_inner)
    return [y]


def kernel(x, mem, norm_mix, norm_cross, norm_mem, norm_ffn, ev_w_in, ev_rg_conv_w, ev_rg_conv_b, ev_rg_wa, ev_rg_ba, ev_rg_wx, ev_rg_bx, ev_rg_lambda, ev_nsa_gate_b, ev_q_norm, ev_k_norm, ev_cmp_pos, ev_cmp_k_w1, ev_cmp_k_w2, ev_cmp_v_w1, ev_cmp_v_w2, ev_w_out, od_w_in, od_conv_w, od_conv_b, od_dt_bias, od_a_log, od_d_skip, od_norm, od_w_out, x_wq, x_wkv, x_q_norm, x_k_norm, x_wo, ff_w13, ff_w2, moe_router, moe_w13, moe_w2):
    batch, seq, d = x.shape
    depth = norm_mix.shape[0]
    xf = x.reshape(batch * seq, d)
    for layer in range(depth):
        i = layer // 2
        if layer % 2 == 0:
            w_out = ev_w_out[i]
            mixed = _even_layer(xf, batch, norm_mix[layer], ev_w_in[i], ev_rg_conv_w[i], ev_rg_conv_b[i], ev_rg_wa[i],
                                ev_rg_ba[i], ev_rg_wx[i], ev_rg_bx[i], ev_rg_lambda[i], ev_nsa_gate_b[i],
                                ev_q_norm[i], ev_k_norm[i], ev_cmp_pos[i], ev_cmp_k_w1[i], ev_cmp_k_w2[i],
                                ev_cmp_v_w1[i], ev_cmp_v_w2[i])
        else:
            w_out = od_w_out[i]
            mixed = _odd_layer(xf, batch, norm_mix[layer], od_w_in[i], od_conv_w[i], od_conv_b[i], od_dt_bias[i],
                               od_a_log[i], od_d_skip[i], od_norm[i], w_out.shape[0])
        k, v = _mem_kv(mem, _row(norm_mem[layer]), x_wkv[layer].astype(BF16), _row(x_k_norm[layer]))
        xf = _cross_attn(mixed, w_out.astype(BF16), xf, _row(norm_cross[layer]), x_wq[layer].astype(BF16),
                         _row(x_q_norm[layer]), k, v, x_wo[layer].astype(BF16))
        if layer % 2 == 0:
            xf = _swiglu(xf, _row(norm_ffn[layer]), ff_w13[i].astype(BF16), ff_w2[i].astype(BF16))
        else:
            xf = _moe(xf, _row(norm_ffn[layer]), moe_router[i], moe_w13[i], moe_w2[i])
    return xf.reshape(batch, seq, d)
```

```python
import functools
import math

import jax
import jax.numpy as jnp
import numpy as np
from jax import lax
from jax.experimental import pallas as pl
from jax.experimental.pallas import tpu as pltpu

F32 = jnp.float32
BF16 = jnp.bfloat16
I32 = jnp.int32

EPS = 1e-6
CONV_W = 4
RG_BLOCKS = 8
RG_C = 8.0
NSA_HEADS = 8
NSA_GROUPS = 2
NSA_REP = NSA_HEADS // NSA_GROUPS
NSA_DK = 128
CMP_LEN = 32
CMP_STRIDE = 16
SLC_LEN = 64
SLC_SHIFT = 6
SLC_WIDE = 4
N_FORCED = 3
SLC_TOPN = 16
WINDOW = 512
SSM_HEADDIM = 64
SSM_GROUPS = 4
SSM_STATE = 128
SSD_CHUNK = 128
X_HEADS = 4
X_HEADDIM = 128
N_EXPERTS = 8
EXPERT_CHUNK = 256
DMA_UNROLL = 8

LANES = 128
SUBLANES = 8
VMEM_LIMIT_BYTES = 56 * 1024 * 1024
NEG = -1e30
SEL_BIAS = float(2 ** 20)
LOG2E = math.log2(math.e)
FIXED_SHIFT_MAX = 56.0

NT_DIMS = (((1,), (1,)), ((), ()))


def _params(*sem):
    return pltpu.CompilerParams(dimension_semantics=sem, vmem_limit_bytes=VMEM_LIMIT_BYTES)


def _dot(a, b):
    return jnp.dot(a, b, preferred_element_type=F32)


def _dot_nt(a, b):
    return lax.dot_general(a, b, NT_DIMS, preferred_element_type=F32)


def _rms(x, g):
    return x * lax.rsqrt(jnp.mean(x * x, axis=-1, keepdims=True) + EPS) * g


def _sigmoid(x):
    return 1.0 / (1.0 + jnp.exp(-x))


def _silu(x):
    return x * _sigmoid(x)


def _gelu_tanh(x):
    c = math.sqrt(2.0 / math.pi)
    return 0.5 * x * (1.0 + jnp.tanh(c * (x + 0.044715 * (x * x * x))))


def _softplus(x):
    return jnp.maximum(x, 0.0) + jnp.log(1.0 + jnp.exp(-jnp.abs(x)))


def _causal_conv(xb, x, tail, w_ref, b_ref):
    n = xb.shape[0]
    delay = lax.broadcasted_iota(I32, (n, n), 0) - lax.broadcasted_iota(I32, (n, n), 1)
    r8 = lax.broadcasted_iota(I32, (SUBLANES, 1), 0)
    y = b_ref[...] + w_ref[CONV_W - 1:CONV_W, :] * x
    head = jnp.zeros(tail.shape, F32)
    for k in range(1, CONV_W):
        wk = w_ref[CONV_W - 1 - k:CONV_W - k, :]
        y = y + wk * _dot(jnp.where(delay == k, 1.0, 0.0).astype(BF16), xb)
        head = head + wk * jnp.where(r8 < k, pltpu.roll(tail, k, 0), 0.0)
    return jnp.concatenate([y[0:SUBLANES] + head, y[SUBLANES:]], axis=0)


def _norm_proj_kernel(x_ref, g_ref, w_ref, eb_ref, kn_ref, *out_refs, n_main, nsa_keys, tiles_per_seq,
                      extra_sigmoid):
    h = _rms(x_ref[...], g_ref[...]).astype(BF16)
    main_ref = out_refs[0]
    for c0 in range(0, n_main, 512):
        main_ref[:, c0:c0 + 512] = _dot(h, w_ref[:, c0:c0 + 512]).astype(main_ref.dtype)
    col = n_main
    ex_ref = out_refs[-1]
    if nsa_keys:
        cmp_ref, kp_ref, vp_ref, kwn_ref, vwp_ref = out_refs[1:6]
        tm = x_ref.shape[0]
        g = NSA_GROUPS
        gsl = lambda r, p: r[:, p * LANES:(p + 1) * LANES]
        r = _dot(h, w_ref[:, col:col + 2 * g * LANES])
        for p in range(2 * g):
            cmp_ref[p] = gsl(r, p).astype(BF16)
        col += 2 * g * LANES
        t0 = (pl.program_id(0) % tiles_per_seq) * tm
        blk = jnp.right_shift(t0 + lax.broadcasted_iota(I32, (tm, LANES), 0), SLC_SHIFT)
        onehot = jnp.where(blk == lax.broadcasted_iota(I32, (tm, LANES), 1), 1.0, 0.0).astype(BF16)
        ones = jnp.ones((tm, LANES), BF16)
        r = _dot(h, w_ref[:, col:col + 2 * g * LANES])
        for gi in range(g):
            kp_ref[gi] = jnp.concatenate([_rms(gsl(r, gi), kn_ref[1:2, :]).astype(BF16), onehot], axis=1)
            vp_ref[gi] = jnp.concatenate([gsl(r, g + gi).astype(BF16), ones], axis=1)
        col += 2 * g * LANES
        r = _dot(h, w_ref[:, col:col + 2 * g * LANES])
        for gi in range(g):
            kwn_ref[gi] = _rms(gsl(r, gi), kn_ref[2:3, :]).astype(BF16)
            vwp_ref[gi] = jnp.concatenate([gsl(r, g + gi).astype(BF16), ones], axis=1)
        col += 2 * g * LANES
    n_extra = ex_ref.shape[1]
    e = _dot(h, w_ref[:, col:col + n_extra]) + eb_ref[...]
    ex_ref[...] = _sigmoid(e) if extra_sigmoid else e


def _norm_proj(x, g, w, eb, kn, *, batch, n_main, nsa_keys, n_extra, extra_sigmoid, tm=512):
    t, d = x.shape
    seq = t // batch
    nt = seq // tm
    out_shape = [jax.ShapeDtypeStruct((t, n_main), BF16)]
    out_specs = [pl.BlockSpec((tm, n_main), lambda i: (i, 0))]
    if nsa_keys:
        for planes, width in ((2 * NSA_GROUPS, LANES), (NSA_GROUPS, 2 * LANES), (NSA_GROUPS, 2 * LANES),
                              (NSA_GROUPS, LANES), (NSA_GROUPS, 2 * LANES)):
            out_shape.append(jax.ShapeDtypeStruct((batch, planes, seq, width), BF16))
            out_specs.append(pl.BlockSpec((None, planes, tm, width), lambda i: (i // nt, 0, i % nt, 0)))
    out_shape.append(jax.ShapeDtypeStruct((t, n_extra), F32))
    out_specs.append(pl.BlockSpec((tm, n_extra), lambda i: (i, 0)))
    kern = functools.partial(_norm_proj_kernel, n_main=n_main, nsa_keys=nsa_keys, tiles_per_seq=nt,
                             extra_sigmoid=extra_sigmoid)
    return pl.pallas_call(
        kern,
        grid=(t // tm,),
        in_specs=[
            pl.BlockSpec((tm, d), lambda i: (i, 0)),
            pl.BlockSpec((1, d), lambda i: (0, 0)),
            pl.BlockSpec(w.shape, lambda i: (0, 0)),
            pl.BlockSpec((1, n_extra), lambda i: (0, 0)),
            pl.BlockSpec(kn.shape, lambda i: (0, 0)),
        ],
        out_specs=out_specs,
        out_shape=out_shape,
        compiler_params=_params("parallel"),
        name="norm_proj",
    )(x, g, w, eb, kn)


def _rglru_kernel(rx_ref, rg_ref, cw_ref, cb_ref, wa_ref, ba_ref, wx_ref, bx_ref, lam_ref, o_ref, tail_ref, h_ref):
    tc, c = rx_ref.shape

    @pl.when(pl.program_id(1) == 0)
    def _():
        tail_ref[...] = jnp.zeros_like(tail_ref)
        h_ref[...] = jnp.zeros_like(h_ref)

    xb = rx_ref[...]
    x = xb.astype(F32)
    xc = _causal_conv(xb, x, tail_ref[...], cw_ref, cb_ref)
    tail_ref[...] = x[tc - 8:tc, :]

    bw = c // RG_BLOCKS
    ra, rx = [], []
    for blk in range(RG_BLOCKS):
        xb = xc[:, blk * bw:(blk + 1) * bw].astype(BF16)
        ra.append(_dot(xb, wa_ref[blk]))
        rx.append(_dot(xb, wx_ref[blk]))
    r = _sigmoid(jnp.concatenate(ra, axis=1) + ba_ref[...])
    ig = _sigmoid(jnp.concatenate(rx, axis=1) + bx_ref[...])
    log_a = (-RG_C) * r * _softplus(-lam_ref[...])
    a = jnp.exp(log_a)
    z = 1.0 - a * a
    u = jnp.where(z > 0.0, z * lax.rsqrt(z), 0.0) * (ig * xc)

    in_group = lax.broadcasted_iota(I32, (tc, 1), 0) & (SUBLANES - 1)
    d = 1
    while d < SUBLANES:
        keep = in_group >= d
        a_sh = jnp.where(keep, pltpu.roll(a, d, 0), 1.0)
        u_sh = jnp.where(keep, pltpu.roll(u, d, 0), 0.0)
        u = a * u_sh + u
        a = a * a_sh
        d *= 2
    carry = h_ref[SUBLANES - 1:SUBLANES, :]
    groups = []
    for g0 in range(0, tc, SUBLANES):
        hg = u[g0:g0 + SUBLANES, :] + a[g0:g0 + SUBLANES, :] * carry
        carry = hg[SUBLANES - 1:SUBLANES, :]
        groups.append(hg)
    h_ref[...] = groups[-1]
    o_ref[...] = (_gelu_tanh(rg_ref[...].astype(F32)) * jnp.concatenate(groups, axis=0)).astype(o_ref.dtype)


def _rglru(main, cw, cb, wa, ba, wx, bx, lam, *, batch, tc=256):
    t = main.shape[0]
    c = cw.shape[1]
    nt = t // batch // tc
    vec = pl.BlockSpec((1, c), lambda b, i: (0, 0))
    blk = pl.BlockSpec(wa.shape, lambda b, i: (0, 0, 0))
    return pl.pallas_call(
        _rglru_kernel,
        grid=(batch, nt),
        in_specs=[
            pl.BlockSpec((tc, c), lambda b, i: (b * nt + i, 0)),
            pl.BlockSpec((tc, c), lambda b, i: (b * nt + i, 1)),
            pl.BlockSpec((CONV_W, c), lambda b, i: (0, 0)),
            vec, blk, vec, blk, vec, vec,
        ],
        out_specs=pl.BlockSpec((tc, c), lambda b, i: (b * nt + i, 0)),
        out_shape=jax.ShapeDtypeStruct((t, c), BF16),
        scratch_shapes=[pltpu.VMEM((8, c), F32), pltpu.VMEM((8, c), F32)],
        compiler_params=_params("parallel", "arbitrary"),
        name="rglru",
    )(main, main, cw, cb, wa, ba, wx, bx, lam)


def _nsa_compress_kernel(xk_ref, xv_ref, pos_ref, kw1_ref, kw2_ref, vw1_ref, vw2_ref, kn_ref, ov_ref, kc_ref, vc_ref):
    n, half = xk_ref.shape
    last = lax.broadcasted_iota(I32, (n, 1), 0) == n - 1
    pos = pos_ref[...]

    def compress(x_ref, w1_ref, w2_ref):
        x = x_ref[...]
        y0 = _dot(x, w1_ref[0:half, :])
        y1 = _dot(x, w1_ref[half:2 * half, :])
        y1_next = jnp.where(last, 0.0, pltpu.roll(y1, n - 1, 0))
        const = _dot(pos, w1_ref[...])[0:1, :]
        hid = _gelu_tanh(y0 + y1_next + const)
        return _dot(hid.astype(BF16), w2_ref[...])

    kc_ref[...] = _rms(compress(xk_ref, kw1_ref, kw2_ref), kn_ref[0:1, :]).astype(BF16)
    vc = compress(xv_ref, vw1_ref, vw2_ref).astype(BF16)
    vc_ref[...] = jnp.concatenate([vc, jnp.ones((n, LANES), BF16), ov_ref[...]], axis=1)


def _nsa_compress(planes, pos_flat, kw1, kw2, vw1, vw2, k_norm, overlap):
    b, _, seq, _ = planes.shape
    g = NSA_GROUPS
    n = seq // CMP_STRIDE
    half = CMP_STRIDE * LANES
    grouped = planes.reshape(b, 2 * g, n, half)
    full = lambda a: pl.BlockSpec(a.shape, lambda bi, gi: (0,) * a.ndim)
    out_spec = lambda w: pl.BlockSpec((None, None, n, w), lambda bi, gi: (bi, gi, 0, 0))
    out_shape = lambda w: jax.ShapeDtypeStruct((b, g, n, w), BF16)
    return pl.pallas_call(
        _nsa_compress_kernel,
        grid=(b, g),
        in_specs=[
            pl.BlockSpec((None, None, n, half), lambda bi, gi: (bi, gi, 0, 0)),
            pl.BlockSpec((None, None, n, half), lambda bi, gi: (bi, 2 + gi, 0, 0)),
            full(pos_flat), full(kw1), full(kw2), full(vw1), full(vw2), full(k_norm), full(overlap),
        ],
        out_specs=[out_spec(LANES), out_spec(3 * LANES)],
        out_shape=[out_shape(LANES), out_shape(3 * LANES)],
        compiler_params=_params("parallel", "parallel"),
        name="nsa_compress",
    )(grouped, grouped, pos_flat, kw1, kw2, vw1, vw2, k_norm, overlap)


def _nsa_cw_kernel(shift_ref, *refs, qn, n_sel):
    fixed = jnp.maximum(shift_ref[1], shift_ref[2]) <= FIXED_SHIFT_MAX
    pl.when(fixed)(functools.partial(_nsa_cw_body, shift_ref, *refs, qn=qn, n_sel=n_sel, fixed=True))
    pl.when(jnp.logical_not(fixed))(functools.partial(_nsa_cw_body, shift_ref, *refs, qn=qn, n_sel=n_sel, fixed=False))


def _nsa_cw_body(shift_ref, q_ref, kc_ref, vc_ref, kw_ref, vw_ref, gt_ref, qn_ref, wb_ref, qp_ref, o_ref, *, qn, n_sel,
                 fixed):
    t0 = pl.program_id(2) * qn
    rep = NSA_REP
    rows = rep * qn
    scale = NSA_DK ** -0.5 * LOG2E
    qf = q_ref[...].astype(F32)
    heads = []
    for r in range(rep):
        qh = _rms(qf[:, r * LANES:(r + 1) * LANES], qn_ref[...]) * scale
        heads.append(qh.astype(BF16))
    qs = jnp.concatenate(heads, axis=0)
    trow = t0 + (lax.broadcasted_iota(I32, (rows, 1), 0) & (qn - 1))

    def attend(sm, vx, shift):
        if not fixed:
            sm = sm - jnp.maximum(jnp.max(sm, axis=-1, keepdims=True), -2.0 * shift)
        return _dot(jnp.exp2(sm).astype(BF16), vx)

    n_cmp = kc_ref.shape[0]
    visible = lax.broadcasted_iota(I32, (1, n_cmp), 1) * CMP_STRIDE + (CMP_LEN - 1) <= trow
    r_cmp = attend(jnp.where(visible, _dot_nt(qs, kc_ref[...]), NEG) - shift_ref[1], vc_ref[...], shift_ref[1])
    inv = 1.0 / jnp.maximum(r_cmp[:, LANES:2 * LANES], 1e-30)
    o_cmp = r_cmp[:, 0:LANES] * inv
    imp_h = r_cmp[:, 2 * LANES:3 * LANES] * inv
    imp = imp_h[0:qn]
    for r in range(1, rep):
        imp = imp + imp_h[r * qn:(r + 1) * qn]

    imp_t = imp.T
    jj = lax.broadcasted_iota(I32, imp_t.shape, 0).astype(F32)
    cur = jnp.right_shift(t0 + lax.broadcasted_iota(I32, imp_t.shape, 1), SLC_SHIFT).astype(F32)
    forced = (jj == 0.0) | (jj == cur) | (jj == cur - 1.0)
    shift = -shift_ref[0]
    taken = forced & (jj <= cur)
    work = jnp.where(taken, -2.0, jnp.where(jj <= cur, imp_t, -1.0))
    bias_t = jnp.where(taken, shift, -SEL_BIAS)
    for _ in range(n_sel - N_FORCED):
        m = jnp.max(work, axis=0, keepdims=True)
        idx = jnp.min(jnp.where(work == m, jj, float(LANES)), axis=0, keepdims=True)
        pick = jj == idx
        bias_t = jnp.where(pick, shift, bias_t)
        work = jnp.where(pick, -2.0, work)
    bias = bias_t.T.astype(BF16)
    for r in range(rep):
        qp_ref[r] = jnp.concatenate([heads[r], bias], axis=1)

    span = WINDOW + qn
    start = pl.multiple_of(jnp.maximum(t0 - WINDOW, 0), qn)
    s = _dot_nt(qs, kw_ref[pl.ds(start, span), :]).reshape(rep, qn, span) + wb_ref[...]
    r_win = attend(s.reshape(rows, span), vw_ref[pl.ds(start, span), :], shift_ref[2])
    o_win = r_win[:, 0:LANES] / jnp.maximum(r_win[:, LANES:2 * LANES], 1e-30)

    gt = gt_ref[...]
    for r in range(rep):
        sl = slice(r * qn, (r + 1) * qn)
        o = gt[:, 3 * r:3 * r + 1] * o_cmp[sl] + gt[:, 3 * r + 2:3 * r + 3] * o_win[sl]
        o_ref[:, r * LANES:(r + 1) * LANES] = o.astype(o_ref.dtype)


def _nsa_cw(shift, main, kcmp, vcmp, kwn, vwp, gates, q_norm, *, qn=256):
    b, g, seq, _ = kwn.shape
    t = main.shape[0]
    nq = seq // qn
    rep = NSA_REP
    gw = rep * LANES
    q_blk0 = (main.shape[1] - NSA_HEADS * NSA_DK) // gw
    n_cmp = kcmp.shape[2]
    n_case = WINDOW // qn + 1
    span = WINDOW + qn
    in_window = []
    for case in range(n_case):
        t0 = case * qn
        diff = (t0 + np.arange(qn)[:, None]) - (max(t0 - WINDOW, 0) + np.arange(span)[None, :])
        in_window.append((diff >= 0) & (diff < WINDOW))
    window_bias = jnp.where(jnp.asarray(np.stack(in_window)), -shift[2], NEG).astype(F32)
    kern = functools.partial(_nsa_cw_kernel, qn=qn, n_sel=min(SLC_TOPN, seq // SLC_LEN))
    return pl.pallas_call(
        kern,
        grid=(b, g, nq),
        in_specs=[
            pl.BlockSpec(memory_space=pltpu.SMEM),
            pl.BlockSpec((qn, gw), lambda bi, gi, i: (bi * nq + i, q_blk0 + gi)),
            pl.BlockSpec((None, None, n_cmp, LANES), lambda bi, gi, i: (bi, gi, 0, 0)),
            pl.BlockSpec((None, None, n_cmp, 3 * LANES), lambda bi, gi, i: (bi, gi, 0, 0)),
            pl.BlockSpec((None, None, seq, LANES), lambda bi, gi, i: (bi, gi, 0, 0)),
            pl.BlockSpec((None, None, seq, 2 * LANES), lambda bi, gi, i: (bi, gi, 0, 0)),
            pl.BlockSpec((qn, LANES), lambda bi, gi, i: (bi * nq + i, gi)),
            pl.BlockSpec((1, LANES), lambda bi, gi, i: (0, 0)),
            pl.BlockSpec((None, qn, span), lambda bi, gi, i: (jnp.minimum(i, n_case - 1), 0, 0)),
        ],
        out_specs=[
            pl.BlockSpec((None, None, rep, qn, 2 * LANES), lambda bi, gi, i: (bi, gi, 0, i, 0)),
            pl.BlockSpec((qn, gw), lambda bi, gi, i: (bi * nq + i, gi)),
        ],
        out_shape=[
            jax.ShapeDtypeStruct((b, g, rep, seq, 2 * LANES), BF16),
            jax.ShapeDtypeStruct((t, NSA_HEADS * NSA_DK), BF16),
        ],
        compiler_params=_params("parallel", "parallel", "parallel"),
        name="nsa_cmp_win",
    )(shift, main, kcmp, vcmp, kwn, vwp, gates, q_norm, window_bias)


def _nsa_slc_kernel(shift_ref, qp_ref, kp_ref, vp_ref, gt_ref, ocw_ref, o_ref, m_ref, acc_ref, *, qn, tk):
    t0 = pl.program_id(2) * qn
    rep = NSA_REP
    rows = rep * qn
    last = (t0 + qn - 1) // tk
    acc_ref[...] = jnp.zeros_like(acc_ref)

    def scores(k0, width, causal):
        qp = qp_ref[...].reshape(rows, qp_ref.shape[2])
        s = _dot_nt(qp, kp_ref[pl.ds(k0, width), :])
        if causal:
            trow = t0 + (lax.broadcasted_iota(I32, (rows, 1), 0) & (qn - 1))
            s = jnp.where(k0 + lax.broadcasted_iota(I32, (1, width), 1) <= trow, s, -SEL_BIAS)
        return s, vp_ref[pl.ds(k0, width), :]

    def fixed_shift_step(k0, width, causal):
        s, v = scores(k0, width, causal)
        acc_ref[...] += _dot(jnp.exp2(s).astype(BF16), v)

    def running_max_step(k0, width, causal):
        s, v = scores(k0, width, causal)
        m_old = m_ref[...]
        m_new = jnp.maximum(m_old, jnp.max(s, axis=-1, keepdims=True))
        acc_ref[...] = jnp.exp2(m_old - m_new) * acc_ref[...] + _dot(jnp.exp2(s - m_new).astype(BF16), v)
        m_ref[...] = m_new

    def sweep(step):
        wide = SLC_WIDE * tk
        n_wide = last // SLC_WIDE
        lax.fori_loop(0, n_wide, lambda j, c: (step(pl.multiple_of(j * wide, wide), wide, False), c)[1], 0)
        for left in range(SLC_WIDE):
            pl.when(last - n_wide * SLC_WIDE == left)(
                functools.partial(step, pl.multiple_of(n_wide * wide, tk), (left + 1) * tk, True))

    fixed = shift_ref[0] <= FIXED_SHIFT_MAX

    @pl.when(fixed)
    def _():
        sweep(fixed_shift_step)

    @pl.when(jnp.logical_not(fixed))
    def _():
        m_ref[...] = jnp.full(m_ref.shape, NEG, F32)
        sweep(running_max_step)

    o_slc = acc_ref[:, 0:LANES] / jnp.maximum(acc_ref[:, LANES:2 * LANES], 1e-30)
    gt = gt_ref[...]
    for r in range(rep):
        o = ocw_ref[:, r * LANES:(r + 1) * LANES].astype(F32) + gt[:, 3 * r + 1:3 * r + 2] * o_slc[r * qn:(r + 1) * qn]
        o_ref[:, r * LANES:(r + 1) * LANES] = o.astype(o_ref.dtype)


def _nsa_slc(shift, qp, kp, vp, gates, ocw, *, qn=256, tk=512):
    b, g, rep, seq, dqk = qp.shape
    t = ocw.shape[0]
    nq = seq // qn
    gw = rep * LANES
    tk = min(tk, seq)
    kern = functools.partial(_nsa_slc_kernel, qn=qn, tk=tk)
    return pl.pallas_call(
        kern,
        grid=(b, g, nq),
        in_specs=[
            pl.BlockSpec(memory_space=pltpu.SMEM),
            pl.BlockSpec((None, None, rep, qn, dqk), lambda bi, gi, i: (bi, gi, 0, i, 0)),
            pl.BlockSpec((None, None, seq, dqk), lambda bi, gi, i: (bi, gi, 0, 0)),
            pl.BlockSpec((None, None, seq, 2 * LANES), lambda bi, gi, i: (bi, gi, 0, 0)),
            pl.BlockSpec((qn, LANES), lambda bi, gi, i: (bi * nq + i, gi)),
            pl.BlockSpec((qn, gw), lambda bi, gi, i: (bi * nq + i, gi)),
        ],
        out_specs=pl.BlockSpec((qn, gw), lambda bi, gi, i: (bi * nq + i, gi)),
        out_shape=jax.ShapeDtypeStruct((t, NSA_HEADS * NSA_DK), BF16),
        scratch_shapes=[pltpu.VMEM((rep * qn, 1), F32), pltpu.VMEM((rep * qn, 2 * LANES), F32)],
        compiler_params=_params("parallel", "parallel", "parallel"),
        name="nsa_selected",
    )(shift, qp, kp, vp, gates, ocw)


def _mem_kv_kernel(mem_ref, g_ref, wkv_ref, kn_ref, k_ref, v_ref):
    memn = _rms(mem_ref[...], g_ref[...]).astype(BF16)
    kv = _dot(memn, wkv_ref[...])
    inner = k_ref.shape[1]
    for h in range(inner // X_HEADDIM):
        sl = slice(h * X_HEADDIM, (h + 1) * X_HEADDIM)
        k_ref[:, sl] = _rms(kv[:, sl], kn_ref[...]).astype(BF16)
    v_ref[...] = kv[:, inner:].astype(BF16)


def _mem_kv(mem, g, wkv, kn):
    b, m, d = mem.shape
    inner = wkv.shape[1] // 2
    out_spec = pl.BlockSpec((None, m, inner), lambda bi: (bi, 0, 0))
    out_shape = jax.ShapeDtypeStruct((b, m, inner), BF16)
    return pl.pallas_call(
        _mem_kv_kernel,
        grid=(b,),
        in_specs=[
            pl.BlockSpec((None, m, d), lambda bi: (bi, 0, 0)),
            pl.BlockSpec((1, d), lambda bi: (0, 0)),
            pl.BlockSpec(wkv.shape, lambda bi: (0, 0)),
            pl.BlockSpec((1, X_HEADDIM), lambda bi: (0, 0)),
        ],
        out_specs=[out_spec, out_spec],
        out_shape=[out_shape, out_shape],
        compiler_params=_params("parallel"),
        name="mem_kv",
    )(mem, g, wkv, kn)


def _cross_attn_kernel(*refs, n_in):
    a_refs = refs[:n_in]
    w_ref, x_ref, g_ref, wq_ref, qn_ref, k_ref, v_ref, wo_ref, o_ref = refs[n_in:]
    x = x_ref[...]
    k0 = 0
    for a_ref in a_refs:
        x = x + _dot(a_ref[...], w_ref[k0:k0 + a_ref.shape[1], :])
        k0 += a_ref.shape[1]
    q = _dot(_rms(x, g_ref[...]).astype(BF16), wq_ref[...])
    scale = X_HEADDIM ** -0.5
    outs = []
    for h in range(q.shape[1] // X_HEADDIM):
        sl = slice(h * X_HEADDIM, (h + 1) * X_HEADDIM)
        qh = (_rms(q[:, sl], qn_ref[...]) * scale).astype(BF16)
        s = _dot_nt(qh, k_ref[:, sl])
        e = jnp.exp(s - jnp.max(s, axis=-1, keepdims=True))
        p = e / jnp.sum(e, axis=-1, keepdims=True)
        outs.append(_dot(p.astype(BF16), v_ref[:, sl]))
    o = jnp.concatenate(outs, axis=1).astype(BF16)
    o_ref[...] = x + _dot(o, wo_ref[...])


def _cross_attn(acts, w_out, x, g, wq, qn, k, v, wo, *, tm=1024):
    t, d = x.shape
    b, m, inner = k.shape
    nt = t // b // tm
    return pl.pallas_call(
        functools.partial(_cross_attn_kernel, n_in=len(acts)),
        grid=(b, nt),
        in_specs=[pl.BlockSpec((tm, a.shape[1]), lambda bi, i: (bi * nt + i, 0)) for a in acts] + [
            pl.BlockSpec(w_out.shape, lambda bi, i: (0, 0)),
            pl.BlockSpec((tm, d), lambda bi, i: (bi * nt + i, 0)),
            pl.BlockSpec((1, d), lambda bi, i: (0, 0)),
            pl.BlockSpec(wq.shape, lambda bi, i: (0, 0)),
            pl.BlockSpec((1, X_HEADDIM), lambda bi, i: (0, 0)),
            pl.BlockSpec((None, m, inner), lambda bi, i: (bi, 0, 0)),
            pl.BlockSpec((None, m, inner), lambda bi, i: (bi, 0, 0)),
            pl.BlockSpec(wo.shape, lambda bi, i: (0, 0)),
        ],
        out_specs=pl.BlockSpec((tm, d), lambda bi, i: (bi * nt + i, 0)),
        out_shape=jax.ShapeDtypeStruct((t, d), F32),
        compiler_params=_params("parallel", "parallel"),
        name="cross_attn",
    )(*acts, w_out, x, g, wq, qn, k, v, wo)


def _swiglu_kernel(x_ref, g_ref, wu_ref, wg_ref, w2_ref, o_ref, h_ref):
    @pl.when(pl.program_id(1) == 0)
    def _():
        h_ref[...] = _rms(x_ref[...], g_ref[...]).astype(BF16)
        o_ref[...] = x_ref[...]

    h = h_ref[...]
    tf = wu_ref.shape[1]
    half = tf // 2 // LANES * LANES
    acts = []
    for c in (slice(0, half), slice(half, tf)):
        acts.append((_silu(_dot(h, wg_ref[:, c])) * _dot(h, wu_ref[:, c])).astype(BF16))
    act = jnp.concatenate(acts, axis=1)
    for c0 in range(0, o_ref.shape[1], EXPERT_CHUNK):
        c = slice(c0, c0 + EXPERT_CHUNK)
        o_ref[:, c] += _dot(act, w2_ref[:, c])


def _swiglu(x, g, w13, w2, *, tm=1024, tf=1408):
    t, d = x.shape
    ff = w2.shape[0]
    nf = ff // tf
    return pl.pallas_call(
        _swiglu_kernel,
        grid=(t // tm, nf),
        in_specs=[
            pl.BlockSpec((tm, d), lambda i, f: (i, 0)),
            pl.BlockSpec((1, d), lambda i, f: (0, 0)),
            pl.BlockSpec((d, tf), lambda i, f: (0, f)),
            pl.BlockSpec((d, tf), lambda i, f: (0, nf + f)),
            pl.BlockSpec((tf, d), lambda i, f: (f, 0)),
        ],
        out_specs=pl.BlockSpec((tm, d), lambda i, f: (i, 0)),
        out_shape=jax.ShapeDtypeStruct((t, d), F32),
        scratch_shapes=[pltpu.VMEM((tm, d), BF16)],
        compiler_params=_params("parallel", "arbitrary"),
        name="swiglu",
    )(x, g, w13, w13, w2)


def _split3(x):
    a = x.astype(BF16)
    r = x - a.astype(F32)
    b = r.astype(BF16)
    c = (r - b.astype(F32)).astype(BF16)
    return a, b, c


def _ssd_kernel(z0_ref, z1_ref, x0_ref, x1_ref, bc_ref, dt_ref, cw_ref, cb_ref, dtb_ref, alog_ref, dskip_ref,
                ng_ref, o_ref, tail_ref, state_ref):
    q = x0_ref.shape[0]
    d_inner = o_ref.shape[1]
    gn = SSM_GROUPS * SSM_STATE
    hpg = d_inner // SSM_HEADDIM // SSM_GROUPS

    @pl.when(pl.program_id(1) == 0)
    def _():
        tail_ref[...] = jnp.zeros_like(tail_ref)
        state_ref[...] = jnp.zeros_like(state_ref)

    raw_b = jnp.concatenate([x0_ref[...], x1_ref[...], bc_ref[...]], axis=1)
    raw = raw_b.astype(F32)
    xbc = _silu(_causal_conv(raw_b, raw, tail_ref[...], cw_ref, cb_ref))
    tail_ref[...] = raw[q - 8:q, :]
    xs = xbc[:, :d_inner]
    bm = xbc[:, d_inner:d_inner + gn]
    cm = xbc[:, d_inner + gn:]

    dt = _softplus(dt_ref[...] + dtb_ref[...])
    a = dt * (-jnp.exp(alog_ref[...]))
    ri = lax.broadcasted_iota(I32, (q, q), 0)
    ci = lax.broadcasted_iota(I32, (q, q), 1)
    causal = ci <= ri
    tri = jnp.where(causal, 1.0, 0.0).astype(BF16)
    a_cs = sum(_dot(tri, part) for part in _split3(a))
    a_cs_t = a_cs.T
    dt_t = dt.T
    lane = lax.broadcasted_iota(I32, (1, LANES), 1)
    lo = lane < SSM_HEADDIM

    y_parts = []
    for g in range(SSM_GROUPS):
        cg = cm[:, g * SSM_STATE:(g + 1) * SSM_STATE].astype(BF16)
        bg = bm[:, g * SSM_STATE:(g + 1) * SSM_STATE]
        gmat = _dot_nt(cg, bg.astype(BF16))
        bg_t = bg.T
        gw = hpg * SSM_HEADDIM
        prev = state_ref[:, g * gw:(g + 1) * gw]
        y_off = _dot(cg, prev.astype(BF16))
        for pr in range(hpg // 2):
            c0 = g * gw + pr * LANES
            x_pair = xs[:, c0:c0 + LANES]
            y_pair = dskip_ref[:, c0:c0 + LANES] * x_pair
            st_pair = jnp.zeros((SSM_STATE, LANES), F32)
            decay_pair = jnp.zeros((1, LANES), F32)
            for half in range(2):
                h = g * hpg + pr * 2 + half
                sel = lo if half == 0 else jnp.logical_not(lo)
                xh = jnp.where(sel, x_pair, 0.0).astype(BF16)
                row_cs = a_cs_t[h:h + 1, :]
                col_cs = a_cs[:, h:h + 1]
                row_dt = dt_t[h:h + 1, :]
                a_last = a_cs_t[h:h + 1, q - 1:q]
                dec = jnp.exp(jnp.where(causal, col_cs - row_cs, NEG))
                y_pair = y_pair + _dot((gmat * dec * row_dt).astype(BF16), xh)
                w_row = jnp.exp(a_last - row_cs) * row_dt
                st_pair = st_pair + _dot((bg_t * w_row).astype(BF16), xh)
                y_pair = y_pair + jnp.where(sel, jnp.exp(col_cs) * y_off[:, pr * LANES:(pr + 1) * LANES], 0.0)
                decay_pair = jnp.where(sel, jnp.exp(a_last), decay_pair)
            state_ref[:, c0:c0 + LANES] = decay_pair * state_ref[:, c0:c0 + LANES] + st_pair
            y_parts.append(y_pair)
    y = jnp.concatenate(y_parts, axis=1)

    z = jnp.concatenate([z0_ref[...], z1_ref[...]], axis=1).astype(F32)
    y = y * _silu(z)
    gsz = d_inner // SSM_GROUPS
    for g in range(SSM_GROUPS):
        sl = slice(g * gsz, (g + 1) * gsz)
        o_ref[:, sl] = _rms(y[:, sl], ng_ref[:, sl]).astype(o_ref.dtype)


def _ssd(main, dt, cw, cb, dtb, alog, dskip, ng, *, batch, d_inner):
    t = main.shape[0]
    q = SSD_CHUNK
    nc = t // batch // q
    conv_ch = cw.shape[1]
    half = d_inner // 2
    col = lambda j: pl.BlockSpec((q, half), lambda b, i: (b * nc + i, j))
    vec = lambda n: pl.BlockSpec((1, n), lambda b, i: (0, 0))
    return pl.pallas_call(
        _ssd_kernel,
        grid=(batch, nc),
        in_specs=[
            col(0), col(1), col(2), col(3), col(4),
            pl.BlockSpec((q, LANES), lambda b, i: (b * nc + i, 0)),
            pl.BlockSpec((CONV_W, conv_ch), lambda b, i: (0, 0)),
            vec(conv_ch), vec(LANES), vec(LANES), vec(d_inner), vec(d_inner),
        ],
        out_specs=pl.BlockSpec((q, d_inner), lambda b, i: (b * nc + i, 0)),
        out_shape=jax.ShapeDtypeStruct((t, d_inner), BF16),
        scratch_shapes=[pltpu.VMEM((8, conv_ch), F32), pltpu.VMEM((SSM_STATE, d_inner), F32)],
        compiler_params=_params("parallel", "arbitrary"),
        name="ssd",
    )(main, main, main, main, main, dt, cw, cb, dtb, alog, dskip, ng)


def _router_kernel(x_ref, g_ref, wr_ref, h_ref, info_ref, cnt_ref, run_ref):
    tm = x_ref.shape[0]

    @pl.when(pl.program_id(0) == 0)
    def _():
        run_ref[...] = jnp.zeros_like(run_ref)

    h = _rms(x_ref[...], g_ref[...])
    h_ref[...] = h
    h_hi = h.astype(BF16)
    h_lo = (h - h_hi.astype(F32)).astype(BF16)
    w = wr_ref[...]
    w_hi = w.astype(BF16)
    w_lo = (w - w_hi.astype(F32)).astype(BF16)
    logits = _dot(h_hi, w_hi) + _dot(h_lo, w_hi) + _dot(h_hi, w_lo)
    lane = lax.broadcasted_iota(I32, (tm, LANES), 1)
    lg = jnp.where(lane < N_EXPERTS, logits, NEG)
    m1 = jnp.max(lg, axis=-1, keepdims=True)
    i1 = jnp.min(jnp.where(lg == m1, lane, LANES), axis=-1, keepdims=True)
    lg2 = jnp.where(lane == i1, NEG, lg)
    m2 = jnp.max(lg2, axis=-1, keepdims=True)
    i2 = jnp.min(jnp.where(lg2 == m2, lane, LANES), axis=-1, keepdims=True)
    e2 = jnp.exp(m2 - m1)
    w1 = 1.0 / (1.0 + e2)
    w2 = e2 / (1.0 + e2)

    hot1 = lane == i1
    hot2 = lane == i2
    hot = jnp.where(hot1 | hot2, 1.0, 0.0)
    ri = lax.broadcasted_iota(I32, (tm, tm), 0)
    ci = lax.broadcasted_iota(I32, (tm, tm), 1)
    before = jnp.where(ci < ri, 1.0, 0.0).astype(BF16)
    seen = run_ref[0:1, :] + _dot(before, hot.astype(BF16))
    rank1 = jnp.sum(jnp.where(hot1, seen, 0.0), axis=-1, keepdims=True)
    rank2 = jnp.sum(jnp.where(hot2, seen, 0.0), axis=-1, keepdims=True)
    run_ref[...] = run_ref[...] + jnp.sum(hot, axis=0, keepdims=True)
    cnt_ref[...] = run_ref[...]

    cols = [i1.astype(F32), i2.astype(F32), w1, w2, rank1, rank2]
    info = jnp.zeros((tm, LANES), F32)
    for c, v in enumerate(cols):
        info = jnp.where(lane == c, v, info)
    info_ref[...] = info


def _router(x, g, wr, *, tm=256):
    t, d = x.shape
    return pl.pallas_call(
        _router_kernel,
        grid=(t // tm,),
        in_specs=[
            pl.BlockSpec((tm, d), lambda i: (i, 0)),
            pl.BlockSpec((1, d), lambda i: (0, 0)),
            pl.BlockSpec(wr.shape, lambda i: (0, 0)),
        ],
        out_specs=[
            pl.BlockSpec((tm, d), lambda i: (i, 0)),
            pl.BlockSpec((tm, LANES), lambda i: (i, 0)),
            pl.BlockSpec((8, LANES), lambda i: (0, 0)),
        ],
        out_shape=[
            jax.ShapeDtypeStruct((t, d), F32),
            jax.ShapeDtypeStruct((t, LANES), F32),
            jax.ShapeDtypeStruct((8, LANES), F32),
        ],
        scratch_shapes=[pltpu.VMEM((8, LANES), F32)],
        compiler_params=_params("arbitrary"),
        name="moe_router",
    )(x, g, wr)


def _row_copy(src_ref, src_row, dst_ref, dst_row, sem):
    return pltpu.make_async_copy(src_ref.at[pl.ds(src_row, 1)], dst_ref.at[pl.ds(dst_row, 1)], sem)


def _dispatch_kernel(pos_ref, h_ref, init_ref, xs_ref, sem):
    del init_ref
    tt = h_ref.shape[0]

    def issue(r, c):
        _row_copy(h_ref, r, xs_ref, pos_ref[0, r], sem).start()
        _row_copy(h_ref, r, xs_ref, pos_ref[1, r], sem).start(priority=1)
        return c

    lax.fori_loop(0, tt, issue, 0, unroll=DMA_UNROLL)
    for _ in range(2):
        pltpu.make_async_copy(h_ref, xs_ref.at[pl.ds(0, tt)], sem).wait()


def _dispatch(h, pos, rows, *, tt=256):
    t, d = h.shape
    init = jnp.zeros((rows, d), h.dtype)
    return pl.pallas_call(
        _dispatch_kernel,
        grid=(t // tt,),
        in_specs=[
            pl.BlockSpec((None, 2, tt), lambda i: (i, 0, 0), memory_space=pltpu.SMEM),
            pl.BlockSpec((tt, d), lambda i: (i, 0)),
            pl.BlockSpec(memory_space=pl.ANY),
        ],
        out_specs=pl.BlockSpec(memory_space=pl.ANY),
        out_shape=jax.ShapeDtypeStruct((rows, d), h.dtype),
        scratch_shapes=[pltpu.SemaphoreType.DMA(())],
        input_output_aliases={2: 0},
        compiler_params=_params("arbitrary"),
        name="moe_dispatch",
    )(pos, h, init)


def _expert_kernel(te_ref, tv_ref, xs_ref, wu_ref, wg_ref, w2_ref, o_ref, xb_ref):
    del te_ref
    i = pl.program_id(0)
    f = pl.program_id(1)
    live = tv_ref[i] > 0

    @pl.when(f == 0)
    def _():
        o_ref[...] = jnp.zeros_like(o_ref)
        xb_ref[...] = xs_ref[...].astype(BF16)

    @pl.when(live)
    def _():
        x = xb_ref[...]
        act = (_silu(_dot(x, wg_ref[...].astype(BF16))) * _dot(x, wu_ref[...].astype(BF16))).astype(BF16)
        for c0 in range(0, o_ref.shape[1], EXPERT_CHUNK):
            c = slice(c0, c0 + EXPERT_CHUNK)
            o_ref[:, c] += _dot(act, w2_ref[:, c].astype(BF16))


def _experts(xs, w13, w2, tile_expert, tile_live, *, tm, tf=512):
    rows = xs.shape[0]
    ff, d = w2.shape[1:]
    nf = ff // tf

    def f_of(i, f, te, tv):
        return jnp.where(tv[i] > 0, f, nf - 1)

    grid_spec = pltpu.PrefetchScalarGridSpec(
        num_scalar_prefetch=2,
        grid=(rows // tm, nf),
        in_specs=[
            pl.BlockSpec((tm, xs.shape[1]), lambda i, f, te, tv: (i, 0)),
            pl.BlockSpec((None, d, tf), lambda i, f, te, tv: (te[i], 0, f_of(i, f, te, tv))),
            pl.BlockSpec((None, d, tf), lambda i, f, te, tv: (te[i], 0, nf + f_of(i, f, te, tv))),
            pl.BlockSpec((None, tf, d), lambda i, f, te, tv: (te[i], f_of(i, f, te, tv), 0)),
        ],
        out_specs=pl.BlockSpec((tm, d), lambda i, f, te, tv: (i, 0)),
        scratch_shapes=[pltpu.VMEM((tm, d), BF16)],
    )
    return pl.pallas_call(
        _expert_kernel,
        grid_spec=grid_spec,
        out_shape=jax.ShapeDtypeStruct((rows, d), F32),
        compiler_params=_params("parallel", "arbitrary"),
        name="moe_experts",
    )(tile_expert, tile_live, xs, w13, w13, w2)


def _combine_kernel(pos_ref, x_ref, info_ref, ys_ref, o_ref, buf_ref, sem):
    tt = x_ref.shape[0]

    def issue(r, c):
        _row_copy(ys_ref, pos_ref[0, r], buf_ref.at[0], r, sem).start()
        _row_copy(ys_ref, pos_ref[1, r], buf_ref.at[1], r, sem).start(priority=1)
        return c

    lax.fori_loop(0, tt, issue, 0, unroll=DMA_UNROLL)
    for k in range(2):
        pltpu.make_async_copy(ys_ref.at[pl.ds(0, tt)], buf_ref.at[k], sem).wait()
    info = info_ref[...]
    o_ref[...] = x_ref[...] + info[:, 2:3] * buf_ref[0] + info[:, 3:4] * buf_ref[1]


def _combine(x, info, pos, ys, *, tt=256):
    t, d = x.shape
    return pl.pallas_call(
        _combine_kernel,
        grid=(t // tt,),
        in_specs=[
            pl.BlockSpec((None, 2, tt), lambda i: (i, 0, 0), memory_space=pltpu.SMEM),
            pl.BlockSpec((tt, d), lambda i: (i, 0)),
            pl.BlockSpec((tt, LANES), lambda i: (i, 0)),
            pl.BlockSpec(memory_space=pl.ANY),
        ],
        out_specs=pl.BlockSpec((tt, d), lambda i: (i, 0)),
        out_shape=jax.ShapeDtypeStruct((t, d), F32),
        scratch_shapes=[pltpu.VMEM((2, tt, d), F32), pltpu.SemaphoreType.DMA(())],
        compiler_params=_params("arbitrary"),
        name="moe_combine",
    )(pos, x, info, ys)


def _moe(x, g, router, w13, w2, *, tm=1024, tt=512):
    t, d = x.shape
    n_exp = router.shape[1]
    wr = jnp.pad(router, ((0, 0), (0, LANES - n_exp)))
    h, info, counts = _router(x, g, wr, tm=tt)

    counts = counts[0, :n_exp].astype(I32)
    seg = (counts + tm - 1) // tm * tm
    seg_end = jnp.cumsum(seg)
    seg_start = seg_end - seg
    e1 = info[:, 0].astype(I32)
    e2 = info[:, 1].astype(I32)
    pos = jnp.stack([seg_start[e1] + info[:, 4].astype(I32), seg_start[e2] + info[:, 5].astype(I32)], axis=0)
    pos = pos.reshape(2, t // tt, tt).transpose(1, 0, 2)
    rows = 2 * t + n_exp * tm
    tile_row0 = jnp.arange(rows // tm, dtype=I32) * tm
    tile_live = (tile_row0 < seg_end[-1]).astype(I32)
    tile_expert = jnp.sum((seg_end[None, :] <= tile_row0[:, None]).astype(I32), axis=1)
    tile_expert = jnp.minimum(tile_expert, n_exp - 1)
    last_live = jnp.maximum(jnp.sum(tile_live) - 1, 0)
    tile_expert = jnp.where(tile_live > 0, tile_expert, tile_expert[last_live])

    xs = _dispatch(h, pos, rows, tt=tt)
    ys = _experts(xs, w13, w2, tile_expert, tile_live, tm=tm)
    return _combine(x, info, pos, ys, tt=tt)


def _row(v, n=None):
    v = v.reshape(1, -1).astype(F32)
    if n is not None and v.shape[1] < n:
        v = jnp.pad(v, ((0, 0), (0, n - v.shape[1])))
    return v


def _overlap_matrix(seq):
    n = seq // CMP_STRIDE
    cmp_start = np.arange(n) * CMP_STRIDE
    slc_start = np.arange(LANES) * SLC_LEN
    ov = (cmp_start[:, None] <= slc_start[None, :] + SLC_LEN - 1) & (cmp_start[:, None] + CMP_LEN - 1 >= slc_start[None, :])
    ov[n - 1] = False
    return jnp.asarray(ov, dtype=BF16)


def _even_layer(x, batch, norm_mix, w_in, conv_w, conv_b, wa, ba, wx, bx, lam, gate_b, q_norm, k_norm, cmp_pos,
                ck_w1, ck_w2, cv_w1, cv_w2):
    t, d = x.shape
    seq = t // batch
    rg = wa.shape[0] * wa.shape[1]
    gdk = NSA_GROUPS * NSA_DK
    n_main = 2 * rg + NSA_HEADS * NSA_DK
    n_kv = 6 * gdk
    per_group = 3 * NSA_REP
    gate_cols = w_in[:, n_main + n_kv:].reshape(d, NSA_GROUPS, per_group)
    gate_cols = jnp.pad(gate_cols, ((0, 0), (0, 0), (0, LANES - per_group))).reshape(d, NSA_GROUPS * LANES)
    gate_bias = jnp.pad(gate_b.reshape(NSA_GROUPS, per_group), ((0, 0), (0, LANES - per_group))).reshape(1, -1)
    w_all = jnp.concatenate([w_in[:, :n_main + n_kv], gate_cols], axis=1).astype(BF16)
    kn = jnp.pad(k_norm, ((0, 8 - k_norm.shape[0]), (0, 0)))
    main, planes, kp, vp, kwn, vwp, gates = _norm_proj(
        x, _row(norm_mix), w_all, gate_bias, kn, batch=batch, n_main=n_main, nsa_keys=True,
        n_extra=NSA_GROUPS * LANES, extra_sigmoid=True)

    rg_out = _rglru(main, conv_w, _row(conv_b), wa.astype(BF16), _row(ba), wx.astype(BF16), _row(bx), _row(lam),
                    batch=batch)

    k_gain = jnp.max(jnp.abs(k_norm), axis=1)[jnp.array([1, 0, 2])]
    shift = ((1.02 * LOG2E * math.sqrt(NSA_DK)) * jnp.max(jnp.abs(q_norm)) * k_gain).astype(F32)
    pos_flat = jnp.broadcast_to(cmp_pos.reshape(1, -1), (8, CMP_LEN * NSA_DK)).astype(BF16)
    kcmp, vcmp = _nsa_compress(planes, pos_flat, ck_w1.astype(BF16), ck_w2.astype(BF16), cv_w1.astype(BF16),
                               cv_w2.astype(BF16), kn, _overlap_matrix(seq))
    qp, ocw = _nsa_cw(shift, main, kcmp, vcmp, kwn, vwp, gates, _row(q_norm))
    att = _nsa_slc(shift, qp, kp, vp, gates, ocw)
    return [rg_out, att]


def _odd_layer(x, batch, norm_mix, w_in, conv_w, conv_b, dt_bias, a_log, d_skip, norm_g, d_inner):
    conv_ch = conv_w.shape[1]
    n_main = d_inner + conv_ch
    heads = dt_bias.shape[0]
    w_all = jnp.concatenate([w_in[:, :n_main], jnp.pad(w_in[:, n_main:], ((0, 0), (0, LANES - heads)))], axis=1)
    main, dt = _norm_proj(x, _row(norm_mix), w_all.astype(BF16), jnp.zeros((1, LANES), F32),
                          jnp.zeros((8, LANES), F32), batch=batch, n_main=n_main, nsa_keys=False, n_extra=LANES,
                          extra_sigmoid=False)
    y = _ssd(main, dt, conv_w, _row(conv_b), _row(dt_bias, LANES), _row(a_log, LANES),
             _row(jnp.repeat(d_skip, SSM_HEADDIM)), _row(norm_g), batch=batch, d_inner=d_inner)
    return [y]


def kernel(x, mem, norm_mix, norm_cross, norm_mem, norm_ffn, ev_w_in, ev_rg_conv_w, ev_rg_conv_b, ev_rg_wa, ev_rg_ba, ev_rg_wx, ev_rg_bx, ev_rg_lambda, ev_nsa_gate_b, ev_q_norm, ev_k_norm, ev_cmp_pos, ev_cmp_k_w1, ev_cmp_k_w2, ev_cmp_v_w1, ev_cmp_v_w2, ev_w_out, od_w_in, od_conv_w, od_conv_b, od_dt_bias, od_a_log, od_d_skip, od_norm, od_w_out, x_wq, x_wkv, x_q_norm, x_k_norm, x_wo, ff_w13, ff_w2, moe_router, moe_w13, moe_w2):
    batch, seq, d = x.shape
    depth = norm_mix.shape[0]
    xf = x.reshape(batch * seq, d)
    for layer in range(depth):
        i = layer // 2
        if layer % 2 == 0:
            w_out = ev_w_out[i]
            mixed = _even_layer(xf, batch, norm_mix[layer], ev_w_in[i], ev_rg_conv_w[i], ev_rg_conv_b[i], ev_rg_wa[i],
                                ev_rg_ba[i], ev_rg_wx[i], ev_rg_bx[i], ev_rg_lambda[i], ev_nsa_gate_b[i],
                                ev_q_norm[i], ev_k_norm[i], ev_cmp_pos[i], ev_cmp_k_w1[i], ev_cmp_k_w2[i],
                                ev_cmp_v_w1[i], ev_cmp_v_w2[i])
        else:
            w_out = od_w_out[i]
            mixed = _odd_layer(xf, batch, norm_mix[layer], od_w_in[i], od_conv_w[i], od_conv_b[i], od_dt_bias[i],
                               od_a_log[i], od_d_skip[i], od_norm[i], w_out.shape[0])
        k, v = _mem_kv(mem, _row(norm_mem[layer]), x_wkv[layer].astype(BF16), _row(x_k_norm[layer]))
        xf = _cross_attn(mixed, w_out.astype(BF16), xf, _row(norm_cross[layer]), x_wq[layer].astype(BF16),
                         _row(x_q_norm[layer]), k, v, x_wo[layer].astype(BF16))
        if layer % 2 == 0:
            xf = _swiglu(xf, _row(norm_ffn[layer]), ff_w13[i].astype(BF16), ff_w2[i].astype(BF16))
        else:
            xf = _moe(xf, _row(norm_ffn[layer]), moe_router[i], moe_w13[i], moe_w2[i])
    return xf.reshape(batch, seq, d)
```

```python
import functools
import math

import jax
import jax.numpy as jnp
import numpy as np
from jax import lax
from jax.experimental import pallas as pl
from jax.experimental.pallas import tpu as pltpu

F32 = jnp.float32
BF16 = jnp.bfloat16
I32 = jnp.int32

EPS = 1e-6
CONV_W = 4
RG_BLOCKS = 8
RG_C = 8.0
NSA_HEADS = 8
NSA_GROUPS = 2
NSA_REP = NSA_HEADS // NSA_GROUPS
NSA_DK = 128
CMP_LEN = 32
CMP_STRIDE = 16
SLC_LEN = 64
SLC_SHIFT = 6
SLC_WIDE = 4
N_FORCED = 3
SLC_TOPN = 16
WINDOW = 512
SSM_HEADDIM = 64
SSM_GROUPS = 4
SSM_STATE = 128
SSD_CHUNK = 128
X_HEADS = 4
X_HEADDIM = 128
N_EXPERTS = 8
EXPERT_CHUNK = 256
DMA_UNROLL = 8

LANES = 128
SUBLANES = 8
VMEM_LIMIT_BYTES = 56 * 1024 * 1024
NEG = -1e30
SEL_BIAS = float(2 ** 20)
LOG2E = math.log2(math.e)
FIXED_SHIFT_MAX = 56.0

NT_DIMS = (((1,), (1,)), ((), ()))


def _params(*sem):
    return pltpu.CompilerParams(dimension_semantics=sem, vmem_limit_bytes=VMEM_LIMIT_BYTES)


def _dot(a, b):
    return jnp.dot(a, b, preferred_element_type=F32)


def _dot_nt(a, b):
    return lax.dot_general(a, b, NT_DIMS, preferred_element_type=F32)


def _rms(x, g):
    return x * lax.rsqrt(jnp.mean(x * x, axis=-1, keepdims=True) + EPS) * g


def _sigmoid(x):
    return 1.0 / (1.0 + jnp.exp(-x))


def _silu(x):
    return x * _sigmoid(x)


def _gelu_tanh(x):
    c = math.sqrt(2.0 / math.pi)
    return 0.5 * x * (1.0 + jnp.tanh(c * (x + 0.044715 * (x * x * x))))


def _softplus(x):
    return jnp.maximum(x, 0.0) + jnp.log(1.0 + jnp.exp(-jnp.abs(x)))


def _causal_conv(xb, x, tail, w_ref, b_ref):
    n = xb.shape[0]
    delay = lax.broadcasted_iota(I32, (n, n), 0) - lax.broadcasted_iota(I32, (n, n), 1)
    r8 = lax.broadcasted_iota(I32, (SUBLANES, 1), 0)
    y = b_ref[...] + w_ref[CONV_W - 1:CONV_W, :] * x
    head = jnp.zeros(tail.shape, F32)
    for k in range(1, CONV_W):
        wk = w_ref[CONV_W - 1 - k:CONV_W - k, :]
        y = y + wk * _dot(jnp.where(delay == k, 1.0, 0.0).astype(BF16), xb)
        head = head + wk * jnp.where(r8 < k, pltpu.roll(tail, k, 0), 0.0)
    return jnp.concatenate([y[0:SUBLANES] + head, y[SUBLANES:]], axis=0)


def _norm_proj_kernel(x_ref, g_ref, w_ref, eb_ref, kn_ref, *out_refs, n_main, nsa_keys, tiles_per_seq,
                      extra_sigmoid):
    h = _rms(x_ref[...], g_ref[...]).astype(BF16)
    main_ref = out_refs[0]
    for c0 in range(0, n_main, 512):
        main_ref[:, c0:c0 + 512] = _dot(h, w_ref[:, c0:c0 + 512]).astype(main_ref.dtype)
    col = n_main
    ex_ref = out_refs[-1]
    if nsa_keys:
        cmp_ref, kp_ref, vp_ref, kwn_ref, vwp_ref = out_refs[1:6]
        tm = x_ref.shape[0]
        g = NSA_GROUPS
        gsl = lambda r, p: r[:, p * LANES:(p + 1) * LANES]
        r = _dot(h, w_ref[:, col:col + 2 * g * LANES])
        for p in range(2 * g):
            cmp_ref[p] = gsl(r, p).astype(BF16)
        col += 2 * g * LANES
        t0 = (pl.program_id(0) % tiles_per_seq) * tm
        blk = jnp.right_shift(t0 + lax.broadcasted_iota(I32, (tm, LANES), 0), SLC_SHIFT)
        onehot = jnp.where(blk == lax.broadcasted_iota(I32, (tm, LANES), 1), 1.0, 0.0).astype(BF16)
        ones = jnp.ones((tm, LANES), BF16)
        r = _dot(h, w_ref[:, col:col + 2 * g * LANES])
        for gi in range(g):
            kp_ref[gi] = jnp.concatenate([_rms(gsl(r, gi), kn_ref[1:2, :]).astype(BF16), onehot], axis=1)
            vp_ref[gi] = jnp.concatenate([gsl(r, g + gi).astype(BF16), ones], axis=1)
        col += 2 * g * LANES
        r = _dot(h, w_ref[:, col:col + 2 * g * LANES])
        for gi in range(g):
            kwn_ref[gi] = _rms(gsl(r, gi), kn_ref[2:3, :]).astype(BF16)
            vwp_ref[gi] = jnp.concatenate([gsl(r, g + gi).astype(BF16), ones], axis=1)
        col += 2 * g * LANES
    n_extra = ex_ref.shape[1]
    e = _dot(h, w_ref[:, col:col + n_extra]) + eb_ref[...]
    ex_ref[...] = _sigmoid(e) if extra_sigmoid else e


def _norm_proj(x, g, w, eb, kn, *, batch, n_main, nsa_keys, n_extra, extra_sigmoid, tm=512):
    t, d = x.shape
    seq = t // batch
    nt = seq // tm
    out_shape = [jax.ShapeDtypeStruct((t, n_main), BF16)]
    out_specs = [pl.BlockSpec((tm, n_main), lambda i: (i, 0))]
    if nsa_keys:
        for planes, width in ((2 * NSA_GROUPS, LANES), (NSA_GROUPS, 2 * LANES), (NSA_GROUPS, 2 * LANES),
                              (NSA_GROUPS, LANES), (NSA_GROUPS, 2 * LANES)):
            out_shape.append(jax.ShapeDtypeStruct((batch, planes, seq, width), BF16))
            out_specs.append(pl.BlockSpec((None, planes, tm, width), lambda i: (i // nt, 0, i % nt, 0)))
    out_shape.append(jax.ShapeDtypeStruct((t, n_extra), F32))
    out_specs.append(pl.BlockSpec((tm, n_extra), lambda i: (i, 0)))
    kern = functools.partial(_norm_proj_kernel, n_main=n_main, nsa_keys=nsa_keys, tiles_per_seq=nt,
                             extra_sigmoid=extra_sigmoid)
    return pl.pallas_call(
        kern,
        grid=(t // tm,),
        in_specs=[
            pl.BlockSpec((tm, d), lambda i: (i, 0)),
            pl.BlockSpec((1, d), lambda i: (0, 0)),
            pl.BlockSpec(w.shape, lambda i: (0, 0)),
            pl.BlockSpec((1, n_extra), lambda i: (0, 0)),
            pl.BlockSpec(kn.shape, lambda i: (0, 0)),
        ],
        out_specs=out_specs,
        out_shape=out_shape,
        compiler_params=_params("parallel"),
        name="norm_proj",
    )(x, g, w, eb, kn)


def _rglru_kernel(rx_ref, rg_ref, cw_ref, cb_ref, wa_ref, ba_ref, wx_ref, bx_ref, lam_ref, o_ref, tail_ref, h_ref):
    tc, c = rx_ref.shape

    @pl.when(pl.program_id(1) == 0)
    def _():
        tail_ref[...] = jnp.zeros_like(tail_ref)
        h_ref[...] = jnp.zeros_like(h_ref)

    xb = rx_ref[...]
    x = xb.astype(F32)
    xc = _causal_conv(xb, x, tail_ref[...], cw_ref, cb_ref)
    tail_ref[...] = x[tc - 8:tc, :]

    bw = c // RG_BLOCKS
    ra, rx = [], []
    for blk in range(RG_BLOCKS):
        xb = xc[:, blk * bw:(blk + 1) * bw].astype(BF16)
        ra.append(_dot(xb, wa_ref[blk]))
        rx.append(_dot(xb, wx_ref[blk]))
    r = _sigmoid(jnp.concatenate(ra, axis=1) + ba_ref[...])
    ig = _sigmoid(jnp.concatenate(rx, axis=1) + bx_ref[...])
    log_a = (-RG_C) * r * _softplus(-lam_ref[...])
    a = jnp.exp(log_a)
    z = 1.0 - a * a
    u = jnp.where(z > 0.0, z * lax.rsqrt(z), 0.0) * (ig * xc)

    in_group = lax.broadcasted_iota(I32, (tc, 1), 0) & (SUBLANES - 1)
    d = 1
    while d < SUBLANES:
        keep = in_group >= d
        a_sh = jnp.where(keep, pltpu.roll(a, d, 0), 1.0)
        u_sh = jnp.where(keep, pltpu.roll(u, d, 0), 0.0)
        u = a * u_sh + u
        a = a * a_sh
        d *= 2
    carry = h_ref[SUBLANES - 1:SUBLANES, :]
    groups = []
    for g0 in range(0, tc, SUBLANES):
        hg = u[g0:g0 + SUBLANES, :] + a[g0:g0 + SUBLANES, :] * carry
        carry = hg[SUBLANES - 1:SUBLANES, :]
        groups.append(hg)
    h_ref[...] = groups[-1]
    o_ref[...] = (_gelu_tanh(rg_ref[...].astype(F32)) * jnp.concatenate(groups, axis=0)).astype(o_ref.dtype)


def _rglru(main, cw, cb, wa, ba, wx, bx, lam, *, batch, tc=256):
    t = main.shape[0]
    c = cw.shape[1]
    nt = t // batch // tc
    vec = pl.BlockSpec((1, c), lambda b, i: (0, 0))
    blk = pl.BlockSpec(wa.shape, lambda b, i: (0, 0, 0))
    return pl.pallas_call(
        _rglru_kernel,
        grid=(batch, nt),
        in_specs=[
            pl.BlockSpec((tc, c), lambda b, i: (b * nt + i, 0)),
            pl.BlockSpec((tc, c), lambda b, i: (b * nt + i, 1)),
            pl.BlockSpec((CONV_W, c), lambda b, i: (0, 0)),
            vec, blk, vec, blk, vec, vec,
        ],
        out_specs=pl.BlockSpec((tc, c), lambda b, i: (b * nt + i, 0)),
        out_shape=jax.ShapeDtypeStruct((t, c), BF16),
        scratch_shapes=[pltpu.VMEM((8, c), F32), pltpu.VMEM((8, c), F32)],
        compiler_params=_params("parallel", "arbitrary"),
        name="rglru",
    )(main, main, cw, cb, wa, ba, wx, bx, lam)


def _nsa_compress_kernel(xk_ref, xv_ref, pos_ref, kw1_ref, kw2_ref, vw1_ref, vw2_ref, kn_ref, ov_ref, kc_ref, vc_ref):
    n, half = xk_ref.shape
    last = lax.broadcasted_iota(I32, (n, 1), 0) == n - 1
    pos = pos_ref[...]

    def compress(x_ref, w1_ref, w2_ref):
        x = x_ref[...]
        y0 = _dot(x, w1_ref[0:half, :])
        y1 = _dot(x, w1_ref[half:2 * half, :])
        y1_next = jnp.where(last, 0.0, pltpu.roll(y1, n - 1, 0))
        const = _dot(pos, w1_ref[...])[0:1, :]
        hid = _gelu_tanh(y0 + y1_next + const)
        return _dot(hid.astype(BF16), w2_ref[...])

    kc_ref[...] = _rms(compress(xk_ref, kw1_ref, kw2_ref), kn_ref[0:1, :]).astype(BF16)
    vc = compress(xv_ref, vw1_ref, vw2_ref).astype(BF16)
    vc_ref[...] = jnp.concatenate([vc, jnp.ones((n, LANES), BF16), ov_ref[...]], axis=1)


def _nsa_compress(planes, pos_flat, kw1, kw2, vw1, vw2, k_norm, overlap):
    b, _, seq, _ = planes.shape
    g = NSA_GROUPS
    n = seq // CMP_STRIDE
    half = CMP_STRIDE * LANES
    grouped = planes.reshape(b, 2 * g, n, half)
    full = lambda a: pl.BlockSpec(a.shape, lambda bi, gi: (0,) * a.ndim)
    out_spec = lambda w: pl.BlockSpec((None, None, n, w), lambda bi, gi: (bi, gi, 0, 0))
    out_shape = lambda w: jax.ShapeDtypeStruct((b, g, n, w), BF16)
    return pl.pallas_call(
        _nsa_compress_kernel,
        grid=(b, g),
        in_specs=[
            pl.BlockSpec((None, None, n, half), lambda bi, gi: (bi, gi, 0, 0)),
            pl.BlockSpec((None, None, n, half), lambda bi, gi: (bi, 2 + gi, 0, 0)),
            full(pos_flat), full(kw1), full(kw2), full(vw1), full(vw2), full(k_norm), full(overlap),
        ],
        out_specs=[out_spec(LANES), out_spec(3 * LANES)],
        out_shape=[out_shape(LANES), out_shape(3 * LANES)],
        compiler_params=_params("parallel", "parallel"),
        name="nsa_compress",
    )(grouped, grouped, pos_flat, kw1, kw2, vw1, vw2, k_norm, overlap)


def _nsa_cw_kernel(shift_ref, *refs, qn, n_sel):
    fixed = jnp.maximum(shift_ref[1], shift_ref[2]) <= FIXED_SHIFT_MAX
    pl.when(fixed)(functools.partial(_nsa_cw_body, shift_ref, *refs, qn=qn, n_sel=n_sel, fixed=True))
    pl.when(jnp.logical_not(fixed))(functools.partial(_nsa_cw_body, shift_ref, *refs, qn=qn, n_sel=n_sel, fixed=False))


def _nsa_cw_body(shift_ref, q_ref, kc_ref, vc_ref, kw_ref, vw_ref, gt_ref, qn_ref, wb_ref, qp_ref, o_ref, *, qn, n_sel,
                 fixed):
    t0 = pl.program_id(2) * qn
    rep = NSA_REP
    rows = rep * qn
    scale = NSA_DK ** -0.5 * LOG2E
    qf = q_ref[...].astype(F32)
    heads = []
    for r in range(rep):
        qh = _rms(qf[:, r * LANES:(r + 1) * LANES], qn_ref[...]) * scale
        heads.append(qh.astype(BF16))
    qs = jnp.concatenate(heads, axis=0)
    trow = t0 + (lax.broadcasted_iota(I32, (rows, 1), 0) & (qn - 1))

    def attend(sm, vx, shift):
        if not fixed:
            sm = sm - jnp.maximum(jnp.max(sm, axis=-1, keepdims=True), -2.0 * shift)
        return _dot(jnp.exp2(sm).astype(BF16), vx)

    n_cmp = kc_ref.shape[0]
    visible = lax.broadcasted_iota(I32, (1, n_cmp), 1) * CMP_STRIDE + (CMP_LEN - 1) <= trow
    r_cmp = attend(jnp.where(visible, _dot_nt(qs, kc_ref[...]), NEG) - shift_ref[1], vc_ref[...], shift_ref[1])
    inv = 1.0 / jnp.maximum(r_cmp[:, LANES:2 * LANES], 1e-30)
    o_cmp = r_cmp[:, 0:LANES] * inv
    imp_h = r_cmp[:, 2 * LANES:3 * LANES] * inv
    imp = imp_h[0:qn]
    for r in range(1, rep):
        imp = imp + imp_h[r * qn:(r + 1) * qn]

    span = WINDOW + qn
    start = pl.multiple_of(jnp.maximum(t0 - WINDOW, 0), qn)
    s = _dot_nt(qs, kw_ref[pl.ds(start, span), :]).reshape(rep, qn, span) + wb_ref[...]
    r_win = attend(s.reshape(rows, span), vw_ref[pl.ds(start, span), :], shift_ref[2])
    o_win = r_win[:, 0:LANES] / jnp.maximum(r_win[:, LANES:2 * LANES], 1e-30)

    imp_t = imp.T
    jj = lax.broadcasted_iota(I32, imp_t.shape, 0).astype(F32)
    cur = jnp.right_shift(t0 + lax.broadcasted_iota(I32, imp_t.shape, 1), SLC_SHIFT).astype(F32)
    forced = (jj == 0.0) | (jj == cur) | (jj == cur - 1.0)
    shift = -shift_ref[0]
    taken = forced & (jj <= cur)
    work = jnp.where(taken, -2.0, jnp.where(jj <= cur, imp_t, -1.0))
    bias_t = jnp.where(taken, shift, -SEL_BIAS)
    for _ in range(n_sel - N_FORCED):
        m = jnp.max(work, axis=0, keepdims=True)
        idx = jnp.min(jnp.where(work == m, jj, float(LANES)), axis=0, keepdims=True)
        pick = jj == idx
        bias_t = jnp.where(pick, shift, bias_t)
        work = jnp.where(pick, -2.0, work)
    bias = bias_t.T.astype(BF16)
    for r in range(rep):
        qp_ref[r] = jnp.concatenate([heads[r], bias], axis=1)

    gt = gt_ref[...]
    for r in range(rep):
        sl = slice(r * qn, (r + 1) * qn)
        o = gt[:, 3 * r:3 * r + 1] * o_cmp[sl] + gt[:, 3 * r + 2:3 * r + 3] * o_win[sl]
        o_ref[:, r * LANES:(r + 1) * LANES] = o.astype(o_ref.dtype)


def _nsa_attn(shift, main, kcmp, vcmp, kwn, vwp, kp, vp, gates, q_norm, *, qn=256, tk=512):
    b, g, seq, _ = kwn.shape
    t = main.shape[0]
    nq = seq // qn
    rep = NSA_REP
    gw = rep * LANES
    q_blk0 = (main.shape[1] - NSA_HEADS * NSA_DK) // gw
    n_cmp = kcmp.shape[2]
    n_case = WINDOW // qn + 1
    span = WINDOW + qn
    in_window = []
    for case in range(n_case):
        t0 = case * qn
        diff = (t0 + np.arange(qn)[:, None]) - (max(t0 - WINDOW, 0) + np.arange(span)[None, :])
        in_window.append((diff >= 0) & (diff < WINDOW))
    window_bias = jnp.where(jnp.asarray(np.stack(in_window)), -shift[2], NEG).astype(F32)
    tk = min(tk, seq)
    kern = functools.partial(_nsa_attn_kernel, qn=qn, n_sel=min(SLC_TOPN, seq // SLC_LEN), tk=tk)
    per_group = lambda rows, width: pl.BlockSpec((None, None, rows, width), lambda bi, gi, i: (bi, gi, 0, 0))
    return pl.pallas_call(
        kern,
        grid=(b, g, nq),
        in_specs=[
            pl.BlockSpec(memory_space=pltpu.SMEM),
            pl.BlockSpec((qn, gw), lambda bi, gi, i: (bi * nq + i, q_blk0 + gi)),
            per_group(n_cmp, LANES), per_group(n_cmp, 3 * LANES), per_group(seq, LANES), per_group(seq, 2 * LANES),
            pl.BlockSpec((qn, LANES), lambda bi, gi, i: (bi * nq + i, gi)),
            pl.BlockSpec((1, LANES), lambda bi, gi, i: (0, 0)),
            pl.BlockSpec((None, qn, span), lambda bi, gi, i: (jnp.minimum(i, n_case - 1), 0, 0)),
            per_group(seq, 2 * LANES), per_group(seq, 2 * LANES),
        ],
        out_specs=pl.BlockSpec((qn, gw), lambda bi, gi, i: (bi * nq + i, gi)),
        out_shape=jax.ShapeDtypeStruct((t, NSA_HEADS * NSA_DK), BF16),
        scratch_shapes=[
            pltpu.VMEM((rep, qn, 2 * LANES), BF16),
            pltpu.VMEM((qn, gw), BF16),
            pltpu.VMEM((rep * qn, 1), F32),
            pltpu.VMEM((rep * qn, 2 * LANES), F32),
        ],
        compiler_params=_params("parallel", "parallel", "parallel"),
        name="nsa_attention",
    )(shift, main, kcmp, vcmp, kwn, vwp, gates, q_norm, window_bias, kp, vp)


def _nsa_attn_kernel(shift_ref, q_ref, kc_ref, vc_ref, kw_ref, vw_ref, gt_ref, qn_ref, wb_ref, kp_ref, vp_ref, o_ref,
                     qp_ref, ocw_ref, m_ref, acc_ref, *, qn, n_sel, tk):
    _nsa_cw_kernel(shift_ref, q_ref, kc_ref, vc_ref, kw_ref, vw_ref, gt_ref, qn_ref, wb_ref, qp_ref, ocw_ref, qn=qn,
                   n_sel=n_sel)
    _nsa_slc_kernel(shift_ref, qp_ref, kp_ref, vp_ref, gt_ref, ocw_ref, o_ref, m_ref, acc_ref, qn=qn, tk=tk)


def _nsa_slc_kernel(shift_ref, qp_ref, kp_ref, vp_ref, gt_ref, ocw_ref, o_ref, m_ref, acc_ref, *, qn, tk):
    t0 = pl.program_id(2) * qn
    rep = NSA_REP
    rows = rep * qn
    last = (t0 + qn - 1) // tk
    acc_ref[...] = jnp.zeros_like(acc_ref)

    def scores(k0, width, causal):
        qp = qp_ref[...].reshape(rows, qp_ref.shape[2])
        s = _dot_nt(qp, kp_ref[pl.ds(k0, width), :])
        if causal:
            trow = t0 + (lax.broadcasted_iota(I32, (rows, 1), 0) & (qn - 1))
            s = jnp.where(k0 + lax.broadcasted_iota(I32, (1, width), 1) <= trow, s, -SEL_BIAS)
        return s, vp_ref[pl.ds(k0, width), :]

    def fixed_shift_step(k0, width, causal):
        s, v = scores(k0, width, causal)
        acc_ref[...] += _dot(jnp.exp2(s).astype(BF16), v)

    def running_max_step(k0, width, causal):
        s, v = scores(k0, width, causal)
        m_old = m_ref[...]
        m_new = jnp.maximum(m_old, jnp.max(s, axis=-1, keepdims=True))
        acc_ref[...] = jnp.exp2(m_old - m_new) * acc_ref[...] + _dot(jnp.exp2(s - m_new).astype(BF16), v)
        m_ref[...] = m_new

    def sweep(step):
        wide = SLC_WIDE * tk
        n_wide = last // SLC_WIDE
        lax.fori_loop(0, n_wide, lambda j, c: (step(pl.multiple_of(j * wide, wide), wide, False), c)[1], 0)
        for left in range(SLC_WIDE):
            pl.when(last - n_wide * SLC_WIDE == left)(
                functools.partial(step, pl.multiple_of(n_wide * wide, tk), (left + 1) * tk, True))

    fixed = shift_ref[0] <= FIXED_SHIFT_MAX

    @pl.when(fixed)
    def _():
        sweep(fixed_shift_step)

    @pl.when(jnp.logical_not(fixed))
    def _():
        m_ref[...] = jnp.full(m_ref.shape, NEG, F32)
        sweep(running_max_step)

    o_slc = acc_ref[:, 0:LANES] / jnp.maximum(acc_ref[:, LANES:2 * LANES], 1e-30)
    gt = gt_ref[...]
    for r in range(rep):
        o = ocw_ref[:, r * LANES:(r + 1) * LANES].astype(F32) + gt[:, 3 * r + 1:3 * r + 2] * o_slc[r * qn:(r + 1) * qn]
        o_ref[:, r * LANES:(r + 1) * LANES] = o.astype(o_ref.dtype)


def _mem_kv_kernel(mem_ref, g_ref, wkv_ref, kn_ref, k_ref, v_ref):
    memn = _rms(mem_ref[...], g_ref[...]).astype(BF16)
    kv = _dot(memn, wkv_ref[...])
    inner = k_ref.shape[1]
    for h in range(inner // X_HEADDIM):
        sl = slice(h * X_HEADDIM, (h + 1) * X_HEADDIM)
        k_ref[:, sl] = _rms(kv[:, sl], kn_ref[...]).astype(BF16)
    v_ref[...] = kv[:, inner:].astype(BF16)


def _mem_kv(mem, g, wkv, kn):
    b, m, d = mem.shape
    inner = wkv.shape[1] // 2
    out_spec = pl.BlockSpec((None, m, inner), lambda bi: (bi, 0, 0))
    out_shape = jax.ShapeDtypeStruct((b, m, inner), BF16)
    return pl.pallas_call(
        _mem_kv_kernel,
        grid=(b,),
        in_specs=[
            pl.BlockSpec((None, m, d), lambda bi: (bi, 0, 0)),
            pl.BlockSpec((1, d), lambda bi: (0, 0)),
            pl.BlockSpec(wkv.shape, lambda bi: (0, 0)),
            pl.BlockSpec((1, X_HEADDIM), lambda bi: (0, 0)),
        ],
        out_specs=[out_spec, out_spec],
        out_shape=[out_shape, out_shape],
        compiler_params=_params("parallel"),
        name="mem_kv",
    )(mem, g, wkv, kn)


def _cross_attn_kernel(*refs, n_in):
    a_refs = refs[:n_in]
    w_ref, x_ref, g_ref, wq_ref, qn_ref, k_ref, v_ref, wo_ref, o_ref = refs[n_in:]
    x = x_ref[...]
    k0 = 0
    for a_ref in a_refs:
        x = x + _dot(a_ref[...], w_ref[k0:k0 + a_ref.shape[1], :])
        k0 += a_ref.shape[1]
    q = _dot(_rms(x, g_ref[...]).astype(BF16), wq_ref[...])
    scale = X_HEADDIM ** -0.5
    outs = []
    for h in range(q.shape[1] // X_HEADDIM):
        sl = slice(h * X_HEADDIM, (h + 1) * X_HEADDIM)
        qh = (_rms(q[:, sl], qn_ref[...]) * scale).astype(BF16)
        s = _dot_nt(qh, k_ref[:, sl])
        e = jnp.exp(s - jnp.max(s, axis=-1, keepdims=True))
        p = e / jnp.sum(e, axis=-1, keepdims=True)
        outs.append(_dot(p.astype(BF16), v_ref[:, sl]))
    o = jnp.concatenate(outs, axis=1).astype(BF16)
    o_ref[...] = x + _dot(o, wo_ref[...])


def _cross_attn(acts, w_out, x, g, wq, qn, k, v, wo, *, tm=1024):
    t, d = x.shape
    b, m, inner = k.shape
    nt = t // b // tm
    return pl.pallas_call(
        functools.partial(_cross_attn_kernel, n_in=len(acts)),
        grid=(b, nt),
        in_specs=[pl.BlockSpec((tm, a.shape[1]), lambda bi, i: (bi * nt + i, 0)) for a in acts] + [
            pl.BlockSpec(w_out.shape, lambda bi, i: (0, 0)),
            pl.BlockSpec((tm, d), lambda bi, i: (bi * nt + i, 0)),
            pl.BlockSpec((1, d), lambda bi, i: (0, 0)),
            pl.BlockSpec(wq.shape, lambda bi, i: (0, 0)),
            pl.BlockSpec((1, X_HEADDIM), lambda bi, i: (0, 0)),
            pl.BlockSpec((None, m, inner), lambda bi, i: (bi, 0, 0)),
            pl.BlockSpec((None, m, inner), lambda bi, i: (bi, 0, 0)),
            pl.BlockSpec(wo.shape, lambda bi, i: (0, 0)),
        ],
        out_specs=pl.BlockSpec((tm, d), lambda bi, i: (bi * nt + i, 0)),
        out_shape=jax.ShapeDtypeStruct((t, d), F32),
        compiler_params=_params("parallel", "parallel"),
        name="cross_attn",
    )(*acts, w_out, x, g, wq, qn, k, v, wo)


def _swiglu_kernel(x_ref, g_ref, wu_ref, wg_ref, w2_ref, o_ref, h_ref):
    @pl.when(pl.program_id(1) == 0)
    def _():
        h_ref[...] = _rms(x_ref[...], g_ref[...]).astype(BF16)
        o_ref[...] = x_ref[...]

    h = h_ref[...]
    tf = wu_ref.shape[1]
    half = tf // 2 // LANES * LANES
    acts = []
    for c in (slice(0, half), slice(half, tf)):
        acts.append((_silu(_dot(h, wg_ref[:, c])) * _dot(h, wu_ref[:, c])).astype(BF16))
    act = jnp.concatenate(acts, axis=1)
    for c0 in range(0, o_ref.shape[1], EXPERT_CHUNK):
        c = slice(c0, c0 + EXPERT_CHUNK)
        o_ref[:, c] += _dot(act, w2_ref[:, c])


def _swiglu(x, g, w13, w2, *, tm=1024, tf=1408):
    t, d = x.shape
    ff = w2.shape[0]
    nf = ff // tf
    return pl.pallas_call(
        _swiglu_kernel,
        grid=(t // tm, nf),
        in_specs=[
            pl.BlockSpec((tm, d), lambda i, f: (i, 0)),
            pl.BlockSpec((1, d), lambda i, f: (0, 0)),
            pl.BlockSpec((d, tf), lambda i, f: (0, f)),
            pl.BlockSpec((d, tf), lambda i, f: (0, nf + f)),
            pl.BlockSpec((tf, d), lambda i, f: (f, 0)),
        ],
        out_specs=pl.BlockSpec((tm, d), lambda i, f: (i, 0)),
        out_shape=jax.ShapeDtypeStruct((t, d), F32),
        scratch_shapes=[pltpu.VMEM((tm, d), BF16)],
        compiler_params=_params("parallel", "arbitrary"),
        name="swiglu",
    )(x, g, w13, w13, w2)


def _split3(x):
    a = x.astype(BF16)
    r = x - a.astype(F32)
    b = r.astype(BF16)
    c = (r - b.astype(F32)).astype(BF16)
    return a, b, c


def _ssd_kernel(z0_ref, z1_ref, x0_ref, x1_ref, bc_ref, dt_ref, cw_ref, cb_ref, dtb_ref, alog_ref, dskip_ref,
                ng_ref, o_ref, tail_ref, state_ref):
    q = x0_ref.shape[0]
    d_inner = o_ref.shape[1]
    gn = SSM_GROUPS * SSM_STATE
    hpg = d_inner // SSM_HEADDIM // SSM_GROUPS

    @pl.when(pl.program_id(1) == 0)
    def _():
        tail_ref[...] = jnp.zeros_like(tail_ref)
        state_ref[...] = jnp.zeros_like(state_ref)

    raw_b = jnp.concatenate([x0_ref[...], x1_ref[...], bc_ref[...]], axis=1)
    raw = raw_b.astype(F32)
    xbc = _silu(_causal_conv(raw_b, raw, tail_ref[...], cw_ref, cb_ref))
    tail_ref[...] = raw[q - 8:q, :]
    xs = xbc[:, :d_inner]
    bm = xbc[:, d_inner:d_inner + gn]
    cm = xbc[:, d_inner + gn:]

    dt = _softplus(dt_ref[...] + dtb_ref[...])
    a = dt * (-jnp.exp(alog_ref[...]))
    ri = lax.broadcasted_iota(I32, (q, q), 0)
    ci = lax.broadcasted_iota(I32, (q, q), 1)
    causal = ci <= ri
    tri = jnp.where(causal, 1.0, 0.0).astype(BF16)
    a_cs = sum(_dot(tri, part) for part in _split3(a))
    a_cs_t = a_cs.T
    dt_t = dt.T
    lane = lax.broadcasted_iota(I32, (1, LANES), 1)
    lo = lane < SSM_HEADDIM

    y_parts = []
    for g in range(SSM_GROUPS):
        cg = cm[:, g * SSM_STATE:(g + 1) * SSM_STATE].astype(BF16)
        bg = bm[:, g * SSM_STATE:(g + 1) * SSM_STATE]
        gmat = _dot_nt(cg, bg.astype(BF16))
        bg_t = bg.T
        gw = hpg * SSM_HEADDIM
        prev = state_ref[:, g * gw:(g + 1) * gw]
        y_off = _dot(cg, prev.astype(BF16))
        for pr in range(hpg // 2):
            c0 = g * gw + pr * LANES
            x_pair = xs[:, c0:c0 + LANES]
            y_pair = dskip_ref[:, c0:c0 + LANES] * x_pair
            st_pair = jnp.zeros((SSM_STATE, LANES), F32)
            decay_pair = jnp.zeros((1, LANES), F32)
            for half in range(2):
                h = g * hpg + pr * 2 + half
                sel = lo if half == 0 else jnp.logical_not(lo)
                xh = jnp.where(sel, x_pair, 0.0).astype(BF16)
                row_cs = a_cs_t[h:h + 1, :]
                col_cs = a_cs[:, h:h + 1]
                row_dt = dt_t[h:h + 1, :]
                a_last = a_cs_t[h:h + 1, q - 1:q]
                dec = jnp.exp(jnp.where(causal, col_cs - row_cs, NEG))
                y_pair = y_pair + _dot((gmat * dec * row_dt).astype(BF16), xh)
                w_row = jnp.exp(a_last - row_cs) * row_dt
                st_pair = st_pair + _dot((bg_t * w_row).astype(BF16), xh)
                y_pair = y_pair + jnp.where(sel, jnp.exp(col_cs) * y_off[:, pr * LANES:(pr + 1) * LANES], 0.0)
                decay_pair = jnp.where(sel, jnp.exp(a_last), decay_pair)
            state_ref[:, c0:c0 + LANES] = decay_pair * state_ref[:, c0:c0 + LANES] + st_pair
            y_parts.append(y_pair)
    y = jnp.concatenate(y_parts, axis=1)

    z = jnp.concatenate([z0_ref[...], z1_ref[...]], axis=1).astype(F32)
    y = y * _silu(z)
    gsz = d_inner // SSM_GROUPS
    for g in range(SSM_GROUPS):
        sl = slice(g * gsz, (g + 1) * gsz)
        o_ref[:, sl] = _rms(y[:, sl], ng_ref[:, sl]).astype(o_ref.dtype)


def _ssd(main, dt, cw, cb, dtb, alog, dskip, ng, *, batch, d_inner):
    t = main.shape[0]
    q = SSD_CHUNK
    nc = t // batch // q
    conv_ch = cw.shape[1]
    half = d_inner // 2
    col = lambda j: pl.BlockSpec((q, half), lambda b, i: (b * nc + i, j))
    vec = lambda n: pl.BlockSpec((1, n), lambda b, i: (0, 0))
    return pl.pallas_call(
        _ssd_kernel,
        grid=(batch, nc),
        in_specs=[
            col(0), col(1), col(2), col(3), col(4),
            pl.BlockSpec((q, LANES), lambda b, i: (b * nc + i, 0)),
            pl.BlockSpec((CONV_W, conv_ch), lambda b, i: (0, 0)),
            vec(conv_ch), vec(LANES), vec(LANES), vec(d_inner), vec(d_inner),
        ],
        out_specs=pl.BlockSpec((q, d_inner), lambda b, i: (b * nc + i, 0)),
        out_shape=jax.ShapeDtypeStruct((t, d_inner), BF16),
        scratch_shapes=[pltpu.VMEM((8, conv_ch), F32), pltpu.VMEM((SSM_STATE, d_inner), F32)],
        compiler_params=_params("parallel", "arbitrary"),
        name="ssd",
    )(main, main, main, main, main, dt, cw, cb, dtb, alog, dskip, ng)


def _router_kernel(x_ref, g_ref, wr_ref, h_ref, info_ref, cnt_ref, run_ref):
    tm = x_ref.shape[0]

    @pl.when(pl.program_id(0) == 0)
    def _():
        run_ref[...] = jnp.zeros_like(run_ref)

    h = _rms(x_ref[...], g_ref[...])
    h_ref[...] = h
    h_hi = h.astype(BF16)
    h_lo = (h - h_hi.astype(F32)).astype(BF16)
    w = wr_ref[...]
    w_hi = w.astype(BF16)
    w_lo = (w - w_hi.astype(F32)).astype(BF16)
    logits = _dot(h_hi, w_hi) + _dot(h_lo, w_hi) + _dot(h_hi, w_lo)
    lane = lax.broadcasted_iota(I32, (tm, LANES), 1)
    lg = jnp.where(lane < N_EXPERTS, logits, NEG)
    m1 = jnp.max(lg, axis=-1, keepdims=True)
    i1 = jnp.min(jnp.where(lg == m1, lane, LANES), axis=-1, keepdims=True)
    lg2 = jnp.where(lane == i1, NEG, lg)
    m2 = jnp.max(lg2, axis=-1, keepdims=True)
    i2 = jnp.min(jnp.where(lg2 == m2, lane, LANES), axis=-1, keepdims=True)
    e2 = jnp.exp(m2 - m1)
    w1 = 1.0 / (1.0 + e2)
    w2 = e2 / (1.0 + e2)

    hot1 = lane == i1
    hot2 = lane == i2
    hot = jnp.where(hot1 | hot2, 1.0, 0.0)
    ri = lax.broadcasted_iota(I32, (tm, tm), 0)
    ci = lax.broadcasted_iota(I32, (tm, tm), 1)
    before = jnp.where(ci < ri, 1.0, 0.0).astype(BF16)
    seen = run_ref[0:1, :] + _dot(before, hot.astype(BF16))
    rank1 = jnp.sum(jnp.where(hot1, seen, 0.0), axis=-1, keepdims=True)
    rank2 = jnp.sum(jnp.where(hot2, seen, 0.0), axis=-1, keepdims=True)
    run_ref[...] = run_ref[...] + jnp.sum(hot, axis=0, keepdims=True)
    cnt_ref[...] = run_ref[...]

    cols = [i1.astype(F32), i2.astype(F32), w1, w2, rank1, rank2]
    info = jnp.zeros((tm, LANES), F32)
    for c, v in enumerate(cols):
        info = jnp.where(lane == c, v, info)
    info_ref[...] = info


def _router(x, g, wr, *, tm=256):
    t, d = x.shape
    return pl.pallas_call(
        _router_kernel,
        grid=(t // tm,),
        in_specs=[
            pl.BlockSpec((tm, d), lambda i: (i, 0)),
            pl.BlockSpec((1, d), lambda i: (0, 0)),
            pl.BlockSpec(wr.shape, lambda i: (0, 0)),
        ],
        out_specs=[
            pl.BlockSpec((tm, d), lambda i: (i, 0)),
            pl.BlockSpec((tm, LANES), lambda i: (i, 0)),
            pl.BlockSpec((8, LANES), lambda i: (0, 0)),
        ],
        out_shape=[
            jax.ShapeDtypeStruct((t, d), F32),
            jax.ShapeDtypeStruct((t, LANES), F32),
            jax.ShapeDtypeStruct((8, LANES), F32),
        ],
        scratch_shapes=[pltpu.VMEM((8, LANES), F32)],
        compiler_params=_params("arbitrary"),
        name="moe_router",
    )(x, g, wr)


def _row_copy(src_ref, src_row, dst_ref, dst_row, sem):
    return pltpu.make_async_copy(src_ref.at[pl.ds(src_row, 1)], dst_ref.at[pl.ds(dst_row, 1)], sem)


def _dispatch_kernel(pos_ref, h_ref, init_ref, xs_ref, sem):
    del init_ref
    tt = h_ref.shape[0]

    def issue(r, c):
        _row_copy(h_ref, r, xs_ref, pos_ref[0, r], sem).start()
        _row_copy(h_ref, r, xs_ref, pos_ref[1, r], sem).start(priority=1)
        return c

    lax.fori_loop(0, tt, issue, 0, unroll=DMA_UNROLL)
    for _ in range(2):
        pltpu.make_async_copy(h_ref, xs_ref.at[pl.ds(0, tt)], sem).wait()


def _dispatch(h, pos, rows, *, tt=256):
    t, d = h.shape
    init = jnp.zeros((rows, d), h.dtype)
    return pl.pallas_call(
        _dispatch_kernel,
        grid=(t // tt,),
        in_specs=[
            pl.BlockSpec((None, 2, tt), lambda i: (i, 0, 0), memory_space=pltpu.SMEM),
            pl.BlockSpec((tt, d), lambda i: (i, 0)),
            pl.BlockSpec(memory_space=pl.ANY),
        ],
        out_specs=pl.BlockSpec(memory_space=pl.ANY),
        out_shape=jax.ShapeDtypeStruct((rows, d), h.dtype),
        scratch_shapes=[pltpu.SemaphoreType.DMA(())],
        input_output_aliases={2: 0},
        compiler_params=_params("arbitrary"),
        name="moe_dispatch",
    )(pos, h, init)


def _expert_kernel(te_ref, tv_ref, xs_ref, wu_ref, wg_ref, w2_ref, o_ref, xb_ref):
    del te_ref
    i = pl.program_id(0)
    f = pl.program_id(1)
    live = tv_ref[i] > 0

    @pl.when(f == 0)
    def _():
        o_ref[...] = jnp.zeros_like(o_ref)
        xb_ref[...] = xs_ref[...].astype(BF16)

    @pl.when(live)
    def _():
        x = xb_ref[...]
        act = (_silu(_dot(x, wg_ref[...].astype(BF16))) * _dot(x, wu_ref[...].astype(BF16))).astype(BF16)
        for c0 in range(0, o_ref.shape[1], EXPERT_CHUNK):
            c = slice(c0, c0 + EXPERT_CHUNK)
            o_ref[:, c] += _dot(act, w2_ref[:, c].astype(BF16))


def _experts(xs, w13, w2, tile_expert, tile_live, *, tm, tf=512):
    rows = xs.shape[0]
    ff, d = w2.shape[1:]
    nf = ff // tf

    def f_of(i, f, te, tv):
        return jnp.where(tv[i] > 0, f, nf - 1)

    grid_spec = pltpu.PrefetchScalarGridSpec(
        num_scalar_prefetch=2,
        grid=(rows // tm, nf),
        in_specs=[
            pl.BlockSpec((tm, xs.shape[1]), lambda i, f, te, tv: (i, 0)),
            pl.BlockSpec((None, d, tf), lambda i, f, te, tv: (te[i], 0, f_of(i, f, te, tv))),
            pl.BlockSpec((None, d, tf), lambda i, f, te, tv: (te[i], 0, nf + f_of(i, f, te, tv))),
            pl.BlockSpec((None, tf, d), lambda i, f, te, tv: (te[i], f_of(i, f, te, tv), 0)),
        ],
        out_specs=pl.BlockSpec((tm, d), lambda i, f, te, tv: (i, 0)),
        scratch_shapes=[pltpu.VMEM((tm, d), BF16)],
    )
    return pl.pallas_call(
        _expert_kernel,
        grid_spec=grid_spec,
        out_shape=jax.ShapeDtypeStruct((rows, d), F32),
        compiler_params=_params("parallel", "arbitrary"),
        name="moe_experts",
    )(tile_expert, tile_live, xs, w13, w13, w2)


def _combine_kernel(pos_ref, x_ref, info_ref, ys_ref, o_ref, buf_ref, sem):
    tt = x_ref.shape[0]

    def issue(r, c):
        _row_copy(ys_ref, pos_ref[0, r], buf_ref.at[0], r, sem).start()
        _row_copy(ys_ref, pos_ref[1, r], buf_ref.at[1], r, sem).start(priority=1)
        return c

    lax.fori_loop(0, tt, issue, 0, unroll=DMA_UNROLL)
    for k in range(2):
        pltpu.make_async_copy(ys_ref.at[pl.ds(0, tt)], buf_ref.at[k], sem).wait()
    info = info_ref[...]
    o_ref[...] = x_ref[...] + info[:, 2:3] * buf_ref[0] + info[:, 3:4] * buf_ref[1]


def _combine(x, info, pos, ys, *, tt=256):
    t, d = x.shape
    return pl.pallas_call(
        _combine_kernel,
        grid=(t // tt,),
        in_specs=[
            pl.BlockSpec((None, 2, tt), lambda i: (i, 0, 0), memory_space=pltpu.SMEM),
            pl.BlockSpec((tt, d), lambda i: (i, 0)),
            pl.BlockSpec((tt, LANES), lambda i: (i, 0)),
            pl.BlockSpec(memory_space=pl.ANY),
        ],
        out_specs=pl.BlockSpec((tt, d), lambda i: (i, 0)),
        out_shape=jax.ShapeDtypeStruct((t, d), F32),
        scratch_shapes=[pltpu.VMEM((2, tt, d), F32), pltpu.SemaphoreType.DMA(())],
        compiler_params=_params("arbitrary"),
        name="moe_combine",
    )(pos, x, info, ys)


def _moe(x, g, router, w13, w2, *, tm=1024, tt=512):
    t, d = x.shape
    n_exp = router.shape[1]
    wr = jnp.pad(router, ((0, 0), (0, LANES - n_exp)))
    h, info, counts = _router(x, g, wr, tm=tt)

    counts = counts[0, :n_exp].astype(I32)
    seg = (counts + tm - 1) // tm * tm
    seg_end = jnp.cumsum(seg)
    seg_start = seg_end - seg
    e1 = info[:, 0].astype(I32)
    e2 = info[:, 1].astype(I32)
    pos = jnp.stack([seg_start[e1] + info[:, 4].astype(I32), seg_start[e2] + info[:, 5].astype(I32)], axis=0)
    pos = pos.reshape(2, t // tt, tt).transpose(1, 0, 2)
    rows = 2 * t + n_exp * tm
    tile_row0 = jnp.arange(rows // tm, dtype=I32) * tm
    tile_live = (tile_row0 < seg_end[-1]).astype(I32)
    tile_expert = jnp.sum((seg_end[None, :] <= tile_row0[:, None]).astype(I32), axis=1)
    tile_expert = jnp.minimum(tile_expert, n_exp - 1)
    last_live = jnp.maximum(jnp.sum(tile_live) - 1, 0)
    tile_expert = jnp.where(tile_live > 0, tile_expert, tile_expert[last_live])

    xs = _dispatch(h, pos, rows, tt=tt)
    ys = _experts(xs, w13, w2, tile_expert, tile_live, tm=tm)
    return _combine(x, info, pos, ys, tt=tt)


def _row(v, n=None):
    v = v.reshape(1, -1).astype(F32)
    if n is not None and v.shape[1] < n:
        v = jnp.pad(v, ((0, 0), (0, n - v.shape[1])))
    return v


def _overlap_matrix(seq):
    n = seq // CMP_STRIDE
    cmp_start = np.arange(n) * CMP_STRIDE
    slc_start = np.arange(LANES) * SLC_LEN
    ov = (cmp_start[:, None] <= slc_start[None, :] + SLC_LEN - 1) & (cmp_start[:, None] + CMP_LEN - 1 >= slc_start[None, :])
    ov[n - 1] = False
    return jnp.asarray(ov, dtype=BF16)


def _even_layer(x, batch, norm_mix, w_in, conv_w, conv_b, wa, ba, wx, bx, lam, gate_b, q_norm, k_norm, cmp_pos,
                ck_w1, ck_w2, cv_w1, cv_w2):
    t, d = x.shape
    seq = t // batch
    rg = wa.shape[0] * wa.shape[1]
    gdk = NSA_GROUPS * NSA_DK
    n_main = 2 * rg + NSA_HEADS * NSA_DK
    n_kv = 6 * gdk
    per_group = 3 * NSA_REP
    gate_cols = w_in[:, n_main + n_kv:].reshape(d, NSA_GROUPS, per_group)
    gate_cols = jnp.pad(gate_cols, ((0, 0), (0, 0), (0, LANES - per_group))).reshape(d, NSA_GROUPS * LANES)
    gate_bias = jnp.pad(gate_b.reshape(NSA_GROUPS, per_group), ((0, 0), (0, LANES - per_group))).reshape(1, -1)
    w_all = jnp.concatenate([w_in[:, :n_main + n_kv], gate_cols], axis=1).astype(BF16)
    kn = jnp.pad(k_norm, ((0, 8 - k_norm.shape[0]), (0, 0)))
    main, planes, kp, vp, kwn, vwp, gates = _norm_proj(
        x, _row(norm_mix), w_all, gate_bias, kn, batch=batch, n_main=n_main, nsa_keys=True,
        n_extra=NSA_GROUPS * LANES, extra_sigmoid=True)

    rg_out = _rglru(main, conv_w, _row(conv_b), wa.astype(BF16), _row(ba), wx.astype(BF16), _row(bx), _row(lam),
                    batch=batch)

    k_gain = jnp.max(jnp.abs(k_norm), axis=1)[jnp.array([1, 0, 2])]
    shift = ((1.02 * LOG2E * math.sqrt(NSA_DK)) * jnp.max(jnp.abs(q_norm)) * k_gain).astype(F32)
    pos_flat = jnp.broadcast_to(cmp_pos.reshape(1, -1), (8, CMP_LEN * NSA_DK)).astype(BF16)
    kcmp, vcmp = _nsa_compress(planes, pos_flat, ck_w1.astype(BF16), ck_w2.astype(BF16), cv_w1.astype(BF16),
                               cv_w2.astype(BF16), kn, _overlap_matrix(seq))
    att = _nsa_attn(shift, main, kcmp, vcmp, kwn, vwp, kp, vp, gates, _row(q_norm))
    return [rg_out, att]


def _odd_layer(x, batch, norm_mix, w_in, conv_w, conv_b, dt_bias, a_log, d_skip, norm_g, d_inner):
    conv_ch = conv_w.shape[1]
    n_main = d_inner + conv_ch
    heads = dt_bias.shape[0]
    w_all = jnp.concatenate([w_in[:, :n_main], jnp.pad(w_in[:, n_main:], ((0, 0), (0, LANES - heads)))], axis=1)
    main, dt = _norm_proj(x, _row(norm_mix), w_all.astype(BF16), jnp.zeros((1, LANES), F32),
                          jnp.zeros((8, LANES), F32), batch=batch, n_main=n_main, nsa_keys=False, n_extra=LANES,
                          extra_sigmoid=False)
    y = _ssd(main, dt, conv_w, _row(conv_b), _row(dt_bias, LANES), _row(a_log, LANES),
             _row(jnp.repeat(d_skip, SSM_HEADDIM)), _row(norm_g), batch=batch, d_inner=d_inner)
    return [y]


def kernel(x, mem, norm_mix, norm_cross, norm_mem, norm_ffn, ev_w_in, ev_rg_conv_w, ev_rg_conv_b, ev_rg_wa, ev_rg_ba, ev_rg_wx, ev_rg_bx, ev_rg_lambda, ev_nsa_gate_b, ev_q_norm, ev_k_norm, ev_cmp_pos, ev_cmp_k_w1, ev_cmp_k_w2, ev_cmp_v_w1, ev_cmp_v_w2, ev_w_out, od_w_in, od_conv_w, od_conv_b, od_dt_bias, od_a_log, od_d_skip, od_norm, od_w_out, x_wq, x_wkv, x_q_norm, x_k_norm, x_wo, ff_w13, ff_w2, moe_router, moe_w13, moe_w2):
    batch, seq, d = x.shape
    depth = norm_mix.shape[0]
    xf = x.reshape(batch * seq, d)
    for layer in range(depth):
        i = layer // 2
        if layer % 2 == 0:
            w_out = ev_w_out[i]
            mixed = _even_layer(xf, batch, norm_mix[layer], ev_w_in[i], ev_rg_conv_w[i], ev_rg_conv_b[i], ev_rg_wa[i],
                                ev_rg_ba[i], ev_rg_wx[i], ev_rg_bx[i], ev_rg_lambda[i], ev_nsa_gate_b[i],
                                ev_q_norm[i], ev_k_norm[i], ev_cmp_pos[i], ev_cmp_k_w1[i], ev_cmp_k_w2[i],
                                ev_cmp_v_w1[i], ev_cmp_v_w2[i])
        else:
            w_out = od_w_out[i]
            mixed = _odd_layer(xf, batch, norm_mix[layer], od_w_in[i], od_conv_w[i], od_conv_b[i], od_dt_bias[i],
                               od_a_log[i], od_d_skip[i], od_norm[i], w_out.shape[0])
        k, v = _mem_kv(mem, _row(norm_mem[layer]), x_wkv[layer].astype(BF16), _row(x_k_norm[layer]))
        xf = _cross_attn(mixed, w_out.astype(BF16), xf, _row(norm_cross[layer]), x_wq[layer].astype(BF16),
                         _row(x_q_norm[layer]), k, v, x_wo[layer].astype(BF16))
        if layer % 2 == 0:
            xf = _swiglu(xf, _row(norm_ffn[layer]), ff_w13[i].astype(BF16), ff_w2[i].astype(BF16))
        else:
            xf = _moe(xf, _row(norm_ffn[layer]), moe_router[i], moe_w13[i], moe_w2[i])
    return xf.reshape(batch, seq, d)
```

```python
import functools
import math

import jax
import jax.numpy as jnp
import numpy as np
from jax import lax
from jax.experimental import pallas as pl
from jax.experimental.pallas import tpu as pltpu

F32 = jnp.float32
BF16 = jnp.bfloat16
I32 = jnp.int32

EPS = 1e-6
CONV_W = 4
RG_BLOCKS = 8
RG_C = 8.0
NSA_HEADS = 8
NSA_GROUPS = 2
NSA_REP = NSA_HEADS // NSA_GROUPS
NSA_DK = 128
CMP_LEN = 32
CMP_STRIDE = 16
SLC_LEN = 64
SLC_SHIFT = 6
SLC_WIDE = 4
N_FORCED = 3
SLC_TOPN = 16
WINDOW = 512
SSM_HEADDIM = 64
SSM_GROUPS = 4
SSM_STATE = 128
SSD_CHUNK = 128
SSD_CHUNKS_PER_STEP = 2
X_HEADS = 4
X_HEADDIM = 128
N_EXPERTS = 8
EXPERT_CHUNK = 256
DMA_UNROLL = 8

LANES = 128
SUBLANES = 8
VMEM_LIMIT_BYTES = 56 * 1024 * 1024
NEG = -1e30
SEL_BIAS = float(2 ** 20)
LOG2E = math.log2(math.e)
FIXED_SHIFT_MAX = 56.0

NT_DIMS = (((1,), (1,)), ((), ()))


def _params(*sem):
    return pltpu.CompilerParams(dimension_semantics=sem, vmem_limit_bytes=VMEM_LIMIT_BYTES)


def _dot(a, b):
    return jnp.dot(a, b, preferred_element_type=F32)


def _dot_nt(a, b):
    return lax.dot_general(a, b, NT_DIMS, preferred_element_type=F32)


def _rms(x, g):
    return x * lax.rsqrt(jnp.mean(x * x, axis=-1, keepdims=True) + EPS) * g


def _sigmoid(x):
    return 1.0 / (1.0 + jnp.exp(-x))


def _silu(x):
    return x * _sigmoid(x)


def _gelu_tanh(x):
    c = math.sqrt(2.0 / math.pi)
    return 0.5 * x * (1.0 + jnp.tanh(c * (x + 0.044715 * (x * x * x))))


def _softplus(x):
    return jnp.maximum(x, 0.0) + jnp.log(1.0 + jnp.exp(-jnp.abs(x)))


def _causal_conv(xb, x, tail, w_ref, b_ref):
    n = xb.shape[0]
    delay = lax.broadcasted_iota(I32, (n, n), 0) - lax.broadcasted_iota(I32, (n, n), 1)
    r8 = lax.broadcasted_iota(I32, (SUBLANES, 1), 0)
    y = b_ref[...] + w_ref[CONV_W - 1:CONV_W, :] * x
    head = jnp.zeros(tail.shape, F32)
    for k in range(1, CONV_W):
        wk = w_ref[CONV_W - 1 - k:CONV_W - k, :]
        y = y + wk * _dot(jnp.where(delay == k, 1.0, 0.0).astype(BF16), xb)
        head = head + wk * jnp.where(r8 < k, pltpu.roll(tail, k, 0), 0.0)
    return jnp.concatenate([y[0:SUBLANES] + head, y[SUBLANES:]], axis=0)


def _norm_proj_kernel(x_ref, g_ref, w_ref, eb_ref, kn_ref, *out_refs, n_main, nsa_keys, tiles_per_seq,
                      extra_sigmoid):
    h = _rms(x_ref[...], g_ref[...]).astype(BF16)
    main_ref = out_refs[0]
    for c0 in range(0, n_main, 512):
        main_ref[:, c0:c0 + 512] = _dot(h, w_ref[:, c0:c0 + 512]).astype(main_ref.dtype)
    col = n_main
    ex_ref = out_refs[-1]
    if nsa_keys:
        cmp_ref, kp_ref, vp_ref, kwn_ref, vwp_ref = out_refs[1:6]
        tm = x_ref.shape[0]
        g = NSA_GROUPS
        gsl = lambda r, p: r[:, p * LANES:(p + 1) * LANES]
        r = _dot(h, w_ref[:, col:col + 2 * g * LANES])
        for p in range(2 * g):
            cmp_ref[p] = gsl(r, p).astype(BF16)
        col += 2 * g * LANES
        t0 = (pl.program_id(0) % tiles_per_seq) * tm
        blk = jnp.right_shift(t0 + lax.broadcasted_iota(I32, (tm, LANES), 0), SLC_SHIFT)
        onehot = jnp.where(blk == lax.broadcasted_iota(I32, (tm, LANES), 1), 1.0, 0.0).astype(BF16)
        ones = jnp.ones((tm, LANES), BF16)
        r = _dot(h, w_ref[:, col:col + 2 * g * LANES])
        for gi in range(g):
            kp_ref[gi] = jnp.concatenate([_rms(gsl(r, gi), kn_ref[1:2, :]).astype(BF16), onehot], axis=1)
            vp_ref[gi] = jnp.concatenate([gsl(r, g + gi).astype(BF16), ones], axis=1)
        col += 2 * g * LANES
        r = _dot(h, w_ref[:, col:col + 2 * g * LANES])
        for gi in range(g):
            kwn_ref[gi] = _rms(gsl(r, gi), kn_ref[2:3, :]).astype(BF16)
            vwp_ref[gi] = jnp.concatenate([gsl(r, g + gi).astype(BF16), ones], axis=1)
        col += 2 * g * LANES
    n_extra = ex_ref.shape[1]
    e = _dot(h, w_ref[:, col:col + n_extra]) + eb_ref[...]
    ex_ref[...] = _sigmoid(e) if extra_sigmoid else e


def _norm_proj(x, g, w, eb, kn, *, batch, n_main, nsa_keys, n_extra, extra_sigmoid, tm=512):
    t, d = x.shape
    seq = t // batch
    nt = seq // tm
    out_shape = [jax.ShapeDtypeStruct((t, n_main), BF16)]
    out_specs = [pl.BlockSpec((tm, n_main), lambda i: (i, 0))]
    if nsa_keys:
        for planes, width in ((2 * NSA_GROUPS, LANES), (NSA_GROUPS, 2 * LANES), (NSA_GROUPS, 2 * LANES),
                              (NSA_GROUPS, LANES), (NSA_GROUPS, 2 * LANES)):
            out_shape.append(jax.ShapeDtypeStruct((batch, planes, seq, width), BF16))
            out_specs.append(pl.BlockSpec((None, planes, tm, width), lambda i: (i // nt, 0, i % nt, 0)))
    out_shape.append(jax.ShapeDtypeStruct((t, n_extra), F32))
    out_specs.append(pl.BlockSpec((tm, n_extra), lambda i: (i, 0)))
    kern = functools.partial(_norm_proj_kernel, n_main=n_main, nsa_keys=nsa_keys, tiles_per_seq=nt,
                             extra_sigmoid=extra_sigmoid)
    return pl.pallas_call(
        kern,
        grid=(t // tm,),
        in_specs=[
            pl.BlockSpec((tm, d), lambda i: (i, 0)),
            pl.BlockSpec((1, d), lambda i: (0, 0)),
            pl.BlockSpec(w.shape, lambda i: (0, 0)),
            pl.BlockSpec((1, n_extra), lambda i: (0, 0)),
            pl.BlockSpec(kn.shape, lambda i: (0, 0)),
        ],
        out_specs=out_specs,
        out_shape=out_shape,
        compiler_params=_params("parallel"),
        name="norm_proj",
    )(x, g, w, eb, kn)


def _rglru_kernel(rx_ref, rg_ref, cw_ref, cb_ref, wa_ref, ba_ref, wx_ref, bx_ref, lam_ref, o_ref, tail_ref, h_ref):
    tc, c = rx_ref.shape

    @pl.when(pl.program_id(1) == 0)
    def _():
        tail_ref[...] = jnp.zeros_like(tail_ref)
        h_ref[...] = jnp.zeros_like(h_ref)

    xb = rx_ref[...]
    x = xb.astype(F32)
    xc = _causal_conv(xb, x, tail_ref[...], cw_ref, cb_ref)
    tail_ref[...] = x[tc - 8:tc, :]

    bw = c // RG_BLOCKS
    ra, rx = [], []
    for blk in range(RG_BLOCKS):
        xb = xc[:, blk * bw:(blk + 1) * bw].astype(BF16)
        ra.append(_dot(xb, wa_ref[blk]))
        rx.append(_dot(xb, wx_ref[blk]))
    r = _sigmoid(jnp.concatenate(ra, axis=1) + ba_ref[...])
    ig = _sigmoid(jnp.concatenate(rx, axis=1) + bx_ref[...])
    log_a = (-RG_C) * r * _softplus(-lam_ref[...])
    a = jnp.exp(log_a)
    z = 1.0 - a * a
    u = jnp.where(z > 0.0, z * lax.rsqrt(z), 0.0) * (ig * xc)

    in_group = lax.broadcasted_iota(I32, (tc, 1), 0) & (SUBLANES - 1)
    d = 1
    while d < SUBLANES:
        keep = in_group >= d
        a_sh = jnp.where(keep, pltpu.roll(a, d, 0), 1.0)
        u_sh = jnp.where(keep, pltpu.roll(u, d, 0), 0.0)
        u = a * u_sh + u
        a = a * a_sh
        d *= 2
    carry = h_ref[SUBLANES - 1:SUBLANES, :]
    groups = []
    for g0 in range(0, tc, SUBLANES):
        hg = u[g0:g0 + SUBLANES, :] + a[g0:g0 + SUBLANES, :] * carry
        carry = hg[SUBLANES - 1:SUBLANES, :]
        groups.append(hg)
    h_ref[...] = groups[-1]
    o_ref[...] = (_gelu_tanh(rg_ref[...].astype(F32)) * jnp.concatenate(groups, axis=0)).astype(o_ref.dtype)


def _rglru(main, cw, cb, wa, ba, wx, bx, lam, *, batch, tc=256):
    t = main.shape[0]
    c = cw.shape[1]
    nt = t // batch // tc
    vec = pl.BlockSpec((1, c), lambda b, i: (0, 0))
    blk = pl.BlockSpec(wa.shape, lambda b, i: (0, 0, 0))
    return pl.pallas_call(
        _rglru_kernel,
        grid=(batch, nt),
        in_specs=[
            pl.BlockSpec((tc, c), lambda b, i: (b * nt + i, 0)),
            pl.BlockSpec((tc, c), lambda b, i: (b * nt + i, 1)),
            pl.BlockSpec((CONV_W, c), lambda b, i: (0, 0)),
            vec, blk, vec, blk, vec, vec,
        ],
        out_specs=pl.BlockSpec((tc, c), lambda b, i: (b * nt + i, 0)),
        out_shape=jax.ShapeDtypeStruct((t, c), BF16),
        scratch_shapes=[pltpu.VMEM((8, c), F32), pltpu.VMEM((8, c), F32)],
        compiler_params=_params("parallel", "arbitrary"),
        name="rglru",
    )(main, main, cw, cb, wa, ba, wx, bx, lam)


def _nsa_compress_kernel(xk_ref, xv_ref, pos_ref, kw1_ref, kw2_ref, vw1_ref, vw2_ref, kn_ref, ov_ref, kc_ref, vc_ref):
    n, half = xk_ref.shape
    last = lax.broadcasted_iota(I32, (n, 1), 0) == n - 1
    pos = pos_ref[...]

    def compress(x_ref, w1_ref, w2_ref):
        x = x_ref[...]
        y0 = _dot(x, w1_ref[0:half, :])
        y1 = _dot(x, w1_ref[half:2 * half, :])
        y1_next = jnp.where(last, 0.0, pltpu.roll(y1, n - 1, 0))
        const = _dot(pos, w1_ref[...])[0:1, :]
        hid = _gelu_tanh(y0 + y1_next + const)
        return _dot(hid.astype(BF16), w2_ref[...])

    kc_ref[...] = _rms(compress(xk_ref, kw1_ref, kw2_ref), kn_ref[0:1, :]).astype(BF16)
    vc = compress(xv_ref, vw1_ref, vw2_ref).astype(BF16)
    vc_ref[...] = jnp.concatenate([vc, jnp.ones((n, LANES), BF16), ov_ref[...]], axis=1)


def _nsa_compress(planes, pos_flat, kw1, kw2, vw1, vw2, k_norm, overlap):
    b, _, seq, _ = planes.shape
    g = NSA_GROUPS
    n = seq // CMP_STRIDE
    half = CMP_STRIDE * LANES
    grouped = planes.reshape(b, 2 * g, n, half)
    full = lambda a: pl.BlockSpec(a.shape, lambda bi, gi: (0,) * a.ndim)
    out_spec = lambda w: pl.BlockSpec((None, None, n, w), lambda bi, gi: (bi, gi, 0, 0))
    out_shape = lambda w: jax.ShapeDtypeStruct((b, g, n, w), BF16)
    return pl.pallas_call(
        _nsa_compress_kernel,
        grid=(b, g),
        in_specs=[
            pl.BlockSpec((None, None, n, half), lambda bi, gi: (bi, gi, 0, 0)),
            pl.BlockSpec((None, None, n, half), lambda bi, gi: (bi, 2 + gi, 0, 0)),
            full(pos_flat), full(kw1), full(kw2), full(vw1), full(vw2), full(k_norm), full(overlap),
        ],
        out_specs=[out_spec(LANES), out_spec(3 * LANES)],
        out_shape=[out_shape(LANES), out_shape(3 * LANES)],
        compiler_params=_params("parallel", "parallel"),
        name="nsa_compress",
    )(grouped, grouped, pos_flat, kw1, kw2, vw1, vw2, k_norm, overlap)


def _nsa_cw_kernel(shift_ref, *refs, qn, n_sel):
    fixed = jnp.maximum(shift_ref[1], shift_ref[2]) <= FIXED_SHIFT_MAX
    pl.when(fixed)(functools.partial(_nsa_cw_body, shift_ref, *refs, qn=qn, n_sel=n_sel, fixed=True))
    pl.when(jnp.logical_not(fixed))(functools.partial(_nsa_cw_body, shift_ref, *refs, qn=qn, n_sel=n_sel, fixed=False))


def _nsa_cw_body(shift_ref, q_ref, kc_ref, vc_ref, kw_ref, vw_ref, gt_ref, qn_ref, wb_ref, qp_ref, o_ref, *, qn, n_sel,
                 fixed):
    t0 = pl.program_id(2) * qn
    rep = NSA_REP
    rows = rep * qn
    scale = NSA_DK ** -0.5 * LOG2E
    qf = q_ref[...].astype(F32)
    heads = []
    for r in range(rep):
        qh = _rms(qf[:, r * LANES:(r + 1) * LANES], qn_ref[...]) * scale
        heads.append(qh.astype(BF16))
    qs = jnp.concatenate(heads, axis=0)
    trow = t0 + (lax.broadcasted_iota(I32, (rows, 1), 0) & (qn - 1))

    def attend(sm, vx, shift):
        if not fixed:
            sm = sm - jnp.maximum(jnp.max(sm, axis=-1, keepdims=True), -2.0 * shift)
        return _dot(jnp.exp2(sm).astype(BF16), vx)

    n_cmp = kc_ref.shape[0]
    visible = lax.broadcasted_iota(I32, (1, n_cmp), 1) * CMP_STRIDE + (CMP_LEN - 1) <= trow
    r_cmp = attend(jnp.where(visible, _dot_nt(qs, kc_ref[...]), NEG) - shift_ref[1], vc_ref[...], shift_ref[1])
    inv = 1.0 / jnp.maximum(r_cmp[:, LANES:2 * LANES], 1e-30)
    o_cmp = r_cmp[:, 0:LANES] * inv
    imp_h = r_cmp[:, 2 * LANES:3 * LANES] * inv
    imp = imp_h[0:qn]
    for r in range(1, rep):
        imp = imp + imp_h[r * qn:(r + 1) * qn]

    span = WINDOW + qn
    start = pl.multiple_of(jnp.maximum(t0 - WINDOW, 0), qn)
    s = _dot_nt(qs, kw_ref[pl.ds(start, span), :]).reshape(rep, qn, span) + wb_ref[...]
    r_win = attend(s.reshape(rows, span), vw_ref[pl.ds(start, span), :], shift_ref[2])
    o_win = r_win[:, 0:LANES] / jnp.maximum(r_win[:, LANES:2 * LANES], 1e-30)

    imp_t = imp.T
    jj = lax.broadcasted_iota(I32, imp_t.shape, 0).astype(F32)
    cur = jnp.right_shift(t0 + lax.broadcasted_iota(I32, imp_t.shape, 1), SLC_SHIFT).astype(F32)
    forced = (jj == 0.0) | (jj == cur) | (jj == cur - 1.0)
    shift = -shift_ref[0]
    taken = forced & (jj <= cur)
    work = jnp.where(taken, -2.0, jnp.where(jj <= cur, imp_t, -1.0))
    bias_t = jnp.where(taken, shift, -SEL_BIAS)
    for _ in range(n_sel - N_FORCED):
        m = jnp.max(work, axis=0, keepdims=True)
        idx = jnp.min(jnp.where(work == m, jj, float(LANES)), axis=0, keepdims=True)
        pick = jj == idx
        bias_t = jnp.where(pick, shift, bias_t)
        work = jnp.where(pick, -2.0, work)
    bias = bias_t.T.astype(BF16)
    for r in range(rep):
        qp_ref[r] = jnp.concatenate([heads[r], bias], axis=1)

    gt = gt_ref[...]
    for r in range(rep):
        sl = slice(r * qn, (r + 1) * qn)
        o = gt[:, 3 * r:3 * r + 1] * o_cmp[sl] + gt[:, 3 * r + 2:3 * r + 3] * o_win[sl]
        o_ref[:, r * LANES:(r + 1) * LANES] = o.astype(o_ref.dtype)


def _nsa_attn(shift, main, kcmp, vcmp, kwn, vwp, kp, vp, gates, q_norm, *, qn=256, tk=512):
    b, g, seq, _ = kwn.shape
    t = main.shape[0]
    nq = seq // qn
    rep = NSA_REP
    gw = rep * LANES
    q_blk0 = (main.shape[1] - NSA_HEADS * NSA_DK) // gw
    n_cmp = kcmp.shape[2]
    n_case = WINDOW // qn + 1
    span = WINDOW + qn
    in_window = []
    for case in range(n_case):
        t0 = case * qn
        diff = (t0 + np.arange(qn)[:, None]) - (max(t0 - WINDOW, 0) + np.arange(span)[None, :])
        in_window.append((diff >= 0) & (diff < WINDOW))
    window_bias = jnp.where(jnp.asarray(np.stack(in_window)), -shift[2], NEG).astype(F32)
    tk = min(tk, seq)
    kern = functools.partial(_nsa_attn_kernel, qn=qn, n_sel=min(SLC_TOPN, seq // SLC_LEN), tk=tk)
    per_group = lambda rows, width: pl.BlockSpec((None, None, rows, width), lambda bi, gi, i: (bi, gi, 0, 0))
    return pl.pallas_call(
        kern,
        grid=(b, g, nq),
        in_specs=[
            pl.BlockSpec(memory_space=pltpu.SMEM),
            pl.BlockSpec((qn, gw), lambda bi, gi, i: (bi * nq + i, q_blk0 + gi)),
            per_group(n_cmp, LANES), per_group(n_cmp, 3 * LANES), per_group(seq, LANES), per_group(seq, 2 * LANES),
            pl.BlockSpec((qn, LANES), lambda bi, gi, i: (bi * nq + i, gi)),
            pl.BlockSpec((1, LANES), lambda bi, gi, i: (0, 0)),
            pl.BlockSpec((None, qn, span), lambda bi, gi, i: (jnp.minimum(i, n_case - 1), 0, 0)),
            per_group(seq, 2 * LANES), per_group(seq, 2 * LANES),
        ],
        out_specs=pl.BlockSpec((qn, gw), lambda bi, gi, i: (bi * nq + i, gi)),
        out_shape=jax.ShapeDtypeStruct((t, NSA_HEADS * NSA_DK), BF16),
        scratch_shapes=[
            pltpu.VMEM((rep, qn, 2 * LANES), BF16),
            pltpu.VMEM((qn, gw), BF16),
            pltpu.VMEM((rep * qn, 1), F32),
            pltpu.VMEM((rep * qn, 2 * LANES), F32),
        ],
        compiler_params=_params("parallel", "parallel", "parallel"),
        name="nsa_attention",
    )(shift, main, kcmp, vcmp, kwn, vwp, gates, q_norm, window_bias, kp, vp)


def _nsa_attn_kernel(shift_ref, q_ref, kc_ref, vc_ref, kw_ref, vw_ref, gt_ref, qn_ref, wb_ref, kp_ref, vp_ref, o_ref,
                     qp_ref, ocw_ref, m_ref, acc_ref, *, qn, n_sel, tk):
    _nsa_cw_kernel(shift_ref, q_ref, kc_ref, vc_ref, kw_ref, vw_ref, gt_ref, qn_ref, wb_ref, qp_ref, ocw_ref, qn=qn,
                   n_sel=n_sel)
    _nsa_slc_kernel(shift_ref, qp_ref, kp_ref, vp_ref, gt_ref, ocw_ref, o_ref, m_ref, acc_ref, qn=qn, tk=tk)


def _nsa_slc_kernel(shift_ref, qp_ref, kp_ref, vp_ref, gt_ref, ocw_ref, o_ref, m_ref, acc_ref, *, qn, tk):
    t0 = pl.program_id(2) * qn
    rep = NSA_REP
    rows = rep * qn
    last = (t0 + qn - 1) // tk
    acc_ref[...] = jnp.zeros_like(acc_ref)

    def scores(k0, width, causal):
        qp = qp_ref[...].reshape(rows, qp_ref.shape[2])
        s = _dot_nt(qp, kp_ref[pl.ds(k0, width), :])
        if causal:
            trow = t0 + (lax.broadcasted_iota(I32, (rows, 1), 0) & (qn - 1))
            s = jnp.where(k0 + lax.broadcasted_iota(I32, (1, width), 1) <= trow, s, -SEL_BIAS)
        return s, vp_ref[pl.ds(k0, width), :]

    def fixed_shift_step(k0, width, causal):
        s, v = scores(k0, width, causal)
        acc_ref[...] += _dot(jnp.exp2(s).astype(BF16), v)

    def running_max_step(k0, width, causal):
        s, v = scores(k0, width, causal)
        m_old = m_ref[...]
        m_new = jnp.maximum(m_old, jnp.max(s, axis=-1, keepdims=True))
        acc_ref[...] = jnp.exp2(m_old - m_new) * acc_ref[...] + _dot(jnp.exp2(s - m_new).astype(BF16), v)
        m_ref[...] = m_new

    def sweep(step):
        wide = SLC_WIDE * tk
        n_wide = last // SLC_WIDE
        lax.fori_loop(0, n_wide, lambda j, c: (step(pl.multiple_of(j * wide, wide), wide, False), c)[1], 0)
        for left in range(SLC_WIDE):
            pl.when(last - n_wide * SLC_WIDE == left)(
                functools.partial(step, pl.multiple_of(n_wide * wide, tk), (left + 1) * tk, True))

    fixed = shift_ref[0] <= FIXED_SHIFT_MAX

    @pl.when(fixed)
    def _():
        sweep(fixed_shift_step)

    @pl.when(jnp.logical_not(fixed))
    def _():
        m_ref[...] = jnp.full(m_ref.shape, NEG, F32)
        sweep(running_max_step)

    o_slc = acc_ref[:, 0:LANES] / jnp.maximum(acc_ref[:, LANES:2 * LANES], 1e-30)
    gt = gt_ref[...]
    for r in range(rep):
        o = ocw_ref[:, r * LANES:(r + 1) * LANES].astype(F32) + gt[:, 3 * r + 1:3 * r + 2] * o_slc[r * qn:(r + 1) * qn]
        o_ref[:, r * LANES:(r + 1) * LANES] = o.astype(o_ref.dtype)


def _mem_kv_kernel(mem_ref, g_ref, wkv_ref, kn_ref, k_ref, v_ref):
    memn = _rms(mem_ref[...], g_ref[...]).astype(BF16)
    kv = _dot(memn, wkv_ref[...])
    inner = k_ref.shape[1]
    for h in range(inner // X_HEADDIM):
        sl = slice(h * X_HEADDIM, (h + 1) * X_HEADDIM)
        k_ref[:, sl] = _rms(kv[:, sl], kn_ref[...]).astype(BF16)
    v_ref[...] = kv[:, inner:].astype(BF16)


def _mem_kv(mem, g, wkv, kn):
    b, m, d = mem.shape
    inner = wkv.shape[1] // 2
    out_spec = pl.BlockSpec((None, m, inner), lambda bi: (bi, 0, 0))
    out_shape = jax.ShapeDtypeStruct((b, m, inner), BF16)
    return pl.pallas_call(
        _mem_kv_kernel,
        grid=(b,),
        in_specs=[
            pl.BlockSpec((None, m, d), lambda bi: (bi, 0, 0)),
            pl.BlockSpec((1, d), lambda bi: (0, 0)),
            pl.BlockSpec(wkv.shape, lambda bi: (0, 0)),
            pl.BlockSpec((1, X_HEADDIM), lambda bi: (0, 0)),
        ],
        out_specs=[out_spec, out_spec],
        out_shape=[out_shape, out_shape],
        compiler_params=_params("parallel"),
        name="mem_kv",
    )(mem, g, wkv, kn)


def _cross_attn_kernel(*refs, n_in):
    a_refs = refs[:n_in]
    w_ref, x_ref, g_ref, wq_ref, qn_ref, k_ref, v_ref, wo_ref, o_ref = refs[n_in:]
    x = x_ref[...]
    k0 = 0
    for a_ref in a_refs:
        x = x + _dot(a_ref[...], w_ref[k0:k0 + a_ref.shape[1], :])
        k0 += a_ref.shape[1]
    q = _dot(_rms(x, g_ref[...]).astype(BF16), wq_ref[...])
    scale = X_HEADDIM ** -0.5
    outs = []
    for h in range(q.shape[1] // X_HEADDIM):
        sl = slice(h * X_HEADDIM, (h + 1) * X_HEADDIM)
        qh = (_rms(q[:, sl], qn_ref[...]) * scale).astype(BF16)
        s = _dot_nt(qh, k_ref[:, sl])
        e = jnp.exp(s - jnp.max(s, axis=-1, keepdims=True))
        p = e / jnp.sum(e, axis=-1, keepdims=True)
        outs.append(_dot(p.astype(BF16), v_ref[:, sl]))
    o = jnp.concatenate(outs, axis=1).astype(BF16)
    o_ref[...] = x + _dot(o, wo_ref[...])


def _cross_attn(acts, w_out, x, g, wq, qn, k, v, wo, *, tm=1024):
    t, d = x.shape
    b, m, inner = k.shape
    nt = t // b // tm
    return pl.pallas_call(
        functools.partial(_cross_attn_kernel, n_in=len(acts)),
        grid=(b, nt),
        in_specs=[pl.BlockSpec((tm, a.shape[1]), lambda bi, i: (bi * nt + i, 0)) for a in acts] + [
            pl.BlockSpec(w_out.shape, lambda bi, i: (0, 0)),
            pl.BlockSpec((tm, d), lambda bi, i: (bi * nt + i, 0)),
            pl.BlockSpec((1, d), lambda bi, i: (0, 0)),
            pl.BlockSpec(wq.shape, lambda bi, i: (0, 0)),
            pl.BlockSpec((1, X_HEADDIM), lambda bi, i: (0, 0)),
            pl.BlockSpec((None, m, inner), lambda bi, i: (bi, 0, 0)),
            pl.BlockSpec((None, m, inner), lambda bi, i: (bi, 0, 0)),
            pl.BlockSpec(wo.shape, lambda bi, i: (0, 0)),
        ],
        out_specs=pl.BlockSpec((tm, d), lambda bi, i: (bi * nt + i, 0)),
        out_shape=jax.ShapeDtypeStruct((t, d), F32),
        compiler_params=_params("parallel", "parallel"),
        name="cross_attn",
    )(*acts, w_out, x, g, wq, qn, k, v, wo)


def _swiglu_kernel(x_ref, g_ref, wu_ref, wg_ref, w2_ref, o_ref, h_ref):
    @pl.when(pl.program_id(1) == 0)
    def _():
        h_ref[...] = _rms(x_ref[...], g_ref[...]).astype(BF16)
        o_ref[...] = x_ref[...]

    h = h_ref[...]
    tf = wu_ref.shape[1]
    half = tf // 2 // LANES * LANES
    acts = []
    for c in (slice(0, half), slice(half, tf)):
        acts.append((_silu(_dot(h, wg_ref[:, c])) * _dot(h, wu_ref[:, c])).astype(BF16))
    act = jnp.concatenate(acts, axis=1)
    for c0 in range(0, o_ref.shape[1], EXPERT_CHUNK):
        c = slice(c0, c0 + EXPERT_CHUNK)
        o_ref[:, c] += _dot(act, w2_ref[:, c])


def _swiglu(x, g, w13, w2, *, tm=1024, tf=1408):
    t, d = x.shape
    ff = w2.shape[0]
    nf = ff // tf
    return pl.pallas_call(
        _swiglu_kernel,
        grid=(t // tm, nf),
        in_specs=[
            pl.BlockSpec((tm, d), lambda i, f: (i, 0)),
            pl.BlockSpec((1, d), lambda i, f: (0, 0)),
            pl.BlockSpec((d, tf), lambda i, f: (0, f)),
            pl.BlockSpec((d, tf), lambda i, f: (0, nf + f)),
            pl.BlockSpec((tf, d), lambda i, f: (f, 0)),
        ],
        out_specs=pl.BlockSpec((tm, d), lambda i, f: (i, 0)),
        out_shape=jax.ShapeDtypeStruct((t, d), F32),
        scratch_shapes=[pltpu.VMEM((tm, d), BF16)],
        compiler_params=_params("parallel", "arbitrary"),
        name="swiglu",
    )(x, g, w13, w13, w2)


def _split3(x):
    a = x.astype(BF16)
    r = x - a.astype(F32)
    b = r.astype(BF16)
    c = (r - b.astype(F32)).astype(BF16)
    return a, b, c


def _ssd_kernel(z0_ref, z1_ref, x0_ref, x1_ref, bc_ref, dt_ref, cw_ref, cb_ref, dtb_ref, alog_ref, dskip_ref,
                ng_ref, o_ref, tail_ref, state_ref, *, q):
    rows = x0_ref.shape[0]
    d_inner = o_ref.shape[1]
    gn = SSM_GROUPS * SSM_STATE
    hpg = d_inner // SSM_HEADDIM // SSM_GROUPS

    @pl.when(pl.program_id(1) == 0)
    def _():
        tail_ref[...] = jnp.zeros_like(tail_ref)
        state_ref[...] = jnp.zeros_like(state_ref)

    raw_b = jnp.concatenate([x0_ref[...], x1_ref[...], bc_ref[...]], axis=1)
    raw = raw_b.astype(F32)
    xbc = _silu(_causal_conv(raw_b, raw, tail_ref[...], cw_ref, cb_ref))
    tail_ref[...] = raw[rows - 8:rows, :]
    dt_all = _softplus(dt_ref[...] + dtb_ref[...])
    a_all = dt_all * (-jnp.exp(alog_ref[...]))
    z_all = jnp.concatenate([z0_ref[...], z1_ref[...]], axis=1).astype(F32)
    ri = lax.broadcasted_iota(I32, (q, q), 0)
    ci = lax.broadcasted_iota(I32, (q, q), 1)
    causal = ci <= ri
    tri = jnp.where(causal, 1.0, 0.0).astype(BF16)
    lane = lax.broadcasted_iota(I32, (1, LANES), 1)
    lo = lane < SSM_HEADDIM
    for r0 in range(0, rows, q):
        _ssd_chunk(xbc[r0:r0 + q], dt_all[r0:r0 + q], a_all[r0:r0 + q], z_all[r0:r0 + q], causal, tri, lo, dskip_ref,
                   ng_ref, state_ref, o_ref.at[pl.ds(r0, q)], d_inner=d_inner, gn=gn, hpg=hpg)


def _ssd_chunk(xbc, dt, a, z, causal, tri, lo, dskip_ref, ng_ref, state_ref, o_ref, *, d_inner, gn, hpg):
    q = xbc.shape[0]
    xs = xbc[:, :d_inner]
    bm = xbc[:, d_inner:d_inner + gn]
    cm = xbc[:, d_inner + gn:]
    a_cs = sum(_dot(tri, part) for part in _split3(a))
    a_cs_t = a_cs.T
    dt_t = dt.T

    y_parts = []
    for g in range(SSM_GROUPS):
        cg = cm[:, g * SSM_STATE:(g + 1) * SSM_STATE].astype(BF16)
        bg = bm[:, g * SSM_STATE:(g + 1) * SSM_STATE]
        gmat = _dot_nt(cg, bg.astype(BF16))
        bg_t = bg.T
        gw = hpg * SSM_HEADDIM
        prev = state_ref[:, g * gw:(g + 1) * gw]
        y_off = _dot(cg, prev.astype(BF16))
        for pr in range(hpg // 2):
            c0 = g * gw + pr * LANES
            x_pair = xs[:, c0:c0 + LANES]
            y_pair = dskip_ref[:, c0:c0 + LANES] * x_pair
            st_pair = jnp.zeros((SSM_STATE, LANES), F32)
            decay_pair = jnp.zeros((1, LANES), F32)
            for half in range(2):
                h = g * hpg + pr * 2 + half
                sel = lo if half == 0 else jnp.logical_not(lo)
                xh = jnp.where(sel, x_pair, 0.0).astype(BF16)
                row_cs = a_cs_t[h:h + 1, :]
                col_cs = a_cs[:, h:h + 1]
                row_dt = dt_t[h:h + 1, :]
                a_last = a_cs_t[h:h + 1, q - 1:q]
                dec = jnp.exp(jnp.where(causal, col_cs - row_cs, NEG))
                y_pair = y_pair + _dot((gmat * dec * row_dt).astype(BF16), xh)
                w_row = jnp.exp(a_last - row_cs) * row_dt
                st_pair = st_pair + _dot((bg_t * w_row).astype(BF16), xh)
                y_pair = y_pair + jnp.where(sel, jnp.exp(col_cs) * y_off[:, pr * LANES:(pr + 1) * LANES], 0.0)
                decay_pair = jnp.where(sel, jnp.exp(a_last), decay_pair)
            state_ref[:, c0:c0 + LANES] = decay_pair * state_ref[:, c0:c0 + LANES] + st_pair
            y_parts.append(y_pair)
    y = jnp.concatenate(y_parts, axis=1) * _silu(z)
    gsz = d_inner // SSM_GROUPS
    for g in range(SSM_GROUPS):
        sl = slice(g * gsz, (g + 1) * gsz)
        o_ref[:, sl] = _rms(y[:, sl], ng_ref[:, sl]).astype(o_ref.dtype)


def _ssd(main, dt, cw, cb, dtb, alog, dskip, ng, *, batch, d_inner):
    t = main.shape[0]
    q = SSD_CHUNK * SSD_CHUNKS_PER_STEP
    nc = t // batch // q
    conv_ch = cw.shape[1]
    half = d_inner // 2
    col = lambda j: pl.BlockSpec((q, half), lambda b, i: (b * nc + i, j))
    vec = lambda n: pl.BlockSpec((1, n), lambda b, i: (0, 0))
    return pl.pallas_call(
        functools.partial(_ssd_kernel, q=SSD_CHUNK),
        grid=(batch, nc),
        in_specs=[
            col(0), col(1), col(2), col(3), col(4),
            pl.BlockSpec((q, LANES), lambda b, i: (b * nc + i, 0)),
            pl.BlockSpec((CONV_W, conv_ch), lambda b, i: (0, 0)),
            vec(conv_ch), vec(LANES), vec(LANES), vec(d_inner), vec(d_inner),
        ],
        out_specs=pl.BlockSpec((q, d_inner), lambda b, i: (b * nc + i, 0)),
        out_shape=jax.ShapeDtypeStruct((t, d_inner), BF16),
        scratch_shapes=[pltpu.VMEM((8, conv_ch), F32), pltpu.VMEM((SSM_STATE, d_inner), F32)],
        compiler_params=_params("parallel", "arbitrary"),
        name="ssd",
    )(main, main, main, main, main, dt, cw, cb, dtb, alog, dskip, ng)


def _router_kernel(x_ref, g_ref, wr_ref, h_ref, info_ref, cnt_ref, run_ref):
    tm = x_ref.shape[0]

    @pl.when(pl.program_id(0) == 0)
    def _():
        run_ref[...] = jnp.zeros_like(run_ref)

    h = _rms(x_ref[...], g_ref[...])
    h_ref[...] = h
    h_hi = h.astype(BF16)
    h_lo = (h - h_hi.astype(F32)).astype(BF16)
    w = wr_ref[...]
    w_hi = w.astype(BF16)
    w_lo = (w - w_hi.astype(F32)).astype(BF16)
    logits = _dot(h_hi, w_hi) + _dot(h_lo, w_hi) + _dot(h_hi, w_lo)
    lane = lax.broadcasted_iota(I32, (tm, LANES), 1)
    lg = jnp.where(lane < N_EXPERTS, logits, NEG)
    m1 = jnp.max(lg, axis=-1, keepdims=True)
    i1 = jnp.min(jnp.where(lg == m1, lane, LANES), axis=-1, keepdims=True)
    lg2 = jnp.where(lane == i1, NEG, lg)
    m2 = jnp.max(lg2, axis=-1, keepdims=True)
    i2 = jnp.min(jnp.where(lg2 == m2, lane, LANES), axis=-1, keepdims=True)
    e2 = jnp.exp(m2 - m1)
    w1 = 1.0 / (1.0 + e2)
    w2 = e2 / (1.0 + e2)

    hot1 = lane == i1
    hot2 = lane == i2
    hot = jnp.where(hot1 | hot2, 1.0, 0.0)
    ri = lax.broadcasted_iota(I32, (tm, tm), 0)
    ci = lax.broadcasted_iota(I32, (tm, tm), 1)
    before = jnp.where(ci < ri, 1.0, 0.0).astype(BF16)
    seen = run_ref[0:1, :] + _dot(before, hot.astype(BF16))
    rank1 = jnp.sum(jnp.where(hot1, seen, 0.0), axis=-1, keepdims=True)
    rank2 = jnp.sum(jnp.where(hot2, seen, 0.0), axis=-1, keepdims=True)
    run_ref[...] = run_ref[...] + jnp.sum(hot, axis=0, keepdims=True)
    cnt_ref[...] = run_ref[...]

    cols = [i1.astype(F32), i2.astype(F32), w1, w2, rank1, rank2]
    info = jnp.zeros((tm, LANES), F32)
    for c, v in enumerate(cols):
        info = jnp.where(lane == c, v, info)
    info_ref[...] = info


def _router(x, g, wr, *, tm=256):
    t, d = x.shape
    return pl.pallas_call(
        _router_kernel,
        grid=(t // tm,),
        in_specs=[
            pl.BlockSpec((tm, d), lambda i: (i, 0)),
            pl.BlockSpec((1, d), lambda i: (0, 0)),
            pl.BlockSpec(wr.shape, lambda i: (0, 0)),
        ],
        out_specs=[
            pl.BlockSpec((tm, d), lambda i: (i, 0)),
            pl.BlockSpec((tm, LANES), lambda i: (i, 0)),
            pl.BlockSpec((8, LANES), lambda i: (0, 0)),
        ],
        out_shape=[
            jax.ShapeDtypeStruct((t, d), F32),
            jax.ShapeDtypeStruct((t, LANES), F32),
            jax.ShapeDtypeStruct((8, LANES), F32),
        ],
        scratch_shapes=[pltpu.VMEM((8, LANES), F32)],
        compiler_params=_params("arbitrary"),
        name="moe_router",
    )(x, g, wr)


def _row_copy(src_ref, src_row, dst_ref, dst_row, sem):
    return pltpu.make_async_copy(src_ref.at[pl.ds(src_row, 1)], dst_ref.at[pl.ds(dst_row, 1)], sem)


def _dispatch_kernel(pos_ref, h_ref, init_ref, xs_ref, sem):
    del init_ref
    tt = h_ref.shape[0]

    def issue(r, c):
        _row_copy(h_ref, r, xs_ref, pos_ref[0, r], sem).start()
        _row_copy(h_ref, r, xs_ref, pos_ref[1, r], sem).start(priority=1)
        return c

    lax.fori_loop(0, tt, issue, 0, unroll=DMA_UNROLL)
    for _ in range(2):
        pltpu.make_async_copy(h_ref, xs_ref.at[pl.ds(0, tt)], sem).wait()


def _dispatch(h, pos, rows, *, tt=256):
    t, d = h.shape
    init = jnp.zeros((rows, d), h.dtype)
    return pl.pallas_call(
        _dispatch_kernel,
        grid=(t // tt,),
        in_specs=[
            pl.BlockSpec((None, 2, tt), lambda i: (i, 0, 0), memory_space=pltpu.SMEM),
            pl.BlockSpec((tt, d), lambda i: (i, 0)),
            pl.BlockSpec(memory_space=pl.ANY),
        ],
        out_specs=pl.BlockSpec(memory_space=pl.ANY),
        out_shape=jax.ShapeDtypeStruct((rows, d), h.dtype),
        scratch_shapes=[pltpu.SemaphoreType.DMA(())],
        input_output_aliases={2: 0},
        compiler_params=_params("arbitrary"),
        name="moe_dispatch",
    )(pos, h, init)


def _expert_kernel(te_ref, tv_ref, xs_ref, wu_ref, wg_ref, w2_ref, o_ref, xb_ref):
    del te_ref
    i = pl.program_id(0)
    f = pl.program_id(1)
    live = tv_ref[i] > 0

    @pl.when(f == 0)
    def _():
        o_ref[...] = jnp.zeros_like(o_ref)
        xb_ref[...] = xs_ref[...].astype(BF16)

    @pl.when(live)
    def _():
        x = xb_ref[...]
        act = (_silu(_dot(x, wg_ref[...].astype(BF16))) * _dot(x, wu_ref[...].astype(BF16))).astype(BF16)
        for c0 in range(0, o_ref.shape[1], EXPERT_CHUNK):
            c = slice(c0, c0 + EXPERT_CHUNK)
            o_ref[:, c] += _dot(act, w2_ref[:, c].astype(BF16))


def _experts(xs, w13, w2, tile_expert, tile_live, *, tm, tf=512):
    rows = xs.shape[0]
    ff, d = w2.shape[1:]
    nf = ff // tf

    def f_of(i, f, te, tv):
        return jnp.where(tv[i] > 0, f, nf - 1)

    grid_spec = pltpu.PrefetchScalarGridSpec(
        num_scalar_prefetch=2,
        grid=(rows // tm, nf),
        in_specs=[
            pl.BlockSpec((tm, xs.shape[1]), lambda i, f, te, tv: (i, 0)),
            pl.BlockSpec((None, d, tf), lambda i, f, te, tv: (te[i], 0, f_of(i, f, te, tv))),
            pl.BlockSpec((None, d, tf), lambda i, f, te, tv: (te[i], 0, nf + f_of(i, f, te, tv))),
            pl.BlockSpec((None, tf, d), lambda i, f, te, tv: (te[i], f_of(i, f, te, tv), 0)),
        ],
        out_specs=pl.BlockSpec((tm, d), lambda i, f, te, tv: (i, 0)),
        scratch_shapes=[pltpu.VMEM((tm, d), BF16)],
    )
    return pl.pallas_call(
        _expert_kernel,
        grid_spec=grid_spec,
        out_shape=jax.ShapeDtypeStruct((rows, d), F32),
        compiler_params=_params("parallel", "arbitrary"),
        name="moe_experts",
    )(tile_expert, tile_live, xs, w13, w13, w2)


def _combine_kernel(pos_ref, x_ref, info_ref, ys_ref, o_ref, buf_ref, sem):
    tt = x_ref.shape[0]

    def issue(r, c):
        _row_copy(ys_ref, pos_ref[0, r], buf_ref.at[0], r, sem).start()
        _row_copy(ys_ref, pos_ref[1, r], buf_ref.at[1], r, sem).start(priority=1)
        return c

    lax.fori_loop(0, tt, issue, 0, unroll=DMA_UNROLL)
    for k in range(2):
        pltpu.make_async_copy(ys_ref.at[pl.ds(0, tt)], buf_ref.at[k], sem).wait()
    info = info_ref[...]
    o_ref[...] = x_ref[...] + info[:, 2:3] * buf_ref[0] + info[:, 3:4] * buf_ref[1]


def _combine(x, info, pos, ys, *, tt=256):
    t, d = x.shape
    return pl.pallas_call(
        _combine_kernel,
        grid=(t // tt,),
        in_specs=[
            pl.BlockSpec((None, 2, tt), lambda i: (i, 0, 0), memory_space=pltpu.SMEM),
            pl.BlockSpec((tt, d), lambda i: (i, 0)),
            pl.BlockSpec((tt, LANES), lambda i: (i, 0)),
            pl.BlockSpec(memory_space=pl.ANY),
        ],
        out_specs=pl.BlockSpec((tt, d), lambda i: (i, 0)),
        out_shape=jax.ShapeDtypeStruct((t, d), F32),
        scratch_shapes=[pltpu.VMEM((2, tt, d), F32), pltpu.SemaphoreType.DMA(())],
        compiler_params=_params("arbitrary"),
        name="moe_combine",
    )(pos, x, info, ys)


def _moe(x, g, router, w13, w2, *, tm=1024, tt=1024):
    t, d = x.shape
    n_exp = router.shape[1]
    wr = jnp.pad(router, ((0, 0), (0, LANES - n_exp)))
    h, info, counts = _router(x, g, wr, tm=tt)

    counts = counts[0, :n_exp].astype(I32)
    seg = (counts + tm - 1) // tm * tm
    seg_end = jnp.cumsum(seg)
    seg_start = seg_end - seg
    e1 = info[:, 0].astype(I32)
    e2 = info[:, 1].astype(I32)
    pos = jnp.stack([seg_start[e1] + info[:, 4].astype(I32), seg_start[e2] + info[:, 5].astype(I32)], axis=0)
    pos = pos.reshape(2, t // tt, tt).transpose(1, 0, 2)
    rows = 2 * t + n_exp * tm
    tile_row0 = jnp.arange(rows // tm, dtype=I32) * tm
    tile_live = (tile_row0 < seg_end[-1]).astype(I32)
    tile_expert = jnp.sum((seg_end[None, :] <= tile_row0[:, None]).astype(I32), axis=1)
    tile_expert = jnp.minimum(tile_expert, n_exp - 1)
    last_live = jnp.maximum(jnp.sum(tile_live) - 1, 0)
    tile_expert = jnp.where(tile_live > 0, tile_expert, tile_expert[last_live])

    xs = _dispatch(h, pos, rows, tt=tt)
    ys = _experts(xs, w13, w2, tile_expert, tile_live, tm=tm)
    return _combine(x, info, pos, ys, tt=tt)


def _row(v, n=None):
    v = v.reshape(1, -1).astype(F32)
    if n is not None and v.shape[1] < n:
        v = jnp.pad(v, ((0, 0), (0, n - v.shape[1])))
    return v


def _overlap_matrix(seq):
    n = seq // CMP_STRIDE
    cmp_start = np.arange(n) * CMP_STRIDE
    slc_start = np.arange(LANES) * SLC_LEN
    ov = (cmp_start[:, None] <= slc_start[None, :] + SLC_LEN - 1) & (cmp_start[:, None] + CMP_LEN - 1 >= slc_start[None, :])
    ov[n - 1] = False
    return jnp.asarray(ov, dtype=BF16)


def _even_layer(x, batch, norm_mix, w_in, conv_w, conv_b, wa, ba, wx, bx, lam, gate_b, q_norm, k_norm, cmp_pos,
                ck_w1, ck_w2, cv_w1, cv_w2):
    t, d = x.shape
    seq = t // batch
    rg = wa.shape[0] * wa.shape[1]
    gdk = NSA_GROUPS * NSA_DK
    n_main = 2 * rg + NSA_HEADS * NSA_DK
    n_kv = 6 * gdk
    per_group = 3 * NSA_REP
    gate_cols = w_in[:, n_main + n_kv:].reshape(d, NSA_GROUPS, per_group)
    gate_cols = jnp.pad(gate_cols, ((0, 0), (0, 0), (0, LANES - per_group))).reshape(d, NSA_GROUPS * LANES)
    gate_bias = jnp.pad(gate_b.reshape(NSA_GROUPS, per_group), ((0, 0), (0, LANES - per_group))).reshape(1, -1)
    w_all = jnp.concatenate([w_in[:, :n_main + n_kv], gate_cols], axis=1).astype(BF16)
    kn = jnp.pad(k_norm, ((0, 8 - k_norm.shape[0]), (0, 0)))
    main, planes, kp, vp, kwn, vwp, gates = _norm_proj(
        x, _row(norm_mix), w_all, gate_bias, kn, batch=batch, n_main=n_main, nsa_keys=True,
        n_extra=NSA_GROUPS * LANES, extra_sigmoid=True)

    rg_out = _rglru(main, conv_w, _row(conv_b), wa.astype(BF16), _row(ba), wx.astype(BF16), _row(bx), _row(lam),
                    batch=batch)

    k_gain = jnp.max(jnp.abs(k_norm), axis=1)[jnp.array([1, 0, 2])]
    shift = ((1.02 * LOG2E * math.sqrt(NSA_DK)) * jnp.max(jnp.abs(q_norm)) * k_gain).astype(F32)
    pos_flat = jnp.broadcast_to(cmp_pos.reshape(1, -1), (8, CMP_LEN * NSA_DK)).astype(BF16)
    kcmp, vcmp = _nsa_compress(planes, pos_flat, ck_w1.astype(BF16), ck_w2.astype(BF16), cv_w1.astype(BF16),
                               cv_w2.astype(BF16), kn, _overlap_matrix(seq))
    att = _nsa_attn(shift, main, kcmp, vcmp, kwn, vwp, kp, vp, gates, _row(q_norm))
    return [rg_out, att]


def _odd_layer(x, batch, norm_mix, w_in, conv_w, conv_b, dt_bias, a_log, d_skip, norm_g, d_inner):
    conv_ch = conv_w.shape[1]
    n_main = d_inner + conv_ch
    heads = dt_bias.shape[0]
    w_all = jnp.concatenate([w_in[:, :n_main], jnp.pad(w_in[:, n_main:], ((0, 0), (0, LANES - heads)))], axis=1)
    main, dt = _norm_proj(x, _row(norm_mix), w_all.astype(BF16), jnp.zeros((1, LANES), F32),
                          jnp.zeros((8, LANES), F32), batch=batch, n_main=n_main, nsa_keys=False, n_extra=LANES,
                          extra_sigmoid=False)
    y = _ssd(main, dt, conv_w, _row(conv_b), _row(dt_bias, LANES), _row(a_log, LANES),
             _row(jnp.repeat(d_skip, SSM_HEADDIM)), _row(norm_g), batch=batch, d_inner=d_inner)
    return [y]


def kernel(x, mem, norm_mix, norm_cross, norm_mem, norm_ffn, ev_w_in, ev_rg_conv_w, ev_rg_conv_b, ev_rg_wa, ev_rg_ba, ev_rg_wx, ev_rg_bx, ev_rg_lambda, ev_nsa_gate_b, ev_q_norm, ev_k_norm, ev_cmp_pos, ev_cmp_k_w1, ev_cmp_k_w2, ev_cmp_v_w1, ev_cmp_v_w2, ev_w_out, od_w_in, od_conv_w, od_conv_b, od_dt_bias, od_a_log, od_d_skip, od_norm, od_w_out, x_wq, x_wkv, x_q_norm, x_k_norm, x_wo, ff_w13, ff_w2, moe_router, moe_w13, moe_w2):
    batch, seq, d = x.shape
    depth = norm_mix.shape[0]
    xf = x.reshape(batch * seq, d)
    for layer in range(depth):
        i = layer // 2
        if layer % 2 == 0:
            w_out = ev_w_out[i]
            mixed = _even_layer(xf, batch, norm_mix[layer], ev_w_in[i], ev_rg_conv_w[i], ev_rg_conv_b[i], ev_rg_wa[i],
                                ev_rg_ba[i], ev_rg_wx[i], ev_rg_bx[i], ev_rg_lambda[i], ev_nsa_gate_b[i],
                                ev_q_norm[i], ev_k_norm[i], ev_cmp_pos[i], ev_cmp_k_w1[i], ev_cmp_k_w2[i],
                                ev_cmp_v_w1[i], ev_cmp_v_w2[i])
        else:
            w_out = od_w_out[i]
            mixed = _odd_layer(xf, batch, norm_mix[layer], od_w_in[i], od_conv_w[i], od_conv_b[i], od_dt_bias[i],
                               od_a_log[i], od_d_skip[i], od_norm[i], w_out.shape[0])
        k, v = _mem_kv(mem, _row(norm_mem[layer]), x_wkv[layer].astype(BF16), _row(x_k_norm[layer]))
        xf = _cross_attn(mixed, w_out.astype(BF16), xf, _row(norm_cross[layer]), x_wq[layer].astype(BF16),
                         _row(x_q_norm[layer]), k, v, x_wo[layer].astype(BF16))
        if layer % 2 == 0:
            xf = _swiglu(xf, _row(norm_ffn[layer]), ff_w13[i].astype(BF16), ff_w2[i].astype(BF16))
        else:
            xf = _moe(xf, _row(norm_ffn[layer]), moe_router[i], moe_w13[i], moe_w2[i])
    return xf.reshape(batch, seq, d)
```

```python
import functools
import math

import jax
import jax.numpy as jnp
import numpy as np
from jax import lax
from jax.experimental import pallas as pl
from jax.experimental.pallas import tpu as pltpu

F32 = jnp.float32
BF16 = jnp.bfloat16
I32 = jnp.int32

EPS = 1e-6
CONV_W = 4
RG_BLOCKS = 8
RG_C = 8.0
NSA_HEADS = 8
NSA_GROUPS = 2
NSA_REP = NSA_HEADS // NSA_GROUPS
NSA_DK = 128
CMP_LEN = 32
CMP_STRIDE = 16
SLC_LEN = 64
SLC_SHIFT = 6
SLC_WIDE = 4
N_FORCED = 3
SLC_TOPN = 16
WINDOW = 512
SSM_HEADDIM = 64
SSM_GROUPS = 4
SSM_STATE = 128
SSD_CHUNK = 128
SSD_CHUNKS_PER_STEP = 2
X_HEADS = 4
X_HEADDIM = 128
N_EXPERTS = 8
EXPERT_CHUNK = 256
DMA_UNROLL = 8

LANES = 128
SUBLANES = 8
VMEM_LIMIT_BYTES = 56 * 1024 * 1024
NEG = -1e30
SEL_BIAS = float(2 ** 20)
LOG2E = math.log2(math.e)
FIXED_SHIFT_MAX = 56.0

NT_DIMS = (((1,), (1,)), ((), ()))


def _params(*sem):
    return pltpu.CompilerParams(dimension_semantics=sem, vmem_limit_bytes=VMEM_LIMIT_BYTES)


def _dot(a, b):
    return jnp.dot(a, b, preferred_element_type=F32)


def _dot_nt(a, b):
    return lax.dot_general(a, b, NT_DIMS, preferred_element_type=F32)


def _rms(x, g):
    return x * lax.rsqrt(jnp.mean(x * x, axis=-1, keepdims=True) + EPS) * g


def _sigmoid(x):
    return 1.0 / (1.0 + jnp.exp(-x))


def _silu(x):
    return x * _sigmoid(x)


def _gelu_tanh(x):
    c = math.sqrt(2.0 / math.pi)
    return 0.5 * x * (1.0 + jnp.tanh(c * (x + 0.044715 * (x * x * x))))


def _softplus(x):
    return jnp.maximum(x, 0.0) + jnp.log(1.0 + jnp.exp(-jnp.abs(x)))


def _causal_conv(xb, x, tail, w_ref, b_ref):
    n = xb.shape[0]
    delay = lax.broadcasted_iota(I32, (n, n), 0) - lax.broadcasted_iota(I32, (n, n), 1)
    r8 = lax.broadcasted_iota(I32, (SUBLANES, 1), 0)
    y = b_ref[...] + w_ref[CONV_W - 1:CONV_W, :] * x
    head = jnp.zeros(tail.shape, F32)
    for k in range(1, CONV_W):
        wk = w_ref[CONV_W - 1 - k:CONV_W - k, :]
        y = y + wk * _dot(jnp.where(delay == k, 1.0, 0.0).astype(BF16), xb)
        head = head + wk * jnp.where(r8 < k, pltpu.roll(tail, k, 0), 0.0)
    return jnp.concatenate([y[0:SUBLANES] + head, y[SUBLANES:]], axis=0)


def _norm_proj_kernel(x_ref, g_ref, w_ref, eb_ref, kn_ref, *out_refs, n_main, nsa_keys, tiles_per_seq,
                      extra_sigmoid):
    h = _rms(x_ref[...], g_ref[...]).astype(BF16)
    main_ref = out_refs[0]
    for c0 in range(0, n_main, 512):
        main_ref[:, c0:c0 + 512] = _dot(h, w_ref[:, c0:c0 + 512]).astype(main_ref.dtype)
    col = n_main
    ex_ref = out_refs[-1]
    if nsa_keys:
        cmp_ref, kp_ref, vp_ref, kwn_ref, vwp_ref = out_refs[1:6]
        tm = x_ref.shape[0]
        g = NSA_GROUPS
        gsl = lambda r, p: r[:, p * LANES:(p + 1) * LANES]
        r = _dot(h, w_ref[:, col:col + 2 * g * LANES])
        for p in range(2 * g):
            cmp_ref[p] = gsl(r, p).astype(BF16)
        col += 2 * g * LANES
        t0 = (pl.program_id(0) % tiles_per_seq) * tm
        blk = jnp.right_shift(t0 + lax.broadcasted_iota(I32, (tm, LANES), 0), SLC_SHIFT)
        onehot = jnp.where(blk == lax.broadcasted_iota(I32, (tm, LANES), 1), 1.0, 0.0).astype(BF16)
        ones = jnp.ones((tm, LANES), BF16)
        r = _dot(h, w_ref[:, col:col + 2 * g * LANES])
        for gi in range(g):
            kp_ref[gi] = jnp.concatenate([_rms(gsl(r, gi), kn_ref[1:2, :]).astype(BF16), onehot], axis=1)
            vp_ref[gi] = jnp.concatenate([gsl(r, g + gi).astype(BF16), ones], axis=1)
        col += 2 * g * LANES
        r = _dot(h, w_ref[:, col:col + 2 * g * LANES])
        for gi in range(g):
            kwn_ref[gi] = _rms(gsl(r, gi), kn_ref[2:3, :]).astype(BF16)
            vwp_ref[gi] = jnp.concatenate([gsl(r, g + gi).astype(BF16), ones], axis=1)
        col += 2 * g * LANES
    n_extra = ex_ref.shape[1]
    e = _dot(h, w_ref[:, col:col + n_extra]) + eb_ref[...]
    ex_ref[...] = _sigmoid(e) if extra_sigmoid else e


def _norm_proj(x, g, w, eb, kn, *, batch, n_main, nsa_keys, n_extra, extra_sigmoid, tm=512):
    t, d = x.shape
    seq = t // batch
    nt = seq // tm
    out_shape = [jax.ShapeDtypeStruct((t, n_main), BF16)]
    out_specs = [pl.BlockSpec((tm, n_main), lambda i: (i, 0))]
    if nsa_keys:
        for planes, width in ((2 * NSA_GROUPS, LANES), (NSA_GROUPS, 2 * LANES), (NSA_GROUPS, 2 * LANES),
                              (NSA_GROUPS, LANES), (NSA_GROUPS, 2 * LANES)):
            out_shape.append(jax.ShapeDtypeStruct((batch, planes, seq, width), BF16))
            out_specs.append(pl.BlockSpec((None, planes, tm, width), lambda i: (i // nt, 0, i % nt, 0)))
    out_shape.append(jax.ShapeDtypeStruct((t, n_extra), F32))
    out_specs.append(pl.BlockSpec((tm, n_extra), lambda i: (i, 0)))
    kern = functools.partial(_norm_proj_kernel, n_main=n_main, nsa_keys=nsa_keys, tiles_per_seq=nt,
                             extra_sigmoid=extra_sigmoid)
    return pl.pallas_call(
        kern,
        grid=(t // tm,),
        in_specs=[
            pl.BlockSpec((tm, d), lambda i: (i, 0)),
            pl.BlockSpec((1, d), lambda i: (0, 0)),
            pl.BlockSpec(w.shape, lambda i: (0, 0)),
            pl.BlockSpec((1, n_extra), lambda i: (0, 0)),
            pl.BlockSpec(kn.shape, lambda i: (0, 0)),
        ],
        out_specs=out_specs,
        out_shape=out_shape,
        compiler_params=_params("parallel"),
        name="norm_proj",
    )(x, g, w, eb, kn)


def _rglru_kernel(rx_ref, rg_ref, cw_ref, cb_ref, wa_ref, ba_ref, wx_ref, bx_ref, lam_ref, o_ref, tail_ref, h_ref):
    tc, c = rx_ref.shape

    @pl.when(pl.program_id(1) == 0)
    def _():
        tail_ref[...] = jnp.zeros_like(tail_ref)
        h_ref[...] = jnp.zeros_like(h_ref)

    xb = rx_ref[...]
    x = xb.astype(F32)
    xc = _causal_conv(xb, x, tail_ref[...], cw_ref, cb_ref)
    tail_ref[...] = x[tc - 8:tc, :]

    bw = c // RG_BLOCKS
    ra, rx = [], []
    for blk in range(RG_BLOCKS):
        xb = xc[:, blk * bw:(blk + 1) * bw].astype(BF16)
        ra.append(_dot(xb, wa_ref[blk]))
        rx.append(_dot(xb, wx_ref[blk]))
    r = _sigmoid(jnp.concatenate(ra, axis=1) + ba_ref[...])
    ig = _sigmoid(jnp.concatenate(rx, axis=1) + bx_ref[...])
    log_a = (-RG_C) * r * _softplus(-lam_ref[...])
    a = jnp.exp(log_a)
    z = 1.0 - a * a
    u = jnp.where(z > 0.0, z * lax.rsqrt(z), 0.0) * (ig * xc)

    in_group = lax.broadcasted_iota(I32, (tc, 1), 0) & (SUBLANES - 1)
    d = 1
    while d < SUBLANES:
        keep = in_group >= d
        a_sh = jnp.where(keep, pltpu.roll(a, d, 0), 1.0)
        u_sh = jnp.where(keep, pltpu.roll(u, d, 0), 0.0)
        u = a * u_sh + u
        a = a * a_sh
        d *= 2
    carry = h_ref[SUBLANES - 1:SUBLANES, :]
    groups = []
    for g0 in range(0, tc, SUBLANES):
        hg = u[g0:g0 + SUBLANES, :] + a[g0:g0 + SUBLANES, :] * carry
        carry = hg[SUBLANES - 1:SUBLANES, :]
        groups.append(hg)
    h_ref[...] = groups[-1]
    o_ref[...] = (_gelu_tanh(rg_ref[...].astype(F32)) * jnp.concatenate(groups, axis=0)).astype(o_ref.dtype)


def _rglru(main, cw, cb, wa, ba, wx, bx, lam, *, batch, tc=256):
    t = main.shape[0]
    c = cw.shape[1]
    nt = t // batch // tc
    vec = pl.BlockSpec((1, c), lambda b, i: (0, 0))
    blk = pl.BlockSpec(wa.shape, lambda b, i: (0, 0, 0))
    return pl.pallas_call(
        _rglru_kernel,
        grid=(batch, nt),
        in_specs=[
            pl.BlockSpec((tc, c), lambda b, i: (b * nt + i, 0)),
            pl.BlockSpec((tc, c), lambda b, i: (b * nt + i, 1)),
            pl.BlockSpec((CONV_W, c), lambda b, i: (0, 0)),
            vec, blk, vec, blk, vec, vec,
        ],
        out_specs=pl.BlockSpec((tc, c), lambda b, i: (b * nt + i, 0)),
        out_shape=jax.ShapeDtypeStruct((t, c), BF16),
        scratch_shapes=[pltpu.VMEM((8, c), F32), pltpu.VMEM((8, c), F32)],
        compiler_params=_params("parallel", "arbitrary"),
        name="rglru",
    )(main, main, cw, cb, wa, ba, wx, bx, lam)


def _nsa_compress_kernel(xk_ref, xv_ref, pos_ref, kw1_ref, kw2_ref, vw1_ref, vw2_ref, kn_ref, ov_ref, kc_ref, vc_ref):
    n, half = xk_ref.shape
    last = lax.broadcasted_iota(I32, (n, 1), 0) == n - 1
    pos = pos_ref[...]

    def compress(x_ref, w1_ref, w2_ref):
        x = x_ref[...]
        y0 = _dot(x, w1_ref[0:half, :])
        y1 = _dot(x, w1_ref[half:2 * half, :])
        y1_next = jnp.where(last, 0.0, pltpu.roll(y1, n - 1, 0))
        const = _dot(pos, w1_ref[...])[0:1, :]
        hid = _gelu_tanh(y0 + y1_next + const)
        return _dot(hid.astype(BF16), w2_ref[...])

    kc_ref[...] = _rms(compress(xk_ref, kw1_ref, kw2_ref), kn_ref[0:1, :]).astype(BF16)
    vc = compress(xv_ref, vw1_ref, vw2_ref).astype(BF16)
    vc_ref[...] = jnp.concatenate([vc, jnp.ones((n, LANES), BF16), ov_ref[...]], axis=1)


def _nsa_compress(planes, pos_flat, kw1, kw2, vw1, vw2, k_norm, overlap):
    b, _, seq, _ = planes.shape
    g = NSA_GROUPS
    n = seq // CMP_STRIDE
    half = CMP_STRIDE * LANES
    grouped = planes.reshape(b, 2 * g, n, half)
    full = lambda a: pl.BlockSpec(a.shape, lambda bi, gi: (0,) * a.ndim)
    out_spec = lambda w: pl.BlockSpec((None, None, n, w), lambda bi, gi: (bi, gi, 0, 0))
    out_shape = lambda w: jax.ShapeDtypeStruct((b, g, n, w), BF16)
    return pl.pallas_call(
        _nsa_compress_kernel,
        grid=(b, g),
        in_specs=[
            pl.BlockSpec((None, None, n, half), lambda bi, gi: (bi, gi, 0, 0)),
            pl.BlockSpec((None, None, n, half), lambda bi, gi: (bi, 2 + gi, 0, 0)),
            full(pos_flat), full(kw1), full(kw2), full(vw1), full(vw2), full(k_norm), full(overlap),
        ],
        out_specs=[out_spec(LANES), out_spec(3 * LANES)],
        out_shape=[out_shape(LANES), out_shape(3 * LANES)],
        compiler_params=_params("parallel", "parallel"),
        name="nsa_compress",
    )(grouped, grouped, pos_flat, kw1, kw2, vw1, vw2, k_norm, overlap)


def _nsa_cw_kernel(shift_ref, *refs, qn, n_sel):
    fixed = jnp.maximum(shift_ref[1], shift_ref[2]) <= FIXED_SHIFT_MAX
    pl.when(fixed)(functools.partial(_nsa_cw_body, shift_ref, *refs, qn=qn, n_sel=n_sel, fixed=True))
    pl.when(jnp.logical_not(fixed))(functools.partial(_nsa_cw_body, shift_ref, *refs, qn=qn, n_sel=n_sel, fixed=False))


def _nsa_cw_body(shift_ref, q_ref, kc_ref, vc_ref, kw_ref, vw_ref, gt_ref, qn_ref, wb_ref, qp_ref, o_ref, *, qn, n_sel,
                 fixed):
    t0 = pl.program_id(2) * qn
    rep = NSA_REP
    rows = rep * qn
    scale = NSA_DK ** -0.5 * LOG2E
    qf = q_ref[...].astype(F32)
    heads = []
    for r in range(rep):
        qh = _rms(qf[:, r * LANES:(r + 1) * LANES], qn_ref[...]) * scale
        heads.append(qh.astype(BF16))
    qs = jnp.concatenate(heads, axis=0)
    trow = t0 + (lax.broadcasted_iota(I32, (rows, 1), 0) & (qn - 1))

    def attend(sm, vx, shift):
        if not fixed:
            sm = sm - jnp.maximum(jnp.max(sm, axis=-1, keepdims=True), -2.0 * shift)
        return _dot(jnp.exp2(sm).astype(BF16), vx)

    n_cmp = kc_ref.shape[0]
    visible = lax.broadcasted_iota(I32, (1, n_cmp), 1) * CMP_STRIDE + (CMP_LEN - 1) <= trow
    r_cmp = attend(jnp.where(visible, _dot_nt(qs, kc_ref[...]), NEG) - shift_ref[1], vc_ref[...], shift_ref[1])
    inv = 1.0 / jnp.maximum(r_cmp[:, LANES:2 * LANES], 1e-30)
    o_cmp = r_cmp[:, 0:LANES] * inv
    imp_h = r_cmp[:, 2 * LANES:3 * LANES] * inv
    imp = imp_h[0:qn]
    for r in range(1, rep):
        imp = imp + imp_h[r * qn:(r + 1) * qn]

    span = WINDOW + qn
    start = pl.multiple_of(jnp.maximum(t0 - WINDOW, 0), qn)
    s = _dot_nt(qs, kw_ref[pl.ds(start, span), :]).reshape(rep, qn, span) + wb_ref[...]
    r_win = attend(s.reshape(rows, span), vw_ref[pl.ds(start, span), :], shift_ref[2])
    o_win = r_win[:, 0:LANES] / jnp.maximum(r_win[:, LANES:2 * LANES], 1e-30)

    imp_t = imp.T
    jj = lax.broadcasted_iota(I32, imp_t.shape, 0).astype(F32)
    cur = jnp.right_shift(t0 + lax.broadcasted_iota(I32, imp_t.shape, 1), SLC_SHIFT).astype(F32)
    forced = (jj == 0.0) | (jj == cur) | (jj == cur - 1.0)
    shift = -shift_ref[0]
    taken = forced & (jj <= cur)
    work = jnp.where(taken, -2.0, jnp.where(jj <= cur, imp_t, -1.0))
    bias_t = jnp.where(taken, shift, -SEL_BIAS)
    for _ in range(n_sel - N_FORCED):
        m = jnp.max(work, axis=0, keepdims=True)
        idx = jnp.min(jnp.where(work == m, jj, float(LANES)), axis=0, keepdims=True)
        pick = jj == idx
        bias_t = jnp.where(pick, shift, bias_t)
        work = jnp.where(pick, -2.0, work)
    bias = bias_t.T.astype(BF16)
    for r in range(rep):
        qp_ref[r] = jnp.concatenate([heads[r], bias], axis=1)

    gt = gt_ref[...]
    for r in range(rep):
        sl = slice(r * qn, (r + 1) * qn)
        o = gt[:, 3 * r:3 * r + 1] * o_cmp[sl] + gt[:, 3 * r + 2:3 * r + 3] * o_win[sl]
        o_ref[:, r * LANES:(r + 1) * LANES] = o.astype(o_ref.dtype)


def _nsa_attn(shift, main, kcmp, vcmp, kwn, vwp, kp, vp, gates, q_norm, *, qn=256, tk=512):
    b, g, seq, _ = kwn.shape
    t = main.shape[0]
    nq = seq // qn
    rep = NSA_REP
    gw = rep * LANES
    q_blk0 = (main.shape[1] - NSA_HEADS * NSA_DK) // gw
    n_cmp = kcmp.shape[2]
    n_case = WINDOW // qn + 1
    span = WINDOW + qn
    in_window = []
    for case in range(n_case):
        t0 = case * qn
        diff = (t0 + np.arange(qn)[:, None]) - (max(t0 - WINDOW, 0) + np.arange(span)[None, :])
        in_window.append((diff >= 0) & (diff < WINDOW))
    window_bias = jnp.where(jnp.asarray(np.stack(in_window)), -shift[2], NEG).astype(F32)
    tk = min(tk, seq)
    kern = functools.partial(_nsa_attn_kernel, qn=qn, n_sel=min(SLC_TOPN, seq // SLC_LEN), tk=tk)
    per_group = lambda rows, width: pl.BlockSpec((None, None, rows, width), lambda bi, gi, i: (bi, gi, 0, 0))
    return pl.pallas_call(
        kern,
        grid=(b, g, nq),
        in_specs=[
            pl.BlockSpec(memory_space=pltpu.SMEM),
            pl.BlockSpec((qn, gw), lambda bi, gi, i: (bi * nq + i, q_blk0 + gi)),
            per_group(n_cmp, LANES), per_group(n_cmp, 3 * LANES), per_group(seq, LANES), per_group(seq, 2 * LANES),
            pl.BlockSpec((qn, LANES), lambda bi, gi, i: (bi * nq + i, gi)),
            pl.BlockSpec((1, LANES), lambda bi, gi, i: (0, 0)),
            pl.BlockSpec((None, qn, span), lambda bi, gi, i: (jnp.minimum(i, n_case - 1), 0, 0)),
            per_group(seq, 2 * LANES), per_group(seq, 2 * LANES),
        ],
        out_specs=pl.BlockSpec((qn, gw), lambda bi, gi, i: (bi * nq + i, gi)),
        out_shape=jax.ShapeDtypeStruct((t, NSA_HEADS * NSA_DK), BF16),
        scratch_shapes=[
            pltpu.VMEM((rep, qn, 2 * LANES), BF16),
            pltpu.VMEM((qn, gw), BF16),
            pltpu.VMEM((rep * qn, 1), F32),
            pltpu.VMEM((rep * qn, 2 * LANES), F32),
        ],
        compiler_params=_params("parallel", "parallel", "parallel"),
        name="nsa_attention",
    )(shift, main, kcmp, vcmp, kwn, vwp, gates, q_norm, window_bias, kp, vp)


def _nsa_attn_kernel(shift_ref, q_ref, kc_ref, vc_ref, kw_ref, vw_ref, gt_ref, qn_ref, wb_ref, kp_ref, vp_ref, o_ref,
                     qp_ref, ocw_ref, m_ref, acc_ref, *, qn, n_sel, tk):
    _nsa_cw_kernel(shift_ref, q_ref, kc_ref, vc_ref, kw_ref, vw_ref, gt_ref, qn_ref, wb_ref, qp_ref, ocw_ref, qn=qn,
                   n_sel=n_sel)
    _nsa_slc_kernel(shift_ref, qp_ref, kp_ref, vp_ref, gt_ref, ocw_ref, o_ref, m_ref, acc_ref, qn=qn, tk=tk)


def _nsa_slc_kernel(shift_ref, qp_ref, kp_ref, vp_ref, gt_ref, ocw_ref, o_ref, m_ref, acc_ref, *, qn, tk):
    t0 = pl.program_id(2) * qn
    rep = NSA_REP
    rows = rep * qn
    last = (t0 + qn - 1) // tk
    acc_ref[...] = jnp.zeros_like(acc_ref)

    def scores(k0, width, causal):
        qp = qp_ref[...].reshape(rows, qp_ref.shape[2])
        s = _dot_nt(qp, kp_ref[pl.ds(k0, width), :])
        if causal:
            trow = t0 + (lax.broadcasted_iota(I32, (rows, 1), 0) & (qn - 1))
            s = jnp.where(k0 + lax.broadcasted_iota(I32, (1, width), 1) <= trow, s, -SEL_BIAS)
        return s, vp_ref[pl.ds(k0, width), :]

    def fixed_shift_step(k0, width, causal):
        s, v = scores(k0, width, causal)
        acc_ref[...] += _dot(jnp.exp2(s).astype(BF16), v)

    def running_max_step(k0, width, causal):
        s, v = scores(k0, width, causal)
        m_old = m_ref[...]
        m_new = jnp.maximum(m_old, jnp.max(s, axis=-1, keepdims=True))
        acc_ref[...] = jnp.exp2(m_old - m_new) * acc_ref[...] + _dot(jnp.exp2(s - m_new).astype(BF16), v)
        m_ref[...] = m_new

    def sweep(step):
        wide = SLC_WIDE * tk
        n_wide = last // SLC_WIDE
        lax.fori_loop(0, n_wide, lambda j, c: (step(pl.multiple_of(j * wide, wide), wide, False), c)[1], 0)
        for left in range(SLC_WIDE):
            pl.when(last - n_wide * SLC_WIDE == left)(
                functools.partial(step, pl.multiple_of(n_wide * wide, tk), (left + 1) * tk, True))

    fixed = shift_ref[0] <= FIXED_SHIFT_MAX

    @pl.when(fixed)
    def _():
        sweep(fixed_shift_step)

    @pl.when(jnp.logical_not(fixed))
    def _():
        m_ref[...] = jnp.full(m_ref.shape, NEG, F32)
        sweep(running_max_step)

    o_slc = acc_ref[:, 0:LANES] / jnp.maximum(acc_ref[:, LANES:2 * LANES], 1e-30)
    gt = gt_ref[...]
    for r in range(rep):
        o = ocw_ref[:, r * LANES:(r + 1) * LANES].astype(F32) + gt[:, 3 * r + 1:3 * r + 2] * o_slc[r * qn:(r + 1) * qn]
        o_ref[:, r * LANES:(r + 1) * LANES] = o.astype(o_ref.dtype)


def _mem_kv_kernel(mem_ref, g_ref, wkv_ref, kn_ref, k_ref, v_ref):
    memn = _rms(mem_ref[...], g_ref[...]).astype(BF16)
    kv = _dot(memn, wkv_ref[...])
    inner = k_ref.shape[1]
    for h in range(inner // X_HEADDIM):
        sl = slice(h * X_HEADDIM, (h + 1) * X_HEADDIM)
        k_ref[:, sl] = _rms(kv[:, sl], kn_ref[...]).astype(BF16)
    v_ref[...] = kv[:, inner:].astype(BF16)


def _mem_kv(mem, g, wkv, kn):
    b, m, d = mem.shape
    inner = wkv.shape[1] // 2
    out_spec = pl.BlockSpec((None, m, inner), lambda bi: (bi, 0, 0))
    out_shape = jax.ShapeDtypeStruct((b, m, inner), BF16)
    return pl.pallas_call(
        _mem_kv_kernel,
        grid=(b,),
        in_specs=[
            pl.BlockSpec((None, m, d), lambda bi: (bi, 0, 0)),
            pl.BlockSpec((1, d), lambda bi: (0, 0)),
            pl.BlockSpec(wkv.shape, lambda bi: (0, 0)),
            pl.BlockSpec((1, X_HEADDIM), lambda bi: (0, 0)),
        ],
        out_specs=[out_spec, out_spec],
        out_shape=[out_shape, out_shape],
        compiler_params=_params("parallel"),
        name="mem_kv",
    )(mem, g, wkv, kn)


def _cross_attn_kernel(*refs, n_in):
    a_refs = refs[:n_in]
    w_ref, x_ref, g_ref, wq_ref, qn_ref, k_ref, v_ref, wo_ref, o_ref = refs[n_in:]
    x = x_ref[...]
    k0 = 0
    for a_ref in a_refs:
        x = x + _dot(a_ref[...], w_ref[k0:k0 + a_ref.shape[1], :])
        k0 += a_ref.shape[1]
    q = _dot(_rms(x, g_ref[...]).astype(BF16), wq_ref[...])
    scale = X_HEADDIM ** -0.5
    outs = []
    for h in range(q.shape[1] // X_HEADDIM):
        sl = slice(h * X_HEADDIM, (h + 1) * X_HEADDIM)
        qh = (_rms(q[:, sl], qn_ref[...]) * scale).astype(BF16)
        s = _dot_nt(qh, k_ref[:, sl])
        e = jnp.exp(s - jnp.max(s, axis=-1, keepdims=True))
        p = e / jnp.sum(e, axis=-1, keepdims=True)
        outs.append(_dot(p.astype(BF16), v_ref[:, sl]))
    o = jnp.concatenate(outs, axis=1).astype(BF16)
    o_ref[...] = x + _dot(o, wo_ref[...])


def _cross_attn(acts, w_out, x, g, wq, qn, k, v, wo, *, tm=1024):
    t, d = x.shape
    b, m, inner = k.shape
    nt = t // b // tm
    return pl.pallas_call(
        functools.partial(_cross_attn_kernel, n_in=len(acts)),
        grid=(b, nt),
        in_specs=[pl.BlockSpec((tm, a.shape[1]), lambda bi, i: (bi * nt + i, 0)) for a in acts] + [
            pl.BlockSpec(w_out.shape, lambda bi, i: (0, 0)),
            pl.BlockSpec((tm, d), lambda bi, i: (bi * nt + i, 0)),
            pl.BlockSpec((1, d), lambda bi, i: (0, 0)),
            pl.BlockSpec(wq.shape, lambda bi, i: (0, 0)),
            pl.BlockSpec((1, X_HEADDIM), lambda bi, i: (0, 0)),
            pl.BlockSpec((None, m, inner), lambda bi, i: (bi, 0, 0)),
            pl.BlockSpec((None, m, inner), lambda bi, i: (bi, 0, 0)),
            pl.BlockSpec(wo.shape, lambda bi, i: (0, 0)),
        ],
        out_specs=pl.BlockSpec((tm, d), lambda bi, i: (bi * nt + i, 0)),
        out_shape=jax.ShapeDtypeStruct((t, d), F32),
        compiler_params=_params("parallel", "parallel"),
        name="cross_attn",
    )(*acts, w_out, x, g, wq, qn, k, v, wo)


def _swiglu_kernel(x_ref, g_ref, wu_ref, wg_ref, w2_ref, o_ref, h_ref):
    @pl.when(pl.program_id(1) == 0)
    def _():
        h_ref[...] = _rms(x_ref[...], g_ref[...]).astype(BF16)
        o_ref[...] = x_ref[...]

    h = h_ref[...]
    tf = wu_ref.shape[1]
    half = tf // 2 // LANES * LANES
    acts = []
    for c in (slice(0, half), slice(half, tf)):
        acts.append((_silu(_dot(h, wg_ref[:, c])) * _dot(h, wu_ref[:, c])).astype(BF16))
    act = jnp.concatenate(acts, axis=1)
    for c0 in range(0, o_ref.shape[1], EXPERT_CHUNK):
        c = slice(c0, c0 + EXPERT_CHUNK)
        o_ref[:, c] += _dot(act, w2_ref[:, c])


def _swiglu(x, g, w13, w2, *, tm=1024, tf=1408):
    t, d = x.shape
    ff = w2.shape[0]
    nf = ff // tf
    return pl.pallas_call(
        _swiglu_kernel,
        grid=(t // tm, nf),
        in_specs=[
            pl.BlockSpec((tm, d), lambda i, f: (i, 0)),
            pl.BlockSpec((1, d), lambda i, f: (0, 0)),
            pl.BlockSpec((d, tf), lambda i, f: (0, f)),
            pl.BlockSpec((d, tf), lambda i, f: (0, nf + f)),
            pl.BlockSpec((tf, d), lambda i, f: (f, 0)),
        ],
        out_specs=pl.BlockSpec((tm, d), lambda i, f: (i, 0)),
        out_shape=jax.ShapeDtypeStruct((t, d), F32),
        scratch_shapes=[pltpu.VMEM((tm, d), BF16)],
        compiler_params=_params("parallel", "arbitrary"),
        name="swiglu",
    )(x, g, w13, w13, w2)


def _split3(x):
    a = x.astype(BF16)
    r = x - a.astype(F32)
    b = r.astype(BF16)
    c = (r - b.astype(F32)).astype(BF16)
    return a, b, c


def _ssd_kernel(z0_ref, z1_ref, x0_ref, x1_ref, bc_ref, dt_ref, cw_ref, cb_ref, dtb_ref, alog_ref, dskip_ref,
                ng_ref, o_ref, tail_ref, state_ref, *, q):
    rows = x0_ref.shape[0]
    d_inner = o_ref.shape[1]
    gn = SSM_GROUPS * SSM_STATE
    hpg = d_inner // SSM_HEADDIM // SSM_GROUPS

    @pl.when(pl.program_id(1) == 0)
    def _():
        tail_ref[...] = jnp.zeros_like(tail_ref)
        state_ref[...] = jnp.zeros_like(state_ref)

    raw_b = jnp.concatenate([x0_ref[...], x1_ref[...], bc_ref[...]], axis=1)
    raw = raw_b.astype(F32)
    xbc = _silu(_causal_conv(raw_b, raw, tail_ref[...], cw_ref, cb_ref))
    tail_ref[...] = raw[rows - 8:rows, :]
    dt_all = _softplus(dt_ref[...] + dtb_ref[...])
    a_all = dt_all * (-jnp.exp(alog_ref[...]))
    z_all = jnp.concatenate([z0_ref[...], z1_ref[...]], axis=1).astype(F32)
    ri = lax.broadcasted_iota(I32, (q, q), 0)
    ci = lax.broadcasted_iota(I32, (q, q), 1)
    causal = ci <= ri
    tri = jnp.where(causal, 1.0, 0.0).astype(BF16)
    lane = lax.broadcasted_iota(I32, (1, LANES), 1)
    lo = lane < SSM_HEADDIM
    for r0 in range(0, rows, q):
        _ssd_chunk(xbc[r0:r0 + q], dt_all[r0:r0 + q], a_all[r0:r0 + q], z_all[r0:r0 + q], causal, tri, lo, dskip_ref,
                   ng_ref, state_ref, o_ref.at[pl.ds(r0, q)], d_inner=d_inner, gn=gn, hpg=hpg)


def _ssd_chunk(xbc, dt, a, z, causal, tri, lo, dskip_ref, ng_ref, state_ref, o_ref, *, d_inner, gn, hpg):
    q = xbc.shape[0]
    xs = xbc[:, :d_inner]
    bm = xbc[:, d_inner:d_inner + gn]
    cm = xbc[:, d_inner + gn:]
    a_cs = sum(_dot(tri, part) for part in _split3(a))
    a_cs_t = a_cs.T
    dt_t = dt.T

    y_parts = []
    for g in range(SSM_GROUPS):
        cg = cm[:, g * SSM_STATE:(g + 1) * SSM_STATE].astype(BF16)
        bg = bm[:, g * SSM_STATE:(g + 1) * SSM_STATE]
        gmat = _dot_nt(cg, bg.astype(BF16))
        bg_t = bg.T
        gw = hpg * SSM_HEADDIM
        prev = state_ref[:, g * gw:(g + 1) * gw]
        y_off = _dot(cg, prev.astype(BF16))
        for pr in range(hpg // 2):
            c0 = g * gw + pr * LANES
            x_pair = xs[:, c0:c0 + LANES]
            y_pair = dskip_ref[:, c0:c0 + LANES] * x_pair
            st_pair = jnp.zeros((SSM_STATE, LANES), F32)
            decay_pair = jnp.zeros((1, LANES), F32)
            for half in range(2):
                h = g * hpg + pr * 2 + half
                sel = lo if half == 0 else jnp.logical_not(lo)
                xh = jnp.where(sel, x_pair, 0.0).astype(BF16)
                row_cs = a_cs_t[h:h + 1, :]
                col_cs = a_cs[:, h:h + 1]
                row_dt = dt_t[h:h + 1, :]
                a_last = a_cs_t[h:h + 1, q - 1:q]
                dec = jnp.exp(jnp.where(causal, col_cs - row_cs, NEG))
                y_pair = y_pair + _dot((gmat * dec * row_dt).astype(BF16), xh)
                w_row = jnp.exp(a_last - row_cs) * row_dt
                st_pair = st_pair + _dot((bg_t * w_row).astype(BF16), xh)
                y_pair = y_pair + jnp.where(sel, jnp.exp(col_cs) * y_off[:, pr * LANES:(pr + 1) * LANES], 0.0)
                decay_pair = jnp.where(sel, jnp.exp(a_last), decay_pair)
            state_ref[:, c0:c0 + LANES] = decay_pair * state_ref[:, c0:c0 + LANES] + st_pair
            y_parts.append(y_pair)
    y = jnp.concatenate(y_parts, axis=1) * _silu(z)
    gsz = d_inner // SSM_GROUPS
    for g in range(SSM_GROUPS):
        sl = slice(g * gsz, (g + 1) * gsz)
        o_ref[:, sl] = _rms(y[:, sl], ng_ref[:, sl]).astype(o_ref.dtype)


def _ssd(main, dt, cw, cb, dtb, alog, dskip, ng, *, batch, d_inner):
    t = main.shape[0]
    q = SSD_CHUNK * SSD_CHUNKS_PER_STEP
    nc = t // batch // q
    conv_ch = cw.shape[1]
    half = d_inner // 2
    col = lambda j: pl.BlockSpec((q, half), lambda b, i: (b * nc + i, j))
    vec = lambda n: pl.BlockSpec((1, n), lambda b, i: (0, 0))
    return pl.pallas_call(
        functools.partial(_ssd_kernel, q=SSD_CHUNK),
        grid=(batch, nc),
        in_specs=[
            col(0), col(1), col(2), col(3), col(4),
            pl.BlockSpec((q, LANES), lambda b, i: (b * nc + i, 0)),
            pl.BlockSpec((CONV_W, conv_ch), lambda b, i: (0, 0)),
            vec(conv_ch), vec(LANES), vec(LANES), vec(d_inner), vec(d_inner),
        ],
        out_specs=pl.BlockSpec((q, d_inner), lambda b, i: (b * nc + i, 0)),
        out_shape=jax.ShapeDtypeStruct((t, d_inner), BF16),
        scratch_shapes=[pltpu.VMEM((8, conv_ch), F32), pltpu.VMEM((SSM_STATE, d_inner), F32)],
        compiler_params=_params("parallel", "arbitrary"),
        name="ssd",
    )(main, main, main, main, main, dt, cw, cb, dtb, alog, dskip, ng)


def _router_kernel(x_ref, g_ref, wr_ref, h_ref, info_ref, cnt_ref, run_ref):
    tm = x_ref.shape[0]

    @pl.when(pl.program_id(0) == 0)
    def _():
        run_ref[...] = jnp.zeros_like(run_ref)

    h = _rms(x_ref[...], g_ref[...])
    h_ref[...] = h
    h_hi = h.astype(BF16)
    h_lo = (h - h_hi.astype(F32)).astype(BF16)
    w = wr_ref[...]
    w_hi = w.astype(BF16)
    w_lo = (w - w_hi.astype(F32)).astype(BF16)
    logits = _dot(h_hi, w_hi) + _dot(h_lo, w_hi) + _dot(h_hi, w_lo)
    lane = lax.broadcasted_iota(I32, (tm, LANES), 1)
    lg = jnp.where(lane < N_EXPERTS, logits, NEG)
    m1 = jnp.max(lg, axis=-1, keepdims=True)
    i1 = jnp.min(jnp.where(lg == m1, lane, LANES), axis=-1, keepdims=True)
    lg2 = jnp.where(lane == i1, NEG, lg)
    m2 = jnp.max(lg2, axis=-1, keepdims=True)
    i2 = jnp.min(jnp.where(lg2 == m2, lane, LANES), axis=-1, keepdims=True)
    e2 = jnp.exp(m2 - m1)
    w1 = 1.0 / (1.0 + e2)
    w2 = e2 / (1.0 + e2)

    hot1 = lane == i1
    hot2 = lane == i2
    hot = jnp.where(hot1 | hot2, 1.0, 0.0)
    ri = lax.broadcasted_iota(I32, (tm, tm), 0)
    ci = lax.broadcasted_iota(I32, (tm, tm), 1)
    before = jnp.where(ci < ri, 1.0, 0.0).astype(BF16)
    seen = run_ref[0:1, :] + _dot(before, hot.astype(BF16))
    rank1 = jnp.sum(jnp.where(hot1, seen, 0.0), axis=-1, keepdims=True)
    rank2 = jnp.sum(jnp.where(hot2, seen, 0.0), axis=-1, keepdims=True)
    run_ref[...] = run_ref[...] + jnp.sum(hot, axis=0, keepdims=True)
    cnt_ref[...] = run_ref[...]

    cols = [i1.astype(F32), i2.astype(F32), w1, w2, rank1, rank2]
    info = jnp.zeros((tm, LANES), F32)
    for c, v in enumerate(cols):
        info = jnp.where(lane == c, v, info)
    info_ref[...] = info


def _router(x, g, wr, *, tm=256):
    t, d = x.shape
    return pl.pallas_call(
        _router_kernel,
        grid=(t // tm,),
        in_specs=[
            pl.BlockSpec((tm, d), lambda i: (i, 0)),
            pl.BlockSpec((1, d), lambda i: (0, 0)),
            pl.BlockSpec(wr.shape, lambda i: (0, 0)),
        ],
        out_specs=[
            pl.BlockSpec((tm, d), lambda i: (i, 0)),
            pl.BlockSpec((tm, LANES), lambda i: (i, 0)),
            pl.BlockSpec((8, LANES), lambda i: (0, 0)),
        ],
        out_shape=[
            jax.ShapeDtypeStruct((t, d), F32),
            jax.ShapeDtypeStruct((t, LANES), F32),
            jax.ShapeDtypeStruct((8, LANES), F32),
        ],
        scratch_shapes=[pltpu.VMEM((8, LANES), F32)],
        compiler_params=_params("arbitrary"),
        name="moe_router",
    )(x, g, wr)


def _row_copy(src_ref, src_row, dst_ref, dst_row, sem):
    return pltpu.make_async_copy(src_ref.at[pl.ds(src_row, 1)], dst_ref.at[pl.ds(dst_row, 1)], sem)


def _dispatch_kernel(pos_ref, h_ref, init_ref, xs_ref, sem):
    del init_ref
    tt = h_ref.shape[0]

    def issue(r, c):
        _row_copy(h_ref, r, xs_ref, pos_ref[0, r], sem).start()
        _row_copy(h_ref, r, xs_ref, pos_ref[1, r], sem).start()
        return c

    lax.fori_loop(0, tt, issue, 0, unroll=DMA_UNROLL)
    for _ in range(2):
        pltpu.make_async_copy(h_ref, xs_ref.at[pl.ds(0, tt)], sem).wait()


def _dispatch(h, pos, rows, *, tt=256):
    t, d = h.shape
    init = jnp.zeros((rows, d), h.dtype)
    return pl.pallas_call(
        _dispatch_kernel,
        grid=(t // tt,),
        in_specs=[
            pl.BlockSpec((None, 2, tt), lambda i: (i, 0, 0), memory_space=pltpu.SMEM),
            pl.BlockSpec((tt, d), lambda i: (i, 0)),
            pl.BlockSpec(memory_space=pl.ANY),
        ],
        out_specs=pl.BlockSpec(memory_space=pl.ANY),
        out_shape=jax.ShapeDtypeStruct((rows, d), h.dtype),
        scratch_shapes=[pltpu.SemaphoreType.DMA(())],
        input_output_aliases={2: 0},
        compiler_params=_params("arbitrary"),
        name="moe_dispatch",
    )(pos, h, init)


def _expert_kernel(te_ref, tv_ref, xs_ref, wu_ref, wg_ref, w2_ref, o_ref, xb_ref):
    del te_ref
    i = pl.program_id(0)
    f = pl.program_id(1)
    live = tv_ref[i] > 0

    @pl.when(f == 0)
    def _():
        o_ref[...] = jnp.zeros_like(o_ref)
        xb_ref[...] = xs_ref[...].astype(BF16)

    @pl.when(live)
    def _():
        x = xb_ref[...]
        act = (_silu(_dot(x, wg_ref[...].astype(BF16))) * _dot(x, wu_ref[...].astype(BF16))).astype(BF16)
        for c0 in range(0, o_ref.shape[1], EXPERT_CHUNK):
            c = slice(c0, c0 + EXPERT_CHUNK)
            o_ref[:, c] += _dot(act, w2_ref[:, c].astype(BF16))


def _experts(xs, w13, w2, tile_expert, tile_live, *, tm, tf=512):
    rows = xs.shape[0]
    ff, d = w2.shape[1:]
    nf = ff // tf

    def f_of(i, f, te, tv):
        return jnp.where(tv[i] > 0, f, nf - 1)

    grid_spec = pltpu.PrefetchScalarGridSpec(
        num_scalar_prefetch=2,
        grid=(rows // tm, nf),
        in_specs=[
            pl.BlockSpec((tm, xs.shape[1]), lambda i, f, te, tv: (i, 0)),
            pl.BlockSpec((None, d, tf), lambda i, f, te, tv: (te[i], 0, f_of(i, f, te, tv))),
            pl.BlockSpec((None, d, tf), lambda i, f, te, tv: (te[i], 0, nf + f_of(i, f, te, tv))),
            pl.BlockSpec((None, tf, d), lambda i, f, te, tv: (te[i], f_of(i, f, te, tv), 0)),
        ],
        out_specs=pl.BlockSpec((tm, d), lambda i, f, te, tv: (i, 0)),
        scratch_shapes=[pltpu.VMEM((tm, d), BF16)],
    )
    return pl.pallas_call(
        _expert_kernel,
        grid_spec=grid_spec,
        out_shape=jax.ShapeDtypeStruct((rows, d), F32),
        compiler_params=_params("parallel", "arbitrary"),
        name="moe_experts",
    )(tile_expert, tile_live, xs, w13, w13, w2)


def _combine_kernel(pos_ref, x_ref, info_ref, ys_ref, o_ref, buf_ref, sem):
    tt = x_ref.shape[0]

    def issue(r, c):
        _row_copy(ys_ref, pos_ref[0, r], buf_ref.at[0], r, sem).start()
        _row_copy(ys_ref, pos_ref[1, r], buf_ref.at[1], r, sem).start()
        return c

    lax.fori_loop(0, tt, issue, 0, unroll=DMA_UNROLL)
    for k in range(2):
        pltpu.make_async_copy(ys_ref.at[pl.ds(0, tt)], buf_ref.at[k], sem).wait()
    info = info_ref[...]
    o_ref[...] = x_ref[...] + info[:, 2:3] * buf_ref[0] + info[:, 3:4] * buf_ref[1]


def _combine(x, info, pos, ys, *, tt=256):
    t, d = x.shape
    return pl.pallas_call(
        _combine_kernel,
        grid=(t // tt,),
        in_specs=[
            pl.BlockSpec((None, 2, tt), lambda i: (i, 0, 0), memory_space=pltpu.SMEM),
            pl.BlockSpec((tt, d), lambda i: (i, 0)),
            pl.BlockSpec((tt, LANES), lambda i: (i, 0)),
            pl.BlockSpec(memory_space=pl.ANY),
        ],
        out_specs=pl.BlockSpec((tt, d), lambda i: (i, 0)),
        out_shape=jax.ShapeDtypeStruct((t, d), F32),
        scratch_shapes=[pltpu.VMEM((2, tt, d), F32), pltpu.SemaphoreType.DMA(())],
        compiler_params=_params("arbitrary"),
        name="moe_combine",
    )(pos, x, info, ys)


def _moe(x, g, router, w13, w2, *, tm=1024, tt=1024):
    t, d = x.shape
    n_exp = router.shape[1]
    wr = jnp.pad(router, ((0, 0), (0, LANES - n_exp)))
    h, info, counts = _router(x, g, wr, tm=tt)

    counts = counts[0, :n_exp].astype(I32)
    seg = (counts + tm - 1) // tm * tm
    seg_end = jnp.cumsum(seg)
    seg_start = seg_end - seg
    e1 = info[:, 0].astype(I32)
    e2 = info[:, 1].astype(I32)
    pos = jnp.stack([seg_start[e1] + info[:, 4].astype(I32), seg_start[e2] + info[:, 5].astype(I32)], axis=0)
    pos = pos.reshape(2, t // tt, tt).transpose(1, 0, 2)
    rows = 2 * t + n_exp * tm
    tile_row0 = jnp.arange(rows // tm, dtype=I32) * tm
    tile_live = (tile_row0 < seg_end[-1]).astype(I32)
    tile_expert = jnp.sum((seg_end[None, :] <= tile_row0[:, None]).astype(I32), axis=1)
    tile_expert = jnp.minimum(tile_expert, n_exp - 1)
    last_live = jnp.maximum(jnp.sum(tile_live) - 1, 0)
    tile_expert = jnp.where(tile_live > 0, tile_expert, tile_expert[last_live])

    xs = _dispatch(h, pos, rows, tt=tt)
    ys = _experts(xs, w13, w2, tile_expert, tile_live, tm=tm)
    return _combine(x, info, pos, ys, tt=tt)


def _row(v, n=None):
    v = v.reshape(1, -1).astype(F32)
    if n is not None and v.shape[1] < n:
        v = jnp.pad(v, ((0, 0), (0, n - v.shape[1])))
    return v


def _overlap_matrix(seq):
    n = seq // CMP_STRIDE
    cmp_start = np.arange(n) * CMP_STRIDE
    slc_start = np.arange(LANES) * SLC_LEN
    ov = (cmp_start[:, None] <= slc_start[None, :] + SLC_LEN - 1) & (cmp_start[:, None] + CMP_LEN - 1 >= slc_start[None, :])
    ov[n - 1] = False
    return jnp.asarray(ov, dtype=BF16)


def _even_layer(x, batch, norm_mix, w_in, conv_w, conv_b, wa, ba, wx, bx, lam, gate_b, q_norm, k_norm, cmp_pos,
                ck_w1, ck_w2, cv_w1, cv_w2):
    t, d = x.shape
    seq = t // batch
    rg = wa.shape[0] * wa.shape[1]
    gdk = NSA_GROUPS * NSA_DK
    n_main = 2 * rg + NSA_HEADS * NSA_DK
    n_kv = 6 * gdk
    per_group = 3 * NSA_REP
    gate_cols = w_in[:, n_main + n_kv:].reshape(d, NSA_GROUPS, per_group)
    gate_cols = jnp.pad(gate_cols, ((0, 0), (0, 0), (0, LANES - per_group))).reshape(d, NSA_GROUPS * LANES)
    gate_bias = jnp.pad(gate_b.reshape(NSA_GROUPS, per_group), ((0, 0), (0, LANES - per_group))).reshape(1, -1)
    w_all = jnp.concatenate([w_in[:, :n_main + n_kv], gate_cols], axis=1).astype(BF16)
    kn = jnp.pad(k_norm, ((0, 8 - k_norm.shape[0]), (0, 0)))
    main, planes, kp, vp, kwn, vwp, gates = _norm_proj(
        x, _row(norm_mix), w_all, gate_bias, kn, batch=batch, n_main=n_main, nsa_keys=True,
        n_extra=NSA_GROUPS * LANES, extra_sigmoid=True)

    rg_out = _rglru(main, conv_w, _row(conv_b), wa.astype(BF16), _row(ba), wx.astype(BF16), _row(bx), _row(lam),
                    batch=batch)

    k_gain = jnp.max(jnp.abs(k_norm), axis=1)[jnp.array([1, 0, 2])]
    shift = ((1.02 * LOG2E * math.sqrt(NSA_DK)) * jnp.max(jnp.abs(q_norm)) * k_gain).astype(F32)
    pos_flat = jnp.broadcast_to(cmp_pos.reshape(1, -1), (8, CMP_LEN * NSA_DK)).astype(BF16)
    kcmp, vcmp = _nsa_compress(planes, pos_flat, ck_w1.astype(BF16), ck_w2.astype(BF16), cv_w1.astype(BF16),
                               cv_w2.astype(BF16), kn, _overlap_matrix(seq))
    att = _nsa_attn(shift, main, kcmp, vcmp, kwn, vwp, kp, vp, gates, _row(q_norm))
    return [rg_out, att]


def _odd_layer(x, batch, norm_mix, w_in, conv_w, conv_b, dt_bias, a_log, d_skip, norm_g, d_inner):
    conv_ch = conv_w.shape[1]
    n_main = d_inner + conv_ch
    heads = dt_bias.shape[0]
    w_all = jnp.concatenate([w_in[:, :n_main], jnp.pad(w_in[:, n_main:], ((0, 0), (0, LANES - heads)))], axis=1)
    main, dt = _norm_proj(x, _row(norm_mix), w_all.astype(BF16), jnp.zeros((1, LANES), F32),
                          jnp.zeros((8, LANES), F32), batch=batch, n_main=n_main, nsa_keys=False, n_extra=LANES,
                          extra_sigmoid=False)
    y = _ssd(main, dt, conv_w, _row(conv_b), _row(dt_bias, LANES), _row(a_log, LANES),
             _row(jnp.repeat(d_skip, SSM_HEADDIM)), _row(norm_g), batch=batch, d_inner=d_inner)
    return [y]


def kernel(x, mem, norm_mix, norm_cross, norm_mem, norm_ffn, ev_w_in, ev_rg_conv_w, ev_rg_conv_b, ev_rg_wa, ev_rg_ba, ev_rg_wx, ev_rg_bx, ev_rg_lambda, ev_nsa_gate_b, ev_q_norm, ev_k_norm, ev_cmp_pos, ev_cmp_k_w1, ev_cmp_k_w2, ev_cmp_v_w1, ev_cmp_v_w2, ev_w_out, od_w_in, od_conv_w, od_conv_b, od_dt_bias, od_a_log, od_d_skip, od_norm, od_w_out, x_wq, x_wkv, x_q_norm, x_k_norm, x_wo, ff_w13, ff_w2, moe_router, moe_w13, moe_w2):
    batch, seq, d = x.shape
    depth = norm_mix.shape[0]
    xf = x.reshape(batch * seq, d)
    for layer in range(depth):
        i = layer // 2
        if layer % 2 == 0:
            w_out = ev_w_out[i]
            mixed = _even_layer(xf, batch, norm_mix[layer], ev_w_in[i], ev_rg_conv_w[i], ev_rg_conv_b[i], ev_rg_wa[i],
                                ev_rg_ba[i], ev_rg_wx[i], ev_rg_bx[i], ev_rg_lambda[i], ev_nsa_gate_b[i],
                                ev_q_norm[i], ev_k_norm[i], ev_cmp_pos[i], ev_cmp_k_w1[i], ev_cmp_k_w2[i],
                                ev_cmp_v_w1[i], ev_cmp_v_w2[i])
        else:
            w_out = od_w_out[i]
            mixed = _odd_layer(xf, batch, norm_mix[layer], od_w_in[i], od_conv_w[i], od_conv_b[i], od_dt_bias[i],
                               od_a_log[i], od_d_skip[i], od_norm[i], w_out.shape[0])
        k, v = _mem_kv(mem, _row(norm_mem[layer]), x_wkv[layer].astype(BF16), _row(x_k_norm[layer]))
        xf = _cross_attn(mixed, w_out.astype(BF16), xf, _row(norm_cross[layer]), x_wq[layer].astype(BF16),
                         _row(x_q_norm[layer]), k, v, x_wo[layer].astype(BF16))
        if layer % 2 == 0:
            xf = _swiglu(xf, _row(norm_ffn[layer]), ff_w13[i].astype(BF16), ff_w2[i].astype(BF16))
        else:
            xf = _moe(xf, _row(norm_ffn[layer]), moe_router[i], moe_w13[i], moe_w2[i])
    return xf.reshape(batch, seq, d)
```
